```python
import jax
import jax.numpy as jnp
from jax import lax
import numpy as np

D_MODEL = 1024
BATCH = 8
SEQ = 4096
DEPTH = 2

GRID_W = 64
CTX_LEN = 256
F32 = jnp.float32
EPS = 1e-6
NEG_INF = -1e30
HEAD_DIM = 64
ROPE_THETA = 10000.0

RET_HEADS = 4
RET_DK = 64
RET_DV = 128
RET_CHUNK = 128
WIN_HEADS = 8
WIN_KV_HEADS = 2
WINDOW = 128
WIN_BLOCK = 128
NA_HEADS = 8
NA_KH = 8
NA_KW = 16
NA_QCOLS = 16
NA_BAND = NA_QCOLS + NA_KW
GDN_HEADS = 4
GDN_DK = 128
GDN_DV = 128
GDN_CHUNK = 64
SHORT_CONV = 3
N_BRANCH = 4
BRANCH_W = 512
N_EXPERTS = 32
N_GROUPS = 8
EXPERTS_PER_GROUP = N_EXPERTS // N_GROUPS
TOP_K = 2
D_EXPERT = 512
MOE_BLOCK = 128

IN_LAYOUT = (
    ('ret_q', RET_HEADS * RET_DK), ('ret_k', RET_HEADS * RET_DK),
    ('ret_v', RET_HEADS * RET_DV), ('ret_g', RET_HEADS * RET_DV),
    ('win_q', WIN_HEADS * HEAD_DIM), ('win_k', WIN_KV_HEADS * HEAD_DIM), ('win_v', WIN_KV_HEADS * HEAD_DIM),
    ('na_q', NA_HEADS * HEAD_DIM), ('na_k', NA_HEADS * HEAD_DIM), ('na_v', NA_HEADS * HEAD_DIM),
    ('gdn_q', GDN_HEADS * GDN_DK), ('gdn_k', GDN_HEADS * GDN_DK), ('gdn_v', GDN_HEADS * GDN_DV),
    ('gdn_g', GDN_HEADS * GDN_DV), ('gdn_a', 2 * GDN_HEADS), ('gdn_b', 2 * GDN_HEADS),
)
IN_NAMES = tuple(n for n, _ in IN_LAYOUT)
IN_OFFSETS = tuple(sum(w for _, w in IN_LAYOUT[:i + 1]) for i in range(len(IN_LAYOUT) - 1))
D_IN = sum(w for _, w in IN_LAYOUT)
GDN_QKV = 2 * GDN_HEADS * GDN_DK + GDN_HEADS * GDN_DV

kernel_name = 'hybrid_prefix_dit_retention_window_natten_gdn_moe'


def rmsnorm(x, gain):
    xf = x.astype(F32)
    y = xf * lax.rsqrt(jnp.mean(xf * xf, axis=-1, keepdims=True) + EPS)
    return (y * gain.astype(F32)).astype(x.dtype)


def l2norm(x):
    return x * lax.rsqrt(jnp.sum(x * x, axis=-1, keepdims=True) + EPS)


def modulate(x, gain, shift, scale):
    return rmsnorm(x, gain) * (1.0 + scale) + shift


def to_heads(z, n_heads):
    b, l, _ = z.shape
    return z.reshape(b, l, n_heads, -1).transpose(0, 2, 1, 3)


def from_heads(z):
    b, h, l, d = z.shape
    return z.transpose(0, 2, 1, 3).reshape(b, l, h * d)


def flip_seq(a):
    return jnp.flip(a, axis=2)


def split_projection(z):
    return dict(zip(IN_NAMES, jnp.split(z, IN_OFFSETS, axis=-1)))


def axial_rope_tables(n_tokens, dim):
    t = jnp.arange(n_tokens)
    row = (t // GRID_W).astype(F32)
    col = (t % GRID_W).astype(F32)
    n_freq = dim // 4
    inv = ROPE_THETA ** (-jnp.arange(n_freq, dtype=F32) / n_freq)
    ang = jnp.concatenate([row[:, None] * inv, col[:, None] * inv], axis=-1)
    return jnp.cos(ang), jnp.sin(ang)


def apply_rope(x, cos, sin):
    xf = x.astype(F32)
    x1, x2 = jnp.split(xf, 2, axis=-1)
    return jnp.concatenate([x1 * cos - x2 * sin, x1 * sin + x2 * cos], axis=-1).astype(x.dtype)


def sink_softmax(s, sink):
    m = jnp.maximum(jnp.max(s, axis=-1, keepdims=True), sink)
    p = jnp.exp(s - m)
    return p / (jnp.sum(p, axis=-1, keepdims=True) + jnp.exp(sink - m))


def context_attention(q, k, v, sink):
    b, hq, lc, d = q.shape
    g = k.shape[1]
    qg = q.reshape(b, g, hq // g, lc, d)
    s = jnp.einsum('bgrqd,bgkd->bgrqk', qg, k).astype(F32) * d ** -0.5
    if sink is None:
        p = jax.nn.softmax(s, axis=-1)
    else:
        p = sink_softmax(s, sink.astype(F32).reshape(g, -1)[None, :, :, None, None])
    o = jnp.einsum('bgrqk,bgkd->bgrqd', p.astype(v.dtype), v)
    return o.reshape(b, hq, lc, d)


def retention_chunked(q, k, v, log_gamma, s0):
    b, h, l, dk = q.shape
    dv = v.shape[-1]
    c = RET_CHUNK
    n = l // c
    qc = q.reshape(b, h, n, c, dk)
    kc = k.reshape(b, h, n, c, dk)
    vc = v.reshape(b, h, n, c, dv)
    pos = jnp.arange(c, dtype=F32)
    diff = pos[:, None] - pos[None, :]
    dmask = jnp.where(diff >= 0, jnp.exp(log_gamma[:, None, None] * jnp.maximum(diff, 0.0)), 0.0)
    q_decay = jnp.exp(log_gamma[:, None] * (pos + 1.0))
    k_decay = jnp.exp(log_gamma[:, None] * (c - 1.0 - pos))
    chunk_decay = jnp.exp(log_gamma * c)
    scores = jnp.einsum('bhncd,bhnsd->bhncs', qc, kc) * dmask[None, :, None]
    o_intra = jnp.einsum('bhncs,bhnse->bhnce', scores, vc)
    kv = jnp.einsum('bhncd,bhnce->bhnde', kc * k_decay[None, :, None, :, None], vc)

    def step(s, kv_n):
        return chunk_decay[None, :, None, None] * s + kv_n, s

    s_fin, s_prev = lax.scan(step, s0, jnp.moveaxis(kv, 2, 0))
    s_prev = jnp.moveaxis(s_prev, 0, 2)
    o_inter = jnp.einsum('bhncd,bhnde->bhnce', qc * q_decay[None, :, None, :, None], s_prev)
    return (o_intra + o_inter).reshape(b, h, l, dv), s_fin


def retention_state(k, v, log_gamma):
    l = k.shape[2]
    w = jnp.exp(log_gamma[:, None] * (l - 1.0 - jnp.arange(l, dtype=F32)))
    return jnp.einsum('bhld,bhle->bhde', k * w[None, :, :, None], v)


def retention_output(o, gate, gain):
    mu = jnp.mean(o, axis=-1, keepdims=True)
    var = jnp.mean(jnp.square(o - mu), axis=-1, keepdims=True)
    y = from_heads((o - mu) * lax.rsqrt(var + EPS)) * gain.astype(F32)
    return (y * jax.nn.silu(gate.astype(F32))).astype(gate.dtype)


def retention_branch(zl, zc, decay_logit, gn_gain, cos, sin, ctx_out):
    log_gamma = jax.nn.log_sigmoid(decay_logit.astype(F32))

    def qkv(z):
        q = to_heads(z['ret_q'], RET_HEADS).astype(F32)
        k = to_heads(z['ret_k'], RET_HEADS).astype(F32) * RET_DK ** -0.5
        v = to_heads(z['ret_v'], RET_HEADS).astype(F32)
        return q, k, v

    qc, kc, vc = qkv(zc)
    ql, kl, vl = qkv(zl)
    ql = apply_rope(ql, cos, sin)
    kl = apply_rope(kl, cos, sin)
    s0 = jnp.zeros((ql.shape[0], RET_HEADS, RET_DK, RET_DV), F32)
    yc = None
    if ctx_out:
        oc_f, sc_f = retention_chunked(qc, kc, vc, log_gamma[0], s0)
        oc_b, sc_b = retention_chunked(flip_seq(qc), flip_seq(kc), flip_seq(vc), log_gamma[1], s0)
        yc = retention_output(oc_f + flip_seq(oc_b), zc['ret_g'], gn_gain)
    else:
        sc_f = retention_state(kc, vc, log_gamma[0])
        sc_b = retention_state(flip_seq(kc), flip_seq(vc), log_gamma[1])
    ol_f, _ = retention_chunked(ql, kl, vl, log_gamma[0], sc_f)
    ol_b, _ = retention_chunked(flip_seq(ql), flip_seq(kl), flip_seq(vl), log_gamma[1], sc_b)
    yl = retention_output(ol_f + flip_seq(ol_b), zl['ret_g'], gn_gain)
    return yc, yl


def window_attention(q, k, v, kc, vc, sink):
    b, hq, l, d = q.shape
    g = k.shape[1]
    r = hq // g
    nb = l // WIN_BLOCK
    span = WIN_BLOCK + 2 * WINDOW
    qg = (q * d ** -0.5).reshape(b, g, r, l, d)
    kp = jnp.pad(k, ((0, 0), (0, 0), (WINDOW, WINDOW), (0, 0)))
    vp = jnp.pad(v, ((0, 0), (0, 0), (WINDOW, WINDOW), (0, 0)))
    sink_b = sink.astype(F32).reshape(g, r)[None, :, :, None, None]

    def block(n):
        s0 = n * WIN_BLOCK
        qb = lax.dynamic_slice_in_dim(qg, s0, WIN_BLOCK, axis=3)
        kb = lax.dynamic_slice_in_dim(kp, s0, span, axis=2)
        vb = lax.dynamic_slice_in_dim(vp, s0, span, axis=2)
        qpos = s0 + jnp.arange(WIN_BLOCK)
        kpos = s0 - WINDOW + jnp.arange(span)
        ok = (kpos[None, :] >= 0) & (kpos[None, :] < l) & (jnp.abs(qpos[:, None] - kpos[None, :]) <= WINDOW)
        s_loc = jnp.where(ok, jnp.einsum('bgrqd,bgkd->bgrqk', qb, kb).astype(F32), NEG_INF)
        s_ctx = jnp.einsum('bgrqd,bgkd->bgrqk', qb, kc).astype(F32)
        p = sink_softmax(jnp.concatenate([s_loc, s_ctx], axis=-1), sink_b).astype(v.dtype)
        return (jnp.einsum('bgrqk,bgkd->bgrqd', p[..., :span], vb)
                + jnp.einsum('bgrqk,bgkd->bgrqd', p[..., span:], vc))

    out = lax.map(block, jnp.arange(nb))
    return jnp.moveaxis(out, 0, 3).reshape(b, hq, l, d)


def window_branch(zl, zc, q_gain, k_gain, sink, cos, sin, ctx_out):
    def qkv(z):
        q = rmsnorm(to_heads(z['win_q'], WIN_HEADS), q_gain)
        k = rmsnorm(to_heads(z['win_k'], WIN_KV_HEADS), k_gain)
        v = to_heads(z['win_v'], WIN_KV_HEADS)
        return q, k, v

    qc, kc, vc = qkv(zc)
    ql, kl, vl = qkv(zl)
    ql = apply_rope(ql, cos, sin)
    kl = apply_rope(kl, cos, sin)
    yl = from_heads(window_attention(ql, kl, vl, kc, vc, sink))
    yc = from_heads(context_attention(qc, kc, vc, sink)) if ctx_out else None
    return yc, yl


def neighborhood_attention(q, k, v, kc, vc, rpb):
    b, h, l, d = q.shape
    rows = l // GRID_W
    kh = min(NA_KH, rows)
    n_cb = GRID_W // NA_QCOLS
    qcol = np.arange(GRID_W).reshape(n_cb, NA_QCOLS)
    band0 = np.clip(qcol[:, 0] - NA_KW // 2, 0, GRID_W - NA_BAND)
    kcol = band0[:, None] + np.arange(NA_BAND)
    cstart = np.clip(qcol - NA_KW // 2, 0, GRID_W - NA_KW)
    col_ok = (kcol[:, None, :] >= cstart[:, :, None]) & (kcol[:, None, :] < cstart[:, :, None] + NA_KW)
    col_off = np.clip(kcol[:, None, :] - qcol[:, :, None] + NA_KW - 1, 0, 2 * NA_KW - 2)
    rstart = np.clip(np.arange(rows) - kh // 2, 0, rows - kh)
    row_off = rstart[:, None] + np.arange(kh)[None, :] - np.arange(rows)[:, None] + NA_KH - 1
    bias = rpb.astype(F32)[:, row_off[:, None, None, :, None], col_off[None, :, :, None, :]]
    bias = jnp.where(col_ok[None, None, :, :, None, :], bias, NEG_INF)
    bias = jnp.moveaxis(bias, 1, 0).reshape(rows, h, n_cb, NA_QCOLS, kh * NA_BAND)
    qg = (q * d ** -0.5).reshape(b, h, rows, n_cb, NA_QCOLS, d)
    kg = k.reshape(b, h, rows, GRID_W, d)
    vg = v.reshape(b, h, rows, GRID_W, d)

    def row_step(args):
        r, rs, bias_r = args
        qr = lax.dynamic_index_in_dim(qg, r, axis=2, keepdims=False)
        kr = lax.dynamic_slice_in_dim(kg, rs, kh, axis=2)[:, :, :, kcol]
        vr = lax.dynamic_slice_in_dim(vg, rs, kh, axis=2)[:, :, :, kcol]
        s_loc = jnp.einsum('bhcqd,bhicjd->bhcqij', qr, kr).astype(F32)
        s_loc = s_loc.reshape(b, h, n_cb, NA_QCOLS, kh * NA_BAND) + bias_r
        s_ctx = jnp.einsum('bhcqd,bhkd->bhcqk', qr, kc).astype(F32)
        p = jax.nn.softmax(jnp.concatenate([s_loc, s_ctx], axis=-1), axis=-1).astype(v.dtype)
        p_loc = p[..., :kh * NA_BAND].reshape(b, h, n_cb, NA_QCOLS, kh, NA_BAND)
        o = (jnp.einsum('bhcqij,bhicjd->bhcqd', p_loc, vr)
             + jnp.einsum('bhcqk,bhkd->bhcqd', p[..., kh * NA_BAND:], vc))
        return o.reshape(b, h, GRID_W, d)

    out = lax.map(row_step, (jnp.arange(rows, dtype=jnp.int32), jnp.asarray(rstart, jnp.int32), bias))
    return jnp.moveaxis(out, 0, 2).reshape(b, h, l, d)


def neighborhood_branch(zl, zc, q_gain, k_gain, rpb, ctx_out):
    def qkv(z):
        q = rmsnorm(to_heads(z['na_q'], NA_HEADS), q_gain)
        k = rmsnorm(to_heads(z['na_k'], NA_HEADS), k_gain)
        v = to_heads(z['na_v'], NA_HEADS)
        return q, k, v

    qc, kc, vc = qkv(zc)
    ql, kl, vl = qkv(zl)
    yl = from_heads(neighborhood_attention(ql, kl, vl, kc, vc, rpb))
    yc = from_heads(context_attention(qc, kc, vc, None)) if ctx_out else None
    return yc, yl


def short_conv(z, w):
    taps, l = w.shape[0], z.shape[1]
    pad = taps // 2
    zp = jnp.pad(z, ((0, 0), (pad, pad), (0, 0)))
    return sum(zp[:, j:j + l] * w[j] for j in range(taps))


def gated_delta_chunked(q, k, v, g, beta, s0, with_out):
    b, h, l, dk = k.shape
    dv = v.shape[-1]
    c = GDN_CHUNK
    n = l // c
    kc = k.reshape(b, h, n, c, dk)
    vc = v.reshape(b, h, n, c, dv)
    gcum = jnp.cumsum(g.reshape(b, h, n, c), axis=-1)
    bc = beta.reshape(b, h, n, c)
    incl = jnp.asarray(np.tril(np.ones((c, c), bool)))
    strict = jnp.asarray(np.tril(np.ones((c, c), bool), -1))
    gdiff = gcum[..., :, None] - gcum[..., None, :]
    decay = jnp.where(incl, jnp.exp(jnp.where(incl, gdiff, 0.0)), 0.0)
    kb = kc * bc[..., None]
    a_low = jnp.where(strict, jnp.einsum('bhnid,bhnjd->bhnij', kb, kc) * decay, 0.0)
    rhs = jnp.concatenate([vc * bc[..., None], kb * jnp.exp(gcum)[..., None]], axis=-1)
    sol = lax.linalg.triangular_solve(a_low + jnp.eye(c, dtype=F32), rhs, left_side=True, lower=True)
    u, w = sol[..., :dv], sol[..., dv:]
    k_end = kc * jnp.exp(gcum[..., -1:] - gcum)[..., None]
    c_decay = jnp.exp(gcum[..., -1])

    def mv(a):
        return jnp.moveaxis(a, 2, 0)

    if with_out:
        qc = q.reshape(b, h, n, c, dk)
        q_dec = qc * jnp.exp(gcum)[..., None]
        a_qk = jnp.where(incl, jnp.einsum('bhnid,bhnjd->bhnij', qc, kc) * decay, 0.0)

        def step(s, xs):
            qd, ke, u_n, w_n, aq, cd = xs
            v_new = u_n - jnp.einsum('bhcd,bhde->bhce', w_n, s)
            o = jnp.einsum('bhcd,bhde->bhce', qd, s) + jnp.einsum('bhcs,bhse->bhce', aq, v_new)
            s = s * cd[..., None, None] + jnp.einsum('bhcd,bhce->bhde', ke, v_new)
            return s, o

        s_fin, o = lax.scan(step, s0, (mv(q_dec), mv(k_end), mv(u), mv(w), mv(a_qk), mv(c_decay)))
        return jnp.moveaxis(o, 0, 2).reshape(b, h, l, dv), s_fin

    def step_state(s, xs):
        ke, u_n, w_n, cd = xs
        v_new = u_n - jnp.einsum('bhcd,bhde->bhce', w_n, s)
        return s * cd[..., None, None] + jnp.einsum('bhcd,bhce->bhde', ke, v_new), None

    s_fin, _ = lax.scan(step_state, s0, (mv(k_end), mv(u), mv(w), mv(c_decay)))
    return None, s_fin


def gdn_output(o, gate, gain):
    y = o * lax.rsqrt(jnp.mean(o * o, axis=-1, keepdims=True) + EPS) * gain.astype(F32)
    return (from_heads(y) * jax.nn.silu(gate.astype(F32))).astype(gate.dtype)


def gdn_branch(zl, zc, conv_w, a_log, dt_bias, norm_gain, ctx_out):
    def prep(z):
        b, l, _ = z['gdn_q'].shape
        qkv = jnp.concatenate([z['gdn_q'], z['gdn_k'], z['gdn_v']], axis=-1)
        qkv = jax.nn.silu(short_conv(qkv, conv_w)).astype(F32)
        q, k, v = jnp.split(qkv, [GDN_HEADS * GDN_DK, 2 * GDN_HEADS * GDN_DK], axis=-1)
        q = l2norm(to_heads(q, GDN_HEADS)) * GDN_DK ** -0.5
        k = l2norm(to_heads(k, GDN_HEADS))
        v = to_heads(v, GDN_HEADS)
        a = z['gdn_a'].astype(F32).reshape(b, l, 2, GDN_HEADS)
        bb = z['gdn_b'].astype(F32).reshape(b, l, 2, GDN_HEADS)
        g = -jnp.exp(a_log.astype(F32)) * jax.nn.softplus(a + dt_bias.astype(F32))
        beta = jax.nn.sigmoid(bb)
        return q, k, v, jnp.transpose(g, (2, 0, 3, 1)), jnp.transpose(beta, (2, 0, 3, 1))

    qc, kc, vc, gc, bc = prep(zc)
    ql, kl, vl, gl, bl = prep(zl)
    s0 = jnp.zeros((ql.shape[0], GDN_HEADS, GDN_DK, GDN_DV), F32)
    oc_f, sc_f = gated_delta_chunked(qc, kc, vc, gc[0], bc[0], s0, ctx_out)
    oc_b, sc_b = gated_delta_chunked(flip_seq(qc), flip_seq(kc), flip_seq(vc),
                                     flip_seq(gc[1]), flip_seq(bc[1]), s0, ctx_out)
    ol_f, _ = gated_delta_chunked(ql, kl, vl, gl[0], bl[0], sc_f, True)
    ol_b, _ = gated_delta_chunked(flip_seq(ql), flip_seq(kl), flip_seq(vl),
                                  flip_seq(gl[1]), flip_seq(bl[1]), sc_b, True)
    yl = gdn_output(ol_f + flip_seq(ol_b), zl['gdn_g'], norm_gain)
    yc = gdn_output(oc_f + flip_seq(oc_b), zc['gdn_g'], norm_gain) if ctx_out else None
    return yc, yl


def merge_branches(h, ys, w_merge, w_branch, w_out):
    acc = 0.0
    for i, y in enumerate(ys):
        gate = jax.nn.sigmoid(h @ w_merge[:, i * D_MODEL:(i + 1) * D_MODEL])
        acc = acc + gate * (y @ w_branch[i])
    return acc @ w_out


def moe(h, w_router, router_bias, w_gate, w_up, w_down):
    t, d = h.shape
    scores = jax.nn.sigmoid((h @ w_router).astype(F32))
    sel = (scores + router_bias.astype(F32)).reshape(t, N_GROUPS, EXPERTS_PER_GROUP)
    grp_score = jnp.sum(lax.top_k(sel, 2)[0], axis=-1)
    g_idx = jnp.argmax(grp_score, axis=-1)
    in_grp = jnp.take_along_axis(sel, g_idx[:, None, None], axis=1)[:, 0]
    _, local = lax.top_k(in_grp, TOP_K)
    e_idx = g_idx[:, None] * EXPERTS_PER_GROUP + local
    wts = jnp.take_along_axis(scores, e_idx, axis=1)
    wts = wts / jnp.sum(wts, axis=-1, keepdims=True)
    a = t * TOP_K
    flat_e = e_idx.reshape(-1)
    flat_tok = jnp.repeat(jnp.arange(t, dtype=jnp.int32), TOP_K)
    flat_w = wts.reshape(-1).astype(h.dtype)
    order = jnp.argsort(flat_e)
    se, stok, sw = flat_e[order], flat_tok[order], flat_w[order]
    counts = jnp.bincount(flat_e, length=N_EXPERTS)
    starts = jnp.cumsum(counts) - counts
    padded = (counts + MOE_BLOCK - 1) // MOE_BLOCK * MOE_BLOCK
    ends = jnp.cumsum(padded)
    pstarts = ends - padded
    dest = pstarts[se] + jnp.arange(a) - starts[se]
    n_blocks = -(-a // MOE_BLOCK) + N_EXPERTS
    size = n_blocks * MOE_BLOCK
    tok_buf = jnp.full((size,), t, jnp.int32).at[dest].set(stok)
    w_buf = jnp.zeros((size,), h.dtype).at[dest].set(sw)
    block_e = jnp.minimum(jnp.searchsorted(ends, jnp.arange(n_blocks) * MOE_BLOCK, side='right'), N_EXPERTS - 1)
    h_pad = jnp.concatenate([h, jnp.zeros((1, d), h.dtype)], axis=0)
    xb = h_pad[tok_buf].reshape(n_blocks, MOE_BLOCK, d)

    def run(args):
        xe, e = args
        return (jax.nn.silu(xe @ w_gate[e]) * (xe @ w_up[e])) @ w_down[e]

    yb = lax.map(run, (xb, block_e)).reshape(size, d)
    out = jnp.zeros((t + 1, d), h.dtype).at[tok_buf].add(yb * w_buf[:, None])
    return out[:t]


def trunk_layer(xl, xc, c, c_ctx, p, w_router, router_bias, cos, sin, ctx_out):
    mod_l = (jax.nn.silu(c) @ p['w_mod'] + p['b_mod'])[:, None, :]
    mod_c = (jax.nn.silu(c_ctx) @ p['w_mod'] + p['b_mod'])[None, None, :]
    sh1l, sc1l, g1l, sh2l, sc2l, g2l = jnp.split(mod_l, 6, axis=-1)
    sh1c, sc1c, g1c, sh2c, sc2c, g2c = jnp.split(mod_c, 6, axis=-1)
    hl = modulate(xl, p['norm1'], sh1l, sc1l)
    hc = modulate(xc, p['norm1'], sh1c, sc1c)
    zl = split_projection(hl @ p['w_in'])
    zc = split_projection(hc @ p['w_in'])
    ret_c, ret_l = retention_branch(zl, zc, p['ret_decay'], p['ret_gn'], cos, sin, ctx_out)
    win_c, win_l = window_branch(zl, zc, p['win_qnorm'], p['win_knorm'], p['win_sink'], cos, sin, ctx_out)
    na_c, na_l = neighborhood_branch(zl, zc, p['na_qnorm'], p['na_knorm'], p['na_rpb'], ctx_out)
    gdn_c, gdn_l = gdn_branch(zl, zc, p['gdn_conv'], p['gdn_a_log'], p['gdn_dt_bias'], p['gdn_norm'], ctx_out)
    xl = xl + g1l * merge_branches(hl, (ret_l, win_l, na_l, gdn_l), p['w_merge'], p['w_branch'], p['w_out'])
    hl2 = modulate(xl, p['norm2'], sh2l, sc2l)
    b, l, d = xl.shape
    if not ctx_out:
        y = moe(hl2.reshape(b * l, d), w_router, router_bias, p['w_e_gate'], p['w_e_up'], p['w_e_down'])
        return xl + g2l * y.reshape(b, l, d), xc
    xc = xc + g1c * merge_branches(hc, (ret_c, win_c, na_c, gdn_c), p['w_merge'], p['w_branch'], p['w_out'])
    hc2 = modulate(xc, p['norm2'], sh2c, sc2c)
    lc = xc.shape[1]
    tokens = jnp.concatenate([hc2.reshape(b * lc, d), hl2.reshape(b * l, d)], axis=0)
    y = moe(tokens, w_router, router_bias, p['w_e_gate'], p['w_e_up'], p['w_e_down'])
    xc = xc + g2c * y[:b * lc].reshape(b, lc, d)
    xl = xl + g2l * y[b * lc:].reshape(b, l, d)
    return xl, xc


def setup_inputs(seed: int = 0) -> dict:
    key = jax.random.key(seed)
    ks = iter(jax.random.split(key, 40))
    D = D_MODEL

    def nrm(shape, scale):
        return scale * jax.random.normal(next(ks), shape, F32)

    def unif(shape, lo, hi):
        return jax.random.uniform(next(ks), shape, F32, lo, hi)

    gamma0 = 1.0 - 2.0 ** (-5.0 - jnp.arange(RET_HEADS, dtype=F32))
    dt = jnp.exp(unif((DEPTH, 2, GDN_HEADS), float(np.log(1e-3)), float(np.log(1e-1))))
    return {
        'x': nrm((BATCH, SEQ, D), 1.0),
        'c': nrm((BATCH, D), 1.0),
        'ctx': nrm((BATCH, CTX_LEN, D), 1.0),
        'c_ctx': nrm((D,), 1.0),
        'w_mod': nrm((DEPTH, D, 6 * D), 0.5 * D ** -0.5),
        'b_mod': nrm((DEPTH, 6 * D), 0.01),
        'norm1': 1.0 + nrm((DEPTH, D), 0.02),
        'norm2': 1.0 + nrm((DEPTH, D), 0.02),
        'w_in': nrm((DEPTH, D, D_IN), D ** -0.5),
        'ret_decay': jnp.log(gamma0 / (1.0 - gamma0))[None, None, :] + nrm((DEPTH, 2, RET_HEADS), 0.1),
        'ret_gn': 1.0 + nrm((DEPTH, RET_HEADS * RET_DV), 0.02),
        'win_qnorm': 1.0 + nrm((DEPTH, HEAD_DIM), 0.02),
        'win_knorm': 1.0 + nrm((DEPTH, HEAD_DIM), 0.02),
        'win_sink': nrm((DEPTH, WIN_HEADS), 0.5),
        'na_qnorm': 1.0 + nrm((DEPTH, HEAD_DIM), 0.02),
        'na_knorm': 1.0 + nrm((DEPTH, HEAD_DIM), 0.02),
        'na_rpb': nrm((DEPTH, NA_HEADS, 2 * NA_KH - 1, 2 * NA_KW - 1), 0.02),
        'gdn_conv': nrm((DEPTH, SHORT_CONV, GDN_QKV), SHORT_CONV ** -0.5),
        'gdn_a_log': jnp.log(unif((DEPTH, 2, GDN_HEADS), 1.0, 16.0)),
        'gdn_dt_bias': dt + jnp.log(-jnp.expm1(-dt)),
        'gdn_norm': 1.0 + nrm((DEPTH, GDN_DV), 0.02),
        'w_branch': nrm((DEPTH, N_BRANCH, BRANCH_W, D), BRANCH_W ** -0.5),
        'w_merge': nrm((DEPTH, D, N_BRANCH * D), D ** -0.5),
        'w_out': nrm((DEPTH, D, D), D ** -0.5),
        'w_router': nrm((D, N_EXPERTS), D ** -0.5),
        'router_bias': nrm((N_EXPERTS,), 0.01),
        'w_e_gate': nrm((DEPTH, N_EXPERTS, D, D_EXPERT), D ** -0.5),
        'w_e_up': nrm((DEPTH, N_EXPERTS, D, D_EXPERT), D ** -0.5),
        'w_e_down': nrm((DEPTH, N_EXPERTS, D_EXPERT, D), D_EXPERT ** -0.5),
    }


def reference(x, c, ctx, c_ctx, w_mod, b_mod, norm1, norm2, w_in, ret_decay, ret_gn, win_qnorm, win_knorm,
              win_sink, na_qnorm, na_knorm, na_rpb, gdn_conv, gdn_a_log, gdn_dt_bias, gdn_norm, w_branch,
              w_merge, w_out, w_router, router_bias, w_e_gate, w_e_up, w_e_down):
    cos, sin = axial_rope_tables(x.shape[1], HEAD_DIM)
    xl, xc = x, ctx
    for layer in range(DEPTH):
        p = {
            'w_mod': w_mod[layer], 'b_mod': b_mod[layer], 'norm1': norm1[layer], 'norm2': norm2[layer],
            'w_in': w_in[layer], 'ret_decay': ret_decay[layer], 'ret_gn': ret_gn[layer],
            'win_qnorm': win_qnorm[layer], 'win_knorm': win_knorm[layer], 'win_sink': win_sink[layer],
            'na_qnorm': na_qnorm[layer], 'na_knorm': na_knorm[layer], 'na_rpb': na_rpb[layer],
            'gdn_conv': gdn_conv[layer], 'gdn_a_log': gdn_a_log[layer], 'gdn_dt_bias': gdn_dt_bias[layer],
            'gdn_norm': gdn_norm[layer], 'w_branch': w_branch[layer], 'w_merge': w_merge[layer],
            'w_out': w_out[layer], 'w_e_gate': w_e_gate[layer], 'w_e_up': w_e_up[layer],
            'w_e_down': w_e_down[layer],
        }
        xl, xc = trunk_layer(xl, xc, c, c_ctx, p, w_router, router_bias, cos, sin, layer < DEPTH - 1)
    return xl
```

```python
import functools

import numpy as np
import jax
import jax.numpy as jnp
from jax import lax
from jax.experimental import pallas as pl
from jax.experimental.pallas import tpu as pltpu

F32 = jnp.float32
BF16 = jnp.bfloat16
EPS = 1e-6
NEG_INF = -1e30
D_MODEL = 1024
GRID_W = 64
HEAD_DIM = 64
ROPE_THETA = 10000.0
RET_HEADS, RET_DK, RET_DV, RET_CHUNK = 4, 64, 128, 128
WIN_HEADS, WIN_KV_HEADS, WINDOW, WIN_BLOCK = 8, 2, 128, 128
NA_HEADS, NA_KH, NA_KW, NA_QCOLS = 8, 8, 16, 16
NA_BAND = NA_QCOLS + NA_KW
GDN_HEADS, GDN_DK, GDN_DV, GDN_CHUNK, SHORT_CONV = 4, 128, 128, 64, 3
GDN_QKV = 2 * GDN_HEADS * GDN_DK + GDN_HEADS * GDN_DV
N_BRANCH, BRANCH_W = 4, 512
N_EXPERTS, N_GROUPS, TOP_K, D_EXPERT = 32, 8, 2, 512
EXPERTS_PER_GROUP = N_EXPERTS // N_GROUPS

LANES = 128
VMEM_LIMIT = 56 * 1024 * 1024
MOE_ROWS = 256

W_RET = 2 * RET_HEADS * RET_DK + 2 * RET_HEADS * RET_DV
W_WIN = (WIN_HEADS + 2 * WIN_KV_HEADS) * HEAD_DIM
W_NA = 3 * NA_HEADS * HEAD_DIM
W_GDN = GDN_QKV + GDN_HEADS * GDN_DV
W_AB = LANES
SECTION_WIDTHS = (W_RET, W_WIN, W_NA, W_GDN, W_AB)
W_ALL = sum(SECTION_WIDTHS)


def _cparams(n_axes):
    return pltpu.CompilerParams(dimension_semantics=("arbitrary",) * n_axes, vmem_limit_bytes=VMEM_LIMIT)


def _mod_kernel(c_ref, w_ref, b_ref, o_ref):
    c = c_ref[...]
    a = (c * jax.nn.sigmoid(c)).astype(BF16)
    o_ref[0] = jnp.dot(a, w_ref[0].astype(BF16), preferred_element_type=F32) + b_ref[0]


def _modulation(cc, w_mod, b_mod):
    depth, d, n = w_mod.shape
    r = cc.shape[0]
    tn = 1536
    return pl.pallas_call(
        _mod_kernel,
        grid=(depth, n // tn),
        in_specs=[pl.BlockSpec((r, d), lambda l, j: (0, 0)),
                  pl.BlockSpec((1, d, tn), lambda l, j: (l, 0, j)),
                  pl.BlockSpec((1, 1, tn), lambda l, j: (l, 0, j))],
        out_specs=pl.BlockSpec((1, r, tn), lambda l, j: (l, 0, j)),
        out_shape=jax.ShapeDtypeStruct((depth, r, n), F32),
        compiler_params=_cparams(2),
        name="modulation",
    )(cc, w_mod, b_mod.reshape(depth, 1, n))


def _inproj_kernel(x_ref, mod_ref, gain_ref, w_ref, h_ref, *z_refs):
    x = x_ref[0]
    ms = jnp.mean(x * x, axis=-1, keepdims=True)
    shift = mod_ref[0, 0:1, :]
    scale = mod_ref[0, 1:2, :]
    h = x * lax.rsqrt(ms + EPS) * gain_ref[...] * (1.0 + scale) + shift
    hb = h.astype(BF16)
    h_ref[0] = hb
    off = 0
    for ref in z_refs:
        width = ref.shape[-1]
        for c0 in range(0, width, 512):
            c1 = min(c0 + 512, width)
            z = jnp.dot(hb, w_ref[:, off + c0:off + c1], preferred_element_type=F32)
            ref[0, :, c0:c1] = z.astype(ref.dtype)
        off += width


def _inproj(x, mod, mod_is_shared, gain, w_all, tm):
    b, l, d = x.shape
    mod_map = (lambda i, j: (0, 0, 0)) if mod_is_shared else (lambda i, j: (i, 0, 0))
    dtypes = (BF16, BF16, BF16, BF16, F32)
    out_shape = [jax.ShapeDtypeStruct((b, l, d), BF16)]
    out_specs = [pl.BlockSpec((1, tm, d), lambda i, j: (i, j, 0))]
    for w, dt in zip(SECTION_WIDTHS, dtypes):
        out_shape.append(jax.ShapeDtypeStruct((b, l, w), dt))
        out_specs.append(pl.BlockSpec((1, tm, w), lambda i, j: (i, j, 0)))
    return pl.pallas_call(
        _inproj_kernel,
        grid=(b, l // tm),
        in_specs=[pl.BlockSpec((1, tm, d), lambda i, j: (i, j, 0)),
                  pl.BlockSpec((1, 6, d), mod_map),
                  pl.BlockSpec((1, d), lambda i, j: (0, 0)),
                  pl.BlockSpec((d, W_ALL), lambda i, j: (0, 0))],
        out_specs=out_specs,
        out_shape=out_shape,
        compiler_params=_cparams(2),
        name="inproj",
    )(x, mod, gain.reshape(1, d), w_all)


def _pack_w_in(w_in):
    d = w_in.shape[0]
    pad = jnp.zeros((d, W_ALL - w_in.shape[1]), w_in.dtype)
    return jnp.concatenate([w_in, pad], axis=1).astype(BF16)


def _merge_kernel(x_ref, h_ref, y0_ref, y1_ref, y2_ref, y3_ref, mod_ref, gain_ref, wm_ref, wb_ref, wo_ref,
                  wrh_ref, wrl_ref, xo_ref, h2_ref, sc_ref):
    d = x_ref.shape[-1]
    h = h_ref[0]
    acc = jnp.zeros(x_ref.shape[1:], F32)
    for i, y_ref in enumerate((y0_ref, y1_ref, y2_ref, y3_ref)):
        gate = jax.nn.sigmoid(jnp.dot(h, wm_ref[:, i * d:(i + 1) * d], preferred_element_type=F32))
        acc = acc + gate * jnp.dot(y_ref[0], wb_ref[i], preferred_element_type=F32)
    m = jnp.dot(acc.astype(BF16), wo_ref[...], preferred_element_type=F32)
    xn = x_ref[0] + mod_ref[0, 2:3, :] * m
    xo_ref[0] = xn
    ms = jnp.mean(xn * xn, axis=-1, keepdims=True)
    h2 = xn * lax.rsqrt(ms + EPS) * gain_ref[...] * (1.0 + mod_ref[0, 4:5, :]) + mod_ref[0, 3:4, :]
    hi = h2.astype(BF16)
    h2_ref[0] = hi
    lo = (h2 - hi.astype(F32)).astype(BF16)
    logits = (jnp.dot(hi, wrh_ref[...], preferred_element_type=F32)
              + jnp.dot(lo, wrh_ref[...], preferred_element_type=F32)
              + jnp.dot(hi, wrl_ref[...], preferred_element_type=F32))
    sc_ref[0] = jax.nn.sigmoid(logits)


def _merge(x, h, ys, mod, mod_is_shared, gain2, wm, wb, wo, wr_hi, wr_lo, tm):
    b, l, d = x.shape
    mod_map = (lambda i, j: (0, 0, 0)) if mod_is_shared else (lambda i, j: (i, 0, 0))
    tok = lambda w: pl.BlockSpec((1, tm, w), lambda i, j: (i, j, 0))
    full2 = lambda s: pl.BlockSpec(s, lambda i, j: (0, 0))
    return pl.pallas_call(
        _merge_kernel,
        grid=(b, l // tm),
        in_specs=[tok(d), tok(d)] + [tok(BRANCH_W)] * 4 + [
            pl.BlockSpec((1, 6, d), mod_map), full2((1, d)), full2(wm.shape),
            pl.BlockSpec(wb.shape, lambda i, j: (0, 0, 0)), full2(wo.shape), full2(wr_hi.shape), full2(wr_lo.shape)],
        out_specs=[tok(d), tok(d), tok(LANES)],
        out_shape=[jax.ShapeDtypeStruct((b, l, d), F32), jax.ShapeDtypeStruct((b, l, d), BF16),
                   jax.ShapeDtypeStruct((b, l, LANES), F32)],
        compiler_params=_cparams(2),
        name="merge",
    )(x, h, *ys, mod, gain2.reshape(1, d), wm, wb, wo, wr_hi, wr_lo)


def _moe_kernel(be_ref, nu_ref, xb_ref, rw_ref, wg_ref, wu_ref, wd_ref, o_ref, wg_s, wu_s, wd_s):
    i = pl.program_id(0)
    e = be_ref[i]
    prev = be_ref[jnp.maximum(i - 1, 0)]
    used = i < nu_ref[0]

    @pl.when(used & ((i == 0) | (e != prev)))
    def _():
        wg_s[...] = wg_ref[0].astype(BF16)
        wu_s[...] = wu_ref[0].astype(BF16)
        wd_s[...] = wd_ref[0].astype(BF16)

    @pl.when(used)
    def _():
        x = xb_ref[...]
        g = jnp.dot(x, wg_s[...], preferred_element_type=F32)
        u = jnp.dot(x, wu_s[...], preferred_element_type=F32)
        a = (g * jax.nn.sigmoid(g) * u).astype(BF16)
        y = jnp.dot(a, wd_s[...], preferred_element_type=F32)
        o_ref[...] = (y * rw_ref[...]).astype(o_ref.dtype)

    @pl.when(jnp.logical_not(used))
    def _():
        o_ref[...] = jnp.zeros_like(o_ref)


def _moe_ffn(block_e, n_used, xb, rw, w_gate, w_up, w_down):
    size, d = xb.shape
    n_blocks = size // MOE_ROWS
    de = w_gate.shape[-1]
    grid_spec = pltpu.PrefetchScalarGridSpec(
        num_scalar_prefetch=2,
        grid=(n_blocks,),
        in_specs=[pl.BlockSpec((MOE_ROWS, d), lambda i, be, nu: (i, 0)),
                  pl.BlockSpec((MOE_ROWS, 1), lambda i, be, nu: (i, 0)),
                  pl.BlockSpec((1, d, de), lambda i, be, nu: (be[i], 0, 0)),
                  pl.BlockSpec((1, d, de), lambda i, be, nu: (be[i], 0, 0)),
                  pl.BlockSpec((1, de, d), lambda i, be, nu: (be[i], 0, 0))],
        out_specs=pl.BlockSpec((MOE_ROWS, d), lambda i, be, nu: (i, 0)),
        scratch_shapes=[pltpu.VMEM((d, de), BF16), pltpu.VMEM((d, de), BF16), pltpu.VMEM((de, d), BF16)],
    )
    return pl.pallas_call(
        _moe_kernel,
        grid_spec=grid_spec,
        out_shape=jax.ShapeDtypeStruct((size, d), BF16),
        compiler_params=_cparams(1),
        name="moe_ffn",
    )(block_e, n_used, xb, rw, w_gate, w_up, w_down)


def _moe(h2, scores, router_bias, w_gate, w_up, w_down):
    t, d = h2.shape
    sel = (scores + router_bias.astype(F32)).reshape(t, N_GROUPS, EXPERTS_PER_GROUP)
    grp_score = jnp.sum(lax.top_k(sel, 2)[0], axis=-1)
    g_idx = jnp.argmax(grp_score, axis=-1)
    in_grp = jnp.take_along_axis(sel, g_idx[:, None, None], axis=1)[:, 0]
    _, local = lax.top_k(in_grp, TOP_K)
    e_idx = g_idx[:, None] * EXPERTS_PER_GROUP + local
    wts = jnp.take_along_axis(scores, e_idx, axis=1)
    wts = wts / jnp.sum(wts, axis=-1, keepdims=True)
    a = t * TOP_K
    flat_e = e_idx.reshape(-1).astype(jnp.int32)
    flat_tok = jnp.repeat(jnp.arange(t, dtype=jnp.int32), TOP_K)
    flat_w = wts.reshape(-1)
    order = jnp.argsort(flat_e)
    se, stok, sw = flat_e[order], flat_tok[order], flat_w[order]
    counts = jnp.bincount(flat_e, length=N_EXPERTS)
    starts = jnp.cumsum(counts) - counts
    padded = (counts + MOE_ROWS - 1) // MOE_ROWS * MOE_ROWS
    ends = jnp.cumsum(padded)
    pstarts = ends - padded
    dest = pstarts[se] + jnp.arange(a) - starts[se]
    n_blocks = -(-a // MOE_ROWS) + N_EXPERTS
    size = n_blocks * MOE_ROWS
    tok_buf = jnp.full((size,), t, jnp.int32).at[dest].set(stok)
    w_buf = jnp.zeros((size,), F32).at[dest].set(sw)
    block_e = jnp.minimum(jnp.searchsorted(ends, jnp.arange(n_blocks) * MOE_ROWS, side='right'),
                          N_EXPERTS - 1).astype(jnp.int32)
    n_used = (ends[-1] // MOE_ROWS).astype(jnp.int32).reshape(1)
    h_pad = jnp.concatenate([h2, jnp.zeros((1, d), h2.dtype)], axis=0)
    xb = h_pad[tok_buf]
    yb = _moe_ffn(block_e, n_used, xb, w_buf[:, None], w_gate, w_up, w_down)
    out = jnp.zeros((t + 1, d), F32).at[tok_buf].add(yb.astype(F32))
    return out[:t]


def rmsnorm(x, gain):
    xf = x.astype(F32)
    y = xf * lax.rsqrt(jnp.mean(xf * xf, axis=-1, keepdims=True) + EPS)
    return (y * gain.astype(F32)).astype(x.dtype)


def l2norm(x):
    return x * lax.rsqrt(jnp.sum(x * x, axis=-1, keepdims=True) + EPS)


def to_heads(z, n_heads):
    b, l, _ = z.shape
    return z.reshape(b, l, n_heads, -1).transpose(0, 2, 1, 3)


def from_heads(z):
    b, h, l, d = z.shape
    return z.transpose(0, 2, 1, 3).reshape(b, l, h * d)


def flip_seq(a):
    return jnp.flip(a, axis=2)


def axial_rope_tables(n_tokens, dim):
    t = jnp.arange(n_tokens)
    row = (t // GRID_W).astype(F32)
    col = (t % GRID_W).astype(F32)
    n_freq = dim // 4
    inv = ROPE_THETA ** (-jnp.arange(n_freq, dtype=F32) / n_freq)
    ang = jnp.concatenate([row[:, None] * inv, col[:, None] * inv], axis=-1)
    return jnp.cos(ang), jnp.sin(ang)


def apply_rope(x, cos, sin):
    xf = x.astype(F32)
    x1, x2 = jnp.split(xf, 2, axis=-1)
    return jnp.concatenate([x1 * cos - x2 * sin, x1 * sin + x2 * cos], axis=-1).astype(x.dtype)


def sink_softmax(s, sink):
    m = jnp.maximum(jnp.max(s, axis=-1, keepdims=True), sink)
    p = jnp.exp(s - m)
    return p / (jnp.sum(p, axis=-1, keepdims=True) + jnp.exp(sink - m))


def context_attention(q, k, v, sink):
    b, hq, lc, d = q.shape
    g = k.shape[1]
    qg = q.reshape(b, g, hq // g, lc, d)
    s = jnp.einsum('bgrqd,bgkd->bgrqk', qg, k).astype(F32) * d ** -0.5
    if sink is None:
        p = jax.nn.softmax(s, axis=-1)
    else:
        p = sink_softmax(s, sink.astype(F32).reshape(g, -1)[None, :, :, None, None])
    o = jnp.einsum('bgrqk,bgkd->bgrqd', p.astype(v.dtype), v)
    return o.reshape(b, hq, lc, d)


def retention_chunked(q, k, v, log_gamma, s0):
    b, h, l, dk = q.shape
    dv = v.shape[-1]
    c = RET_CHUNK
    n = l // c
    qc = q.reshape(b, h, n, c, dk)
    kc = k.reshape(b, h, n, c, dk)
    vc = v.reshape(b, h, n, c, dv)
    pos = jnp.arange(c, dtype=F32)
    diff = pos[:, None] - pos[None, :]
    dmask = jnp.where(diff >= 0, jnp.exp(log_gamma[:, None, None] * jnp.maximum(diff, 0.0)), 0.0)
    q_decay = jnp.exp(log_gamma[:, None] * (pos + 1.0))
    k_decay = jnp.exp(log_gamma[:, None] * (c - 1.0 - pos))
    chunk_decay = jnp.exp(log_gamma * c)
    scores = jnp.einsum('bhncd,bhnsd->bhncs', qc, kc) * dmask[None, :, None]
    o_intra = jnp.einsum('bhncs,bhnse->bhnce', scores, vc)
    kv = jnp.einsum('bhncd,bhnce->bhnde', kc * k_decay[None, :, None, :, None], vc)

    def step(s, kv_n):
        return chunk_decay[None, :, None, None] * s + kv_n, s

    s_fin, s_prev = lax.scan(step, s0, jnp.moveaxis(kv, 2, 0))
    s_prev = jnp.moveaxis(s_prev, 0, 2)
    o_inter = jnp.einsum('bhncd,bhnde->bhnce', qc * q_decay[None, :, None, :, None], s_prev)
    return (o_intra + o_inter).reshape(b, h, l, dv), s_fin


def retention_state(k, v, log_gamma):
    l = k.shape[2]
    w = jnp.exp(log_gamma[:, None] * (l - 1.0 - jnp.arange(l, dtype=F32)))
    return jnp.einsum('bhld,bhle->bhde', k * w[None, :, :, None], v)


def retention_output(o, gate, gain):
    mu = jnp.mean(o, axis=-1, keepdims=True)
    var = jnp.mean(jnp.square(o - mu), axis=-1, keepdims=True)
    y = from_heads((o - mu) * lax.rsqrt(var + EPS)) * gain.astype(F32)
    return (y * jax.nn.silu(gate.astype(F32))).astype(gate.dtype)


def retention_branch(zl, zc, decay_logit, gn_gain, cos, sin, ctx_out):
    log_gamma = jax.nn.log_sigmoid(decay_logit.astype(F32))

    def qkv(z):
        q = to_heads(z['ret_q'], RET_HEADS).astype(F32)
        k = to_heads(z['ret_k'], RET_HEADS).astype(F32) * RET_DK ** -0.5
        v = to_heads(z['ret_v'], RET_HEADS).astype(F32)
        return q, k, v

    qc, kc, vc = qkv(zc)
    ql, kl, vl = qkv(zl)
    ql = apply_rope(ql, cos, sin)
    kl = apply_rope(kl, cos, sin)
    s0 = jnp.zeros((ql.shape[0], RET_HEADS, RET_DK, RET_DV), F32)
    yc = None
    if ctx_out:
        oc_f, sc_f = retention_chunked(qc, kc, vc, log_gamma[0], s0)
        oc_b, sc_b = retention_chunked(flip_seq(qc), flip_seq(kc), flip_seq(vc), log_gamma[1], s0)
        yc = retention_output(oc_f + flip_seq(oc_b), zc['ret_g'], gn_gain)
    else:
        sc_f = retention_state(kc, vc, log_gamma[0])
        sc_b = retention_state(flip_seq(kc), flip_seq(vc), log_gamma[1])
    ol_f, _ = retention_chunked(ql, kl, vl, log_gamma[0], sc_f)
    ol_b, _ = retention_chunked(flip_seq(ql), flip_seq(kl), flip_seq(vl), log_gamma[1], sc_b)
    yl = retention_output(ol_f + flip_seq(ol_b), zl['ret_g'], gn_gain)
    return yc, yl


def window_attention(q, k, v, kc, vc, sink):
    b, hq, l, d = q.shape
    g = k.shape[1]
    r = hq // g
    nb = l // WIN_BLOCK
    span = WIN_BLOCK + 2 * WINDOW
    qg = (q * d ** -0.5).reshape(b, g, r, l, d)
    kp = jnp.pad(k, ((0, 0), (0, 0), (WINDOW, WINDOW), (0, 0)))
    vp = jnp.pad(v, ((0, 0), (0, 0), (WINDOW, WINDOW), (0, 0)))
    sink_b = sink.astype(F32).reshape(g, r)[None, :, :, None, None]

    def block(n):
        s0 = n * WIN_BLOCK
        qb = lax.dynamic_slice_in_dim(qg, s0, WIN_BLOCK, axis=3)
        kb = lax.dynamic_slice_in_dim(kp, s0, span, axis=2)
        vb = lax.dynamic_slice_in_dim(vp, s0, span, axis=2)
        qpos = s0 + jnp.arange(WIN_BLOCK)
        kpos = s0 - WINDOW + jnp.arange(span)
        ok = (kpos[None, :] >= 0) & (kpos[None, :] < l) & (jnp.abs(qpos[:, None] - kpos[None, :]) <= WINDOW)
        s_loc = jnp.where(ok, jnp.einsum('bgrqd,bgkd->bgrqk', qb, kb).astype(F32), NEG_INF)
        s_ctx = jnp.einsum('bgrqd,bgkd->bgrqk', qb, kc).astype(F32)
        p = sink_softmax(jnp.concatenate([s_loc, s_ctx], axis=-1), sink_b).astype(v.dtype)
        return (jnp.einsum('bgrqk,bgkd->bgrqd', p[..., :span], vb)
                + jnp.einsum('bgrqk,bgkd->bgrqd', p[..., span:], vc))

    out = lax.map(block, jnp.arange(nb))
    return jnp.moveaxis(out, 0, 3).reshape(b, hq, l, d)


def window_branch(zl, zc, q_gain, k_gain, sink, cos, sin, ctx_out):
    def qkv(z):
        q = rmsnorm(to_heads(z['win_q'], WIN_HEADS), q_gain)
        k = rmsnorm(to_heads(z['win_k'], WIN_KV_HEADS), k_gain)
        v = to_heads(z['win_v'], WIN_KV_HEADS)
        return q, k, v

    qc, kc, vc = qkv(zc)
    ql, kl, vl = qkv(zl)
    ql = apply_rope(ql, cos, sin)
    kl = apply_rope(kl, cos, sin)
    yl = from_heads(window_attention(ql, kl, vl, kc, vc, sink))
    yc = from_heads(context_attention(qc, kc, vc, sink)) if ctx_out else None
    return yc, yl


def neighborhood_attention(q, k, v, kc, vc, rpb):
    b, h, l, d = q.shape
    rows = l // GRID_W
    kh = min(NA_KH, rows)
    n_cb = GRID_W // NA_QCOLS
    qcol = np.arange(GRID_W).reshape(n_cb, NA_QCOLS)
    band0 = np.clip(qcol[:, 0] - NA_KW // 2, 0, GRID_W - NA_BAND)
    kcol = band0[:, None] + np.arange(NA_BAND)
    cstart = np.clip(qcol - NA_KW // 2, 0, GRID_W - NA_KW)
    col_ok = (kcol[:, None, :] >= cstart[:, :, None]) & (kcol[:, None, :] < cstart[:, :, None] + NA_KW)
    col_off = np.clip(kcol[:, None, :] - qcol[:, :, None] + NA_KW - 1, 0, 2 * NA_KW - 2)
    rstart = np.clip(np.arange(rows) - kh // 2, 0, rows - kh)
    row_off = rstart[:, None] + np.arange(kh)[None, :] - np.arange(rows)[:, None] + NA_KH - 1
    bias = rpb.astype(F32)[:, row_off[:, None, None, :, None], col_off[None, :, :, None, :]]
    bias = jnp.where(col_ok[None, None, :, :, None, :], bias, NEG_INF)
    bias = jnp.moveaxis(bias, 1, 0).reshape(rows, h, n_cb, NA_QCOLS, kh * NA_BAND)
    qg = (q * d ** -0.5).reshape(b, h, rows, n_cb, NA_QCOLS, d)
    kg = k.reshape(b, h, rows, GRID_W, d)
    vg = v.reshape(b, h, rows, GRID_W, d)

    def row_step(args):
        r, rs, bias_r = args
        qr = lax.dynamic_index_in_dim(qg, r, axis=2, keepdims=False)
        kr = lax.dynamic_slice_in_dim(kg, rs, kh, axis=2)[:, :, :, kcol]
        vr = lax.dynamic_slice_in_dim(vg, rs, kh, axis=2)[:, :, :, kcol]
        s_loc = jnp.einsum('bhcqd,bhicjd->bhcqij', qr, kr).astype(F32)
        s_loc = s_loc.reshape(b, h, n_cb, NA_QCOLS, kh * NA_BAND) + bias_r
        s_ctx = jnp.einsum('bhcqd,bhkd->bhcqk', qr, kc).astype(F32)
        p = jax.nn.softmax(jnp.concatenate([s_loc, s_ctx], axis=-1), axis=-1).astype(v.dtype)
        p_loc = p[..., :kh * NA_BAND].reshape(b, h, n_cb, NA_QCOLS, kh, NA_BAND)
        o = (jnp.einsum('bhcqij,bhicjd->bhcqd', p_loc, vr)
             + jnp.einsum('bhcqk,bhkd->bhcqd', p[..., kh * NA_BAND:], vc))
        return o.reshape(b, h, GRID_W, d)

    out = lax.map(row_step, (jnp.arange(rows, dtype=jnp.int32), jnp.asarray(rstart, jnp.int32), bias))
    return jnp.moveaxis(out, 0, 2).reshape(b, h, l, d)


def neighborhood_branch(zl, zc, q_gain, k_gain, rpb, ctx_out):
    def qkv(z):
        q = rmsnorm(to_heads(z['na_q'], NA_HEADS), q_gain)
        k = rmsnorm(to_heads(z['na_k'], NA_HEADS), k_gain)
        v = to_heads(z['na_v'], NA_HEADS)
        return q, k, v

    qc, kc, vc = qkv(zc)
    ql, kl, vl = qkv(zl)
    yl = from_heads(neighborhood_attention(ql, kl, vl, kc, vc, rpb))
    yc = from_heads(context_attention(qc, kc, vc, None)) if ctx_out else None
    return yc, yl


def short_conv(z, w):
    taps, l = w.shape[0], z.shape[1]
    pad = taps // 2
    zp = jnp.pad(z, ((0, 0), (pad, pad), (0, 0)))
    return sum(zp[:, j:j + l] * w[j] for j in range(taps))


def gated_delta_chunked(q, k, v, g, beta, s0, with_out):
    b, h, l, dk = k.shape
    dv = v.shape[-1]
    c = GDN_CHUNK
    n = l // c
    kc = k.reshape(b, h, n, c, dk)
    vc = v.reshape(b, h, n, c, dv)
    gcum = jnp.cumsum(g.reshape(b, h, n, c), axis=-1)
    bc = beta.reshape(b, h, n, c)
    incl = jnp.asarray(np.tril(np.ones((c, c), bool)))
    strict = jnp.asarray(np.tril(np.ones((c, c), bool), -1))
    gdiff = gcum[..., :, None] - gcum[..., None, :]
    decay = jnp.where(incl, jnp.exp(jnp.where(incl, gdiff, 0.0)), 0.0)
    kb = kc * bc[..., None]
    a_low = jnp.where(strict, jnp.einsum('bhnid,bhnjd->bhnij', kb, kc) * decay, 0.0)
    rhs = jnp.concatenate([vc * bc[..., None], kb * jnp.exp(gcum)[..., None]], axis=-1)
    sol = lax.linalg.triangular_solve(a_low + jnp.eye(c, dtype=F32), rhs, left_side=True, lower=True)
    u, w = sol[..., :dv], sol[..., dv:]
    k_end = kc * jnp.exp(gcum[..., -1:] - gcum)[..., None]
    c_decay = jnp.exp(gcum[..., -1])

    def mv(a):
        return jnp.moveaxis(a, 2, 0)

    if with_out:
        qc = q.reshape(b, h, n, c, dk)
        q_dec = qc * jnp.exp(gcum)[..., None]
        a_qk = jnp.where(incl, jnp.einsum('bhnid,bhnjd->bhnij', qc, kc) * decay, 0.0)

        def step(s, xs):
            qd, ke, u_n, w_n, aq, cd = xs
            v_new = u_n - jnp.einsum('bhcd,bhde->bhce', w_n, s)
            o = jnp.einsum('bhcd,bhde->bhce', qd, s) + jnp.einsum('bhcs,bhse->bhce', aq, v_new)
            s = s * cd[..., None, None] + jnp.einsum('bhcd,bhce->bhde', ke, v_new)
            return s, o

        s_fin, o = lax.scan(step, s0, (mv(q_dec), mv(k_end), mv(u), mv(w), mv(a_qk), mv(c_decay)))
        return jnp.moveaxis(o, 0, 2).reshape(b, h, l, dv), s_fin

    def step_state(s, xs):
        ke, u_n, w_n, cd = xs
        v_new = u_n - jnp.einsum('bhcd,bhde->bhce', w_n, s)
        return s * cd[..., None, None] + jnp.einsum('bhcd,bhce->bhde', ke, v_new), None

    s_fin, _ = lax.scan(step_state, s0, (mv(k_end), mv(u), mv(w), mv(c_decay)))
    return None, s_fin


def gdn_output(o, gate, gain):
    y = o * lax.rsqrt(jnp.mean(o * o, axis=-1, keepdims=True) + EPS) * gain.astype(F32)
    return (from_heads(y) * jax.nn.silu(gate.astype(F32))).astype(gate.dtype)


def gdn_branch(zl, zc, conv_w, a_log, dt_bias, norm_gain, ctx_out):
    def prep(z):
        b, l, _ = z['gdn_q'].shape
        qkv = jnp.concatenate([z['gdn_q'], z['gdn_k'], z['gdn_v']], axis=-1)
        qkv = jax.nn.silu(short_conv(qkv, conv_w)).astype(F32)
        q, k, v = jnp.split(qkv, [GDN_HEADS * GDN_DK, 2 * GDN_HEADS * GDN_DK], axis=-1)
        q = l2norm(to_heads(q, GDN_HEADS)) * GDN_DK ** -0.5
        k = l2norm(to_heads(k, GDN_HEADS))
        v = to_heads(v, GDN_HEADS)
        a = z['gdn_a'].astype(F32).reshape(b, l, 2, GDN_HEADS)
        bb = z['gdn_b'].astype(F32).reshape(b, l, 2, GDN_HEADS)
        g = -jnp.exp(a_log.astype(F32)) * jax.nn.softplus(a + dt_bias.astype(F32))
        beta = jax.nn.sigmoid(bb)
        return q, k, v, jnp.transpose(g, (2, 0, 3, 1)), jnp.transpose(beta, (2, 0, 3, 1))

    qc, kc, vc, gc, bc = prep(zc)
    ql, kl, vl, gl, bl = prep(zl)
    s0 = jnp.zeros((ql.shape[0], GDN_HEADS, GDN_DK, GDN_DV), F32)
    oc_f, sc_f = gated_delta_chunked(qc, kc, vc, gc[0], bc[0], s0, ctx_out)
    oc_b, sc_b = gated_delta_chunked(flip_seq(qc), flip_seq(kc), flip_seq(vc),
                                     flip_seq(gc[1]), flip_seq(bc[1]), s0, ctx_out)
    ol_f, _ = gated_delta_chunked(ql, kl, vl, gl[0], bl[0], sc_f, True)
    ol_b, _ = gated_delta_chunked(flip_seq(ql), flip_seq(kl), flip_seq(vl),
                                  flip_seq(gl[1]), flip_seq(bl[1]), sc_b, True)
    yl = gdn_output(ol_f + flip_seq(ol_b), zl['gdn_g'], norm_gain)
    yc = gdn_output(oc_f + flip_seq(oc_b), zc['gdn_g'], norm_gain) if ctx_out else None
    return yc, yl


def _sections_to_dict(zr, zw, zn, zg, zab):
    f = lambda a: a.astype(F32)
    hq = RET_HEADS * RET_DK
    hv = RET_HEADS * RET_DV
    wq = WIN_HEADS * HEAD_DIM
    wk = WIN_KV_HEADS * HEAD_DIM
    nq = NA_HEADS * HEAD_DIM
    gq = GDN_HEADS * GDN_DK
    return {
        'ret_q': f(zr[..., :hq]), 'ret_k': f(zr[..., hq:2 * hq]), 'ret_v': f(zr[..., 2 * hq:2 * hq + hv]),
        'ret_g': f(zr[..., 2 * hq + hv:]),
        'win_q': f(zw[..., :wq]), 'win_k': f(zw[..., wq:wq + wk]), 'win_v': f(zw[..., wq + wk:]),
        'na_q': f(zn[..., :nq]), 'na_k': f(zn[..., nq:2 * nq]), 'na_v': f(zn[..., 2 * nq:]),
        'gdn_q': f(zg[..., :gq]), 'gdn_k': f(zg[..., gq:2 * gq]), 'gdn_v': f(zg[..., 2 * gq:3 * gq]),
        'gdn_g': f(zg[..., 3 * gq:]), 'gdn_a': zab[..., :2 * GDN_HEADS],
        'gdn_b': zab[..., 2 * GDN_HEADS:4 * GDN_HEADS],
    }


def kernel(x, c, ctx, c_ctx, w_mod, b_mod, norm1, norm2, w_in, ret_decay, ret_gn, win_qnorm, win_knorm, win_sink,
           na_qnorm, na_knorm, na_rpb, gdn_conv, gdn_a_log, gdn_dt_bias, gdn_norm, w_branch, w_merge, w_out,
           w_router, router_bias, w_e_gate, w_e_up, w_e_down):
    b, l, d = x.shape
    lc = ctx.shape[1]
    depth = w_mod.shape[0]
    cos, sin = axial_rope_tables(l, HEAD_DIM)

    n_rows = 16
    cc = jnp.zeros((n_rows, d), F32).at[:b].set(c).at[b].set(c_ctx)
    mod = _modulation(cc, w_mod, b_mod).reshape(depth, n_rows, 6, d)

    wr = jnp.zeros((d, LANES), F32).at[:, :N_EXPERTS].set(w_router)
    wr_hi = wr.astype(BF16)
    wr_lo = (wr - wr_hi.astype(F32)).astype(BF16)

    xl, xc = x, ctx
    for layer in range(depth):
        ctx_out = layer < depth - 1
        mod_l = mod[layer, :b]
        mod_c = mod[layer, b:b + 1]
        w_all = _pack_w_in(w_in[layer])
        hl, *zl_s = _inproj(xl, mod_l, False, norm1[layer], w_all, 256)
        hc, *zc_s = _inproj(xc, mod_c, True, norm1[layer], w_all, 256)
        zl = _sections_to_dict(*zl_s)
        zc = _sections_to_dict(*zc_s)
        ret_c, ret_l = retention_branch(zl, zc, ret_decay[layer], ret_gn[layer], cos, sin, ctx_out)
        win_c, win_l = window_branch(zl, zc, win_qnorm[layer], win_knorm[layer], win_sink[layer], cos, sin, ctx_out)
        na_c, na_l = neighborhood_branch(zl, zc, na_qnorm[layer], na_knorm[layer], na_rpb[layer], ctx_out)
        gdn_c, gdn_l = gdn_branch(zl, zc, gdn_conv[layer], gdn_a_log[layer], gdn_dt_bias[layer], gdn_norm[layer],
                                  ctx_out)
        wm = w_merge[layer].astype(BF16)
        wb = w_branch[layer].astype(BF16)
        wo = w_out[layer].astype(BF16)
        ys_l = [y.astype(BF16) for y in (ret_l, win_l, na_l, gdn_l)]
        xl, h2l, sc_l = _merge(xl, hl, ys_l, mod_l, False, norm2[layer], wm, wb, wo, wr_hi, wr_lo, 256)
        g2l = mod_l[:, 5][:, None, :]
        if ctx_out:
            ys_c = [y.astype(BF16) for y in (ret_c, win_c, na_c, gdn_c)]
            xc, h2c, sc_c = _merge(xc, hc, ys_c, mod_c, True, norm2[layer], wm, wb, wo, wr_hi, wr_lo, 256)
            g2c = mod_c[:, 5][:, None, :]
            tokens = jnp.concatenate([h2c.reshape(b * lc, d), h2l.reshape(b * l, d)], axis=0)
            scores = jnp.concatenate([sc_c.reshape(b * lc, LANES), sc_l.reshape(b * l, LANES)], axis=0)
            y = _moe(tokens, scores[:, :N_EXPERTS], router_bias, w_e_gate[layer], w_e_up[layer], w_e_down[layer])
            xc = xc + g2c * y[:b * lc].reshape(b, lc, d)
            xl = xl + g2l * y[b * lc:].reshape(b, l, d)
        else:
            y = _moe(h2l.reshape(b * l, d), sc_l.reshape(b * l, LANES)[:, :N_EXPERTS], router_bias,
                     w_e_gate[layer], w_e_up[layer], w_e_down[layer])
            xl = xl + g2l * y.reshape(b, l, d)
    return xl
```

```python
import functools

import numpy as np
import jax
import jax.numpy as jnp
from jax import lax
from jax.experimental import pallas as pl
from jax.experimental.pallas import tpu as pltpu

F32 = jnp.float32
BF16 = jnp.bfloat16
EPS = 1e-6
NEG_INF = -1e30
D_MODEL = 1024
GRID_W = 64
HEAD_DIM = 64
ROPE_THETA = 10000.0
RET_HEADS, RET_DK, RET_DV, RET_CHUNK = 4, 64, 128, 128
WIN_HEADS, WIN_KV_HEADS, WINDOW, WIN_BLOCK = 8, 2, 128, 128
NA_HEADS, NA_KH, NA_KW, NA_QCOLS = 8, 8, 16, 16
NA_BAND = NA_QCOLS + NA_KW
GDN_HEADS, GDN_DK, GDN_DV, GDN_CHUNK, SHORT_CONV = 4, 128, 128, 64, 3
GDN_QKV = 2 * GDN_HEADS * GDN_DK + GDN_HEADS * GDN_DV
N_BRANCH, BRANCH_W = 4, 512
N_EXPERTS, N_GROUPS, TOP_K, D_EXPERT = 32, 8, 2, 512
EXPERTS_PER_GROUP = N_EXPERTS // N_GROUPS

LANES = 128
VMEM_LIMIT = 56 * 1024 * 1024
MOE_ROWS = 256

W_RET = RET_HEADS * 4 * LANES
W_WIN = (WIN_HEADS + 2 * WIN_KV_HEADS) * HEAD_DIM
W_NA = 3 * NA_HEADS * HEAD_DIM
W_GDN = GDN_QKV + GDN_HEADS * GDN_DV
W_AB = LANES
SECTION_WIDTHS = (W_RET, W_WIN, W_NA, W_GDN, W_AB)
W_ALL = sum(SECTION_WIDTHS)


def _cparams(n_axes):
    return pltpu.CompilerParams(dimension_semantics=("arbitrary",) * n_axes, vmem_limit_bytes=VMEM_LIMIT)


def _mod_kernel(c_ref, w_ref, b_ref, o_ref):
    c = c_ref[...]
    a = (c * jax.nn.sigmoid(c)).astype(BF16)
    o_ref[0] = jnp.dot(a, w_ref[0].astype(BF16), preferred_element_type=F32) + b_ref[0]


def _modulation(cc, w_mod, b_mod):
    depth, d, n = w_mod.shape
    r = cc.shape[0]
    tn = 1536
    return pl.pallas_call(
        _mod_kernel,
        grid=(depth, n // tn),
        in_specs=[pl.BlockSpec((r, d), lambda l, j: (0, 0)),
                  pl.BlockSpec((1, d, tn), lambda l, j: (l, 0, j)),
                  pl.BlockSpec((1, 1, tn), lambda l, j: (l, 0, j))],
        out_specs=pl.BlockSpec((1, r, tn), lambda l, j: (l, 0, j)),
        out_shape=jax.ShapeDtypeStruct((depth, r, n), F32),
        compiler_params=_cparams(2),
        name="modulation",
    )(cc, w_mod, b_mod.reshape(depth, 1, n))


def _inproj_kernel(x_ref, mod_ref, gain_ref, w_ref, h_ref, *z_refs):
    x = x_ref[0]
    ms = jnp.mean(x * x, axis=-1, keepdims=True)
    shift = mod_ref[0, 0:1, :]
    scale = mod_ref[0, 1:2, :]
    h = x * lax.rsqrt(ms + EPS) * gain_ref[...] * (1.0 + scale) + shift
    hb = h.astype(BF16)
    h_ref[0] = hb
    off = 0
    for ref in z_refs:
        width = ref.shape[-1]
        for c0 in range(0, width, 512):
            c1 = min(c0 + 512, width)
            z = jnp.dot(hb, w_ref[:, off + c0:off + c1], preferred_element_type=F32)
            ref[0, :, c0:c1] = z.astype(ref.dtype)
        off += width


def _inproj(x, mod, mod_is_shared, gain, w_all, tm):
    b, l, d = x.shape
    mod_map = (lambda i, j: (0, 0, 0)) if mod_is_shared else (lambda i, j: (i, 0, 0))
    dtypes = (BF16, BF16, BF16, BF16, F32)
    out_shape = [jax.ShapeDtypeStruct((b, l, d), BF16)]
    out_specs = [pl.BlockSpec((1, tm, d), lambda i, j: (i, j, 0))]
    for w, dt in zip(SECTION_WIDTHS, dtypes):
        out_shape.append(jax.ShapeDtypeStruct((b, l, w), dt))
        out_specs.append(pl.BlockSpec((1, tm, w), lambda i, j: (i, j, 0)))
    return pl.pallas_call(
        _inproj_kernel,
        grid=(b, l // tm),
        in_specs=[pl.BlockSpec((1, tm, d), lambda i, j: (i, j, 0)),
                  pl.BlockSpec((1, 6, d), mod_map),
                  pl.BlockSpec((1, d), lambda i, j: (0, 0)),
                  pl.BlockSpec((d, W_ALL), lambda i, j: (0, 0))],
        out_specs=out_specs,
        out_shape=out_shape,
        compiler_params=_cparams(2),
        name="inproj",
    )(x, mod, gain.reshape(1, d), w_all)


def _pack_w_in(w_in):
    d = w_in.shape[0]
    hq, hv = RET_HEADS * RET_DK, RET_HEADS * RET_DV
    cols = []
    for h in range(RET_HEADS):
        q = w_in[:, h * RET_DK:(h + 1) * RET_DK]
        k = w_in[:, hq + h * RET_DK:hq + (h + 1) * RET_DK]
        cols += [q, q, k, k, w_in[:, 2 * hq + h * RET_DV:2 * hq + (h + 1) * RET_DV],
                 w_in[:, 2 * hq + hv + h * RET_DV:2 * hq + hv + (h + 1) * RET_DV]]
    rest = w_in[:, 2 * hq + 2 * hv:]
    pad = jnp.zeros((d, W_ALL - W_RET - rest.shape[1]), w_in.dtype)
    return jnp.concatenate(cols + [rest, pad], axis=1).astype(BF16)


def _merge_kernel(x_ref, h_ref, y0_ref, y1_ref, y2_ref, y3_ref, mod_ref, gain_ref, wm_ref, wb_ref, wo_ref,
                  wrh_ref, wrl_ref, xo_ref, h2_ref, sc_ref):
    d = x_ref.shape[-1]
    h = h_ref[0]
    acc = jnp.zeros(x_ref.shape[1:], F32)
    for i, y_ref in enumerate((y0_ref, y1_ref, y2_ref, y3_ref)):
        gate = jax.nn.sigmoid(jnp.dot(h, wm_ref[:, i * d:(i + 1) * d], preferred_element_type=F32))
        acc = acc + gate * jnp.dot(y_ref[0], wb_ref[i], preferred_element_type=F32)
    m = jnp.dot(acc.astype(BF16), wo_ref[...], preferred_element_type=F32)
    xn = x_ref[0] + mod_ref[0, 2:3, :] * m
    xo_ref[0] = xn
    ms = jnp.mean(xn * xn, axis=-1, keepdims=True)
    h2 = xn * lax.rsqrt(ms + EPS) * gain_ref[...] * (1.0 + mod_ref[0, 4:5, :]) + mod_ref[0, 3:4, :]
    hi = h2.astype(BF16)
    h2_ref[0] = hi
    lo = (h2 - hi.astype(F32)).astype(BF16)
    logits = (jnp.dot(hi, wrh_ref[...], preferred_element_type=F32)
              + jnp.dot(lo, wrh_ref[...], preferred_element_type=F32)
              + jnp.dot(hi, wrl_ref[...], preferred_element_type=F32))
    sc_ref[0] = jax.nn.sigmoid(logits)


def _merge(x, h, ys, mod, mod_is_shared, gain2, wm, wb, wo, wr_hi, wr_lo, tm):
    b, l, d = x.shape
    mod_map = (lambda i, j: (0, 0, 0)) if mod_is_shared else (lambda i, j: (i, 0, 0))
    tok = lambda w: pl.BlockSpec((1, tm, w), lambda i, j: (i, j, 0))
    full2 = lambda s: pl.BlockSpec(s, lambda i, j: (0, 0))
    return pl.pallas_call(
        _merge_kernel,
        grid=(b, l // tm),
        in_specs=[tok(d), tok(d)] + [tok(BRANCH_W)] * 4 + [
            pl.BlockSpec((1, 6, d), mod_map), full2((1, d)), full2(wm.shape),
            pl.BlockSpec(wb.shape, lambda i, j: (0, 0, 0)), full2(wo.shape), full2(wr_hi.shape), full2(wr_lo.shape)],
        out_specs=[tok(d), tok(d), tok(LANES)],
        out_shape=[jax.ShapeDtypeStruct((b, l, d), F32), jax.ShapeDtypeStruct((b, l, d), BF16),
                   jax.ShapeDtypeStruct((b, l, LANES), F32)],
        compiler_params=_cparams(2),
        name="merge",
    )(x, h, *ys, mod, gain2.reshape(1, d), wm, wb, wo, wr_hi, wr_lo)


def _moe_kernel(be_ref, nu_ref, xb_ref, rw_ref, wg_ref, wu_ref, wd_ref, o_ref, wg_s, wu_s, wd_s):
    i = pl.program_id(0)
    e = be_ref[i]
    prev = be_ref[jnp.maximum(i - 1, 0)]
    used = i < nu_ref[0]

    @pl.when(used & ((i == 0) | (e != prev)))
    def _():
        wg_s[...] = wg_ref[0].astype(BF16)
        wu_s[...] = wu_ref[0].astype(BF16)
        wd_s[...] = wd_ref[0].astype(BF16)

    @pl.when(used)
    def _():
        x = xb_ref[...]
        g = jnp.dot(x, wg_s[...], preferred_element_type=F32)
        u = jnp.dot(x, wu_s[...], preferred_element_type=F32)
        a = (g * jax.nn.sigmoid(g) * u).astype(BF16)
        y = jnp.dot(a, wd_s[...], preferred_element_type=F32)
        o_ref[...] = (y * rw_ref[...]).astype(o_ref.dtype)

    @pl.when(jnp.logical_not(used))
    def _():
        o_ref[...] = jnp.zeros_like(o_ref)


def _moe_ffn(block_e, n_used, xb, rw, w_gate, w_up, w_down):
    size, d = xb.shape
    n_blocks = size // MOE_ROWS
    de = w_gate.shape[-1]
    grid_spec = pltpu.PrefetchScalarGridSpec(
        num_scalar_prefetch=2,
        grid=(n_blocks,),
        in_specs=[pl.BlockSpec((MOE_ROWS, d), lambda i, be, nu: (i, 0)),
                  pl.BlockSpec((MOE_ROWS, 1), lambda i, be, nu: (i, 0)),
                  pl.BlockSpec((1, d, de), lambda i, be, nu: (be[i], 0, 0)),
                  pl.BlockSpec((1, d, de), lambda i, be, nu: (be[i], 0, 0)),
                  pl.BlockSpec((1, de, d), lambda i, be, nu: (be[i], 0, 0))],
        out_specs=pl.BlockSpec((MOE_ROWS, d), lambda i, be, nu: (i, 0)),
        scratch_shapes=[pltpu.VMEM((d, de), BF16), pltpu.VMEM((d, de), BF16), pltpu.VMEM((de, d), BF16)],
    )
    return pl.pallas_call(
        _moe_kernel,
        grid_spec=grid_spec,
        out_shape=jax.ShapeDtypeStruct((size, d), BF16),
        compiler_params=_cparams(1),
        name="moe_ffn",
    )(block_e, n_used, xb, rw, w_gate, w_up, w_down)


def _moe(h2, scores, router_bias, w_gate, w_up, w_down):
    t, d = h2.shape
    sel = (scores + router_bias.astype(F32)).reshape(t, N_GROUPS, EXPERTS_PER_GROUP)
    grp_score = jnp.sum(lax.top_k(sel, 2)[0], axis=-1)
    g_idx = jnp.argmax(grp_score, axis=-1)
    in_grp = jnp.take_along_axis(sel, g_idx[:, None, None], axis=1)[:, 0]
    _, local = lax.top_k(in_grp, TOP_K)
    e_idx = g_idx[:, None] * EXPERTS_PER_GROUP + local
    wts = jnp.take_along_axis(scores, e_idx, axis=1)
    wts = wts / jnp.sum(wts, axis=-1, keepdims=True)
    a = t * TOP_K
    flat_e = e_idx.reshape(-1).astype(jnp.int32)
    flat_w = wts.reshape(-1)
    onehot = (flat_e[:, None] == jnp.arange(N_EXPERTS, dtype=jnp.int32)[None, :]).astype(jnp.int32)
    csum = jnp.cumsum(onehot, axis=0)
    counts = csum[-1]
    rank = jnp.sum(csum * onehot, axis=1) - 1
    padded = (counts + MOE_ROWS - 1) // MOE_ROWS * MOE_ROWS
    ends = jnp.cumsum(padded)
    pstarts = ends - padded
    dest = jnp.sum(pstarts[None, :] * onehot, axis=1) + rank
    n_blocks = -(-a // MOE_ROWS) + N_EXPERTS
    size = n_blocks * MOE_ROWS
    flat_tok = jnp.arange(a, dtype=jnp.int32) // TOP_K
    tok_buf = jnp.full((size,), t, jnp.int32).at[dest].set(flat_tok)
    w_buf = jnp.zeros((size,), F32).at[dest].set(flat_w)
    block_e = jnp.minimum(jnp.searchsorted(ends, jnp.arange(n_blocks) * MOE_ROWS, side='right'),
                          N_EXPERTS - 1).astype(jnp.int32)
    n_used = (ends[-1] // MOE_ROWS).astype(jnp.int32).reshape(1)
    h_pad = jnp.concatenate([h2, jnp.zeros((1, d), h2.dtype)], axis=0)
    xb = h_pad[tok_buf]
    yb = _moe_ffn(block_e, n_used, xb, w_buf[:, None], w_gate, w_up, w_down)
    return jnp.sum(yb[dest].reshape(t, TOP_K, d).astype(F32), axis=1)


def _head_rms(x, gain, e_ref):
    ms = jnp.dot((x * x).astype(BF16), e_ref[...], preferred_element_type=F32) * (1.0 / HEAD_DIM)
    return x * lax.rsqrt(ms + EPS) * gain


def _rope(y, cos, sin):
    lane = lax.broadcasted_iota(jnp.int32, y.shape, 1)
    half = HEAD_DIM // 2
    rot = jnp.where((lane % HEAD_DIM) < half, pltpu.roll(y, LANES - half, 1), pltpu.roll(y, half, 1))
    return y * cos + rot * sin


def _attend(q, tiles, sink):
    scores = []
    m = sink
    for k, _, bias in tiles:
        s = lax.dot_general(q, k, (((1,), (1,)), ((), ())), preferred_element_type=F32)
        if bias is not None:
            s = s + bias
        scores.append(s)
        mt = jnp.max(s, axis=-1, keepdims=True)
        m = mt if m is None else jnp.maximum(m, mt)
    den = jnp.exp(sink - m) if sink is not None else jnp.zeros_like(m)
    o = None
    for s, (_, v, _) in zip(scores, tiles):
        p = jnp.exp(s - m)
        den = den + jnp.sum(p, axis=-1, keepdims=True)
        pv = jnp.dot(p.astype(BF16), v, preferred_element_type=F32)
        o = pv if o is None else o + pv
    return o * (1.0 / den)


def _lane_half(shape):
    return lax.broadcasted_iota(jnp.int32, shape, 1) // HEAD_DIM


def _win_group_queries(slabs, g, sink_ref):
    r = WIN_HEADS // WIN_KV_HEADS
    rows = slabs[0].shape[0]
    half = _lane_half(slabs[0].shape)
    parts, sinks = [], []
    for j in range(r):
        head = g * r + j
        slab = slabs[head // 2]
        if head % 2 != g:
            slab = pltpu.roll(slab, HEAD_DIM, 1)
        parts.append(jnp.where(half == g, slab, 0.0))
        sinks.append(jnp.full((rows, 1), sink_ref[head], F32))
    return jnp.concatenate(parts, axis=0).astype(BF16), jnp.concatenate(sinks, axis=0)


def _win_store(o_ref, o, g, rows):
    r = WIN_HEADS // WIN_KV_HEADS
    half = _lane_half((rows, LANES))
    for pair in range(r // 2):
        a = o[(2 * pair) * rows:(2 * pair + 1) * rows]
        b = o[(2 * pair + 1) * rows:(2 * pair + 2) * rows]
        if g == 0:
            b = pltpu.roll(b, HEAD_DIM, 1)
        else:
            a = pltpu.roll(a, HEAD_DIM, 1)
        s = (g * r + 2 * pair) // 2
        o_ref[0, :, s * LANES:(s + 1) * LANES] = jnp.where(half == 0, a, b).astype(o_ref.dtype)


def _win_kernel(sink_ref, q_ref, kv_ref, ckv_ref, cos_ref, sin_ref, qg_ref, kg_ref, e_ref, o_ref, ks_ref, kcs_ref):
    n = pl.program_id(1)
    nb = pl.num_programs(1)
    l = kv_ref.shape[1]
    blk = WIN_BLOCK
    prep_rows = 512

    @pl.when(n == 0)
    def _():
        def body(i, carry):
            r0 = pl.multiple_of(i * prep_rows, prep_rows)
            y = _head_rms(kv_ref[0, pl.ds(r0, prep_rows), 0:LANES].astype(F32), kg_ref[...], e_ref)
            ks_ref[pl.ds(r0, prep_rows), :] = _rope(y, cos_ref[pl.ds(r0, prep_rows), :],
                                                    sin_ref[pl.ds(r0, prep_rows), :]).astype(BF16)
            return carry
        lax.fori_loop(0, l // prep_rows, body, 0)
        kcs_ref[...] = _head_rms(ckv_ref[0, :, 0:LANES].astype(F32), kg_ref[...], e_ref).astype(BF16)

    r0 = pl.multiple_of(n * blk, blk)
    cos = cos_ref[pl.ds(r0, blk), :]
    sin = sin_ref[pl.ds(r0, blk), :]
    slabs = []
    for s in range(q_ref.shape[-1] // LANES):
        y = _head_rms(q_ref[0, :, s * LANES:(s + 1) * LANES].astype(F32), qg_ref[...], e_ref)
        slabs.append(_rope(y, cos, sin))

    r = WIN_HEADS // WIN_KV_HEADS
    qi = lax.broadcasted_iota(jnp.int32, (r * blk, blk), 0) % blk
    kj = lax.broadcasted_iota(jnp.int32, (r * blk, blk), 1)
    ok_prev = (kj >= qi) & (n > 0)
    ok_next = (kj <= qi) & (n < nb - 1)
    bias_prev = jnp.where(ok_prev, 0.0, NEG_INF)
    bias_next = jnp.where(ok_next, 0.0, NEG_INF)
    tiles = []
    for kb, bias in ((jnp.maximum(n - 1, 0), bias_prev), (n, None), (jnp.minimum(n + 1, nb - 1), bias_next)):
        k0 = pl.multiple_of(kb * blk, blk)
        tiles.append((ks_ref[pl.ds(k0, blk), :], kv_ref[0, pl.ds(k0, blk), LANES:2 * LANES], bias))
    tiles.append((kcs_ref[...], ckv_ref[0, :, LANES:2 * LANES], None))
    for g in range(WIN_KV_HEADS):
        q, sink = _win_group_queries(slabs, g, sink_ref)
        _win_store(o_ref, _attend(q, tiles, sink), g, blk)


def _win_ctx_kernel(sink_ref, q_ref, ckv_ref, qg_ref, kg_ref, e_ref, o_ref):
    kc = _head_rms(ckv_ref[0, :, 0:LANES].astype(F32), kg_ref[...], e_ref).astype(BF16)
    tiles = [(kc, ckv_ref[0, :, LANES:2 * LANES], None)]
    slabs = [_head_rms(q_ref[0, :, s * LANES:(s + 1) * LANES].astype(F32), qg_ref[...], e_ref)
             for s in range(q_ref.shape[-1] // LANES)]
    for g in range(WIN_KV_HEADS):
        q, sink = _win_group_queries(slabs, g, sink_ref)
        _win_store(o_ref, _attend(q, tiles, sink), g, q_ref.shape[1])


def _block_diag_ones(width):
    i = np.arange(width) // HEAD_DIM
    return jnp.asarray(i[:, None] == i[None, :], BF16)


def _window_branch(zw_l, zw_c, q_gain, k_gain, sink, cos2, sin2, ctx_out):
    b, l, _ = zw_l.shape
    lc = zw_c.shape[1]
    wq = WIN_HEADS * HEAD_DIM
    qg = (jnp.tile(q_gain.astype(F32), 2) * HEAD_DIM ** -0.5).reshape(1, LANES)
    kg = jnp.tile(k_gain.astype(F32), 2).reshape(1, LANES)
    e = _block_diag_ones(LANES)
    smem = pl.BlockSpec(memory_space=pltpu.SMEM)
    full = lambda s, nd: pl.BlockSpec(s, lambda *a: (0,) * nd)
    yl = pl.pallas_call(
        _win_kernel,
        grid=(b, l // WIN_BLOCK),
        in_specs=[smem,
                  pl.BlockSpec((1, WIN_BLOCK, wq), lambda i, n: (i, n, 0)),
                  pl.BlockSpec((1, l, 2 * LANES), lambda i, n: (i, 0, wq // (2 * LANES))),
                  pl.BlockSpec((1, lc, 2 * LANES), lambda i, n: (i, 0, wq // (2 * LANES))),
                  full((l, LANES), 2), full((l, LANES), 2), full((1, LANES), 2), full((1, LANES), 2),
                  full((LANES, LANES), 2)],
        out_specs=pl.BlockSpec((1, WIN_BLOCK, wq), lambda i, n: (i, n, 0)),
        out_shape=jax.ShapeDtypeStruct((b, l, wq), BF16),
        scratch_shapes=[pltpu.VMEM((l, LANES), BF16), pltpu.VMEM((lc, LANES), BF16)],
        compiler_params=_cparams(2),
        name="window_attn",
    )(sink.astype(F32), zw_l, zw_l, zw_c, cos2, sin2, qg, kg, e)
    yc = None
    if ctx_out:
        yc = pl.pallas_call(
            _win_ctx_kernel,
            grid=(b,),
            in_specs=[smem,
                      pl.BlockSpec((1, lc, wq), lambda i: (i, 0, 0)),
                      pl.BlockSpec((1, lc, 2 * LANES), lambda i: (i, 0, wq // (2 * LANES))),
                      full((1, LANES), 2), full((1, LANES), 2), full((LANES, LANES), 2)],
            out_specs=pl.BlockSpec((1, lc, wq), lambda i: (i, 0, 0)),
            out_shape=jax.ShapeDtypeStruct((b, lc, wq), BF16),
            compiler_params=_cparams(1),
            name="window_ctx_attn",
        )(sink.astype(F32), zw_c, zw_c, qg, kg, e)
    return yc, yl


NA_ROWS = 8
NA_KROWS = NA_ROWS + NA_KH - 1


def _na_key_base(rg, rows):
    return jnp.clip(rg * NA_ROWS - NA_KH // 2, 0, rows - NA_KROWS)


def _na_kernel(q_ref, k_ref, v_ref, ck_ref, cv_ref, bias_ref, qg_ref, kg_ref, e_ref, o_ref, ks_ref, kcs_ref):
    rg = pl.program_id(1)
    l = k_ref.shape[1]
    rows = l // GRID_W
    n_slab = q_ref.shape[-1] // LANES
    prep_rows = 512

    @pl.when(rg == 0)
    def _():
        def body(i, carry):
            r0 = pl.multiple_of(i * prep_rows, prep_rows)
            ks_ref[pl.ds(r0, prep_rows), :] = _head_rms(k_ref[0, pl.ds(r0, prep_rows), :].astype(F32), kg_ref[...],
                                                        e_ref).astype(BF16)
            return carry
        lax.fori_loop(0, l // prep_rows, body, 0)
        kcs_ref[...] = _head_rms(ck_ref[0].astype(F32), kg_ref[...], e_ref).astype(BF16)

    nk = NA_KROWS * GRID_W
    k0 = pl.multiple_of(_na_key_base(rg, rows) * GRID_W, GRID_W)
    qn = _head_rms(q_ref[0].astype(F32), qg_ref[...], e_ref)
    half = _lane_half((q_ref.shape[1], LANES))
    for s in range(n_slab):
        cols = slice(s * LANES, (s + 1) * LANES)
        slab = qn[:, cols]
        tiles_kv = (ks_ref[pl.ds(k0, nk), cols], v_ref[0, pl.ds(k0, nk), cols])
        ctx_kv = (kcs_ref[:, cols], cv_ref[0, :, cols])
        outs = []
        for hh in range(2):
            q = jnp.where(half == hh, slab, 0.0).astype(BF16)
            bias = bias_ref[0, 2 * s + hh].astype(F32)
            outs.append(_attend(q, [tiles_kv + (bias,), ctx_kv + (None,)], None))
        o_ref[0, :, cols] = jnp.where(half == 0, outs[0], outs[1]).astype(o_ref.dtype)


def _na_ctx_kernel(q_ref, ck_ref, cv_ref, qg_ref, kg_ref, e_ref, o_ref):
    kc = _head_rms(ck_ref[0].astype(F32), kg_ref[...], e_ref).astype(BF16)
    qn = _head_rms(q_ref[0].astype(F32), qg_ref[...], e_ref)
    half = _lane_half((q_ref.shape[1], LANES))
    for s in range(q_ref.shape[-1] // LANES):
        cols = slice(s * LANES, (s + 1) * LANES)
        outs = []
        for hh in range(2):
            q = jnp.where(half == hh, qn[:, cols], 0.0).astype(BF16)
            outs.append(_attend(q, [(kc[:, cols], cv_ref[0, :, cols], None)], None))
        o_ref[0, :, cols] = jnp.where(half == 0, outs[0], outs[1]).astype(o_ref.dtype)


def _na_bias_classes(rows):
    n_rg = rows // NA_ROWS
    return list(range(n_rg)) if n_rg <= 3 else [0, 1, n_rg - 1]


def _na_bias_table(rpb, rows):
    kh = NA_KH
    ro, rv = [], []
    for rg in _na_bias_classes(rows):
        kbase = int(np.clip(rg * NA_ROWS - kh // 2, 0, rows - NA_KROWS))
        r = rg * NA_ROWS + np.arange(NA_ROWS)
        rstart = np.clip(r - kh // 2, 0, rows - kh)
        kr = kbase + np.arange(NA_KROWS)
        rv.append((kr[None, :] >= rstart[:, None]) & (kr[None, :] < rstart[:, None] + kh))
        ro.append(np.clip(kr[None, :] - r[:, None] + kh - 1, 0, 2 * kh - 2))
    ro, rv = np.stack(ro), np.stack(rv)
    qc = np.arange(GRID_W)
    cstart = np.clip(qc - NA_KW // 2, 0, GRID_W - NA_KW)
    cv = (qc[None, :] >= cstart[:, None]) & (qc[None, :] < cstart[:, None] + NA_KW)
    co = np.clip(qc[None, :] - qc[:, None] + NA_KW - 1, 0, 2 * NA_KW - 2)
    tab = rpb.astype(F32)[:, ro[:, :, None, :, None], co[None, None, :, None, :]]
    ok = rv[:, :, None, :, None] & cv[None, None, :, None, :]
    tab = jnp.where(ok[None], tab, NEG_INF)
    n_cls = ro.shape[0]
    tab = jnp.moveaxis(tab, 0, 1).reshape(n_cls, rpb.shape[0], NA_ROWS * GRID_W, NA_KROWS * GRID_W)
    return tab.astype(BF16)


def _neighborhood_branch(zn_l, zn_c, q_gain, k_gain, rpb, ctx_out):
    b, l, _ = zn_l.shape
    lc = zn_c.shape[1]
    w = NA_HEADS * HEAD_DIM
    rows = l // GRID_W
    n_rg = rows // NA_ROWS
    n_slab = w // LANES
    qg = (jnp.tile(q_gain.astype(F32), 2 * n_slab) * HEAD_DIM ** -0.5).reshape(1, w)
    kg = jnp.tile(k_gain.astype(F32), 2 * n_slab).reshape(1, w)
    e = _block_diag_ones(w)
    bias = _na_bias_table(rpb, rows)
    if n_rg <= 3:
        cls_map = lambda i, r: (r, 0, 0, 0)
    else:
        cls_map = lambda i, r: ((r > 0).astype(jnp.int32) + (r == n_rg - 1).astype(jnp.int32), 0, 0, 0)
    tq = NA_ROWS * GRID_W
    full = lambda s, nd: pl.BlockSpec(s, lambda *a: (0,) * nd)
    yl = pl.pallas_call(
        _na_kernel,
        grid=(b, n_rg),
        in_specs=[pl.BlockSpec((1, tq, w), lambda i, r: (i, r, 0)),
                  pl.BlockSpec((1, l, w), lambda i, r: (i, 0, 1)),
                  pl.BlockSpec((1, l, w), lambda i, r: (i, 0, 2)),
                  pl.BlockSpec((1, lc, w), lambda i, r: (i, 0, 1)),
                  pl.BlockSpec((1, lc, w), lambda i, r: (i, 0, 2)),
                  pl.BlockSpec((1,) + bias.shape[1:], cls_map),
                  full((1, w), 2), full((1, w), 2), full((w, w), 2)],
        out_specs=pl.BlockSpec((1, tq, w), lambda i, r: (i, r, 0)),
        out_shape=jax.ShapeDtypeStruct((b, l, w), BF16),
        scratch_shapes=[pltpu.VMEM((l, w), BF16), pltpu.VMEM((lc, w), BF16)],
        compiler_params=_cparams(2),
        name="neighborhood_attn",
    )(zn_l, zn_l, zn_l, zn_c, zn_c, bias, qg, kg, e)
    yc = None
    if ctx_out:
        yc = pl.pallas_call(
            _na_ctx_kernel,
            grid=(b,),
            in_specs=[pl.BlockSpec((1, lc, w), lambda i: (i, 0, 0)),
                      pl.BlockSpec((1, lc, w), lambda i: (i, 0, 1)),
                      pl.BlockSpec((1, lc, w), lambda i: (i, 0, 2)),
                      full((1, w), 2), full((1, w), 2), full((w, w), 2)],
            out_specs=pl.BlockSpec((1, lc, w), lambda i: (i, 0, 0)),
            out_shape=jax.ShapeDtypeStruct((b, lc, w), BF16),
            compiler_params=_cparams(1),
            name="neighborhood_ctx_attn",
        )(zn_c, zn_c, zn_c, qg, kg, e)
    return yc, yl


def _rope_tables(n_tokens):
    t = np.arange(n_tokens)
    n_freq = HEAD_DIM // 4
    inv = ROPE_THETA ** (-jnp.arange(n_freq, dtype=F32) / n_freq)
    ang = jnp.concatenate([jnp.asarray(t // GRID_W, F32)[:, None] * inv, jnp.asarray(t % GRID_W, F32)[:, None] * inv],
                          axis=-1)
    cos, sin = jnp.cos(ang), jnp.sin(ang)
    return jnp.tile(jnp.concatenate([cos, cos], -1), (1, 2)), jnp.tile(jnp.concatenate([-sin, sin], -1), (1, 2))


def _ret_kernel(qq_l, kk_l, v_l, g_l, qq_c, kk_c, v_c, g_c, cos_ref, sin_ref, dmask_ref, tq_ref, tk_ref, cdec_ref,
                gain_ref, yl_ref, yc_ref, kr_ref, kv_ref, sin_state_ref, *, ctx_out):
    c = RET_CHUNK
    ncc = qq_c.shape[1] // c
    ncl = qq_l.shape[1] // c
    nc = ncc + ncl
    tk = tk_ref[0]
    tq = tq_ref[0]
    dmask = dmask_ref[0]
    fwd_lanes = lax.broadcasted_iota(jnp.int32, (c, LANES), 1) < RET_DK

    def chunk_kv(k2, v):
        kd = (k2 * tk).astype(BF16)
        return lax.dot_general(kd, v, (((0,), (0,)), ((), ())), preferred_element_type=F32)

    for n in range(ncc):
        kv_ref[n] = chunk_kv(kk_c[0, n * c:(n + 1) * c, :].astype(F32), v_c[0, n * c:(n + 1) * c, :])

    def kv_body(n, carry):
        r0 = pl.multiple_of(n * c, c)
        k2 = _rope(kk_l[0, pl.ds(r0, c), :].astype(F32), cos_ref[pl.ds(r0, c), :], sin_ref[pl.ds(r0, c), :])
        kr_ref[pl.ds(r0, c), :] = k2.astype(BF16)
        kv_ref[ncc + n] = chunk_kv(k2, v_l[0, pl.ds(r0, c), :])
        return carry
    lax.fori_loop(0, ncl, kv_body, 0)

    dec_f = cdec_ref[0, 0:1, :]
    dec_b = cdec_ref[0, 1:2, :]

    def scan_body(t, carry):
        sf, sb = carry
        sin_state_ref[t, 0:RET_DK, :] = sf.astype(BF16)
        sf = sf * dec_f + kv_ref[t, 0:RET_DK, :]
        tb = jnp.where(t < ncc, ncc - 1 - t, nc - 1 - (t - ncc))
        sin_state_ref[tb, RET_DK:2 * RET_DK, :] = sb.astype(BF16)
        sb = sb * dec_b + kv_ref[tb, RET_DK:2 * RET_DK, :]
        return sf, sb
    zero = jnp.zeros((RET_DK, RET_DV), F32)
    lax.fori_loop(0, nc, scan_body, (zero, zero))

    def chunk_out(q2, k2b, v, gate, state):
        qm = jnp.where(fwd_lanes, q2, 0.0).astype(BF16)
        scores = lax.dot_general(qm, k2b, (((1,), (1,)), ((), ())), preferred_element_type=F32) * dmask
        o = (jnp.dot(scores.astype(BF16), v, preferred_element_type=F32)
             + jnp.dot((q2 * tq).astype(BF16), state, preferred_element_type=F32))
        mu = jnp.mean(o, axis=-1, keepdims=True)
        var = jnp.mean(jnp.square(o - mu), axis=-1, keepdims=True)
        y = (o - mu) * lax.rsqrt(var + EPS) * gain_ref[0]
        gf = gate.astype(F32)
        return y * gf * jax.nn.sigmoid(gf)

    if ctx_out:
        for n in range(ncc):
            rows = slice(n * c, (n + 1) * c)
            yc_ref[0, rows, :] = chunk_out(qq_c[0, rows, :].astype(F32), kk_c[0, rows, :], v_c[0, rows, :],
                                           g_c[0, rows, :], sin_state_ref[n]).astype(yc_ref.dtype)
    else:
        yc_ref[...] = jnp.zeros_like(yc_ref)

    def out_body(n, carry):
        r0 = pl.multiple_of(n * c, c)
        q2 = _rope(qq_l[0, pl.ds(r0, c), :].astype(F32), cos_ref[pl.ds(r0, c), :], sin_ref[pl.ds(r0, c), :])
        yl_ref[0, pl.ds(r0, c), :] = chunk_out(q2, kr_ref[pl.ds(r0, c), :], v_l[0, pl.ds(r0, c), :],
                                               g_l[0, pl.ds(r0, c), :], sin_state_ref[ncc + n]).astype(yl_ref.dtype)
        return carry
    lax.fori_loop(0, ncl, out_body, 0)


def _ret_tables(decay_logit):
    c = RET_CHUNK
    lg = jax.nn.log_sigmoid(decay_logit.astype(F32))
    lf, lb = lg[0][:, None, None], lg[1][:, None, None]
    pos = jnp.arange(c, dtype=F32)
    diff = pos[:, None] - pos[None, :]
    dmask = (jnp.where(diff >= 0, jnp.exp(lf * jnp.maximum(diff, 0.0)), 0.0)
             + jnp.where(diff <= 0, jnp.exp(lb * jnp.maximum(-diff, 0.0)), 0.0)) * RET_DK ** -0.5
    col = lambda a, b_: jnp.concatenate([jnp.broadcast_to(a, a.shape[:2] + (RET_DK,)),
                                         jnp.broadcast_to(b_, b_.shape[:2] + (RET_DK,))], axis=-1)
    p = pos[None, :, None]
    tq = col(jnp.exp(lf * (p + 1.0)), jnp.exp(lb * (c - p)))
    tk = col(jnp.exp(lf * (c - 1.0 - p)), jnp.exp(lb * p)) * RET_DK ** -0.5
    cdec = jnp.zeros((lg.shape[1], 8, RET_DV), F32)
    cdec = cdec.at[:, 0, :].set(jnp.exp(lg[0] * c)[:, None]).at[:, 1, :].set(jnp.exp(lg[1] * c)[:, None])
    return dmask, tq, tk, cdec


def _retention_branch(zr_l, zr_c, decay_logit, gn_gain, cos2, sin2, ctx_out):
    b, l, _ = zr_l.shape
    lc = zr_c.shape[1]
    h = RET_HEADS
    nc = (l + lc) // RET_CHUNK
    dmask, tq, tk, cdec = _ret_tables(decay_logit)
    gain = gn_gain.astype(F32).reshape(h, 1, RET_DV)
    seq = lambda n, j: pl.BlockSpec((1, n, LANES), lambda i, hh: (i, 0, 4 * hh + j))
    head = lambda s: pl.BlockSpec((1,) + s, lambda i, hh: (hh, 0, 0))
    full = lambda s: pl.BlockSpec(s, lambda i, hh: (0, 0))
    yl, yc = pl.pallas_call(
        functools.partial(_ret_kernel, ctx_out=ctx_out),
        grid=(b, h),
        in_specs=[seq(l, 0), seq(l, 1), seq(l, 2), seq(l, 3), seq(lc, 0), seq(lc, 1), seq(lc, 2), seq(lc, 3),
                  full((l, LANES)), full((l, LANES)),
                  head((RET_CHUNK, RET_CHUNK)), head((RET_CHUNK, LANES)), head((RET_CHUNK, LANES)), head((8, RET_DV)),
                  head((1, RET_DV))],
        out_specs=[pl.BlockSpec((1, l, RET_DV), lambda i, hh: (i, 0, hh)),
                   pl.BlockSpec((1, lc, RET_DV), lambda i, hh: (i, 0, hh))],
        out_shape=[jax.ShapeDtypeStruct((b, l, h * RET_DV), BF16), jax.ShapeDtypeStruct((b, lc, h * RET_DV), BF16)],
        scratch_shapes=[pltpu.VMEM((l, LANES), BF16), pltpu.VMEM((nc, 2 * RET_DK, RET_DV), F32),
                        pltpu.VMEM((nc, 2 * RET_DK, RET_DV), BF16)],
        compiler_params=_cparams(2),
        name="retention",
    )(zr_l, zr_l, zr_l, zr_l, zr_c, zr_c, zr_c, zr_c, cos2, sin2, dmask, tq, tk, cdec, gain)
    return (yc if ctx_out else None), yl


GDN_SUPER = 256
GDN_HALO = 128


def _split_bf16(a):
    hi = a.astype(BF16)
    return hi, (a - hi.astype(F32)).astype(BF16)


def _dot3(a, b):
    ah, al = _split_bf16(a)
    bh, bl = _split_bf16(b)
    return (jnp.dot(ah, bh, preferred_element_type=F32) + jnp.dot(al, bh, preferred_element_type=F32)
            + jnp.dot(ah, bl, preferred_element_type=F32))


def _mask_dot(mask_bf16, a):
    ah, al = _split_bf16(a)
    return jnp.dot(mask_bf16, ah, preferred_element_type=F32) + jnp.dot(mask_bf16, al, preferred_element_type=F32)


def _softplus(x):
    return jnp.maximum(x, 0.0) + jnp.log(1.0 + jnp.exp(-jnp.abs(x)))


def _gdn_kernel(nega_ref, dtb_ref, q_l, k_l, v_l, g_l, ab_l, abt_l, q_c, k_c, v_c, g_c, ab_c, abt_c,
                cwq_ref, cwk_ref, cwv_ref, pd_ref, pu_ref, gain_ref, yl_ref, yc_ref,
                u_s, w_s, qd_s, ke_s, aqk_s, cd_s, o_s, *, ctx_out):
    hd = pl.program_id(1)
    c = GDN_CHUNK
    sup = GDN_SUPER
    per = sup // c
    lc, l = q_c.shape[1], q_l.shape[1]
    ncc, ncl = lc // c, l // c
    nc = ncc + ncl

    ri = lax.broadcasted_iota(jnp.int32, (sup, sup), 0)
    ci = lax.broadcasted_iota(jnp.int32, (sup, sup), 1)
    same = (ri // c) == (ci // c)
    eye = (ri == ci).astype(F32)
    incl = (same & (ri >= ci), same & (ri <= ci))
    strict = (same & (ri > ci), same & (ri < ci))
    incl_b = tuple(m.astype(BF16) for m in incl)
    same_b = same.astype(BF16)
    lane = lax.broadcasted_iota(jnp.int32, (sup, LANES), 1)
    sub16 = lax.broadcasted_iota(jnp.int32, (16, sup), 0)

    def conv_silu(z_ref, w_ref, r0, ls, static_edges):
        z = z_ref[0, pl.ds(r0, sup), :]
        if static_edges:
            prev = jnp.zeros((GDN_HALO, LANES), BF16)
            nxt = prev
        else:
            p0 = pl.multiple_of(jnp.maximum(r0 - GDN_HALO, 0), GDN_HALO)
            n0 = pl.multiple_of(jnp.minimum(r0 + sup, ls - GDN_HALO), GDN_HALO)
            prev = jnp.where(r0 > 0, z_ref[0, pl.ds(p0, GDN_HALO), :], jnp.zeros((), BF16))
            nxt = jnp.where(r0 + sup < ls, z_ref[0, pl.ds(n0, GDN_HALO), :], jnp.zeros((), BF16))
        win = jnp.concatenate([prev, z, nxt], axis=0)
        z_dn = jnp.dot(pd_ref[...], win, preferred_element_type=F32)
        z_up = jnp.dot(pu_ref[...], win, preferred_element_type=F32)
        y = z_dn * w_ref[0, 0:1, :] + z.astype(F32) * w_ref[0, 1:2, :] + z_up * w_ref[0, 2:3, :]
        return y * jax.nn.sigmoid(y)

    def prep(refs, r0, base, ls, static_edges):
        q_ref, k_ref, v_ref, ab_ref, abt_ref = refs
        q = conv_silu(q_ref, cwq_ref, r0, ls, static_edges)
        k = conv_silu(k_ref, cwk_ref, r0, ls, static_edges)
        v = conv_silu(v_ref, cwv_ref, r0, ls, static_edges)
        q = q * lax.rsqrt(jnp.sum(q * q, axis=-1, keepdims=True) + EPS) * GDN_DK ** -0.5
        k = k * lax.rsqrt(jnp.sum(k * k, axis=-1, keepdims=True) + EPS)
        qb, kb16 = q.astype(BF16), k.astype(BF16)
        qk = lax.dot_general(qb, kb16, (((1,), (1,)), ((), ())), preferred_element_type=F32)
        ab = ab_ref[0, pl.ds(r0, sup), :]
        abt = abt_ref[0, r0 // sup] if not static_edges else abt_ref[0, 0]
        for d in range(2):
            ia = d * GDN_HEADS + hd
            ib = 2 * GDN_HEADS + ia
            a_col = jnp.sum(jnp.where(lane == ia, ab, 0.0), axis=1, keepdims=True)
            b_col = jnp.sum(jnp.where(lane == ib, ab, 0.0), axis=1, keepdims=True)
            a_row = jnp.sum(jnp.where(sub16 == ia, abt, 0.0), axis=0, keepdims=True)
            g_col = nega_ref[d, hd] * _softplus(a_col + dtb_ref[d, hd])
            g_row = nega_ref[d, hd] * _softplus(a_row + dtb_ref[d, hd])
            beta = jax.nn.sigmoid(b_col)
            gcum = _mask_dot(incl_b[d], jnp.broadcast_to(g_col, (sup, LANES)))
            gtot = _mask_dot(same_b, jnp.broadcast_to(g_col, (sup, LANES)))
            gr_hi, gr_lo = _split_bf16(jnp.broadcast_to(g_row, (8, sup)))
            grow = (jnp.dot(gr_hi, incl_b[1 - d], preferred_element_type=F32)
                    + jnp.dot(gr_lo, incl_b[1 - d], preferred_element_type=F32))[0:1, :]
            gc = gcum[:, 0:1]
            decay = jnp.where(incl[d], jnp.exp(jnp.where(incl[d], gc - grow, 0.0)), 0.0)
            kbeta = k * beta
            kk = lax.dot_general(kbeta.astype(BF16), kb16, (((1,), (1,)), ((), ())), preferred_element_type=F32)
            a_mat = jnp.where(strict[d], kk * decay, 0.0)
            inv = eye - a_mat
            pw = a_mat
            for _ in range(5):
                pw = _dot3(pw, pw)
                inv = inv + _dot3(inv, pw)
            eg = jnp.exp(gc)
            rhs = jnp.concatenate([v * beta, kbeta * eg], axis=1)
            sol = _dot3(inv, rhs)
            rows = pl.ds(base + r0, sup)
            u_s[d, rows, :] = sol[:, :GDN_DV]
            w_s[d, rows, :] = sol[:, GDN_DV:].astype(BF16)
            qd_s[d, rows, :] = (q * eg).astype(BF16)
            ke_s[d, rows, :] = (k * jnp.exp(gtot[:, 0:1] - gc)).astype(BF16)
            aqk = jnp.where(incl[d], qk * decay, 0.0)
            for j in range(per):
                aqk_s[d, pl.ds(base + r0 + j * c, c), :] = aqk[j * c:(j + 1) * c, j * c:(j + 1) * c].astype(BF16)
                cd_s[d, (base + r0) // c + j] = jnp.exp(gtot[j * c:j * c + 1, :])

    for n in range(lc // sup):
        prep((q_c, k_c, v_c, ab_c, abt_c), n * sup, 0, lc, lc == sup)

    def prep_body(n, carry):
        prep((q_l, k_l, v_l, ab_l, abt_l), pl.multiple_of(n * sup, sup), lc, l, False)
        return carry
    lax.fori_loop(0, l // sup, prep_body, 0)

    def chunk_step(d, idx, s):
        r0 = pl.multiple_of(idx * c, c)
        rows = pl.ds(r0, c)
        wq = jnp.concatenate([w_s[d, rows, :], qd_s[d, rows, :]], axis=0)
        ws = jnp.dot(wq, s.astype(BF16), preferred_element_type=F32)
        v_new = u_s[d, rows, :] - ws[:c]
        vb = v_new.astype(BF16)
        o_s[d, rows, :] = ws[c:] + jnp.dot(aqk_s[d, rows, :], vb, preferred_element_type=F32)
        return s * cd_s[d, idx] + lax.dot_general(ke_s[d, rows, :], vb, (((0,), (0,)), ((), ())),
                                                  preferred_element_type=F32)

    def scan_body(t, carry):
        sf, sb = carry
        tb = jnp.where(t < ncc, ncc - 1 - t, nc - 1 - (t - ncc))
        return chunk_step(0, t, sf), chunk_step(1, tb, sb)
    zero = jnp.zeros((GDN_DK, GDN_DV), F32)
    lax.fori_loop(0, nc, scan_body, (zero, zero))

    def finish(y_ref, gate_ref, base, n_rows):
        def body(n, carry):
            r0 = pl.multiple_of(n * sup, sup)
            o = o_s[0, pl.ds(base + r0, sup), :] + o_s[1, pl.ds(base + r0, sup), :]
            y = o * lax.rsqrt(jnp.mean(o * o, axis=-1, keepdims=True) + EPS) * gain_ref[...]
            gf = gate_ref[0, pl.ds(r0, sup), :].astype(F32)
            y_ref[0, pl.ds(r0, sup), :] = (y * gf * jax.nn.sigmoid(gf)).astype(y_ref.dtype)
            return carry
        lax.fori_loop(0, n_rows // sup, body, 0)

    finish(yl_ref, g_l, lc, l)
    if ctx_out:
        finish(yc_ref, g_c, 0, lc)
    else:
        yc_ref[...] = jnp.zeros_like(yc_ref)


def _gdn_branch(zg_l, zab_l, zg_c, zab_c, conv_w, a_log, dt_bias, norm_gain, ctx_out):
    b, l, _ = zg_l.shape
    lc = zg_c.shape[1]
    h = GDN_HEADS
    sup = GDN_SUPER
    ltot = l + lc
    nc = ltot // GDN_CHUNK
    neg_a = -jnp.exp(a_log.astype(F32))
    cw = conv_w.astype(F32).T.reshape(3 * h, LANES, SHORT_CONV).transpose(0, 2, 1)
    win = sup + 2 * GDN_HALO
    i = np.arange(sup)
    pd = np.zeros((sup, win), np.float32)
    pu = np.zeros((sup, win), np.float32)
    pd[i, GDN_HALO + i - 1] = 1.0
    pu[i, GDN_HALO + i + 1] = 1.0
    abt = lambda z: z[..., :16].reshape(b, z.shape[1] // sup, sup, 16).transpose(0, 1, 3, 2)
    seq = lambda n, j: pl.BlockSpec((1, n, LANES), lambda bi, hh: (bi, 0, j * h + hh))
    abs_ = lambda n: pl.BlockSpec((1, n, LANES), lambda bi, hh: (bi, 0, 0))
    abts = lambda n: pl.BlockSpec((1, n // sup, 16, sup), lambda bi, hh: (bi, 0, 0, 0))
    cws = lambda j: pl.BlockSpec((1, SHORT_CONV, LANES), lambda bi, hh: (j * h + hh, 0, 0))
    full = lambda s: pl.BlockSpec(s, lambda bi, hh: (0, 0))
    smem = pl.BlockSpec(memory_space=pltpu.SMEM)
    yl, yc = pl.pallas_call(
        functools.partial(_gdn_kernel, ctx_out=ctx_out),
        grid=(b, h),
        in_specs=[smem, smem, seq(l, 0), seq(l, 1), seq(l, 2), seq(l, 3), abs_(l), abts(l),
                  seq(lc, 0), seq(lc, 1), seq(lc, 2), seq(lc, 3), abs_(lc), abts(lc),
                  cws(0), cws(1), cws(2), full((sup, win)), full((sup, win)), full((1, GDN_DV))],
        out_specs=[pl.BlockSpec((1, l, GDN_DV), lambda bi, hh: (bi, 0, hh)),
                   pl.BlockSpec((1, lc, GDN_DV), lambda bi, hh: (bi, 0, hh))],
        out_shape=[jax.ShapeDtypeStruct((b, l, h * GDN_DV), BF16), jax.ShapeDtypeStruct((b, lc, h * GDN_DV), BF16)],
        scratch_shapes=[pltpu.VMEM((2, ltot, GDN_DV), F32), pltpu.VMEM((2, ltot, GDN_DK), BF16),
                        pltpu.VMEM((2, ltot, GDN_DK), BF16), pltpu.VMEM((2, ltot, GDN_DK), BF16),
                        pltpu.VMEM((2, ltot, GDN_CHUNK), BF16), pltpu.VMEM((2, nc, 1, GDN_DV), F32),
                        pltpu.VMEM((2, ltot, GDN_DV), F32)],
        compiler_params=_cparams(2),
        name="gated_deltanet",
    )(neg_a, dt_bias.astype(F32), zg_l, zg_l, zg_l, zg_l, zab_l, abt(zab_l), zg_c, zg_c, zg_c, zg_c, zab_c,
      abt(zab_c), cw, cw, cw, jnp.asarray(pd, BF16), jnp.asarray(pu, BF16), norm_gain.astype(F32).reshape(1, GDN_DV))
    return (yc if ctx_out else None), yl


def kernel(x, c, ctx, c_ctx, w_mod, b_mod, norm1, norm2, w_in, ret_decay, ret_gn, win_qnorm, win_knorm, win_sink,
           na_qnorm, na_knorm, na_rpb, gdn_conv, gdn_a_log, gdn_dt_bias, gdn_norm, w_branch, w_merge, w_out,
           w_router, router_bias, w_e_gate, w_e_up, w_e_down):
    b, l, d = x.shape
    lc = ctx.shape[1]
    depth = w_mod.shape[0]
    cos2, sin2 = _rope_tables(l)

    n_rows = 16
    cc = jnp.zeros((n_rows, d), F32).at[:b].set(c).at[b].set(c_ctx)
    mod = _modulation(cc, w_mod, b_mod).reshape(depth, n_rows, 6, d)

    wr = jnp.zeros((d, LANES), F32).at[:, :N_EXPERTS].set(w_router)
    wr_hi = wr.astype(BF16)
    wr_lo = (wr - wr_hi.astype(F32)).astype(BF16)

    xl, xc = x, ctx
    for layer in range(depth):
        ctx_out = layer < depth - 1
        mod_l = mod[layer, :b]
        mod_c = mod[layer, b:b + 1]
        w_all = _pack_w_in(w_in[layer])
        hl, *zl_s = _inproj(xl, mod_l, False, norm1[layer], w_all, 256)
        hc, *zc_s = _inproj(xc, mod_c, True, norm1[layer], w_all, 256)
        zr_l, zw_l, zn_l, zg_l, zab_l = zl_s
        zr_c, zw_c, zn_c, zg_c, zab_c = zc_s
        ret_c, ret_l = _retention_branch(zr_l, zr_c, ret_decay[layer], ret_gn[layer], cos2, sin2, ctx_out)
        win_c, win_l = _window_branch(zw_l, zw_c, win_qnorm[layer], win_knorm[layer], win_sink[layer], cos2, sin2,
                                      ctx_out)
        na_c, na_l = _neighborhood_branch(zn_l, zn_c, na_qnorm[layer], na_knorm[layer], na_rpb[layer], ctx_out)
        gdn_c, gdn_l = _gdn_branch(zg_l, zab_l, zg_c, zab_c, gdn_conv[layer], gdn_a_log[layer], gdn_dt_bias[layer],
                                   gdn_norm[layer], ctx_out)
        wm = w_merge[layer].astype(BF16)
        wb = w_branch[layer].astype(BF16)
        wo = w_out[layer].astype(BF16)
        ys_l = [ret_l, win_l, na_l, gdn_l]
        xl, h2l, sc_l = _merge(xl, hl, ys_l, mod_l, False, norm2[layer], wm, wb, wo, wr_hi, wr_lo, 256)
        g2l = mod_l[:, 5][:, None, :]
        if ctx_out:
            ys_c = [ret_c, win_c, na_c, gdn_c]
            xc, h2c, sc_c = _merge(xc, hc, ys_c, mod_c, True, norm2[layer], wm, wb, wo, wr_hi, wr_lo, 256)
            g2c = mod_c[:, 5][:, None, :]
            tokens = jnp.concatenate([h2c.reshape(b * lc, d), h2l.reshape(b * l, d)], axis=0)
            scores = jnp.concatenate([sc_c.reshape(b * lc, LANES), sc_l.reshape(b * l, LANES)], axis=0)
            y = _moe(tokens, scores[:, :N_EXPERTS], router_bias, w_e_gate[layer], w_e_up[layer], w_e_down[layer])
            xc = xc + g2c * y[:b * lc].reshape(b, lc, d)
            xl = xl + g2l * y[b * lc:].reshape(b, l, d)
        else:
            y = _moe(h2l.reshape(b * l, d), sc_l.reshape(b * l, LANES)[:, :N_EXPERTS], router_bias,
                     w_e_gate[layer], w_e_up[layer], w_e_down[layer])
            xl = xl + g2l * y.reshape(b, l, d)
    return xl
```

```python
import functools

import numpy as np
import jax
import jax.numpy as jnp
from jax import lax
from jax.experimental import pallas as pl
from jax.experimental.pallas import tpu as pltpu

F32 = jnp.float32
BF16 = jnp.bfloat16
EPS = 1e-6
NEG_INF = -1e30
D_MODEL = 1024
GRID_W = 64
HEAD_DIM = 64
ROPE_THETA = 10000.0
RET_HEADS, RET_DK, RET_DV, RET_CHUNK = 4, 64, 128, 128
WIN_HEADS, WIN_KV_HEADS, WINDOW, WIN_BLOCK = 8, 2, 128, 128
NA_HEADS, NA_KH, NA_KW, NA_QCOLS = 8, 8, 16, 16
NA_BAND = NA_QCOLS + NA_KW
GDN_HEADS, GDN_DK, GDN_DV, GDN_CHUNK, SHORT_CONV = 4, 128, 128, 64, 3
GDN_QKV = 2 * GDN_HEADS * GDN_DK + GDN_HEADS * GDN_DV
N_BRANCH, BRANCH_W = 4, 512
N_EXPERTS, N_GROUPS, TOP_K, D_EXPERT = 32, 8, 2, 512
EXPERTS_PER_GROUP = N_EXPERTS // N_GROUPS

LANES = 128
VMEM_LIMIT = 56 * 1024 * 1024
MOE_ROWS = 256

W_RET = RET_HEADS * 4 * LANES
W_WIN = (WIN_HEADS + 2 * WIN_KV_HEADS) * HEAD_DIM
W_NA = 3 * NA_HEADS * HEAD_DIM
W_GDN = GDN_QKV + GDN_HEADS * GDN_DV
W_AB = LANES
SECTION_WIDTHS = (W_RET, W_WIN, W_NA, W_GDN, W_AB)
W_ALL = sum(SECTION_WIDTHS)


def _cparams(n_axes):
    return pltpu.CompilerParams(dimension_semantics=("arbitrary",) * n_axes, vmem_limit_bytes=VMEM_LIMIT)


def _mod_kernel(c_ref, w_ref, b_ref, o_ref):
    c = c_ref[...]
    a = (c * jax.nn.sigmoid(c)).astype(BF16)
    o_ref[0] = jnp.dot(a, w_ref[0].astype(BF16), preferred_element_type=F32) + b_ref[0]


def _modulation(cc, w_mod, b_mod):
    depth, d, n = w_mod.shape
    r = cc.shape[0]
    tn = 1536
    return pl.pallas_call(
        _mod_kernel,
        grid=(depth, n // tn),
        in_specs=[pl.BlockSpec((r, d), lambda l, j: (0, 0)),
                  pl.BlockSpec((1, d, tn), lambda l, j: (l, 0, j)),
                  pl.BlockSpec((1, 1, tn), lambda l, j: (l, 0, j))],
        out_specs=pl.BlockSpec((1, r, tn), lambda l, j: (l, 0, j)),
        out_shape=jax.ShapeDtypeStruct((depth, r, n), F32),
        compiler_params=_cparams(2),
        name="modulation",
    )(cc, w_mod, b_mod.reshape(depth, 1, n))


def _inproj_kernel(x_ref, mod_ref, gain_ref, w_ref, h_ref, *z_refs):
    x = x_ref[0]
    ms = jnp.mean(x * x, axis=-1, keepdims=True)
    shift = mod_ref[0, 0:1, :]
    scale = mod_ref[0, 1:2, :]
    h = x * lax.rsqrt(ms + EPS) * gain_ref[...] * (1.0 + scale) + shift
    hb = h.astype(BF16)
    h_ref[0] = hb
    off = 0
    for ref in z_refs:
        width = ref.shape[-1]
        for c0 in range(0, width, 512):
            c1 = min(c0 + 512, width)
            z = jnp.dot(hb, w_ref[:, off + c0:off + c1], preferred_element_type=F32)
            ref[0, :, c0:c1] = z.astype(ref.dtype)
        off += width


def _inproj(x, mod, mod_is_shared, gain, w_all, tm):
    b, l, d = x.shape
    mod_map = (lambda i, j: (0, 0, 0)) if mod_is_shared else (lambda i, j: (i, 0, 0))
    dtypes = (BF16, BF16, BF16, BF16, F32)
    out_shape = [jax.ShapeDtypeStruct((b, l, d), BF16)]
    out_specs = [pl.BlockSpec((1, tm, d), lambda i, j: (i, j, 0))]
    for w, dt in zip(SECTION_WIDTHS, dtypes):
        out_shape.append(jax.ShapeDtypeStruct((b, l, w), dt))
        out_specs.append(pl.BlockSpec((1, tm, w), lambda i, j: (i, j, 0)))
    return pl.pallas_call(
        _inproj_kernel,
        grid=(b, l // tm),
        in_specs=[pl.BlockSpec((1, tm, d), lambda i, j: (i, j, 0)),
                  pl.BlockSpec((1, 6, d), mod_map),
                  pl.BlockSpec((1, d), lambda i, j: (0, 0)),
                  pl.BlockSpec((d, W_ALL), lambda i, j: (0, 0))],
        out_specs=out_specs,
        out_shape=out_shape,
        compiler_params=_cparams(2),
        name="inproj",
    )(x, mod, gain.reshape(1, d), w_all)


def _pack_w_in(w_in):
    d = w_in.shape[0]
    hq, hv = RET_HEADS * RET_DK, RET_HEADS * RET_DV
    cols = []
    for h in range(RET_HEADS):
        q = w_in[:, h * RET_DK:(h + 1) * RET_DK]
        k = w_in[:, hq + h * RET_DK:hq + (h + 1) * RET_DK]
        cols += [q, q, k, k, w_in[:, 2 * hq + h * RET_DV:2 * hq + (h + 1) * RET_DV],
                 w_in[:, 2 * hq + hv + h * RET_DV:2 * hq + hv + (h + 1) * RET_DV]]
    rest = w_in[:, 2 * hq + 2 * hv:]
    pad = jnp.zeros((d, W_ALL - W_RET - rest.shape[1]), w_in.dtype)
    return jnp.concatenate(cols + [rest, pad], axis=1).astype(BF16)


def _merge_kernel(x_ref, h_ref, y0_ref, y1_ref, y2_ref, y3_ref, mod_ref, gain_ref, wm_ref, wb_ref, wo_ref,
                  wrh_ref, wrl_ref, xo_ref, h2_ref, sc_ref):
    d = x_ref.shape[-1]
    h = h_ref[0]
    acc = jnp.zeros(x_ref.shape[1:], F32)
    for i, y_ref in enumerate((y0_ref, y1_ref, y2_ref, y3_ref)):
        gate = jax.nn.sigmoid(jnp.dot(h, wm_ref[:, i * d:(i + 1) * d], preferred_element_type=F32))
        acc = acc + gate * jnp.dot(y_ref[0], wb_ref[i], preferred_element_type=F32)
    m = jnp.dot(acc.astype(BF16), wo_ref[...], preferred_element_type=F32)
    xn = x_ref[0] + mod_ref[0, 2:3, :] * m
    xo_ref[0] = xn
    ms = jnp.mean(xn * xn, axis=-1, keepdims=True)
    h2 = xn * lax.rsqrt(ms + EPS) * gain_ref[...] * (1.0 + mod_ref[0, 4:5, :]) + mod_ref[0, 3:4, :]
    hi = h2.astype(BF16)
    h2_ref[0] = hi
    lo = (h2 - hi.astype(F32)).astype(BF16)
    logits = (jnp.dot(hi, wrh_ref[...], preferred_element_type=F32)
              + jnp.dot(lo, wrh_ref[...], preferred_element_type=F32)
              + jnp.dot(hi, wrl_ref[...], preferred_element_type=F32))
    sc_ref[0] = jax.nn.sigmoid(logits)


def _merge(x, h, ys, mod, mod_is_shared, gain2, wm, wb, wo, wr_hi, wr_lo, tm):
    b, l, d = x.shape
    mod_map = (lambda i, j: (0, 0, 0)) if mod_is_shared else (lambda i, j: (i, 0, 0))
    tok = lambda w: pl.BlockSpec((1, tm, w), lambda i, j: (i, j, 0))
    full2 = lambda s: pl.BlockSpec(s, lambda i, j: (0, 0))
    return pl.pallas_call(
        _merge_kernel,
        grid=(b, l // tm),
        in_specs=[tok(d), tok(d)] + [tok(BRANCH_W)] * 4 + [
            pl.BlockSpec((1, 6, d), mod_map), full2((1, d)), full2(wm.shape),
            pl.BlockSpec(wb.shape, lambda i, j: (0, 0, 0)), full2(wo.shape), full2(wr_hi.shape), full2(wr_lo.shape)],
        out_specs=[tok(d), tok(d), tok(LANES)],
        out_shape=[jax.ShapeDtypeStruct((b, l, d), F32), jax.ShapeDtypeStruct((b, l, d), BF16),
                   jax.ShapeDtypeStruct((b, l, LANES), F32)],
        compiler_params=_cparams(2),
        name="merge",
    )(x, h, *ys, mod, gain2.reshape(1, d), wm, wb, wo, wr_hi, wr_lo)


def _moe_kernel(be_ref, nu_ref, xb_ref, rw_ref, wg_ref, wu_ref, wd_ref, o_ref, wg_s, wu_s, wd_s):
    i = pl.program_id(0)
    e = be_ref[i]
    prev = be_ref[jnp.maximum(i - 1, 0)]
    used = i < nu_ref[0]

    @pl.when(used & ((i == 0) | (e != prev)))
    def _():
        wg_s[...] = wg_ref[0].astype(BF16)
        wu_s[...] = wu_ref[0].astype(BF16)
        wd_s[...] = wd_ref[0].astype(BF16)

    @pl.when(used)
    def _():
        x = xb_ref[...]
        g = jnp.dot(x, wg_s[...], preferred_element_type=F32)
        u = jnp.dot(x, wu_s[...], preferred_element_type=F32)
        a = (g * jax.nn.sigmoid(g) * u).astype(BF16)
        y = jnp.dot(a, wd_s[...], preferred_element_type=F32)
        o_ref[...] = (y * rw_ref[...]).astype(o_ref.dtype)

    @pl.when(jnp.logical_not(used))
    def _():
        o_ref[...] = jnp.zeros_like(o_ref)


def _moe_ffn(block_e, n_used, xb, rw, w_gate, w_up, w_down):
    size, d = xb.shape
    n_blocks = size // MOE_ROWS
    de = w_gate.shape[-1]
    grid_spec = pltpu.PrefetchScalarGridSpec(
        num_scalar_prefetch=2,
        grid=(n_blocks,),
        in_specs=[pl.BlockSpec((MOE_ROWS, d), lambda i, be, nu: (i, 0)),
                  pl.BlockSpec((MOE_ROWS, 1), lambda i, be, nu: (i, 0)),
                  pl.BlockSpec((1, d, de), lambda i, be, nu: (be[i], 0, 0)),
                  pl.BlockSpec((1, d, de), lambda i, be, nu: (be[i], 0, 0)),
                  pl.BlockSpec((1, de, d), lambda i, be, nu: (be[i], 0, 0))],
        out_specs=pl.BlockSpec((MOE_ROWS, d), lambda i, be, nu: (i, 0)),
        scratch_shapes=[pltpu.VMEM((d, de), BF16), pltpu.VMEM((d, de), BF16), pltpu.VMEM((de, d), BF16)],
    )
    return pl.pallas_call(
        _moe_kernel,
        grid_spec=grid_spec,
        out_shape=jax.ShapeDtypeStruct((size, d), BF16),
        compiler_params=_cparams(1),
        name="moe_ffn",
    )(block_e, n_used, xb, rw, w_gate, w_up, w_down)


def _moe(h2, scores, router_bias, w_gate, w_up, w_down):
    t, d = h2.shape
    sel = (scores + router_bias.astype(F32)).reshape(t, N_GROUPS, EXPERTS_PER_GROUP)
    pairs = [sel[..., i] + sel[..., j] for i in range(EXPERTS_PER_GROUP) for j in range(i + 1, EXPERTS_PER_GROUP)]
    grp_score = functools.reduce(jnp.maximum, pairs)
    g_idx = jnp.argmax(grp_score, axis=-1).astype(jnp.int32)
    g_hot = (g_idx[:, None] == jnp.arange(N_GROUPS, dtype=jnp.int32)[None, :])[:, :, None]
    in_grp = jnp.sum(jnp.where(g_hot, sel, 0.0), axis=1)
    sc_grp = jnp.sum(jnp.where(g_hot, scores.reshape(t, N_GROUPS, EXPERTS_PER_GROUP), 0.0), axis=1)
    lane4 = jnp.arange(EXPERTS_PER_GROUP, dtype=jnp.int32)[None, :]
    i1 = jnp.argmax(in_grp, axis=-1).astype(jnp.int32)
    i2 = jnp.argmax(jnp.where(lane4 == i1[:, None], -jnp.inf, in_grp), axis=-1).astype(jnp.int32)
    local = jnp.stack([i1, i2], axis=-1)
    e_idx = g_idx[:, None] * EXPERTS_PER_GROUP + local
    wts = jnp.stack([jnp.sum(jnp.where(lane4 == i1[:, None], sc_grp, 0.0), axis=-1),
                     jnp.sum(jnp.where(lane4 == i2[:, None], sc_grp, 0.0), axis=-1)], axis=-1)
    wts = wts / jnp.sum(wts, axis=-1, keepdims=True)
    a = t * TOP_K
    flat_e = e_idx.reshape(-1).astype(jnp.int32)
    flat_w = wts.reshape(-1)
    onehot = (flat_e[:, None] == jnp.arange(N_EXPERTS, dtype=jnp.int32)[None, :]).astype(jnp.int32)
    csum = jnp.cumsum(onehot, axis=0)
    counts = csum[-1]
    rank = jnp.sum(csum * onehot, axis=1) - 1
    padded = (counts + MOE_ROWS - 1) // MOE_ROWS * MOE_ROWS
    ends = jnp.cumsum(padded)
    pstarts = ends - padded
    dest = jnp.sum(pstarts[None, :] * onehot, axis=1) + rank
    n_blocks = -(-a // MOE_ROWS) + N_EXPERTS
    size = n_blocks * MOE_ROWS
    flat_tok = jnp.arange(a, dtype=jnp.int32) // TOP_K
    tok_buf = jnp.full((size,), t, jnp.int32).at[dest].set(flat_tok)
    w_buf = jnp.zeros((size,), F32).at[dest].set(flat_w)
    block_e = jnp.minimum(jnp.searchsorted(ends, jnp.arange(n_blocks) * MOE_ROWS, side='right'),
                          N_EXPERTS - 1).astype(jnp.int32)
    n_used = (ends[-1] // MOE_ROWS).astype(jnp.int32).reshape(1)
    h_pad = jnp.concatenate([h2, jnp.zeros((1, d), h2.dtype)], axis=0)
    xb = h_pad[tok_buf]
    yb = _moe_ffn(block_e, n_used, xb, w_buf[:, None], w_gate, w_up, w_down)
    return jnp.sum(yb[dest].reshape(t, TOP_K, d).astype(F32), axis=1)


def _head_rms(x, gain, e_ref):
    ms = jnp.dot((x * x).astype(BF16), e_ref[...], preferred_element_type=F32) * (1.0 / HEAD_DIM)
    return x * lax.rsqrt(ms + EPS) * gain


def _rope(y, cos, sin):
    lane = lax.broadcasted_iota(jnp.int32, y.shape, 1)
    half = HEAD_DIM // 2
    rot = jnp.where((lane % HEAD_DIM) < half, pltpu.roll(y, LANES - half, 1), pltpu.roll(y, half, 1))
    return y * cos + rot * sin


def _attend(q, tiles, sink):
    scores = []
    m = sink
    for k, _, bias in tiles:
        s = lax.dot_general(q, k, (((1,), (1,)), ((), ())), preferred_element_type=F32)
        if bias is not None:
            s = s + bias
        scores.append(s)
        mt = jnp.max(s, axis=-1, keepdims=True)
        m = mt if m is None else jnp.maximum(m, mt)
    den = jnp.exp(sink - m) if sink is not None else jnp.zeros_like(m)
    o = None
    for s, (_, v, _) in zip(scores, tiles):
        p = jnp.exp(s - m)
        den = den + jnp.sum(p, axis=-1, keepdims=True)
        pv = jnp.dot(p.astype(BF16), v, preferred_element_type=F32)
        o = pv if o is None else o + pv
    return o * (1.0 / den)


def _lane_half(shape):
    return lax.broadcasted_iota(jnp.int32, shape, 1) // HEAD_DIM


def _win_group_queries(slabs, g, sink_ref):
    r = WIN_HEADS // WIN_KV_HEADS
    rows = slabs[0].shape[0]
    half = _lane_half(slabs[0].shape)
    parts, sinks = [], []
    for j in range(r):
        head = g * r + j
        slab = slabs[head // 2]
        if head % 2 != g:
            slab = pltpu.roll(slab, HEAD_DIM, 1)
        parts.append(jnp.where(half == g, slab, 0.0))
        sinks.append(jnp.full((rows, 1), sink_ref[head], F32))
    return jnp.concatenate(parts, axis=0).astype(BF16), jnp.concatenate(sinks, axis=0)


def _win_store(o_ref, o, g, rows):
    r = WIN_HEADS // WIN_KV_HEADS
    half = _lane_half((rows, LANES))
    for pair in range(r // 2):
        a = o[(2 * pair) * rows:(2 * pair + 1) * rows]
        b = o[(2 * pair + 1) * rows:(2 * pair + 2) * rows]
        if g == 0:
            b = pltpu.roll(b, HEAD_DIM, 1)
        else:
            a = pltpu.roll(a, HEAD_DIM, 1)
        s = (g * r + 2 * pair) // 2
        o_ref[0, :, s * LANES:(s + 1) * LANES] = jnp.where(half == 0, a, b).astype(o_ref.dtype)


def _win_kernel(sink_ref, q_ref, kv_ref, ckv_ref, cos_ref, sin_ref, qg_ref, kg_ref, e_ref, o_ref, ks_ref, kcs_ref):
    n = pl.program_id(1)
    nb = pl.num_programs(1)
    l = kv_ref.shape[1]
    blk = WIN_BLOCK
    prep_rows = 512

    @pl.when(n == 0)
    def _():
        def body(i, carry):
            r0 = pl.multiple_of(i * prep_rows, prep_rows)
            y = _head_rms(kv_ref[0, pl.ds(r0, prep_rows), 0:LANES].astype(F32), kg_ref[...], e_ref)
            ks_ref[pl.ds(r0, prep_rows), :] = _rope(y, cos_ref[pl.ds(r0, prep_rows), :],
                                                    sin_ref[pl.ds(r0, prep_rows), :]).astype(BF16)
            return carry
        lax.fori_loop(0, l // prep_rows, body, 0)
        kcs_ref[...] = _head_rms(ckv_ref[0, :, 0:LANES].astype(F32), kg_ref[...], e_ref).astype(BF16)

    r0 = pl.multiple_of(n * blk, blk)
    cos = cos_ref[pl.ds(r0, blk), :]
    sin = sin_ref[pl.ds(r0, blk), :]
    slabs = []
    for s in range(q_ref.shape[-1] // LANES):
        y = _head_rms(q_ref[0, :, s * LANES:(s + 1) * LANES].astype(F32), qg_ref[...], e_ref)
        slabs.append(_rope(y, cos, sin))

    r = WIN_HEADS // WIN_KV_HEADS
    qi = lax.broadcasted_iota(jnp.int32, (r * blk, blk), 0) % blk
    kj = lax.broadcasted_iota(jnp.int32, (r * blk, blk), 1)
    ok_prev = (kj >= qi) & (n > 0)
    ok_next = (kj <= qi) & (n < nb - 1)
    bias_prev = jnp.where(ok_prev, 0.0, NEG_INF)
    bias_next = jnp.where(ok_next, 0.0, NEG_INF)
    tiles = []
    for kb, bias in ((jnp.maximum(n - 1, 0), bias_prev), (n, None), (jnp.minimum(n + 1, nb - 1), bias_next)):
        k0 = pl.multiple_of(kb * blk, blk)
        tiles.append((ks_ref[pl.ds(k0, blk), :], kv_ref[0, pl.ds(k0, blk), LANES:2 * LANES], bias))
    tiles.append((kcs_ref[...], ckv_ref[0, :, LANES:2 * LANES], None))
    for g in range(WIN_KV_HEADS):
        q, sink = _win_group_queries(slabs, g, sink_ref)
        _win_store(o_ref, _attend(q, tiles, sink), g, blk)


def _win_ctx_kernel(sink_ref, q_ref, ckv_ref, qg_ref, kg_ref, e_ref, o_ref):
    kc = _head_rms(ckv_ref[0, :, 0:LANES].astype(F32), kg_ref[...], e_ref).astype(BF16)
    tiles = [(kc, ckv_ref[0, :, LANES:2 * LANES], None)]
    slabs = [_head_rms(q_ref[0, :, s * LANES:(s + 1) * LANES].astype(F32), qg_ref[...], e_ref)
             for s in range(q_ref.shape[-1] // LANES)]
    for g in range(WIN_KV_HEADS):
        q, sink = _win_group_queries(slabs, g, sink_ref)
        _win_store(o_ref, _attend(q, tiles, sink), g, q_ref.shape[1])


def _block_diag_ones(width):
    i = np.arange(width) // HEAD_DIM
    return jnp.asarray(i[:, None] == i[None, :], BF16)


def _window_branch(zw_l, zw_c, q_gain, k_gain, sink, cos2, sin2, ctx_out):
    b, l, _ = zw_l.shape
    lc = zw_c.shape[1]
    wq = WIN_HEADS * HEAD_DIM
    qg = (jnp.tile(q_gain.astype(F32), 2) * HEAD_DIM ** -0.5).reshape(1, LANES)
    kg = jnp.tile(k_gain.astype(F32), 2).reshape(1, LANES)
    e = _block_diag_ones(LANES)
    smem = pl.BlockSpec(memory_space=pltpu.SMEM)
    full = lambda s, nd: pl.BlockSpec(s, lambda *a: (0,) * nd)
    yl = pl.pallas_call(
        _win_kernel,
        grid=(b, l // WIN_BLOCK),
        in_specs=[smem,
                  pl.BlockSpec((1, WIN_BLOCK, wq), lambda i, n: (i, n, 0)),
                  pl.BlockSpec((1, l, 2 * LANES), lambda i, n: (i, 0, wq // (2 * LANES))),
                  pl.BlockSpec((1, lc, 2 * LANES), lambda i, n: (i, 0, wq // (2 * LANES))),
                  full((l, LANES), 2), full((l, LANES), 2), full((1, LANES), 2), full((1, LANES), 2),
                  full((LANES, LANES), 2)],
        out_specs=pl.BlockSpec((1, WIN_BLOCK, wq), lambda i, n: (i, n, 0)),
        out_shape=jax.ShapeDtypeStruct((b, l, wq), BF16),
        scratch_shapes=[pltpu.VMEM((l, LANES), BF16), pltpu.VMEM((lc, LANES), BF16)],
        compiler_params=_cparams(2),
        name="window_attn",
    )(sink.astype(F32), zw_l, zw_l, zw_c, cos2, sin2, qg, kg, e)
    yc = None
    if ctx_out:
        yc = pl.pallas_call(
            _win_ctx_kernel,
            grid=(b,),
            in_specs=[smem,
                      pl.BlockSpec((1, lc, wq), lambda i: (i, 0, 0)),
                      pl.BlockSpec((1, lc, 2 * LANES), lambda i: (i, 0, wq // (2 * LANES))),
                      full((1, LANES), 2), full((1, LANES), 2), full((LANES, LANES), 2)],
            out_specs=pl.BlockSpec((1, lc, wq), lambda i: (i, 0, 0)),
            out_shape=jax.ShapeDtypeStruct((b, lc, wq), BF16),
            compiler_params=_cparams(1),
            name="window_ctx_attn",
        )(sink.astype(F32), zw_c, zw_c, qg, kg, e)
    return yc, yl


NA_ROWS = 8
NA_KROWS = NA_ROWS + NA_KH - 1


def _na_key_base(rg, rows):
    return jnp.clip(rg * NA_ROWS - NA_KH // 2, 0, rows - NA_KROWS)


def _na_kernel(q_ref, k_ref, v_ref, ck_ref, cv_ref, bias_ref, qg_ref, kg_ref, e_ref, o_ref, ks_ref, kcs_ref):
    rg = pl.program_id(1)
    l = k_ref.shape[1]
    rows = l // GRID_W
    n_slab = q_ref.shape[-1] // LANES
    prep_rows = 512

    @pl.when(rg == 0)
    def _():
        def body(i, carry):
            r0 = pl.multiple_of(i * prep_rows, prep_rows)
            ks_ref[pl.ds(r0, prep_rows), :] = _head_rms(k_ref[0, pl.ds(r0, prep_rows), :].astype(F32), kg_ref[...],
                                                        e_ref).astype(BF16)
            return carry
        lax.fori_loop(0, l // prep_rows, body, 0)
        kcs_ref[...] = _head_rms(ck_ref[0].astype(F32), kg_ref[...], e_ref).astype(BF16)

    nk = NA_KROWS * GRID_W
    k0 = pl.multiple_of(_na_key_base(rg, rows) * GRID_W, GRID_W)
    qn = _head_rms(q_ref[0].astype(F32), qg_ref[...], e_ref)
    half = _lane_half((q_ref.shape[1], LANES))
    for s in range(n_slab):
        cols = slice(s * LANES, (s + 1) * LANES)
        slab = qn[:, cols]
        tiles_kv = (ks_ref[pl.ds(k0, nk), cols], v_ref[0, pl.ds(k0, nk), cols])
        ctx_kv = (kcs_ref[:, cols], cv_ref[0, :, cols])
        outs = []
        for hh in range(2):
            q = jnp.where(half == hh, slab, 0.0).astype(BF16)
            bias = bias_ref[0, 2 * s + hh].astype(F32)
            outs.append(_attend(q, [tiles_kv + (bias,), ctx_kv + (None,)], None))
        o_ref[0, :, cols] = jnp.where(half == 0, outs[0], outs[1]).astype(o_ref.dtype)


def _na_ctx_kernel(q_ref, ck_ref, cv_ref, qg_ref, kg_ref, e_ref, o_ref):
    kc = _head_rms(ck_ref[0].astype(F32), kg_ref[...], e_ref).astype(BF16)
    qn = _head_rms(q_ref[0].astype(F32), qg_ref[...], e_ref)
    half = _lane_half((q_ref.shape[1], LANES))
    for s in range(q_ref.shape[-1] // LANES):
        cols = slice(s * LANES, (s + 1) * LANES)
        outs = []
        for hh in range(2):
            q = jnp.where(half == hh, qn[:, cols], 0.0).astype(BF16)
            outs.append(_attend(q, [(kc[:, cols], cv_ref[0, :, cols], None)], None))
        o_ref[0, :, cols] = jnp.where(half == 0, outs[0], outs[1]).astype(o_ref.dtype)


def _na_bias_classes(rows):
    n_rg = rows // NA_ROWS
    return list(range(n_rg)) if n_rg <= 3 else [0, 1, n_rg - 1]


def _na_bias_table(rpb, rows):
    kh = NA_KH
    ro, rv = [], []
    for rg in _na_bias_classes(rows):
        kbase = int(np.clip(rg * NA_ROWS - kh // 2, 0, rows - NA_KROWS))
        r = rg * NA_ROWS + np.arange(NA_ROWS)
        rstart = np.clip(r - kh // 2, 0, rows - kh)
        kr = kbase + np.arange(NA_KROWS)
        rv.append((kr[None, :] >= rstart[:, None]) & (kr[None, :] < rstart[:, None] + kh))
        ro.append(np.clip(kr[None, :] - r[:, None] + kh - 1, 0, 2 * kh - 2))
    ro, rv = np.stack(ro), np.stack(rv)
    qc = np.arange(GRID_W)
    cstart = np.clip(qc - NA_KW // 2, 0, GRID_W - NA_KW)
    cv = (qc[None, :] >= cstart[:, None]) & (qc[None, :] < cstart[:, None] + NA_KW)
    co = np.clip(qc[None, :] - qc[:, None] + NA_KW - 1, 0, 2 * NA_KW - 2)
    n_cls = ro.shape[0]
    hp = lax.Precision.HIGHEST
    co_hot = jnp.asarray(co[None] == np.arange(2 * NA_KW - 1)[:, None, None], F32)
    ro_hot = jnp.asarray(ro[..., None] == np.arange(2 * kh - 1), F32)
    cols = jnp.einsum('hdc,cwx->hdwx', rpb.astype(F32), co_hot, precision=hp)
    tab = jnp.einsum('crkd,hdwx->chrwkx', ro_hot, cols, precision=hp)
    ok = rv[:, None, :, None, :, None] & cv[None, None, None, :, None, :]
    tab = jnp.where(ok, tab, NEG_INF)
    return tab.reshape(n_cls, rpb.shape[0], NA_ROWS * GRID_W, NA_KROWS * GRID_W).astype(BF16)


def _neighborhood_branch(zn_l, zn_c, q_gain, k_gain, rpb, ctx_out):
    b, l, _ = zn_l.shape
    lc = zn_c.shape[1]
    w = NA_HEADS * HEAD_DIM
    rows = l // GRID_W
    n_rg = rows // NA_ROWS
    n_slab = w // LANES
    qg = (jnp.tile(q_gain.astype(F32), 2 * n_slab) * HEAD_DIM ** -0.5).reshape(1, w)
    kg = jnp.tile(k_gain.astype(F32), 2 * n_slab).reshape(1, w)
    e = _block_diag_ones(w)
    bias = _na_bias_table(rpb, rows)
    if n_rg <= 3:
        cls_map = lambda i, r: (r, 0, 0, 0)
    else:
        cls_map = lambda i, r: ((r > 0).astype(jnp.int32) + (r == n_rg - 1).astype(jnp.int32), 0, 0, 0)
    tq = NA_ROWS * GRID_W
    full = lambda s, nd: pl.BlockSpec(s, lambda *a: (0,) * nd)
    yl = pl.pallas_call(
        _na_kernel,
        grid=(b, n_rg),
        in_specs=[pl.BlockSpec((1, tq, w), lambda i, r: (i, r, 0)),
                  pl.BlockSpec((1, l, w), lambda i, r: (i, 0, 1)),
                  pl.BlockSpec((1, l, w), lambda i, r: (i, 0, 2)),
                  pl.BlockSpec((1, lc, w), lambda i, r: (i, 0, 1)),
                  pl.BlockSpec((1, lc, w), lambda i, r: (i, 0, 2)),
                  pl.BlockSpec((1,) + bias.shape[1:], cls_map),
                  full((1, w), 2), full((1, w), 2), full((w, w), 2)],
        out_specs=pl.BlockSpec((1, tq, w), lambda i, r: (i, r, 0)),
        out_shape=jax.ShapeDtypeStruct((b, l, w), BF16),
        scratch_shapes=[pltpu.VMEM((l, w), BF16), pltpu.VMEM((lc, w), BF16)],
        compiler_params=_cparams(2),
        name="neighborhood_attn",
    )(zn_l, zn_l, zn_l, zn_c, zn_c, bias, qg, kg, e)
    yc = None
    if ctx_out:
        yc = pl.pallas_call(
            _na_ctx_kernel,
            grid=(b,),
            in_specs=[pl.BlockSpec((1, lc, w), lambda i: (i, 0, 0)),
                      pl.BlockSpec((1, lc, w), lambda i: (i, 0, 1)),
                      pl.BlockSpec((1, lc, w), lambda i: (i, 0, 2)),
                      full((1, w), 2), full((1, w), 2), full((w, w), 2)],
            out_specs=pl.BlockSpec((1, lc, w), lambda i: (i, 0, 0)),
            out_shape=jax.ShapeDtypeStruct((b, lc, w), BF16),
            compiler_params=_cparams(1),
            name="neighborhood_ctx_attn",
        )(zn_c, zn_c, zn_c, qg, kg, e)
    return yc, yl


def _rope_tables(n_tokens):
    t = np.arange(n_tokens)
    n_freq = HEAD_DIM // 4
    inv = ROPE_THETA ** (-jnp.arange(n_freq, dtype=F32) / n_freq)
    ang = jnp.concatenate([jnp.asarray(t // GRID_W, F32)[:, None] * inv, jnp.asarray(t % GRID_W, F32)[:, None] * inv],
                          axis=-1)
    cos, sin = jnp.cos(ang), jnp.sin(ang)
    return jnp.tile(jnp.concatenate([cos, cos], -1), (1, 2)), jnp.tile(jnp.concatenate([-sin, sin], -1), (1, 2))


def _ret_kernel(qq_l, kk_l, v_l, g_l, qq_c, kk_c, v_c, g_c, cos_ref, sin_ref, dmask_ref, tq_ref, tk_ref, cdec_ref,
                gain_ref, yl_ref, yc_ref, kr_ref, kv_ref, sin_state_ref, *, ctx_out):
    c = RET_CHUNK
    ncc = qq_c.shape[1] // c
    ncl = qq_l.shape[1] // c
    nc = ncc + ncl
    tk = tk_ref[0]
    tq = tq_ref[0]
    dmask = dmask_ref[0]
    fwd_lanes = lax.broadcasted_iota(jnp.int32, (c, LANES), 1) < RET_DK

    def chunk_kv(k2, v):
        kd = (k2 * tk).astype(BF16)
        return lax.dot_general(kd, v, (((0,), (0,)), ((), ())), preferred_element_type=F32)

    for n in range(ncc):
        kv_ref[n] = chunk_kv(kk_c[0, n * c:(n + 1) * c, :].astype(F32), v_c[0, n * c:(n + 1) * c, :])

    def kv_body(n, carry):
        r0 = pl.multiple_of(n * c, c)
        k2 = _rope(kk_l[0, pl.ds(r0, c), :].astype(F32), cos_ref[pl.ds(r0, c), :], sin_ref[pl.ds(r0, c), :])
        kr_ref[pl.ds(r0, c), :] = k2.astype(BF16)
        kv_ref[ncc + n] = chunk_kv(k2, v_l[0, pl.ds(r0, c), :])
        return carry
    lax.fori_loop(0, ncl, kv_body, 0)

    dec_f = cdec_ref[0, 0:1, :]
    dec_b = cdec_ref[0, 1:2, :]

    def scan_body(t, carry):
        sf, sb = carry
        sin_state_ref[t, 0:RET_DK, :] = sf.astype(BF16)
        sf = sf * dec_f + kv_ref[t, 0:RET_DK, :]
        tb = jnp.where(t < ncc, ncc - 1 - t, nc - 1 - (t - ncc))
        sin_state_ref[tb, RET_DK:2 * RET_DK, :] = sb.astype(BF16)
        sb = sb * dec_b + kv_ref[tb, RET_DK:2 * RET_DK, :]
        return sf, sb
    zero = jnp.zeros((RET_DK, RET_DV), F32)
    lax.fori_loop(0, nc, scan_body, (zero, zero))

    def chunk_out(q2, k2b, v, gate, state):
        qm = jnp.where(fwd_lanes, q2, 0.0).astype(BF16)
        scores = lax.dot_general(qm, k2b, (((1,), (1,)), ((), ())), preferred_element_type=F32) * dmask
        o = (jnp.dot(scores.astype(BF16), v, preferred_element_type=F32)
             + jnp.dot((q2 * tq).astype(BF16), state, preferred_element_type=F32))
        mu = jnp.mean(o, axis=-1, keepdims=True)
        var = jnp.mean(jnp.square(o - mu), axis=-1, keepdims=True)
        y = (o - mu) * lax.rsqrt(var + EPS) * gain_ref[0]
        gf = gate.astype(F32)
        return y * gf * jax.nn.sigmoid(gf)

    if ctx_out:
        for n in range(ncc):
            rows = slice(n * c, (n + 1) * c)
            yc_ref[0, rows, :] = chunk_out(qq_c[0, rows, :].astype(F32), kk_c[0, rows, :], v_c[0, rows, :],
                                           g_c[0, rows, :], sin_state_ref[n]).astype(yc_ref.dtype)
    else:
        yc_ref[...] = jnp.zeros_like(yc_ref)

    def out_body(n, carry):
        r0 = pl.multiple_of(n * c, c)
        q2 = _rope(qq_l[0, pl.ds(r0, c), :].astype(F32), cos_ref[pl.ds(r0, c), :], sin_ref[pl.ds(r0, c), :])
        yl_ref[0, pl.ds(r0, c), :] = chunk_out(q2, kr_ref[pl.ds(r0, c), :], v_l[0, pl.ds(r0, c), :],
                                               g_l[0, pl.ds(r0, c), :], sin_state_ref[ncc + n]).astype(yl_ref.dtype)
        return carry
    lax.fori_loop(0, ncl, out_body, 0)


def _ret_tables(decay_logit):
    c = RET_CHUNK
    lg = jax.nn.log_sigmoid(decay_logit.astype(F32))
    lf, lb = lg[0][:, None, None], lg[1][:, None, None]
    pos = jnp.arange(c, dtype=F32)
    diff = pos[:, None] - pos[None, :]
    dmask = (jnp.where(diff >= 0, jnp.exp(lf * jnp.maximum(diff, 0.0)), 0.0)
             + jnp.where(diff <= 0, jnp.exp(lb * jnp.maximum(-diff, 0.0)), 0.0)) * RET_DK ** -0.5
    col = lambda a, b_: jnp.concatenate([jnp.broadcast_to(a, a.shape[:2] + (RET_DK,)),
                                         jnp.broadcast_to(b_, b_.shape[:2] + (RET_DK,))], axis=-1)
    p = pos[None, :, None]
    tq = col(jnp.exp(lf * (p + 1.0)), jnp.exp(lb * (c - p)))
    tk = col(jnp.exp(lf * (c - 1.0 - p)), jnp.exp(lb * p)) * RET_DK ** -0.5
    cdec = jnp.zeros((lg.shape[1], 8, RET_DV), F32)
    cdec = cdec.at[:, 0, :].set(jnp.exp(lg[0] * c)[:, None]).at[:, 1, :].set(jnp.exp(lg[1] * c)[:, None])
    return dmask, tq, tk, cdec


def _retention_branch(zr_l, zr_c, decay_logit, gn_gain, cos2, sin2, ctx_out):
    b, l, _ = zr_l.shape
    lc = zr_c.shape[1]
    h = RET_HEADS
    nc = (l + lc) // RET_CHUNK
    dmask, tq, tk, cdec = _ret_tables(decay_logit)
    gain = gn_gain.astype(F32).reshape(h, 1, RET_DV)
    seq = lambda n, j: pl.BlockSpec((1, n, LANES), lambda i, hh: (i, 0, 4 * hh + j))
    head = lambda s: pl.BlockSpec((1,) + s, lambda i, hh: (hh, 0, 0))
    full = lambda s: pl.BlockSpec(s, lambda i, hh: (0, 0))
    yl, yc = pl.pallas_call(
        functools.partial(_ret_kernel, ctx_out=ctx_out),
        grid=(b, h),
        in_specs=[seq(l, 0), seq(l, 1), seq(l, 2), seq(l, 3), seq(lc, 0), seq(lc, 1), seq(lc, 2), seq(lc, 3),
                  full((l, LANES)), full((l, LANES)),
                  head((RET_CHUNK, RET_CHUNK)), head((RET_CHUNK, LANES)), head((RET_CHUNK, LANES)), head((8, RET_DV)),
                  head((1, RET_DV))],
        out_specs=[pl.BlockSpec((1, l, RET_DV), lambda i, hh: (i, 0, hh)),
                   pl.BlockSpec((1, lc, RET_DV), lambda i, hh: (i, 0, hh))],
        out_shape=[jax.ShapeDtypeStruct((b, l, h * RET_DV), BF16), jax.ShapeDtypeStruct((b, lc, h * RET_DV), BF16)],
        scratch_shapes=[pltpu.VMEM((l, LANES), BF16), pltpu.VMEM((nc, 2 * RET_DK, RET_DV), F32),
                        pltpu.VMEM((nc, 2 * RET_DK, RET_DV), BF16)],
        compiler_params=_cparams(2),
        name="retention",
    )(zr_l, zr_l, zr_l, zr_l, zr_c, zr_c, zr_c, zr_c, cos2, sin2, dmask, tq, tk, cdec, gain)
    return (yc if ctx_out else None), yl


GDN_SUPER = 256
GDN_HALO = 128


def _split_bf16(a):
    hi = a.astype(BF16)
    return hi, (a - hi.astype(F32)).astype(BF16)


def _mask_dot(mask_bf16, a):
    ah, al = _split_bf16(a)
    return jnp.dot(mask_bf16, ah, preferred_element_type=F32) + jnp.dot(mask_bf16, al, preferred_element_type=F32)


def _softplus(x):
    return jnp.maximum(x, 0.0) + jnp.log(1.0 + jnp.exp(-jnp.abs(x)))


def _gdn_kernel(nega_ref, dtb_ref, q_l, k_l, v_l, g_l, ab_l, abt_l, q_c, k_c, v_c, g_c, ab_c, abt_c,
                cwq_ref, cwk_ref, cwv_ref, pd_ref, pu_ref, gain_ref, yl_ref, yc_ref,
                kn_s, sin_s, qp_s, o0_s, cd_s, *, ctx_out):
    hd = pl.program_id(1)
    c = GDN_CHUNK
    sup = GDN_SUPER
    per = sup // c
    lc, l = q_c.shape[1], q_l.shape[1]
    ncc, ncl = lc // c, l // c
    nc = ncc + ncl

    ri = lax.broadcasted_iota(jnp.int32, (sup, sup), 0)
    ci = lax.broadcasted_iota(jnp.int32, (sup, sup), 1)
    same = (ri // c) == (ci // c)
    eye = (ri == ci).astype(F32)
    incl = (same & (ri >= ci), same & (ri <= ci))
    strict = (same & (ri > ci), same & (ri < ci))
    incl_b = tuple(m.astype(BF16) for m in incl)
    same_b = same.astype(BF16)
    lane = lax.broadcasted_iota(jnp.int32, (sup, LANES), 1)
    sub16 = lax.broadcasted_iota(jnp.int32, (16, sup), 0)
    rowc = lax.broadcasted_iota(jnp.int32, (sup, LANES), 0)

    def conv_silu(z_ref, w_ref, r0, ls, static_edges):
        z = z_ref[0, pl.ds(r0, sup), :]
        if static_edges:
            prev = jnp.zeros((GDN_HALO, LANES), BF16)
            nxt = prev
        else:
            p0 = pl.multiple_of(jnp.maximum(r0 - GDN_HALO, 0), GDN_HALO)
            n0 = pl.multiple_of(jnp.minimum(r0 + sup, ls - GDN_HALO), GDN_HALO)
            prev = jnp.where(r0 > 0, z_ref[0, pl.ds(p0, GDN_HALO), :], jnp.zeros((), BF16))
            nxt = jnp.where(r0 + sup < ls, z_ref[0, pl.ds(n0, GDN_HALO), :], jnp.zeros((), BF16))
        win = jnp.concatenate([prev, z, nxt], axis=0)
        z_dn = jnp.dot(pd_ref[...], win, preferred_element_type=F32)
        z_up = jnp.dot(pu_ref[...], win, preferred_element_type=F32)
        y = z_dn * w_ref[0, 0:1, :] + z.astype(F32) * w_ref[0, 1:2, :] + z_up * w_ref[0, 2:3, :]
        return y * jax.nn.sigmoid(y)

    def prep(refs, r0, base, ls, static_edges):
        q_ref, k_ref, v_ref, ab_ref, abt_ref = refs
        q = conv_silu(q_ref, cwq_ref, r0, ls, static_edges)
        k = conv_silu(k_ref, cwk_ref, r0, ls, static_edges)
        v = conv_silu(v_ref, cwv_ref, r0, ls, static_edges)
        q = q * lax.rsqrt(jnp.sum(q * q, axis=-1, keepdims=True) + EPS) * GDN_DK ** -0.5
        k = k * lax.rsqrt(jnp.sum(k * k, axis=-1, keepdims=True) + EPS)
        qb, kb16 = q.astype(BF16), k.astype(BF16)
        qk = lax.dot_general(qb, kb16, (((1,), (1,)), ((), ())), preferred_element_type=F32)
        ab = ab_ref[0, pl.ds(r0, sup), :]
        abt = abt_ref[0, r0 // sup] if not static_edges else abt_ref[0, 0]
        for d in range(2):
            ia = d * GDN_HEADS + hd
            ib = 2 * GDN_HEADS + ia
            a_col = jnp.sum(jnp.where(lane == ia, ab, 0.0), axis=1, keepdims=True)
            b_col = jnp.sum(jnp.where(lane == ib, ab, 0.0), axis=1, keepdims=True)
            a_row = jnp.sum(jnp.where(sub16 == ia, abt, 0.0), axis=0, keepdims=True)
            g_col = nega_ref[d, hd] * _softplus(a_col + dtb_ref[d, hd])
            g_row = nega_ref[d, hd] * _softplus(a_row + dtb_ref[d, hd])
            beta = jax.nn.sigmoid(b_col)
            gcum = _mask_dot(incl_b[d], jnp.broadcast_to(g_col, (sup, LANES)))
            gtot = _mask_dot(same_b, jnp.broadcast_to(g_col, (sup, LANES)))
            gr_hi, gr_lo = _split_bf16(jnp.broadcast_to(g_row, (8, sup)))
            grow = (jnp.dot(gr_hi, incl_b[1 - d], preferred_element_type=F32)
                    + jnp.dot(gr_lo, incl_b[1 - d], preferred_element_type=F32))[0:1, :]
            gc = gcum[:, 0:1]
            decay = jnp.where(incl[d], jnp.exp(jnp.where(incl[d], gc - grow, 0.0)), 0.0)
            kbeta = k * beta
            kk = lax.dot_general(kbeta.astype(BF16), kb16, (((1,), (1,)), ((), ())), preferred_element_type=F32)
            a_mat = jnp.where(strict[d], kk * decay, 0.0)
            inv = eye - a_mat
            pw = a_mat
            for _ in range(5):
                pwb = pw.astype(BF16)
                pw = jnp.dot(pwb, pwb, preferred_element_type=F32)
                inv = inv + jnp.dot(inv.astype(BF16), pw.astype(BF16), preferred_element_type=F32)
            eg = jnp.exp(gc)
            rhs = jnp.concatenate([kbeta * eg, v * beta], axis=1)
            wu = jnp.dot(inv.astype(BF16), rhs.astype(BF16), preferred_element_type=F32).astype(BF16)
            aqk = jnp.where(incl[d], qk * decay, 0.0).astype(BF16)
            awu = jnp.dot(aqk, wu, preferred_element_type=F32)
            rows = pl.ds(base + r0, sup)
            qp_s[d, rows, :] = (q * eg - awu[:, :GDN_DK]).astype(BF16)
            o0_s[d, rows, :] = awu[:, GDN_DK:]
            ke = (k * jnp.exp(gtot[:, 0:1] - gc)).astype(BF16)
            for j in range(per):
                kej = jnp.where((rowc // c) == j, ke, jnp.zeros((), BF16))
                idx = (base + r0) // c + j
                kn_s[d, idx] = lax.dot_general(kej, wu, (((0,), (0,)), ((), ())),
                                               preferred_element_type=F32).astype(BF16)
                cd_s[d, idx] = jnp.exp(gtot[j * c:j * c + 1, :])

    for n in range(lc // sup):
        prep((q_c, k_c, v_c, ab_c, abt_c), n * sup, 0, lc, lc == sup)

    def prep_body(n, carry):
        prep((q_l, k_l, v_l, ab_l, abt_l), pl.multiple_of(n * sup, sup), lc, l, False)
        return carry
    lax.fori_loop(0, l // sup, prep_body, 0)

    def chunk_step(d, idx, s):
        sb = s.astype(BF16)
        sin_s[d, idx] = sb
        kn = kn_s[d, idx]
        return (s * cd_s[d, idx] - jnp.dot(kn[:, :GDN_DK], sb, preferred_element_type=F32)
                + kn[:, GDN_DK:].astype(F32))

    def scan_body(t, carry):
        sf, sb = carry
        tb = jnp.where(t < ncc, ncc - 1 - t, nc - 1 - (t - ncc))
        return chunk_step(0, t, sf), chunk_step(1, tb, sb)
    zero = jnp.zeros((GDN_DK, GDN_DV), F32)
    lax.fori_loop(0, nc, scan_body, (zero, zero))

    def finish(y_ref, gate_ref, base, n_rows):
        def body(n, carry):
            r0 = pl.multiple_of(n * c, c)
            rows = pl.ds(base + r0, c)
            idx = base // c + n
            o = (o0_s[0, rows, :] + o0_s[1, rows, :]
                 + jnp.dot(qp_s[0, rows, :], sin_s[0, idx], preferred_element_type=F32)
                 + jnp.dot(qp_s[1, rows, :], sin_s[1, idx], preferred_element_type=F32))
            y = o * lax.rsqrt(jnp.mean(o * o, axis=-1, keepdims=True) + EPS) * gain_ref[...]
            gf = gate_ref[0, pl.ds(r0, c), :].astype(F32)
            y_ref[0, pl.ds(r0, c), :] = (y * gf * jax.nn.sigmoid(gf)).astype(y_ref.dtype)
            return carry
        lax.fori_loop(0, n_rows // c, body, 0)

    finish(yl_ref, g_l, lc, l)
    if ctx_out:
        finish(yc_ref, g_c, 0, lc)
    else:
        yc_ref[...] = jnp.zeros_like(yc_ref)


def _gdn_branch(zg_l, zab_l, zg_c, zab_c, conv_w, a_log, dt_bias, norm_gain, ctx_out):
    b, l, _ = zg_l.shape
    lc = zg_c.shape[1]
    h = GDN_HEADS
    sup = GDN_SUPER
    ltot = l + lc
    nc = ltot // GDN_CHUNK
    neg_a = -jnp.exp(a_log.astype(F32))
    cw = conv_w.astype(F32).T.reshape(3 * h, LANES, SHORT_CONV).transpose(0, 2, 1)
    win = sup + 2 * GDN_HALO
    i = np.arange(sup)
    pd = np.zeros((sup, win), np.float32)
    pu = np.zeros((sup, win), np.float32)
    pd[i, GDN_HALO + i - 1] = 1.0
    pu[i, GDN_HALO + i + 1] = 1.0
    abt = lambda z: z[..., :16].reshape(b, z.shape[1] // sup, sup, 16).transpose(0, 1, 3, 2)
    seq = lambda n, j: pl.BlockSpec((1, n, LANES), lambda bi, hh: (bi, 0, j * h + hh))
    abs_ = lambda n: pl.BlockSpec((1, n, LANES), lambda bi, hh: (bi, 0, 0))
    abts = lambda n: pl.BlockSpec((1, n // sup, 16, sup), lambda bi, hh: (bi, 0, 0, 0))
    cws = lambda j: pl.BlockSpec((1, SHORT_CONV, LANES), lambda bi, hh: (j * h + hh, 0, 0))
    full = lambda s: pl.BlockSpec(s, lambda bi, hh: (0, 0))
    smem = pl.BlockSpec(memory_space=pltpu.SMEM)
    yl, yc = pl.pallas_call(
        functools.partial(_gdn_kernel, ctx_out=ctx_out),
        grid=(b, h),
        in_specs=[smem, smem, seq(l, 0), seq(l, 1), seq(l, 2), seq(l, 3), abs_(l), abts(l),
                  seq(lc, 0), seq(lc, 1), seq(lc, 2), seq(lc, 3), abs_(lc), abts(lc),
                  cws(0), cws(1), cws(2), full((sup, win)), full((sup, win)), full((1, GDN_DV))],
        out_specs=[pl.BlockSpec((1, l, GDN_DV), lambda bi, hh: (bi, 0, hh)),
                   pl.BlockSpec((1, lc, GDN_DV), lambda bi, hh: (bi, 0, hh))],
        out_shape=[jax.ShapeDtypeStruct((b, l, h * GDN_DV), BF16), jax.ShapeDtypeStruct((b, lc, h * GDN_DV), BF16)],
        scratch_shapes=[pltpu.VMEM((2, nc, GDN_DK, GDN_DK + GDN_DV), BF16), pltpu.VMEM((2, nc, GDN_DK, GDN_DV), BF16),
                        pltpu.VMEM((2, ltot, GDN_DK), BF16), pltpu.VMEM((2, ltot, GDN_DV), F32),
                        pltpu.VMEM((2, nc, 1, GDN_DV), F32)],
        compiler_params=_cparams(2),
        name="gated_deltanet",
    )(neg_a, dt_bias.astype(F32), zg_l, zg_l, zg_l, zg_l, zab_l, abt(zab_l), zg_c, zg_c, zg_c, zg_c, zab_c,
      abt(zab_c), cw, cw, cw, jnp.asarray(pd, BF16), jnp.asarray(pu, BF16), norm_gain.astype(F32).reshape(1, GDN_DV))
    return (yc if ctx_out else None), yl


def kernel(x, c, ctx, c_ctx, w_mod, b_mod, norm1, norm2, w_in, ret_decay, ret_gn, win_qnorm, win_knorm, win_sink,
           na_qnorm, na_knorm, na_rpb, gdn_conv, gdn_a_log, gdn_dt_bias, gdn_norm, w_branch, w_merge, w_out,
           w_router, router_bias, w_e_gate, w_e_up, w_e_down):
    b, l, d = x.shape
    lc = ctx.shape[1]
    depth = w_mod.shape[0]
    cos2, sin2 = _rope_tables(l)

    n_rows = 16
    cc = jnp.zeros((n_rows, d), F32).at[:b].set(c).at[b].set(c_ctx)
    mod = _modulation(cc, w_mod, b_mod).reshape(depth, n_rows, 6, d)

    wr = jnp.zeros((d, LANES), F32).at[:, :N_EXPERTS].set(w_router)
    wr_hi = wr.astype(BF16)
    wr_lo = (wr - wr_hi.astype(F32)).astype(BF16)

    xl, xc = x, ctx
    for layer in range(depth):
        ctx_out = layer < depth - 1
        mod_l = mod[layer, :b]
        mod_c = mod[layer, b:b + 1]
        w_all = _pack_w_in(w_in[layer])
        hl, *zl_s = _inproj(xl, mod_l, False, norm1[layer], w_all, 256)
        hc, *zc_s = _inproj(xc, mod_c, True, norm1[layer], w_all, 256)
        zr_l, zw_l, zn_l, zg_l, zab_l = zl_s
        zr_c, zw_c, zn_c, zg_c, zab_c = zc_s
        ret_c, ret_l = _retention_branch(zr_l, zr_c, ret_decay[layer], ret_gn[layer], cos2, sin2, ctx_out)
        win_c, win_l = _window_branch(zw_l, zw_c, win_qnorm[layer], win_knorm[layer], win_sink[layer], cos2, sin2,
                                      ctx_out)
        na_c, na_l = _neighborhood_branch(zn_l, zn_c, na_qnorm[layer], na_knorm[layer], na_rpb[layer], ctx_out)
        gdn_c, gdn_l = _gdn_branch(zg_l, zab_l, zg_c, zab_c, gdn_conv[layer], gdn_a_log[layer], gdn_dt_bias[layer],
                                   gdn_norm[layer], ctx_out)
        wm = w_merge[layer].astype(BF16)
        wb = w_branch[layer].astype(BF16)
        wo = w_out[layer].astype(BF16)
        ys_l = [ret_l, win_l, na_l, gdn_l]
        xl, h2l, sc_l = _merge(xl, hl, ys_l, mod_l, False, norm2[layer], wm, wb, wo, wr_hi, wr_lo, 256)
        g2l = mod_l[:, 5][:, None, :]
        if ctx_out:
            ys_c = [ret_c, win_c, na_c, gdn_c]
            xc, h2c, sc_c = _merge(xc, hc, ys_c, mod_c, True, norm2[layer], wm, wb, wo, wr_hi, wr_lo, 256)
            g2c = mod_c[:, 5][:, None, :]
            tokens = jnp.concatenate([h2c.reshape(b * lc, d), h2l.reshape(b * l, d)], axis=0)
            scores = jnp.concatenate([sc_c.reshape(b * lc, LANES), sc_l.reshape(b * l, LANES)], axis=0)
            y = _moe(tokens, scores[:, :N_EXPERTS], router_bias, w_e_gate[layer], w_e_up[layer], w_e_down[layer])
            xc = xc + g2c * y[:b * lc].reshape(b, lc, d)
            xl = xl + g2l * y[b * lc:].reshape(b, l, d)
        else:
            y = _moe(h2l.reshape(b * l, d), sc_l.reshape(b * l, LANES)[:, :N_EXPERTS], router_bias,
                     w_e_gate[layer], w_e_up[layer], w_e_down[layer])
            xl = xl + g2l * y.reshape(b, l, d)
    return xl
```

```python
import functools

import numpy as np
import jax
import jax.numpy as jnp
from jax import lax
from jax.experimental import pallas as pl
from jax.experimental.pallas import tpu as pltpu

F32 = jnp.float32
BF16 = jnp.bfloat16
EPS = 1e-6
NEG_INF = -1e30
D_MODEL = 1024
GRID_W = 64
HEAD_DIM = 64
ROPE_THETA = 10000.0
RET_HEADS, RET_DK, RET_DV, RET_CHUNK = 4, 64, 128, 128
WIN_HEADS, WIN_KV_HEADS, WINDOW, WIN_BLOCK = 8, 2, 128, 128
NA_HEADS, NA_KH, NA_KW, NA_QCOLS = 8, 8, 16, 16
NA_BAND = NA_QCOLS + NA_KW
GDN_HEADS, GDN_DK, GDN_DV, GDN_CHUNK, SHORT_CONV = 4, 128, 128, 64, 3
GDN_QKV = 2 * GDN_HEADS * GDN_DK + GDN_HEADS * GDN_DV
N_BRANCH, BRANCH_W = 4, 512
N_EXPERTS, N_GROUPS, TOP_K, D_EXPERT = 32, 8, 2, 512
EXPERTS_PER_GROUP = N_EXPERTS // N_GROUPS

LANES = 128
VMEM_LIMIT = 56 * 1024 * 1024
MOE_ROWS = 256

W_RET = RET_HEADS * 4 * LANES
W_WIN = (WIN_HEADS + 2 * WIN_KV_HEADS) * HEAD_DIM
W_NA = 3 * NA_HEADS * HEAD_DIM
W_GDN = GDN_QKV + GDN_HEADS * GDN_DV
W_AB = LANES
SECTION_WIDTHS = (W_RET, W_WIN, W_NA, W_GDN, W_AB)
W_ALL = sum(SECTION_WIDTHS)


def _cparams(n_axes):
    return pltpu.CompilerParams(dimension_semantics=("arbitrary",) * n_axes, vmem_limit_bytes=VMEM_LIMIT)


def _mod_kernel(c_ref, w_ref, b_ref, o_ref):
    c = c_ref[...]
    a = (c * jax.nn.sigmoid(c)).astype(BF16)
    o_ref[0] = jnp.dot(a, w_ref[0].astype(BF16), preferred_element_type=F32) + b_ref[0]


def _modulation(cc, w_mod, b_mod):
    depth, d, n = w_mod.shape
    r = cc.shape[0]
    tn = 1536
    return pl.pallas_call(
        _mod_kernel,
        grid=(depth, n // tn),
        in_specs=[pl.BlockSpec((r, d), lambda l, j: (0, 0)),
                  pl.BlockSpec((1, d, tn), lambda l, j: (l, 0, j)),
                  pl.BlockSpec((1, 1, tn), lambda l, j: (l, 0, j))],
        out_specs=pl.BlockSpec((1, r, tn), lambda l, j: (l, 0, j)),
        out_shape=jax.ShapeDtypeStruct((depth, r, n), F32),
        compiler_params=_cparams(2),
        name="modulation",
    )(cc, w_mod, b_mod.reshape(depth, 1, n))


def _inproj_kernel(x_ref, mod_ref, gain_ref, w_ref, h_ref, *z_refs):
    x = x_ref[0]
    ms = jnp.mean(x * x, axis=-1, keepdims=True)
    shift = mod_ref[0, 0:1, :]
    scale = mod_ref[0, 1:2, :]
    h = x * lax.rsqrt(ms + EPS) * gain_ref[...] * (1.0 + scale) + shift
    hb = h.astype(BF16)
    h_ref[0] = hb
    off = 0
    for ref in z_refs:
        width = ref.shape[-1]
        for c0 in range(0, width, 512):
            c1 = min(c0 + 512, width)
            z = jnp.dot(hb, w_ref[:, off + c0:off + c1], preferred_element_type=F32)
            ref[0, :, c0:c1] = z.astype(ref.dtype)
        off += width


def _inproj(x, mod, mod_is_shared, gain, w_all, tm):
    b, l, d = x.shape
    mod_map = (lambda i, j: (0, 0, 0)) if mod_is_shared else (lambda i, j: (i, 0, 0))
    dtypes = (BF16, BF16, BF16, BF16, F32)
    out_shape = [jax.ShapeDtypeStruct((b, l, d), BF16)]
    out_specs = [pl.BlockSpec((1, tm, d), lambda i, j: (i, j, 0))]
    for w, dt in zip(SECTION_WIDTHS, dtypes):
        out_shape.append(jax.ShapeDtypeStruct((b, l, w), dt))
        out_specs.append(pl.BlockSpec((1, tm, w), lambda i, j: (i, j, 0)))
    return pl.pallas_call(
        _inproj_kernel,
        grid=(b, l // tm),
        in_specs=[pl.BlockSpec((1, tm, d), lambda i, j: (i, j, 0)),
                  pl.BlockSpec((1, 6, d), mod_map),
                  pl.BlockSpec((1, d), lambda i, j: (0, 0)),
                  pl.BlockSpec((d, W_ALL), lambda i, j: (0, 0))],
        out_specs=out_specs,
        out_shape=out_shape,
        compiler_params=_cparams(2),
        name="inproj",
    )(x, mod, gain.reshape(1, d), w_all)


def _pack_w_in(w_in):
    d = w_in.shape[0]
    hq, hv = RET_HEADS * RET_DK, RET_HEADS * RET_DV
    cols = []
    for h in range(RET_HEADS):
        q = w_in[:, h * RET_DK:(h + 1) * RET_DK]
        k = w_in[:, hq + h * RET_DK:hq + (h + 1) * RET_DK]
        cols += [q, q, k, k, w_in[:, 2 * hq + h * RET_DV:2 * hq + (h + 1) * RET_DV],
                 w_in[:, 2 * hq + hv + h * RET_DV:2 * hq + hv + (h + 1) * RET_DV]]
    rest = w_in[:, 2 * hq + 2 * hv:]
    pad = jnp.zeros((d, W_ALL - W_RET - rest.shape[1]), w_in.dtype)
    return jnp.concatenate(cols + [rest, pad], axis=1).astype(BF16)


def _merge_kernel(x_ref, h_ref, y0_ref, y1_ref, y2_ref, y3_ref, mod_ref, gain_ref, wm_ref, wb_ref, wo_ref,
                  wrh_ref, wrl_ref, xo_ref, h2_ref, sc_ref):
    d = x_ref.shape[-1]
    h = h_ref[0]
    acc = jnp.zeros(x_ref.shape[1:], F32)
    for i, y_ref in enumerate((y0_ref, y1_ref, y2_ref, y3_ref)):
        gate = jax.nn.sigmoid(jnp.dot(h, wm_ref[:, i * d:(i + 1) * d], preferred_element_type=F32))
        acc = acc + gate * jnp.dot(y_ref[0], wb_ref[i], preferred_element_type=F32)
    m = jnp.dot(acc.astype(BF16), wo_ref[...], preferred_element_type=F32)
    xn = x_ref[0] + mod_ref[0, 2:3, :] * m
    xo_ref[0] = xn
    ms = jnp.mean(xn * xn, axis=-1, keepdims=True)
    h2 = xn * lax.rsqrt(ms + EPS) * gain_ref[...] * (1.0 + mod_ref[0, 4:5, :]) + mod_ref[0, 3:4, :]
    hi = h2.astype(BF16)
    h2_ref[0] = hi
    lo = (h2 - hi.astype(F32)).astype(BF16)
    logits = (jnp.dot(hi, wrh_ref[...], preferred_element_type=F32)
              + jnp.dot(lo, wrh_ref[...], preferred_element_type=F32)
              + jnp.dot(hi, wrl_ref[...], preferred_element_type=F32))
    sc_ref[0] = jax.nn.sigmoid(logits)


def _merge(x, h, ys, mod, mod_is_shared, gain2, wm, wb, wo, wr_hi, wr_lo, tm):
    b, l, d = x.shape
    mod_map = (lambda i, j: (0, 0, 0)) if mod_is_shared else (lambda i, j: (i, 0, 0))
    tok = lambda w: pl.BlockSpec((1, tm, w), lambda i, j: (i, j, 0))
    full2 = lambda s: pl.BlockSpec(s, lambda i, j: (0, 0))
    return pl.pallas_call(
        _merge_kernel,
        grid=(b, l // tm),
        in_specs=[tok(d), tok(d)] + [tok(BRANCH_W)] * 4 + [
            pl.BlockSpec((1, 6, d), mod_map), full2((1, d)), full2(wm.shape),
            pl.BlockSpec(wb.shape, lambda i, j: (0, 0, 0)), full2(wo.shape), full2(wr_hi.shape), full2(wr_lo.shape)],
        out_specs=[tok(d), tok(d), tok(LANES)],
        out_shape=[jax.ShapeDtypeStruct((b, l, d), F32), jax.ShapeDtypeStruct((b, l, d), BF16),
                   jax.ShapeDtypeStruct((b, l, LANES), F32)],
        compiler_params=_cparams(2),
        name="merge",
    )(x, h, *ys, mod, gain2.reshape(1, d), wm, wb, wo, wr_hi, wr_lo)


def _moe_kernel(be_ref, nu_ref, xb_ref, rw_ref, wg_ref, wu_ref, wd_ref, o_ref, wg_s, wu_s, wd_s):
    i = pl.program_id(0)
    e = be_ref[i]
    prev = be_ref[jnp.maximum(i - 1, 0)]
    used = i < nu_ref[0]

    @pl.when(used & ((i == 0) | (e != prev)))
    def _():
        wg_s[...] = wg_ref[0].astype(BF16)
        wu_s[...] = wu_ref[0].astype(BF16)
        wd_s[...] = wd_ref[0].astype(BF16)

    @pl.when(used)
    def _():
        x = xb_ref[...]
        g = jnp.dot(x, wg_s[...], preferred_element_type=F32)
        u = jnp.dot(x, wu_s[...], preferred_element_type=F32)
        a = (g * jax.nn.sigmoid(g) * u).astype(BF16)
        y = jnp.dot(a, wd_s[...], preferred_element_type=F32)
        o_ref[...] = (y * rw_ref[...]).astype(o_ref.dtype)

    @pl.when(jnp.logical_not(used))
    def _():
        o_ref[...] = jnp.zeros_like(o_ref)


def _moe_ffn(block_e, n_used, xb, rw, w_gate, w_up, w_down):
    size, d = xb.shape
    n_blocks = size // MOE_ROWS
    de = w_gate.shape[-1]
    grid_spec = pltpu.PrefetchScalarGridSpec(
        num_scalar_prefetch=2,
        grid=(n_blocks,),
        in_specs=[pl.BlockSpec((MOE_ROWS, d), lambda i, be, nu: (i, 0)),
                  pl.BlockSpec((MOE_ROWS, 1), lambda i, be, nu: (i, 0)),
                  pl.BlockSpec((1, d, de), lambda i, be, nu: (be[i], 0, 0)),
                  pl.BlockSpec((1, d, de), lambda i, be, nu: (be[i], 0, 0)),
                  pl.BlockSpec((1, de, d), lambda i, be, nu: (be[i], 0, 0))],
        out_specs=pl.BlockSpec((MOE_ROWS, d), lambda i, be, nu: (i, 0)),
        scratch_shapes=[pltpu.VMEM((d, de), BF16), pltpu.VMEM((d, de), BF16), pltpu.VMEM((de, d), BF16)],
    )
    return pl.pallas_call(
        _moe_kernel,
        grid_spec=grid_spec,
        out_shape=jax.ShapeDtypeStruct((size, d), BF16),
        compiler_params=_cparams(1),
        name="moe_ffn",
    )(block_e, n_used, xb, rw, w_gate, w_up, w_down)


def _moe(h2, scores, router_bias, w_gate, w_up, w_down):
    t, d = h2.shape
    sel = (scores + router_bias.astype(F32)).reshape(t, N_GROUPS, EXPERTS_PER_GROUP)
    pairs = [sel[..., i] + sel[..., j] for i in range(EXPERTS_PER_GROUP) for j in range(i + 1, EXPERTS_PER_GROUP)]
    grp_score = functools.reduce(jnp.maximum, pairs)
    g_idx = jnp.argmax(grp_score, axis=-1).astype(jnp.int32)
    g_hot = (g_idx[:, None] == jnp.arange(N_GROUPS, dtype=jnp.int32)[None, :])[:, :, None]
    in_grp = jnp.sum(jnp.where(g_hot, sel, 0.0), axis=1)
    sc_grp = jnp.sum(jnp.where(g_hot, scores.reshape(t, N_GROUPS, EXPERTS_PER_GROUP), 0.0), axis=1)
    lane4 = jnp.arange(EXPERTS_PER_GROUP, dtype=jnp.int32)[None, :]
    i1 = jnp.argmax(in_grp, axis=-1).astype(jnp.int32)
    i2 = jnp.argmax(jnp.where(lane4 == i1[:, None], -jnp.inf, in_grp), axis=-1).astype(jnp.int32)
    local = jnp.stack([i1, i2], axis=-1)
    e_idx = g_idx[:, None] * EXPERTS_PER_GROUP + local
    wts = jnp.stack([jnp.sum(jnp.where(lane4 == i1[:, None], sc_grp, 0.0), axis=-1),
                     jnp.sum(jnp.where(lane4 == i2[:, None], sc_grp, 0.0), axis=-1)], axis=-1)
    wts = wts / jnp.sum(wts, axis=-1, keepdims=True)
    a = t * TOP_K
    flat_e = e_idx.reshape(-1).astype(jnp.int32)
    flat_w = wts.reshape(-1)
    onehot = (flat_e[:, None] == jnp.arange(N_EXPERTS, dtype=jnp.int32)[None, :]).astype(jnp.int32)
    csum = jnp.cumsum(onehot, axis=0)
    counts = csum[-1]
    rank = jnp.sum(csum * onehot, axis=1) - 1
    padded = (counts + MOE_ROWS - 1) // MOE_ROWS * MOE_ROWS
    ends = jnp.cumsum(padded)
    pstarts = ends - padded
    dest = jnp.sum(pstarts[None, :] * onehot, axis=1) + rank
    n_blocks = -(-a // MOE_ROWS) + N_EXPERTS
    size = n_blocks * MOE_ROWS
    flat_tok = jnp.arange(a, dtype=jnp.int32) // TOP_K
    tok_buf = jnp.full((size,), t, jnp.int32).at[dest].set(flat_tok)
    w_buf = jnp.zeros((size,), F32).at[dest].set(flat_w)
    block_e = jnp.minimum(jnp.searchsorted(ends, jnp.arange(n_blocks) * MOE_ROWS, side='right'),
                          N_EXPERTS - 1).astype(jnp.int32)
    n_used = (ends[-1] // MOE_ROWS).astype(jnp.int32).reshape(1)
    h_pad = jnp.concatenate([h2, jnp.zeros((1, d), h2.dtype)], axis=0)
    xb = h_pad[tok_buf]
    yb = _moe_ffn(block_e, n_used, xb, w_buf[:, None], w_gate, w_up, w_down)
    return jnp.sum(yb[dest].reshape(t, TOP_K, d).astype(F32), axis=1)


def _head_rms(x, gain, e_ref):
    ms = jnp.dot((x * x).astype(BF16), e_ref[...], preferred_element_type=F32) * (1.0 / HEAD_DIM)
    return x * lax.rsqrt(ms + EPS) * gain


def _rope(y, cos, sin):
    lane = lax.broadcasted_iota(jnp.int32, y.shape, 1)
    half = HEAD_DIM // 2
    rot = jnp.where((lane % HEAD_DIM) < half, pltpu.roll(y, LANES - half, 1), pltpu.roll(y, half, 1))
    return y * cos + rot * sin


def _attend(q, tiles, sink):
    scores = []
    for k, _, bias in tiles:
        s = lax.dot_general(q, k, (((1,), (1,)), ((), ())), preferred_element_type=F32)
        scores.append(s if bias is None else s + bias)
    if all(s.shape[1] % LANES == 0 for s in scores):
        scores = [jnp.concatenate(scores, axis=1)]
    m = sink
    for s in scores:
        mt = jnp.max(s, axis=-1, keepdims=True)
        m = mt if m is None else jnp.maximum(m, mt)
    den = jnp.exp(sink - m) if sink is not None else jnp.zeros_like(m)
    probs = []
    for s in scores:
        p = jnp.exp(s - m)
        den = den + jnp.sum(p, axis=-1, keepdims=True)
        probs.append(p.astype(BF16))
    if len(probs) != len(tiles):
        offs = np.cumsum([0] + [k.shape[0] for k, _, _ in tiles])
        probs = [probs[0][:, offs[i]:offs[i + 1]] for i in range(len(tiles))]
    o = None
    for p, (_, v, _) in zip(probs, tiles):
        pv = jnp.dot(p, v, preferred_element_type=F32)
        o = pv if o is None else o + pv
    return o * (1.0 / den)


def _lane_half(shape):
    return lax.broadcasted_iota(jnp.int32, shape, 1) // HEAD_DIM


def _win_group_queries(slabs, g, sink_ref):
    r = WIN_HEADS // WIN_KV_HEADS
    rows = slabs[0].shape[0]
    half = _lane_half(slabs[0].shape)
    parts, sinks = [], []
    for j in range(r):
        head = g * r + j
        slab = slabs[head // 2]
        if head % 2 != g:
            slab = pltpu.roll(slab, HEAD_DIM, 1)
        parts.append(jnp.where(half == g, slab, 0.0))
        sinks.append(jnp.full((rows, 1), sink_ref[head], F32))
    return jnp.concatenate(parts, axis=0).astype(BF16), jnp.concatenate(sinks, axis=0)


def _win_store(o_ref, o, g, rows):
    r = WIN_HEADS // WIN_KV_HEADS
    half = _lane_half((rows, LANES))
    for pair in range(r // 2):
        a = o[(2 * pair) * rows:(2 * pair + 1) * rows]
        b = o[(2 * pair + 1) * rows:(2 * pair + 2) * rows]
        if g == 0:
            b = pltpu.roll(b, HEAD_DIM, 1)
        else:
            a = pltpu.roll(a, HEAD_DIM, 1)
        s = (g * r + 2 * pair) // 2
        o_ref[0, :, s * LANES:(s + 1) * LANES] = jnp.where(half == 0, a, b).astype(o_ref.dtype)


def _win_kernel(sink_ref, q_ref, kv_ref, ckv_ref, cos_ref, sin_ref, qg_ref, kg_ref, e_ref, o_ref, ks_ref, kcs_ref):
    n = pl.program_id(1)
    nb = pl.num_programs(1)
    l = kv_ref.shape[1]
    blk = WIN_BLOCK
    prep_rows = 512

    @pl.when(n == 0)
    def _():
        def body(i, carry):
            r0 = pl.multiple_of(i * prep_rows, prep_rows)
            y = _head_rms(kv_ref[0, pl.ds(r0, prep_rows), 0:LANES].astype(F32), kg_ref[...], e_ref)
            ks_ref[pl.ds(r0, prep_rows), :] = _rope(y, cos_ref[pl.ds(r0, prep_rows), :],
                                                    sin_ref[pl.ds(r0, prep_rows), :]).astype(BF16)
            return carry
        lax.fori_loop(0, l // prep_rows, body, 0)
        kcs_ref[...] = _head_rms(ckv_ref[0, :, 0:LANES].astype(F32), kg_ref[...], e_ref).astype(BF16)

    r0 = pl.multiple_of(n * blk, blk)
    cos = cos_ref[pl.ds(r0, blk), :]
    sin = sin_ref[pl.ds(r0, blk), :]
    slabs = []
    for s in range(q_ref.shape[-1] // LANES):
        y = _head_rms(q_ref[0, :, s * LANES:(s + 1) * LANES].astype(F32), qg_ref[...], e_ref)
        slabs.append(_rope(y, cos, sin))

    r = WIN_HEADS // WIN_KV_HEADS
    qi = lax.broadcasted_iota(jnp.int32, (r * blk, blk), 0) % blk
    kj = lax.broadcasted_iota(jnp.int32, (r * blk, blk), 1)
    ok_prev = (kj >= qi) & (n > 0)
    ok_next = (kj <= qi) & (n < nb - 1)
    bias_prev = jnp.where(ok_prev, 0.0, NEG_INF)
    bias_next = jnp.where(ok_next, 0.0, NEG_INF)
    tiles = []
    for kb, bias in ((jnp.maximum(n - 1, 0), bias_prev), (n, None), (jnp.minimum(n + 1, nb - 1), bias_next)):
        k0 = pl.multiple_of(kb * blk, blk)
        tiles.append((ks_ref[pl.ds(k0, blk), :], kv_ref[0, pl.ds(k0, blk), LANES:2 * LANES], bias))
    tiles.append((kcs_ref[...], ckv_ref[0, :, LANES:2 * LANES], None))
    for g in range(WIN_KV_HEADS):
        q, sink = _win_group_queries(slabs, g, sink_ref)
        _win_store(o_ref, _attend(q, tiles, sink), g, blk)


def _win_ctx_kernel(sink_ref, q_ref, ckv_ref, qg_ref, kg_ref, e_ref, o_ref):
    kc = _head_rms(ckv_ref[0, :, 0:LANES].astype(F32), kg_ref[...], e_ref).astype(BF16)
    tiles = [(kc, ckv_ref[0, :, LANES:2 * LANES], None)]
    slabs = [_head_rms(q_ref[0, :, s * LANES:(s + 1) * LANES].astype(F32), qg_ref[...], e_ref)
             for s in range(q_ref.shape[-1] // LANES)]
    for g in range(WIN_KV_HEADS):
        q, sink = _win_group_queries(slabs, g, sink_ref)
        _win_store(o_ref, _attend(q, tiles, sink), g, q_ref.shape[1])


def _block_diag_ones(width):
    i = np.arange(width) // HEAD_DIM
    return jnp.asarray(i[:, None] == i[None, :], BF16)


def _window_branch(zw_l, zw_c, q_gain, k_gain, sink, cos2, sin2, ctx_out):
    b, l, _ = zw_l.shape
    lc = zw_c.shape[1]
    wq = WIN_HEADS * HEAD_DIM
    qg = (jnp.tile(q_gain.astype(F32), 2) * HEAD_DIM ** -0.5).reshape(1, LANES)
    kg = jnp.tile(k_gain.astype(F32), 2).reshape(1, LANES)
    e = _block_diag_ones(LANES)
    smem = pl.BlockSpec(memory_space=pltpu.SMEM)
    full = lambda s, nd: pl.BlockSpec(s, lambda *a: (0,) * nd)
    yl = pl.pallas_call(
        _win_kernel,
        grid=(b, l // WIN_BLOCK),
        in_specs=[smem,
                  pl.BlockSpec((1, WIN_BLOCK, wq), lambda i, n: (i, n, 0)),
                  pl.BlockSpec((1, l, 2 * LANES), lambda i, n: (i, 0, wq // (2 * LANES))),
                  pl.BlockSpec((1, lc, 2 * LANES), lambda i, n: (i, 0, wq // (2 * LANES))),
                  full((l, LANES), 2), full((l, LANES), 2), full((1, LANES), 2), full((1, LANES), 2),
                  full((LANES, LANES), 2)],
        out_specs=pl.BlockSpec((1, WIN_BLOCK, wq), lambda i, n: (i, n, 0)),
        out_shape=jax.ShapeDtypeStruct((b, l, wq), BF16),
        scratch_shapes=[pltpu.VMEM((l, LANES), BF16), pltpu.VMEM((lc, LANES), BF16)],
        compiler_params=_cparams(2),
        name="window_attn",
    )(sink.astype(F32), zw_l, zw_l, zw_c, cos2, sin2, qg, kg, e)
    yc = None
    if ctx_out:
        yc = pl.pallas_call(
            _win_ctx_kernel,
            grid=(b,),
            in_specs=[smem,
                      pl.BlockSpec((1, lc, wq), lambda i: (i, 0, 0)),
                      pl.BlockSpec((1, lc, 2 * LANES), lambda i: (i, 0, wq // (2 * LANES))),
                      full((1, LANES), 2), full((1, LANES), 2), full((LANES, LANES), 2)],
            out_specs=pl.BlockSpec((1, lc, wq), lambda i: (i, 0, 0)),
            out_shape=jax.ShapeDtypeStruct((b, lc, wq), BF16),
            compiler_params=_cparams(1),
            name="window_ctx_attn",
        )(sink.astype(F32), zw_c, zw_c, qg, kg, e)
    return yc, yl


NA_ROWS = 8
NA_KROWS = NA_ROWS + NA_KH - 1


def _na_key_base(rg, rows):
    return jnp.clip(rg * NA_ROWS - NA_KH // 2, 0, rows - NA_KROWS)


def _na_kernel(q_ref, k_ref, v_ref, ck_ref, cv_ref, bias_ref, qg_ref, kg_ref, e_ref, o_ref, ks_ref, kcs_ref):
    rg = pl.program_id(1)
    l = k_ref.shape[1]
    rows = l // GRID_W
    n_slab = q_ref.shape[-1] // LANES
    prep_rows = 512

    @pl.when(rg == 0)
    def _():
        def body(i, carry):
            r0 = pl.multiple_of(i * prep_rows, prep_rows)
            ks_ref[pl.ds(r0, prep_rows), :] = _head_rms(k_ref[0, pl.ds(r0, prep_rows), :].astype(F32), kg_ref[...],
                                                        e_ref).astype(BF16)
            return carry
        lax.fori_loop(0, l // prep_rows, body, 0)
        kcs_ref[...] = _head_rms(ck_ref[0].astype(F32), kg_ref[...], e_ref).astype(BF16)

    nk = NA_KROWS * GRID_W
    k0 = pl.multiple_of(_na_key_base(rg, rows) * GRID_W, GRID_W)
    qn = _head_rms(q_ref[0].astype(F32), qg_ref[...], e_ref)
    half = _lane_half((q_ref.shape[1], LANES))
    for s in range(n_slab):
        cols = slice(s * LANES, (s + 1) * LANES)
        slab = qn[:, cols]
        tiles_kv = (ks_ref[pl.ds(k0, nk), cols], v_ref[0, pl.ds(k0, nk), cols])
        ctx_kv = (kcs_ref[:, cols], cv_ref[0, :, cols])
        outs = []
        for hh in range(2):
            q = jnp.where(half == hh, slab, 0.0).astype(BF16)
            bias = bias_ref[0, 2 * s + hh].astype(F32)
            outs.append(_attend(q, [tiles_kv + (bias,), ctx_kv + (None,)], None))
        o_ref[0, :, cols] = jnp.where(half == 0, outs[0], outs[1]).astype(o_ref.dtype)


def _na_ctx_kernel(q_ref, ck_ref, cv_ref, qg_ref, kg_ref, e_ref, o_ref):
    kc = _head_rms(ck_ref[0].astype(F32), kg_ref[...], e_ref).astype(BF16)
    qn = _head_rms(q_ref[0].astype(F32), qg_ref[...], e_ref)
    half = _lane_half((q_ref.shape[1], LANES))
    for s in range(q_ref.shape[-1] // LANES):
        cols = slice(s * LANES, (s + 1) * LANES)
        outs = []
        for hh in range(2):
            q = jnp.where(half == hh, qn[:, cols], 0.0).astype(BF16)
            outs.append(_attend(q, [(kc[:, cols], cv_ref[0, :, cols], None)], None))
        o_ref[0, :, cols] = jnp.where(half == 0, outs[0], outs[1]).astype(o_ref.dtype)


def _na_bias_classes(rows):
    n_rg = rows // NA_ROWS
    return list(range(n_rg)) if n_rg <= 3 else [0, 1, n_rg - 1]


def _na_bias_table(rpb, rows):
    kh = NA_KH
    ro, rv = [], []
    for rg in _na_bias_classes(rows):
        kbase = int(np.clip(rg * NA_ROWS - kh // 2, 0, rows - NA_KROWS))
        r = rg * NA_ROWS + np.arange(NA_ROWS)
        rstart = np.clip(r - kh // 2, 0, rows - kh)
        kr = kbase + np.arange(NA_KROWS)
        rv.append((kr[None, :] >= rstart[:, None]) & (kr[None, :] < rstart[:, None] + kh))
        ro.append(np.clip(kr[None, :] - r[:, None] + kh - 1, 0, 2 * kh - 2))
    ro, rv = np.stack(ro), np.stack(rv)
    qc = np.arange(GRID_W)
    cstart = np.clip(qc - NA_KW // 2, 0, GRID_W - NA_KW)
    cv = (qc[None, :] >= cstart[:, None]) & (qc[None, :] < cstart[:, None] + NA_KW)
    co = np.clip(qc[None, :] - qc[:, None] + NA_KW - 1, 0, 2 * NA_KW - 2)
    n_cls = ro.shape[0]
    hp = lax.Precision.HIGHEST
    co_hot = jnp.asarray(co[None] == np.arange(2 * NA_KW - 1)[:, None, None], F32)
    ro_hot = jnp.asarray(ro[..., None] == np.arange(2 * kh - 1), F32)
    cols = jnp.einsum('hdc,cwx->hdwx', rpb.astype(F32), co_hot, precision=hp)
    tab = jnp.einsum('crkd,hdwx->chrwkx', ro_hot, cols, precision=hp)
    ok = rv[:, None, :, None, :, None] & cv[None, None, None, :, None, :]
    tab = jnp.where(ok, tab, NEG_INF)
    return tab.reshape(n_cls, rpb.shape[0], NA_ROWS * GRID_W, NA_KROWS * GRID_W).astype(BF16)


def _neighborhood_branch(zn_l, zn_c, q_gain, k_gain, rpb, ctx_out):
    b, l, _ = zn_l.shape
    lc = zn_c.shape[1]
    w = NA_HEADS * HEAD_DIM
    rows = l // GRID_W
    n_rg = rows // NA_ROWS
    n_slab = w // LANES
    qg = (jnp.tile(q_gain.astype(F32), 2 * n_slab) * HEAD_DIM ** -0.5).reshape(1, w)
    kg = jnp.tile(k_gain.astype(F32), 2 * n_slab).reshape(1, w)
    e = _block_diag_ones(w)
    bias = _na_bias_table(rpb, rows)
    if n_rg <= 3:
        cls_map = lambda i, r: (r, 0, 0, 0)
    else:
        cls_map = lambda i, r: ((r > 0).astype(jnp.int32) + (r == n_rg - 1).astype(jnp.int32), 0, 0, 0)
    tq = NA_ROWS * GRID_W
    full = lambda s, nd: pl.BlockSpec(s, lambda *a: (0,) * nd)
    yl = pl.pallas_call(
        _na_kernel,
        grid=(b, n_rg),
        in_specs=[pl.BlockSpec((1, tq, w), lambda i, r: (i, r, 0)),
                  pl.BlockSpec((1, l, w), lambda i, r: (i, 0, 1)),
                  pl.BlockSpec((1, l, w), lambda i, r: (i, 0, 2)),
                  pl.BlockSpec((1, lc, w), lambda i, r: (i, 0, 1)),
                  pl.BlockSpec((1, lc, w), lambda i, r: (i, 0, 2)),
                  pl.BlockSpec((1,) + bias.shape[1:], cls_map),
                  full((1, w), 2), full((1, w), 2), full((w, w), 2)],
        out_specs=pl.BlockSpec((1, tq, w), lambda i, r: (i, r, 0)),
        out_shape=jax.ShapeDtypeStruct((b, l, w), BF16),
        scratch_shapes=[pltpu.VMEM((l, w), BF16), pltpu.VMEM((lc, w), BF16)],
        compiler_params=_cparams(2),
        name="neighborhood_attn",
    )(zn_l, zn_l, zn_l, zn_c, zn_c, bias, qg, kg, e)
    yc = None
    if ctx_out:
        yc = pl.pallas_call(
            _na_ctx_kernel,
            grid=(b,),
            in_specs=[pl.BlockSpec((1, lc, w), lambda i: (i, 0, 0)),
                      pl.BlockSpec((1, lc, w), lambda i: (i, 0, 1)),
                      pl.BlockSpec((1, lc, w), lambda i: (i, 0, 2)),
                      full((1, w), 2), full((1, w), 2), full((w, w), 2)],
            out_specs=pl.BlockSpec((1, lc, w), lambda i: (i, 0, 0)),
            out_shape=jax.ShapeDtypeStruct((b, lc, w), BF16),
            compiler_params=_cparams(1),
            name="neighborhood_ctx_attn",
        )(zn_c, zn_c, zn_c, qg, kg, e)
    return yc, yl


def _rope_tables(n_tokens):
    t = np.arange(n_tokens)
    n_freq = HEAD_DIM // 4
    inv = ROPE_THETA ** (-jnp.arange(n_freq, dtype=F32) / n_freq)
    ang = jnp.concatenate([jnp.asarray(t // GRID_W, F32)[:, None] * inv, jnp.asarray(t % GRID_W, F32)[:, None] * inv],
                          axis=-1)
    cos, sin = jnp.cos(ang), jnp.sin(ang)
    return jnp.tile(jnp.concatenate([cos, cos], -1), (1, 2)), jnp.tile(jnp.concatenate([-sin, sin], -1), (1, 2))


def _ret_kernel(qq_l, kk_l, v_l, g_l, qq_c, kk_c, v_c, g_c, cos_ref, sin_ref, dmask_ref, tq_ref, tk_ref, cdec_ref,
                gain_ref, yl_ref, yc_ref, kr_ref, kv_ref, sin_state_ref, *, ctx_out):
    c = RET_CHUNK
    ncc = qq_c.shape[1] // c
    ncl = qq_l.shape[1] // c
    nc = ncc + ncl
    tk = tk_ref[0]
    tq = tq_ref[0]
    dmask = dmask_ref[0]
    fwd_lanes = lax.broadcasted_iota(jnp.int32, (c, LANES), 1) < RET_DK

    def chunk_kv(k2, v):
        kd = (k2 * tk).astype(BF16)
        return lax.dot_general(kd, v, (((0,), (0,)), ((), ())), preferred_element_type=F32)

    for n in range(ncc):
        kv_ref[n] = chunk_kv(kk_c[0, n * c:(n + 1) * c, :].astype(F32), v_c[0, n * c:(n + 1) * c, :])

    def kv_body(n, carry):
        r0 = pl.multiple_of(n * c, c)
        k2 = _rope(kk_l[0, pl.ds(r0, c), :].astype(F32), cos_ref[pl.ds(r0, c), :], sin_ref[pl.ds(r0, c), :])
        kr_ref[pl.ds(r0, c), :] = k2.astype(BF16)
        kv_ref[ncc + n] = chunk_kv(k2, v_l[0, pl.ds(r0, c), :])
        return carry
    lax.fori_loop(0, ncl, kv_body, 0, unroll=4)

    dec_f = cdec_ref[0, 0:1, :]
    dec_b = cdec_ref[0, 1:2, :]

    def scan_body(t, carry):
        sf, sb = carry
        sin_state_ref[t, 0:RET_DK, :] = sf.astype(BF16)
        sf = sf * dec_f + kv_ref[t, 0:RET_DK, :]
        tb = jnp.where(t < ncc, ncc - 1 - t, nc - 1 - (t - ncc))
        sin_state_ref[tb, RET_DK:2 * RET_DK, :] = sb.astype(BF16)
        sb = sb * dec_b + kv_ref[tb, RET_DK:2 * RET_DK, :]
        return sf, sb
    zero = jnp.zeros((RET_DK, RET_DV), F32)
    lax.fori_loop(0, nc, scan_body, (zero, zero))

    def chunk_out(q2, k2b, v, gate, state):
        qm = jnp.where(fwd_lanes, q2, 0.0).astype(BF16)
        scores = lax.dot_general(qm, k2b, (((1,), (1,)), ((), ())), preferred_element_type=F32) * dmask
        o = (jnp.dot(scores.astype(BF16), v, preferred_element_type=F32)
             + jnp.dot((q2 * tq).astype(BF16), state, preferred_element_type=F32))
        mu = jnp.mean(o, axis=-1, keepdims=True)
        var = jnp.mean(jnp.square(o - mu), axis=-1, keepdims=True)
        y = (o - mu) * lax.rsqrt(var + EPS) * gain_ref[0]
        gf = gate.astype(F32)
        return y * gf * jax.nn.sigmoid(gf)

    if ctx_out:
        for n in range(ncc):
            rows = slice(n * c, (n + 1) * c)
            yc_ref[0, rows, :] = chunk_out(qq_c[0, rows, :].astype(F32), kk_c[0, rows, :], v_c[0, rows, :],
                                           g_c[0, rows, :], sin_state_ref[n]).astype(yc_ref.dtype)
    else:
        yc_ref[...] = jnp.zeros_like(yc_ref)

    def out_body(n, carry):
        r0 = pl.multiple_of(n * c, c)
        q2 = _rope(qq_l[0, pl.ds(r0, c), :].astype(F32), cos_ref[pl.ds(r0, c), :], sin_ref[pl.ds(r0, c), :])
        yl_ref[0, pl.ds(r0, c), :] = chunk_out(q2, kr_ref[pl.ds(r0, c), :], v_l[0, pl.ds(r0, c), :],
                                               g_l[0, pl.ds(r0, c), :], sin_state_ref[ncc + n]).astype(yl_ref.dtype)
        return carry
    lax.fori_loop(0, ncl, out_body, 0, unroll=4)


def _ret_tables(decay_logit):
    c = RET_CHUNK
    lg = jax.nn.log_sigmoid(decay_logit.astype(F32))
    lf, lb = lg[0][:, None, None], lg[1][:, None, None]
    pos = jnp.arange(c, dtype=F32)
    diff = pos[:, None] - pos[None, :]
    dmask = (jnp.where(diff >= 0, jnp.exp(lf * jnp.maximum(diff, 0.0)), 0.0)
             + jnp.where(diff <= 0, jnp.exp(lb * jnp.maximum(-diff, 0.0)), 0.0)) * RET_DK ** -0.5
    col = lambda a, b_: jnp.concatenate([jnp.broadcast_to(a, a.shape[:2] + (RET_DK,)),
                                         jnp.broadcast_to(b_, b_.shape[:2] + (RET_DK,))], axis=-1)
    p = pos[None, :, None]
    tq = col(jnp.exp(lf * (p + 1.0)), jnp.exp(lb * (c - p)))
    tk = col(jnp.exp(lf * (c - 1.0 - p)), jnp.exp(lb * p)) * RET_DK ** -0.5
    cdec = jnp.zeros((lg.shape[1], 8, RET_DV), F32)
    cdec = cdec.at[:, 0, :].set(jnp.exp(lg[0] * c)[:, None]).at[:, 1, :].set(jnp.exp(lg[1] * c)[:, None])
    return dmask, tq, tk, cdec


def _retention_branch(zr_l, zr_c, decay_logit, gn_gain, cos2, sin2, ctx_out):
    b, l, _ = zr_l.shape
    lc = zr_c.shape[1]
    h = RET_HEADS
    nc = (l + lc) // RET_CHUNK
    dmask, tq, tk, cdec = _ret_tables(decay_logit)
    gain = gn_gain.astype(F32).reshape(h, 1, RET_DV)
    seq = lambda n, j: pl.BlockSpec((1, n, LANES), lambda i, hh: (i, 0, 4 * hh + j))
    head = lambda s: pl.BlockSpec((1,) + s, lambda i, hh: (hh, 0, 0))
    full = lambda s: pl.BlockSpec(s, lambda i, hh: (0, 0))
    yl, yc = pl.pallas_call(
        functools.partial(_ret_kernel, ctx_out=ctx_out),
        grid=(b, h),
        in_specs=[seq(l, 0), seq(l, 1), seq(l, 2), seq(l, 3), seq(lc, 0), seq(lc, 1), seq(lc, 2), seq(lc, 3),
                  full((l, LANES)), full((l, LANES)),
                  head((RET_CHUNK, RET_CHUNK)), head((RET_CHUNK, LANES)), head((RET_CHUNK, LANES)), head((8, RET_DV)),
                  head((1, RET_DV))],
        out_specs=[pl.BlockSpec((1, l, RET_DV), lambda i, hh: (i, 0, hh)),
                   pl.BlockSpec((1, lc, RET_DV), lambda i, hh: (i, 0, hh))],
        out_shape=[jax.ShapeDtypeStruct((b, l, h * RET_DV), BF16), jax.ShapeDtypeStruct((b, lc, h * RET_DV), BF16)],
        scratch_shapes=[pltpu.VMEM((l, LANES), BF16), pltpu.VMEM((nc, 2 * RET_DK, RET_DV), F32),
                        pltpu.VMEM((nc, 2 * RET_DK, RET_DV), BF16)],
        compiler_params=_cparams(2),
        name="retention",
    )(zr_l, zr_l, zr_l, zr_l, zr_c, zr_c, zr_c, zr_c, cos2, sin2, dmask, tq, tk, cdec, gain)
    return (yc if ctx_out else None), yl


GDN_SUPER = 128
GDN_UNITS = 4
GDN_HALO = 128


def _split_bf16(a):
    hi = a.astype(BF16)
    return hi, (a - hi.astype(F32)).astype(BF16)


def _mask_dot(mask_bf16, a):
    ah, al = _split_bf16(a)
    return jnp.dot(mask_bf16, ah, preferred_element_type=F32) + jnp.dot(mask_bf16, al, preferred_element_type=F32)


def _softplus(x):
    return jnp.maximum(x, 0.0) + jnp.log(1.0 + jnp.exp(-jnp.abs(x)))


def _gdn_kernel(nega_ref, dtb_ref, q_l, k_l, v_l, g_l, ab_l, q_c, k_c, v_c, g_c, ab_c,
                cwq_ref, cwk_ref, cwv_ref, pd_ref, pu_ref, gain_ref, yl_ref, yc_ref,
                kn_s, sin_s, qp_s, o0_s, cd_s, *, ctx_out):
    hd = pl.program_id(1)
    c = GDN_CHUNK
    sup = GDN_SUPER
    per = sup // c
    lc, l = q_c.shape[1], q_l.shape[1]
    ncc, ncl = lc // c, l // c
    nc = ncc + ncl

    ri = lax.broadcasted_iota(jnp.int32, (sup, sup), 0)
    ci = lax.broadcasted_iota(jnp.int32, (sup, sup), 1)
    same = (ri // c) == (ci // c)
    eye = (ri == ci).astype(F32)
    incl = (same & (ri >= ci), same & (ri <= ci))
    strict = (same & (ri > ci), same & (ri < ci))
    incl_b = tuple(m.astype(BF16) for m in incl)
    same_b = same.astype(BF16)
    lane = lax.broadcasted_iota(jnp.int32, (sup, LANES), 1)
    rowc = lax.broadcasted_iota(jnp.int32, (sup, LANES), 0)

    def conv_silu(z_ref, w_ref, r0, ls):
        z = z_ref[0, pl.ds(r0, sup), :]
        if isinstance(r0, int):
            zero = jnp.zeros((GDN_HALO, LANES), BF16)
            prev = z_ref[0, r0 - GDN_HALO:r0, :] if r0 > 0 else zero
            nxt = z_ref[0, r0 + sup:r0 + sup + GDN_HALO, :] if r0 + sup < ls else zero
        else:
            p0 = pl.multiple_of(jnp.maximum(r0 - GDN_HALO, 0), GDN_HALO)
            n0 = pl.multiple_of(jnp.minimum(r0 + sup, ls - GDN_HALO), GDN_HALO)
            prev = jnp.where(r0 > 0, z_ref[0, pl.ds(p0, GDN_HALO), :], jnp.zeros((), BF16))
            nxt = jnp.where(r0 + sup < ls, z_ref[0, pl.ds(n0, GDN_HALO), :], jnp.zeros((), BF16))
        win = jnp.concatenate([prev, z, nxt], axis=0)
        z_dn = jnp.dot(pd_ref[...], win, preferred_element_type=F32)
        z_up = jnp.dot(pu_ref[...], win, preferred_element_type=F32)
        y = z_dn * w_ref[0, 0:1, :] + z.astype(F32) * w_ref[0, 1:2, :] + z_up * w_ref[0, 2:3, :]
        return y * jax.nn.sigmoid(y)

    def prep(units):
        common = []
        for refs, r0, base, ls in units:
            q_ref, k_ref, v_ref, ab_ref = refs
            q = conv_silu(q_ref, cwq_ref, r0, ls)
            k = conv_silu(k_ref, cwk_ref, r0, ls)
            v = conv_silu(v_ref, cwv_ref, r0, ls)
            q = q * lax.rsqrt(jnp.sum(q * q, axis=-1, keepdims=True) + EPS) * GDN_DK ** -0.5
            k = k * lax.rsqrt(jnp.sum(k * k, axis=-1, keepdims=True) + EPS)
            kb16 = k.astype(BF16)
            qk = lax.dot_general(q.astype(BF16), kb16, (((1,), (1,)), ((), ())), preferred_element_type=F32)
            ab = ab_ref[0, pl.ds(r0, sup), :]
            common.append((q, k, v, kb16, qk, ab, base + r0))
        chains = []
        for q, k, v, kb16, qk, ab, row0 in common:
            for d in range(2):
                ia = d * GDN_HEADS + hd
                ib = 2 * GDN_HEADS + ia
                a_col = jnp.sum(jnp.where(lane == ia, ab, 0.0), axis=1, keepdims=True)
                b_col = jnp.sum(jnp.where(lane == ib, ab, 0.0), axis=1, keepdims=True)
                g_col = nega_ref[d, hd] * _softplus(a_col + dtb_ref[d, hd])
                beta = jax.nn.sigmoid(b_col)
                gcum = _mask_dot(incl_b[d], jnp.broadcast_to(g_col, (sup, LANES)))
                gtot = _mask_dot(same_b, jnp.broadcast_to(g_col, (sup, LANES)))
                gc = gcum[:, 0:1]
                decay = jnp.where(incl[d], jnp.exp(jnp.where(incl[d], gcum - gcum.T, 0.0)), 0.0)
                kbeta = k * beta
                kk = lax.dot_general(kbeta.astype(BF16), kb16, (((1,), (1,)), ((), ())),
                                     preferred_element_type=F32)
                a_mat = jnp.where(strict[d], kk * decay, 0.0)
                eg = jnp.exp(gc)
                rhs = jnp.concatenate([kbeta * eg, v * beta], axis=1).astype(BF16)
                aqk = jnp.where(incl[d], qk * decay, 0.0).astype(BF16)
                ke = (k * jnp.exp(gtot[:, 0:1] - gc)).astype(BF16)
                cds = [jnp.exp(gtot[j * c:j * c + 1, :]) for j in range(per)]
                chains.append(dict(d=d, row0=row0, a=a_mat, rhs=rhs, aqk=aqk, ke=ke, cds=cds, qeg=q * eg))
        for ch in chains:
            ch['inv'] = eye - ch['a']
            ch['pw'] = ch['a'].astype(BF16)
        for _ in range(5):
            for ch in chains:
                ch['pw'] = jnp.dot(ch['pw'], ch['pw'], preferred_element_type=F32).astype(BF16)
            for ch in chains:
                ch['inv'] = ch['inv'] + jnp.dot(ch['inv'].astype(BF16), ch['pw'], preferred_element_type=F32)
        for ch in chains:
            ch['wu'] = jnp.dot(ch['inv'].astype(BF16), ch['rhs'], preferred_element_type=F32).astype(BF16)
        for ch in chains:
            d, row0 = ch['d'], ch['row0']
            awu = jnp.dot(ch['aqk'], ch['wu'], preferred_element_type=F32)
            rows = pl.ds(row0, sup)
            qp_s[d, rows, :] = (ch['qeg'] - awu[:, :GDN_DK]).astype(BF16)
            o0_s[d, rows, :] = awu[:, GDN_DK:]
            for j in range(per):
                kej = jnp.where((rowc // c) == j, ch['ke'], jnp.zeros((), BF16))
                idx = row0 // c + j
                kn_s[d, idx] = lax.dot_general(kej, ch['wu'], (((0,), (0,)), ((), ())),
                                               preferred_element_type=F32).astype(BF16)
                cd_s[d, idx] = ch['cds'][j]

    prep([((q_c, k_c, v_c, ab_c), n * sup, 0, lc) for n in range(lc // sup)])

    def prep_body(n, carry):
        prep([((q_l, k_l, v_l, ab_l), pl.multiple_of((GDN_UNITS * n + j) * sup, sup), lc, l)
              for j in range(GDN_UNITS)])
        return carry
    lax.fori_loop(0, l // (GDN_UNITS * sup), prep_body, 0)

    def chunk_step(d, idx, s):
        sb = s.astype(BF16)
        sin_s[d, idx] = sb
        kn = kn_s[d, idx]
        return (s * cd_s[d, idx] - jnp.dot(kn[:, :GDN_DK], sb, preferred_element_type=F32)
                + kn[:, GDN_DK:].astype(F32))

    def scan_body(t, carry):
        sf, sb = carry
        tb = jnp.where(t < ncc, ncc - 1 - t, nc - 1 - (t - ncc))
        return chunk_step(0, t, sf), chunk_step(1, tb, sb)
    zero = jnp.zeros((GDN_DK, GDN_DV), F32)
    lax.fori_loop(0, nc, scan_body, (zero, zero))

    def finish(y_ref, gate_ref, base, n_rows):
        def body(n, carry):
            r0 = pl.multiple_of(n * c, c)
            rows = pl.ds(base + r0, c)
            idx = base // c + n
            o = (o0_s[0, rows, :] + o0_s[1, rows, :]
                 + jnp.dot(qp_s[0, rows, :], sin_s[0, idx], preferred_element_type=F32)
                 + jnp.dot(qp_s[1, rows, :], sin_s[1, idx], preferred_element_type=F32))
            y = o * lax.rsqrt(jnp.mean(o * o, axis=-1, keepdims=True) + EPS) * gain_ref[...]
            gf = gate_ref[0, pl.ds(r0, c), :].astype(F32)
            y_ref[0, pl.ds(r0, c), :] = (y * gf * jax.nn.sigmoid(gf)).astype(y_ref.dtype)
            return carry
        lax.fori_loop(0, n_rows // c, body, 0, unroll=4)

    finish(yl_ref, g_l, lc, l)
    if ctx_out:
        finish(yc_ref, g_c, 0, lc)
    else:
        yc_ref[...] = jnp.zeros_like(yc_ref)


def _gdn_branch(zg_l, zab_l, zg_c, zab_c, conv_w, a_log, dt_bias, norm_gain, ctx_out):
    b, l, _ = zg_l.shape
    lc = zg_c.shape[1]
    h = GDN_HEADS
    sup = GDN_SUPER
    ltot = l + lc
    nc = ltot // GDN_CHUNK
    neg_a = -jnp.exp(a_log.astype(F32))
    cw = conv_w.astype(F32).T.reshape(3 * h, LANES, SHORT_CONV).transpose(0, 2, 1)
    win = sup + 2 * GDN_HALO
    i = np.arange(sup)
    pd = np.zeros((sup, win), np.float32)
    pu = np.zeros((sup, win), np.float32)
    pd[i, GDN_HALO + i - 1] = 1.0
    pu[i, GDN_HALO + i + 1] = 1.0
    seq = lambda n, j: pl.BlockSpec((1, n, LANES), lambda bi, hh: (bi, 0, j * h + hh))
    abs_ = lambda n: pl.BlockSpec((1, n, LANES), lambda bi, hh: (bi, 0, 0))
    cws = lambda j: pl.BlockSpec((1, SHORT_CONV, LANES), lambda bi, hh: (j * h + hh, 0, 0))
    full = lambda s: pl.BlockSpec(s, lambda bi, hh: (0, 0))
    smem = pl.BlockSpec(memory_space=pltpu.SMEM)
    yl, yc = pl.pallas_call(
        functools.partial(_gdn_kernel, ctx_out=ctx_out),
        grid=(b, h),
        in_specs=[smem, smem, seq(l, 0), seq(l, 1), seq(l, 2), seq(l, 3), abs_(l),
                  seq(lc, 0), seq(lc, 1), seq(lc, 2), seq(lc, 3), abs_(lc),
                  cws(0), cws(1), cws(2), full((sup, win)), full((sup, win)), full((1, GDN_DV))],
        out_specs=[pl.BlockSpec((1, l, GDN_DV), lambda bi, hh: (bi, 0, hh)),
                   pl.BlockSpec((1, lc, GDN_DV), lambda bi, hh: (bi, 0, hh))],
        out_shape=[jax.ShapeDtypeStruct((b, l, h * GDN_DV), BF16), jax.ShapeDtypeStruct((b, lc, h * GDN_DV), BF16)],
        scratch_shapes=[pltpu.VMEM((2, nc, GDN_DK, GDN_DK + GDN_DV), BF16), pltpu.VMEM((2, nc, GDN_DK, GDN_DV), BF16),
                        pltpu.VMEM((2, ltot, GDN_DK), BF16), pltpu.VMEM((2, ltot, GDN_DV), F32),
                        pltpu.VMEM((2, nc, 1, GDN_DV), F32)],
        compiler_params=_cparams(2),
        name="gated_deltanet",
    )(neg_a, dt_bias.astype(F32), zg_l, zg_l, zg_l, zg_l, zab_l, zg_c, zg_c, zg_c, zg_c, zab_c,
      cw, cw, cw, jnp.asarray(pd, BF16), jnp.asarray(pu, BF16), norm_gain.astype(F32).reshape(1, GDN_DV))
    return (yc if ctx_out else None), yl


def kernel(x, c, ctx, c_ctx, w_mod, b_mod, norm1, norm2, w_in, ret_decay, ret_gn, win_qnorm, win_knorm, win_sink,
           na_qnorm, na_knorm, na_rpb, gdn_conv, gdn_a_log, gdn_dt_bias, gdn_norm, w_branch, w_merge, w_out,
           w_router, router_bias, w_e_gate, w_e_up, w_e_down):
    b, l, d = x.shape
    lc = ctx.shape[1]
    depth = w_mod.shape[0]
    cos2, sin2 = _rope_tables(l)

    n_rows = 16
    cc = jnp.zeros((n_rows, d), F32).at[:b].set(c).at[b].set(c_ctx)
    mod = _modulation(cc, w_mod, b_mod).reshape(depth, n_rows, 6, d)

    wr = jnp.zeros((d, LANES), F32).at[:, :N_EXPERTS].set(w_router)
    wr_hi = wr.astype(BF16)
    wr_lo = (wr - wr_hi.astype(F32)).astype(BF16)

    xl, xc = x, ctx
    for layer in range(depth):
        ctx_out = layer < depth - 1
        mod_l = mod[layer, :b]
        mod_c = mod[layer, b:b + 1]
        w_all = _pack_w_in(w_in[layer])
        hl, *zl_s = _inproj(xl, mod_l, False, norm1[layer], w_all, 256)
        hc, *zc_s = _inproj(xc, mod_c, True, norm1[layer], w_all, 256)
        zr_l, zw_l, zn_l, zg_l, zab_l = zl_s
        zr_c, zw_c, zn_c, zg_c, zab_c = zc_s
        ret_c, ret_l = _retention_branch(zr_l, zr_c, ret_decay[layer], ret_gn[layer], cos2, sin2, ctx_out)
        win_c, win_l = _window_branch(zw_l, zw_c, win_qnorm[layer], win_knorm[layer], win_sink[layer], cos2, sin2,
                                      ctx_out)
        na_c, na_l = _neighborhood_branch(zn_l, zn_c, na_qnorm[layer], na_knorm[layer], na_rpb[layer], ctx_out)
        gdn_c, gdn_l = _gdn_branch(zg_l, zab_l, zg_c, zab_c, gdn_conv[layer], gdn_a_log[layer], gdn_dt_bias[layer],
                                   gdn_norm[layer], ctx_out)
        wm = w_merge[layer].astype(BF16)
        wb = w_branch[layer].astype(BF16)
        wo = w_out[layer].astype(BF16)
        ys_l = [ret_l, win_l, na_l, gdn_l]
        xl, h2l, sc_l = _merge(xl, hl, ys_l, mod_l, False, norm2[layer], wm, wb, wo, wr_hi, wr_lo, 256)
        g2l = mod_l[:, 5][:, None, :]
        if ctx_out:
            ys_c = [ret_c, win_c, na_c, gdn_c]
            xc, h2c, sc_c = _merge(xc, hc, ys_c, mod_c, True, norm2[layer], wm, wb, wo, wr_hi, wr_lo, 256)
            g2c = mod_c[:, 5][:, None, :]
            tokens = jnp.concatenate([h2c.reshape(b * lc, d), h2l.reshape(b * l, d)], axis=0)
            scores = jnp.concatenate([sc_c.reshape(b * lc, LANES), sc_l.reshape(b * l, LANES)], axis=0)
            y = _moe(tokens, scores[:, :N_EXPERTS], router_bias, w_e_gate[layer], w_e_up[layer], w_e_down[layer])
            xc = xc + g2c * y[:b * lc].reshape(b, lc, d)
            xl = xl + g2l * y[b * lc:].reshape(b, l, d)
        else:
            y = _moe(h2l.reshape(b * l, d), sc_l.reshape(b * l, LANES)[:, :N_EXPERTS], router_bias,
                     w_e_gate[layer], w_e_up[layer], w_e_down[layer])
            xl = xl + g2l * y.reshape(b, l, d)
    return xl
```

```python
import functools

import numpy as np
import jax
import jax.numpy as jnp
from jax import lax
from jax.experimental import pallas as pl
from jax.experimental.pallas import tpu as pltpu

F32 = jnp.float32
BF16 = jnp.bfloat16
EPS = 1e-6
NEG_INF = -1e30
D_MODEL = 1024
GRID_W = 64
HEAD_DIM = 64
ROPE_THETA = 10000.0
RET_HEADS, RET_DK, RET_DV, RET_CHUNK = 4, 64, 128, 128
WIN_HEADS, WIN_KV_HEADS, WINDOW, WIN_BLOCK = 8, 2, 128, 128
NA_HEADS, NA_KH, NA_KW, NA_QCOLS = 8, 8, 16, 16
NA_BAND = NA_QCOLS + NA_KW
GDN_HEADS, GDN_DK, GDN_DV, GDN_CHUNK, SHORT_CONV = 4, 128, 128, 64, 3
GDN_QKV = 2 * GDN_HEADS * GDN_DK + GDN_HEADS * GDN_DV
N_BRANCH, BRANCH_W = 4, 512
N_EXPERTS, N_GROUPS, TOP_K, D_EXPERT = 32, 8, 2, 512
EXPERTS_PER_GROUP = N_EXPERTS // N_GROUPS

LANES = 128
VMEM_LIMIT = 56 * 1024 * 1024
MOE_ROWS = 256
MOE_ROW_ALIGN = 8

W_RET = RET_HEADS * 4 * LANES
W_WIN = (WIN_HEADS + 2 * WIN_KV_HEADS) * HEAD_DIM
W_NA = 3 * NA_HEADS * HEAD_DIM
W_GDN = GDN_QKV + GDN_HEADS * GDN_DV
W_AB = LANES
SECTION_WIDTHS = (W_RET, W_WIN, W_NA, W_GDN, W_AB)
W_ALL = sum(SECTION_WIDTHS)


def _cparams(n_axes):
    return pltpu.CompilerParams(dimension_semantics=("arbitrary",) * n_axes, vmem_limit_bytes=VMEM_LIMIT)


def _mod_kernel(c_ref, w_ref, b_ref, o_ref):
    c = c_ref[...]
    a = (c * jax.nn.sigmoid(c)).astype(BF16)
    o_ref[0] = jnp.dot(a, w_ref[0].astype(BF16), preferred_element_type=F32) + b_ref[0]


def _modulation(cc, w_mod, b_mod):
    depth, d, n = w_mod.shape
    r = cc.shape[0]
    tn = 1536
    return pl.pallas_call(
        _mod_kernel,
        grid=(depth, n // tn),
        in_specs=[pl.BlockSpec((r, d), lambda l, j: (0, 0)),
                  pl.BlockSpec((1, d, tn), lambda l, j: (l, 0, j)),
                  pl.BlockSpec((1, 1, tn), lambda l, j: (l, 0, j))],
        out_specs=pl.BlockSpec((1, r, tn), lambda l, j: (l, 0, j)),
        out_shape=jax.ShapeDtypeStruct((depth, r, n), F32),
        compiler_params=_cparams(2),
        name="modulation",
    )(cc, w_mod, b_mod.reshape(depth, 1, n))


def _inproj_kernel(x_ref, mod_ref, gain_ref, w_ref, h_ref, *z_refs):
    x = x_ref[0]
    ms = jnp.mean(x * x, axis=-1, keepdims=True)
    shift = mod_ref[0, 0:1, :]
    scale = mod_ref[0, 1:2, :]
    h = x * lax.rsqrt(ms + EPS) * gain_ref[...] * (1.0 + scale) + shift
    hb = h.astype(BF16)
    h_ref[0] = hb
    off = 0
    for ref in z_refs:
        width = ref.shape[-1]
        for c0 in range(0, width, 512):
            c1 = min(c0 + 512, width)
            z = jnp.dot(hb, w_ref[:, off + c0:off + c1], preferred_element_type=F32)
            ref[0, :, c0:c1] = z.astype(ref.dtype)
        off += width


def _inproj(x, mod, mod_is_shared, gain, w_all, tm):
    b, l, d = x.shape
    mod_map = (lambda i, j: (0, 0, 0)) if mod_is_shared else (lambda i, j: (i, 0, 0))
    dtypes = (BF16, BF16, BF16, BF16, F32)
    out_shape = [jax.ShapeDtypeStruct((b, l, d), BF16)]
    out_specs = [pl.BlockSpec((1, tm, d), lambda i, j: (i, j, 0))]
    for w, dt in zip(SECTION_WIDTHS, dtypes):
        out_shape.append(jax.ShapeDtypeStruct((b, l, w), dt))
        out_specs.append(pl.BlockSpec((1, tm, w), lambda i, j: (i, j, 0)))
    return pl.pallas_call(
        _inproj_kernel,
        grid=(b, l // tm),
        in_specs=[pl.BlockSpec((1, tm, d), lambda i, j: (i, j, 0)),
                  pl.BlockSpec((1, 6, d), mod_map),
                  pl.BlockSpec((1, d), lambda i, j: (0, 0)),
                  pl.BlockSpec((d, W_ALL), lambda i, j: (0, 0))],
        out_specs=out_specs,
        out_shape=out_shape,
        compiler_params=_cparams(2),
        name="inproj",
    )(x, mod, gain.reshape(1, d), w_all)


def _pack_w_in(w_in):
    d = w_in.shape[0]
    hq, hv = RET_HEADS * RET_DK, RET_HEADS * RET_DV
    cols = []
    for h in range(RET_HEADS):
        q = w_in[:, h * RET_DK:(h + 1) * RET_DK]
        k = w_in[:, hq + h * RET_DK:hq + (h + 1) * RET_DK]
        cols += [q, q, k, k, w_in[:, 2 * hq + h * RET_DV:2 * hq + (h + 1) * RET_DV],
                 w_in[:, 2 * hq + hv + h * RET_DV:2 * hq + hv + (h + 1) * RET_DV]]
    rest = w_in[:, 2 * hq + 2 * hv:]
    pad = jnp.zeros((d, W_ALL - W_RET - rest.shape[1]), w_in.dtype)
    return jnp.concatenate(cols + [rest, pad], axis=1).astype(BF16)


def _merge_kernel(x_ref, h_ref, y0_ref, y1_ref, y2_ref, y3_ref, mod_ref, gain_ref, wm_ref, wb_ref, wo_ref,
                  wrh_ref, wrl_ref, xo_ref, h2_ref, sc_ref):
    d = x_ref.shape[-1]
    h = h_ref[0]
    acc = jnp.zeros(x_ref.shape[1:], F32)
    for i, y_ref in enumerate((y0_ref, y1_ref, y2_ref, y3_ref)):
        gate = jax.nn.sigmoid(jnp.dot(h, wm_ref[:, i * d:(i + 1) * d], preferred_element_type=F32))
        acc = acc + gate * jnp.dot(y_ref[0], wb_ref[i], preferred_element_type=F32)
    m = jnp.dot(acc.astype(BF16), wo_ref[...], preferred_element_type=F32)
    xn = x_ref[0] + mod_ref[0, 2:3, :] * m
    xo_ref[0] = xn
    ms = jnp.mean(xn * xn, axis=-1, keepdims=True)
    h2 = xn * lax.rsqrt(ms + EPS) * gain_ref[...] * (1.0 + mod_ref[0, 4:5, :]) + mod_ref[0, 3:4, :]
    hi = h2.astype(BF16)
    h2_ref[0] = h2
    lo = (h2 - hi.astype(F32)).astype(BF16)
    logits = (jnp.dot(hi, wrh_ref[...], preferred_element_type=F32)
              + jnp.dot(lo, wrh_ref[...], preferred_element_type=F32)
              + jnp.dot(hi, wrl_ref[...], preferred_element_type=F32))
    sc_ref[0] = jax.nn.sigmoid(logits)


def _merge(x, h, ys, mod, mod_is_shared, gain2, wm, wb, wo, wr_hi, wr_lo, tm):
    b, l, d = x.shape
    mod_map = (lambda i, j: (0, 0, 0)) if mod_is_shared else (lambda i, j: (i, 0, 0))
    tok = lambda w: pl.BlockSpec((1, tm, w), lambda i, j: (i, j, 0))
    full2 = lambda s: pl.BlockSpec(s, lambda i, j: (0, 0))
    return pl.pallas_call(
        _merge_kernel,
        grid=(b, l // tm),
        in_specs=[tok(d), tok(d)] + [tok(BRANCH_W)] * 4 + [
            pl.BlockSpec((1, 6, d), mod_map), full2((1, d)), full2(wm.shape),
            pl.BlockSpec(wb.shape, lambda i, j: (0, 0, 0)), full2(wo.shape), full2(wr_hi.shape), full2(wr_lo.shape)],
        out_specs=[tok(d), tok(d), tok(LANES)],
        out_shape=[jax.ShapeDtypeStruct((b, l, d), F32), jax.ShapeDtypeStruct((b, l, d), F32),
                   jax.ShapeDtypeStruct((b, l, LANES), F32)],
        compiler_params=_cparams(2),
        name="merge",
    )(x, h, *ys, mod, gain2.reshape(1, d), wm, wb, wo, wr_hi, wr_lo)


def _moe_kernel(be_ref, nv_ref, asg_ref, h_hbm, wg_ref, wu_ref, wd_ref, y_hbm, xbuf, obuf, wg_s, wu_s, wd_s,
                gsem, ssem):
    i = pl.program_id(0)
    n = pl.num_programs(0)
    slot = i % 2
    nv = nv_ref[i]

    def start_gather(blk, dst_slot):
        def body(grp, carry):
            for j in range(MOE_ROW_ALIGN):
                r = grp * MOE_ROW_ALIGN + j
                tok = jnp.minimum(asg_ref[blk * MOE_ROWS + r] // TOP_K, h_hbm.shape[0] - 1)
                pltpu.make_async_copy(h_hbm.at[pl.ds(tok, 1)], xbuf.at[dst_slot, pl.ds(r, 1)],
                                      gsem.at[dst_slot]).start()
            return carry
        lax.fori_loop(0, nv_ref[blk] // MOE_ROW_ALIGN, body, 0)

    def wait_rows(buf, sem, cnt):
        cnt = pl.multiple_of(cnt, MOE_ROW_ALIGN)
        pltpu.make_async_copy(h_hbm.at[pl.ds(0, cnt)], buf.at[pl.ds(0, cnt)], sem).wait()

    @pl.when(i == 0)
    def _():
        xbuf[...] = jnp.zeros_like(xbuf)
        n_real = TOP_K * h_hbm.shape[0]
        n_spare = y_hbm.shape[0] - n_real
        zero_fill = pltpu.make_async_copy(xbuf.at[1, pl.ds(0, n_spare)], y_hbm.at[pl.ds(n_real, n_spare)], ssem.at[1])
        zero_fill.start()
        zero_fill.wait()
        start_gather(0, 0)

    @pl.when(i + 1 < n)
    def _():
        start_gather(i + 1, 1 - slot)

    @pl.when((i >= 2) & (nv_ref[jnp.maximum(i - 2, 0)] > 0))
    def _():
        wait_rows(obuf.at[slot], ssem.at[slot], nv_ref[jnp.maximum(i - 2, 0)])

    @pl.when(nv > 0)
    def _():
        wait_rows(xbuf.at[slot], gsem.at[slot], nv)
        e = be_ref[i]

        @pl.when((i == 0) | (e != be_ref[jnp.maximum(i - 1, 0)]))
        def _():
            wg_s[...] = wg_ref[0].astype(BF16)
            wu_s[...] = wu_ref[0].astype(BF16)
            wd_s[...] = wd_ref[0].astype(BF16)

        x = xbuf[slot].astype(BF16)
        g = jnp.dot(x, wg_s[...], preferred_element_type=F32)
        u = jnp.dot(x, wu_s[...], preferred_element_type=F32)
        a = (g * jax.nn.sigmoid(g) * u).astype(BF16)
        obuf[slot] = jnp.dot(a, wd_s[...], preferred_element_type=F32)

        def body(grp, carry):
            for j in range(MOE_ROW_ALIGN):
                r = grp * MOE_ROW_ALIGN + j
                dst = asg_ref[i * MOE_ROWS + r]
                pltpu.make_async_copy(obuf.at[slot, pl.ds(r, 1)], y_hbm.at[pl.ds(dst, 1)], ssem.at[slot]).start()
            return carry
        lax.fori_loop(0, nv // MOE_ROW_ALIGN, body, 0)

    @pl.when(i == n - 1)
    def _():
        @pl.when(nv > 0)
        def _():
            wait_rows(obuf.at[slot], ssem.at[slot], nv)

        @pl.when((n >= 2) & (nv_ref[jnp.maximum(i - 1, 0)] > 0))
        def _():
            wait_rows(obuf.at[1 - slot], ssem.at[1 - slot], nv_ref[jnp.maximum(i - 1, 0)])


def _moe_ffn(block_e, n_valid, asg, h2, w_gate, w_up, w_down):
    t, d = h2.shape
    n_blocks = block_e.shape[0]
    de = w_gate.shape[-1]
    wspec = lambda s: pl.BlockSpec((1,) + s, lambda i, be, nv, asg: (be[i], 0, 0))
    grid_spec = pltpu.PrefetchScalarGridSpec(
        num_scalar_prefetch=3,
        grid=(n_blocks,),
        in_specs=[pl.BlockSpec(memory_space=pl.ANY), wspec((d, de)), wspec((d, de)), wspec((de, d))],
        out_specs=pl.BlockSpec(memory_space=pl.ANY),
        scratch_shapes=[pltpu.VMEM((2, MOE_ROWS, d), F32), pltpu.VMEM((2, MOE_ROWS, d), F32),
                        pltpu.VMEM((d, de), BF16), pltpu.VMEM((d, de), BF16), pltpu.VMEM((de, d), BF16),
                        pltpu.SemaphoreType.DMA((2,)), pltpu.SemaphoreType.DMA((2,))],
    )
    return pl.pallas_call(
        _moe_kernel,
        grid_spec=grid_spec,
        out_shape=jax.ShapeDtypeStruct((t * TOP_K + MOE_ROW_ALIGN * N_EXPERTS, d), F32),
        compiler_params=pltpu.CompilerParams(dimension_semantics=("arbitrary",), vmem_limit_bytes=VMEM_LIMIT,
                                             has_side_effects=True),
        name="moe_ffn",
    )(block_e, n_valid, asg, h2, w_gate, w_up, w_down)


def _route(scores, router_bias):
    t = scores.shape[0]
    sel = (scores + router_bias.astype(F32)).reshape(t, N_GROUPS, EXPERTS_PER_GROUP)
    pairs = [sel[..., i] + sel[..., j] for i in range(EXPERTS_PER_GROUP) for j in range(i + 1, EXPERTS_PER_GROUP)]
    grp_score = functools.reduce(jnp.maximum, pairs)
    g_idx = jnp.argmax(grp_score, axis=-1).astype(jnp.int32)
    g_hot = (g_idx[:, None] == jnp.arange(N_GROUPS, dtype=jnp.int32)[None, :])[:, :, None]
    in_grp = jnp.sum(jnp.where(g_hot, sel, 0.0), axis=1)
    sc_grp = jnp.sum(jnp.where(g_hot, scores.reshape(t, N_GROUPS, EXPERTS_PER_GROUP), 0.0), axis=1)
    lane4 = jnp.arange(EXPERTS_PER_GROUP, dtype=jnp.int32)[None, :]
    i1 = jnp.argmax(in_grp, axis=-1).astype(jnp.int32)
    i2 = jnp.argmax(jnp.where(lane4 == i1[:, None], -jnp.inf, in_grp), axis=-1).astype(jnp.int32)
    e_idx = g_idx[:, None] * EXPERTS_PER_GROUP + jnp.stack([i1, i2], axis=-1)
    wts = jnp.stack([jnp.sum(jnp.where(lane4 == i1[:, None], sc_grp, 0.0), axis=-1),
                     jnp.sum(jnp.where(lane4 == i2[:, None], sc_grp, 0.0), axis=-1)], axis=-1)
    return e_idx, wts / jnp.sum(wts, axis=-1, keepdims=True)


def _moe(h2, scores, router_bias, w_gate, w_up, w_down):
    t, d = h2.shape
    e_idx, wts = _route(scores, router_bias)
    a = t * TOP_K
    flat_e = e_idx.reshape(-1)
    onehot = (flat_e[:, None] == jnp.arange(N_EXPERTS, dtype=jnp.int32)[None, :]).astype(jnp.int32)
    csum = jnp.cumsum(onehot, axis=0)
    counts = csum[-1]
    rank = jnp.sum(csum * onehot, axis=1) - 1
    padded = (counts + MOE_ROWS - 1) // MOE_ROWS * MOE_ROWS
    ends = jnp.cumsum(padded)
    pstarts = ends - padded
    dest = jnp.sum(pstarts[None, :] * onehot, axis=1) + rank
    n_blocks = -(-a // MOE_ROWS) + N_EXPERTS
    blk0 = jnp.arange(n_blocks, dtype=jnp.int32) * MOE_ROWS
    block_e = jnp.minimum(jnp.searchsorted(ends, blk0, side='right'), N_EXPERTS - 1).astype(jnp.int32)
    n_valid = jnp.clip((pstarts + counts)[block_e] - blk0, 0, MOE_ROWS)
    n_valid = ((n_valid + MOE_ROW_ALIGN - 1) // MOE_ROW_ALIGN * MOE_ROW_ALIGN).astype(jnp.int32)
    slot = jnp.arange(n_blocks * MOE_ROWS, dtype=jnp.int32)
    spare = a + MOE_ROW_ALIGN * jnp.repeat(block_e, MOE_ROWS) + slot % MOE_ROW_ALIGN
    asg = spare.at[dest].set(jnp.arange(a, dtype=jnp.int32))
    y = _moe_ffn(block_e, n_valid, asg, h2, w_gate, w_up, w_down)
    return y.reshape(-1, TOP_K * d), wts


def _combine_kernel(x_ref, y_ref, w_ref, mod_ref, o_ref):
    d = x_ref.shape[-1]
    w = w_ref[...]
    y = y_ref[:, :d] * w[:, 0:1] + y_ref[:, d:] * w[:, 1:2]
    o_ref[0] = x_ref[0] + mod_ref[0, 5:6, :] * y


def _combine(x, y2, wts, mod, mod_is_shared, row_offset, tm):
    b, l, d = x.shape
    mod_map = (lambda i, j: (0, 0, 0)) if mod_is_shared else (lambda i, j: (i, 0, 0))
    off = row_offset // tm
    per = l // tm
    return pl.pallas_call(
        _combine_kernel,
        grid=(b, per),
        in_specs=[pl.BlockSpec((1, tm, d), lambda i, j: (i, j, 0)),
                  pl.BlockSpec((tm, TOP_K * d), lambda i, j: (off + i * per + j, 0)),
                  pl.BlockSpec((tm, TOP_K), lambda i, j: (off + i * per + j, 0)),
                  pl.BlockSpec((1, 6, d), mod_map)],
        out_specs=pl.BlockSpec((1, tm, d), lambda i, j: (i, j, 0)),
        out_shape=jax.ShapeDtypeStruct((b, l, d), F32),
        compiler_params=_cparams(2),
        name="moe_combine",
    )(x, y2, wts, mod)


def _head_rms(x, gain, e_ref):
    ms = jnp.dot((x * x).astype(BF16), e_ref[...], preferred_element_type=F32) * (1.0 / HEAD_DIM)
    return x * lax.rsqrt(ms + EPS) * gain


def _rope(y, cos, sin):
    lane = lax.broadcasted_iota(jnp.int32, y.shape, 1)
    half = HEAD_DIM // 2
    rot = jnp.where((lane % HEAD_DIM) < half, pltpu.roll(y, LANES - half, 1), pltpu.roll(y, half, 1))
    return y * cos + rot * sin


def _attend(q, tiles, sink):
    scores = []
    for k, _, bias in tiles:
        s = lax.dot_general(q, k, (((1,), (1,)), ((), ())), preferred_element_type=F32)
        scores.append(s if bias is None else s + bias)
    if all(s.shape[1] % LANES == 0 for s in scores):
        scores = [jnp.concatenate(scores, axis=1)]
    m = sink
    for s in scores:
        mt = jnp.max(s, axis=-1, keepdims=True)
        m = mt if m is None else jnp.maximum(m, mt)
    den = jnp.exp(sink - m) if sink is not None else jnp.zeros_like(m)
    probs = []
    for s in scores:
        p = jnp.exp(s - m)
        den = den + jnp.sum(p, axis=-1, keepdims=True)
        probs.append(p.astype(BF16))
    if len(probs) != len(tiles):
        offs = np.cumsum([0] + [k.shape[0] for k, _, _ in tiles])
        probs = [probs[0][:, offs[i]:offs[i + 1]] for i in range(len(tiles))]
    o = None
    for p, (_, v, _) in zip(probs, tiles):
        pv = jnp.dot(p, v, preferred_element_type=F32)
        o = pv if o is None else o + pv
    return o * (1.0 / den)


def _lane_half(shape):
    return lax.broadcasted_iota(jnp.int32, shape, 1) // HEAD_DIM


def _win_group_queries(slabs, g, sink_ref):
    r = WIN_HEADS // WIN_KV_HEADS
    rows = slabs[0].shape[0]
    half = _lane_half(slabs[0].shape)
    parts, sinks = [], []
    for j in range(r):
        head = g * r + j
        slab = slabs[head // 2]
        if head % 2 != g:
            slab = pltpu.roll(slab, HEAD_DIM, 1)
        parts.append(jnp.where(half == g, slab, 0.0))
        sinks.append(jnp.full((rows, 1), sink_ref[head], F32))
    return jnp.concatenate(parts, axis=0).astype(BF16), jnp.concatenate(sinks, axis=0)


def _win_store(o_ref, o, g, rows):
    r = WIN_HEADS // WIN_KV_HEADS
    half = _lane_half((rows, LANES))
    for pair in range(r // 2):
        a = o[(2 * pair) * rows:(2 * pair + 1) * rows]
        b = o[(2 * pair + 1) * rows:(2 * pair + 2) * rows]
        if g == 0:
            b = pltpu.roll(b, HEAD_DIM, 1)
        else:
            a = pltpu.roll(a, HEAD_DIM, 1)
        s = (g * r + 2 * pair) // 2
        o_ref[0, :, s * LANES:(s + 1) * LANES] = jnp.where(half == 0, a, b).astype(o_ref.dtype)


def _win_kernel(sink_ref, q_ref, kv_ref, ckv_ref, cos_ref, sin_ref, qg_ref, kg_ref, e_ref, o_ref, ks_ref, kcs_ref):
    n = pl.program_id(1)
    nb = pl.num_programs(1)
    l = kv_ref.shape[1]
    blk = WIN_BLOCK
    prep_rows = 512

    @pl.when(n == 0)
    def _():
        def body(i, carry):
            r0 = pl.multiple_of(i * prep_rows, prep_rows)
            y = _head_rms(kv_ref[0, pl.ds(r0, prep_rows), 0:LANES].astype(F32), kg_ref[...], e_ref)
            ks_ref[pl.ds(r0, prep_rows), :] = _rope(y, cos_ref[pl.ds(r0, prep_rows), :],
                                                    sin_ref[pl.ds(r0, prep_rows), :]).astype(BF16)
            return carry
        lax.fori_loop(0, l // prep_rows, body, 0)
        kcs_ref[...] = _head_rms(ckv_ref[0, :, 0:LANES].astype(F32), kg_ref[...], e_ref).astype(BF16)

    r0 = pl.multiple_of(n * blk, blk)
    cos = cos_ref[pl.ds(r0, blk), :]
    sin = sin_ref[pl.ds(r0, blk), :]
    slabs = []
    for s in range(q_ref.shape[-1] // LANES):
        y = _head_rms(q_ref[0, :, s * LANES:(s + 1) * LANES].astype(F32), qg_ref[...], e_ref)
        slabs.append(_rope(y, cos, sin))

    r = WIN_HEADS // WIN_KV_HEADS
    qi = lax.broadcasted_iota(jnp.int32, (r * blk, blk), 0) % blk
    kj = lax.broadcasted_iota(jnp.int32, (r * blk, blk), 1)
    ok_prev = (kj >= qi) & (n > 0)
    ok_next = (kj <= qi) & (n < nb - 1)
    bias_prev = jnp.where(ok_prev, 0.0, NEG_INF)
    bias_next = jnp.where(ok_next, 0.0, NEG_INF)
    tiles = []
    for kb, bias in ((jnp.maximum(n - 1, 0), bias_prev), (n, None), (jnp.minimum(n + 1, nb - 1), bias_next)):
        k0 = pl.multiple_of(kb * blk, blk)
        tiles.append((ks_ref[pl.ds(k0, blk), :], kv_ref[0, pl.ds(k0, blk), LANES:2 * LANES], bias))
    tiles.append((kcs_ref[...], ckv_ref[0, :, LANES:2 * LANES], None))
    for g in range(WIN_KV_HEADS):
        q, sink = _win_group_queries(slabs, g, sink_ref)
        _win_store(o_ref, _attend(q, tiles, sink), g, blk)


def _win_ctx_kernel(sink_ref, q_ref, ckv_ref, qg_ref, kg_ref, e_ref, o_ref):
    kc = _head_rms(ckv_ref[0, :, 0:LANES].astype(F32), kg_ref[...], e_ref).astype(BF16)
    tiles = [(kc, ckv_ref[0, :, LANES:2 * LANES], None)]
    slabs = [_head_rms(q_ref[0, :, s * LANES:(s + 1) * LANES].astype(F32), qg_ref[...], e_ref)
             for s in range(q_ref.shape[-1] // LANES)]
    for g in range(WIN_KV_HEADS):
        q, sink = _win_group_queries(slabs, g, sink_ref)
        _win_store(o_ref, _attend(q, tiles, sink), g, q_ref.shape[1])


def _block_diag_ones(width):
    i = np.arange(width) // HEAD_DIM
    return jnp.asarray(i[:, None] == i[None, :], BF16)


def _window_branch(zw_l, zw_c, q_gain, k_gain, sink, cos2, sin2, ctx_out):
    b, l, _ = zw_l.shape
    lc = zw_c.shape[1]
    wq = WIN_HEADS * HEAD_DIM
    qg = (jnp.tile(q_gain.astype(F32), 2) * HEAD_DIM ** -0.5).reshape(1, LANES)
    kg = jnp.tile(k_gain.astype(F32), 2).reshape(1, LANES)
    e = _block_diag_ones(LANES)
    smem = pl.BlockSpec(memory_space=pltpu.SMEM)
    full = lambda s, nd: pl.BlockSpec(s, lambda *a: (0,) * nd)
    yl = pl.pallas_call(
        _win_kernel,
        grid=(b, l // WIN_BLOCK),
        in_specs=[smem,
                  pl.BlockSpec((1, WIN_BLOCK, wq), lambda i, n: (i, n, 0)),
                  pl.BlockSpec((1, l, 2 * LANES), lambda i, n: (i, 0, wq // (2 * LANES))),
                  pl.BlockSpec((1, lc, 2 * LANES), lambda i, n: (i, 0, wq // (2 * LANES))),
                  full((l, LANES), 2), full((l, LANES), 2), full((1, LANES), 2), full((1, LANES), 2),
                  full((LANES, LANES), 2)],
        out_specs=pl.BlockSpec((1, WIN_BLOCK, wq), lambda i, n: (i, n, 0)),
        out_shape=jax.ShapeDtypeStruct((b, l, wq), BF16),
        scratch_shapes=[pltpu.VMEM((l, LANES), BF16), pltpu.VMEM((lc, LANES), BF16)],
        compiler_params=_cparams(2),
        name="window_attn",
    )(sink.astype(F32), zw_l, zw_l, zw_c, cos2, sin2, qg, kg, e)
    yc = None
    if ctx_out:
        yc = pl.pallas_call(
            _win_ctx_kernel,
            grid=(b,),
            in_specs=[smem,
                      pl.BlockSpec((1, lc, wq), lambda i: (i, 0, 0)),
                      pl.BlockSpec((1, lc, 2 * LANES), lambda i: (i, 0, wq // (2 * LANES))),
                      full((1, LANES), 2), full((1, LANES), 2), full((LANES, LANES), 2)],
            out_specs=pl.BlockSpec((1, lc, wq), lambda i: (i, 0, 0)),
            out_shape=jax.ShapeDtypeStruct((b, lc, wq), BF16),
            compiler_params=_cparams(1),
            name="window_ctx_attn",
        )(sink.astype(F32), zw_c, zw_c, qg, kg, e)
    return yc, yl


NA_ROWS = 8
NA_KROWS = NA_ROWS + NA_KH - 1


def _na_key_base(rg, rows):
    return jnp.clip(rg * NA_ROWS - NA_KH // 2, 0, rows - NA_KROWS)


def _na_kernel(q_ref, k_ref, v_ref, ck_ref, cv_ref, bias_ref, qg_ref, kg_ref, e_ref, o_ref, ks_ref, kcs_ref):
    rg = pl.program_id(1)
    l = k_ref.shape[1]
    rows = l // GRID_W
    n_slab = q_ref.shape[-1] // LANES
    prep_rows = 512

    @pl.when(rg == 0)
    def _():
        def body(i, carry):
            r0 = pl.multiple_of(i * prep_rows, prep_rows)
            ks_ref[pl.ds(r0, prep_rows), :] = _head_rms(k_ref[0, pl.ds(r0, prep_rows), :].astype(F32), kg_ref[...],
                                                        e_ref).astype(BF16)
            return carry
        lax.fori_loop(0, l // prep_rows, body, 0)
        kcs_ref[...] = _head_rms(ck_ref[0].astype(F32), kg_ref[...], e_ref).astype(BF16)

    nk = NA_KROWS * GRID_W
    k0 = pl.multiple_of(_na_key_base(rg, rows) * GRID_W, GRID_W)
    qn = _head_rms(q_ref[0].astype(F32), qg_ref[...], e_ref)
    half = _lane_half((q_ref.shape[1], LANES))
    for s in range(n_slab):
        cols = slice(s * LANES, (s + 1) * LANES)
        slab = qn[:, cols]
        tiles_kv = (ks_ref[pl.ds(k0, nk), cols], v_ref[0, pl.ds(k0, nk), cols])
        ctx_kv = (kcs_ref[:, cols], cv_ref[0, :, cols])
        outs = []
        for hh in range(2):
            q = jnp.where(half == hh, slab, 0.0).astype(BF16)
            bias = bias_ref[0, 2 * s + hh].astype(F32)
            outs.append(_attend(q, [tiles_kv + (bias,), ctx_kv + (None,)], None))
        o_ref[0, :, cols] = jnp.where(half == 0, outs[0], outs[1]).astype(o_ref.dtype)


def _na_ctx_kernel(q_ref, ck_ref, cv_ref, qg_ref, kg_ref, e_ref, o_ref):
    kc = _head_rms(ck_ref[0].astype(F32), kg_ref[...], e_ref).astype(BF16)
    qn = _head_rms(q_ref[0].astype(F32), qg_ref[...], e_ref)
    half = _lane_half((q_ref.shape[1], LANES))
    for s in range(q_ref.shape[-1] // LANES):
        cols = slice(s * LANES, (s + 1) * LANES)
        outs = []
        for hh in range(2):
            q = jnp.where(half == hh, qn[:, cols], 0.0).astype(BF16)
            outs.append(_attend(q, [(kc[:, cols], cv_ref[0, :, cols], None)], None))
        o_ref[0, :, cols] = jnp.where(half == 0, outs[0], outs[1]).astype(o_ref.dtype)


def _na_bias_classes(rows):
    n_rg = rows // NA_ROWS
    return list(range(n_rg)) if n_rg <= 3 else [0, 1, n_rg - 1]


def _na_bias_table(rpb, rows):
    kh = NA_KH
    ro, rv = [], []
    for rg in _na_bias_classes(rows):
        kbase = int(np.clip(rg * NA_ROWS - kh // 2, 0, rows - NA_KROWS))
        r = rg * NA_ROWS + np.arange(NA_ROWS)
        rstart = np.clip(r - kh // 2, 0, rows - kh)
        kr = kbase + np.arange(NA_KROWS)
        rv.append((kr[None, :] >= rstart[:, None]) & (kr[None, :] < rstart[:, None] + kh))
        ro.append(np.clip(kr[None, :] - r[:, None] + kh - 1, 0, 2 * kh - 2))
    ro, rv = np.stack(ro), np.stack(rv)
    qc = np.arange(GRID_W)
    cstart = np.clip(qc - NA_KW // 2, 0, GRID_W - NA_KW)
    cv = (qc[None, :] >= cstart[:, None]) & (qc[None, :] < cstart[:, None] + NA_KW)
    co = np.clip(qc[None, :] - qc[:, None] + NA_KW - 1, 0, 2 * NA_KW - 2)
    n_cls = ro.shape[0]
    hp = lax.Precision.HIGHEST
    co_hot = jnp.asarray(co[None] == np.arange(2 * NA_KW - 1)[:, None, None], F32)
    ro_hot = jnp.asarray(ro[..., None] == np.arange(2 * kh - 1), F32)
    cols = jnp.einsum('hdc,cwx->hdwx', rpb.astype(F32), co_hot, precision=hp)
    tab = jnp.einsum('crkd,hdwx->chrwkx', ro_hot, cols, precision=hp)
    ok = rv[:, None, :, None, :, None] & cv[None, None, None, :, None, :]
    tab = jnp.where(ok, tab, NEG_INF)
    return tab.reshape(n_cls, rpb.shape[0], NA_ROWS * GRID_W, NA_KROWS * GRID_W).astype(BF16)


def _neighborhood_branch(zn_l, zn_c, q_gain, k_gain, rpb, ctx_out):
    b, l, _ = zn_l.shape
    lc = zn_c.shape[1]
    w = NA_HEADS * HEAD_DIM
    rows = l // GRID_W
    n_rg = rows // NA_ROWS
    n_slab = w // LANES
    qg = (jnp.tile(q_gain.astype(F32), 2 * n_slab) * HEAD_DIM ** -0.5).reshape(1, w)
    kg = jnp.tile(k_gain.astype(F32), 2 * n_slab).reshape(1, w)
    e = _block_diag_ones(w)
    bias = _na_bias_table(rpb, rows)
    if n_rg <= 3:
        cls_map = lambda i, r: (r, 0, 0, 0)
    else:
        cls_map = lambda i, r: ((r > 0).astype(jnp.int32) + (r == n_rg - 1).astype(jnp.int32), 0, 0, 0)
    tq = NA_ROWS * GRID_W
    full = lambda s, nd: pl.BlockSpec(s, lambda *a: (0,) * nd)
    yl = pl.pallas_call(
        _na_kernel,
        grid=(b, n_rg),
        in_specs=[pl.BlockSpec((1, tq, w), lambda i, r: (i, r, 0)),
                  pl.BlockSpec((1, l, w), lambda i, r: (i, 0, 1)),
                  pl.BlockSpec((1, l, w), lambda i, r: (i, 0, 2)),
                  pl.BlockSpec((1, lc, w), lambda i, r: (i, 0, 1)),
                  pl.BlockSpec((1, lc, w), lambda i, r: (i, 0, 2)),
                  pl.BlockSpec((1,) + bias.shape[1:], cls_map),
                  full((1, w), 2), full((1, w), 2), full((w, w), 2)],
        out_specs=pl.BlockSpec((1, tq, w), lambda i, r: (i, r, 0)),
        out_shape=jax.ShapeDtypeStruct((b, l, w), BF16),
        scratch_shapes=[pltpu.VMEM((l, w), BF16), pltpu.VMEM((lc, w), BF16)],
        compiler_params=_cparams(2),
        name="neighborhood_attn",
    )(zn_l, zn_l, zn_l, zn_c, zn_c, bias, qg, kg, e)
    yc = None
    if ctx_out:
        yc = pl.pallas_call(
            _na_ctx_kernel,
            grid=(b,),
            in_specs=[pl.BlockSpec((1, lc, w), lambda i: (i, 0, 0)),
                      pl.BlockSpec((1, lc, w), lambda i: (i, 0, 1)),
                      pl.BlockSpec((1, lc, w), lambda i: (i, 0, 2)),
                      full((1, w), 2), full((1, w), 2), full((w, w), 2)],
            out_specs=pl.BlockSpec((1, lc, w), lambda i: (i, 0, 0)),
            out_shape=jax.ShapeDtypeStruct((b, lc, w), BF16),
            compiler_params=_cparams(1),
            name="neighborhood_ctx_attn",
        )(zn_c, zn_c, zn_c, qg, kg, e)
    return yc, yl


def _rope_tables(n_tokens):
    t = np.arange(n_tokens)
    n_freq = HEAD_DIM // 4
    inv = ROPE_THETA ** (-jnp.arange(n_freq, dtype=F32) / n_freq)
    ang = jnp.concatenate([jnp.asarray(t // GRID_W, F32)[:, None] * inv, jnp.asarray(t % GRID_W, F32)[:, None] * inv],
                          axis=-1)
    cos, sin = jnp.cos(ang), jnp.sin(ang)
    return jnp.tile(jnp.concatenate([cos, cos], -1), (1, 2)), jnp.tile(jnp.concatenate([-sin, sin], -1), (1, 2))


def _ret_kernel(qq_l, kk_l, v_l, g_l, qq_c, kk_c, v_c, g_c, cos_ref, sin_ref, dmask_ref, tq_ref, tk_ref, cdec_ref,
                gain_ref, yl_ref, yc_ref, kr_ref, kv_ref, sin_state_ref, *, ctx_out):
    c = RET_CHUNK
    ncc = qq_c.shape[1] // c
    ncl = qq_l.shape[1] // c
    nc = ncc + ncl
    tk = tk_ref[0]
    tq = tq_ref[0]
    dmask = dmask_ref[0]
    fwd_lanes = lax.broadcasted_iota(jnp.int32, (c, LANES), 1) < RET_DK

    def chunk_kv(k2, v):
        kd = (k2 * tk).astype(BF16)
        return lax.dot_general(kd, v, (((0,), (0,)), ((), ())), preferred_element_type=F32)

    for n in range(ncc):
        kv_ref[n] = chunk_kv(kk_c[0, n * c:(n + 1) * c, :].astype(F32), v_c[0, n * c:(n + 1) * c, :])

    def kv_body(n, carry):
        r0 = pl.multiple_of(n * c, c)
        k2 = _rope(kk_l[0, pl.ds(r0, c), :].astype(F32), cos_ref[pl.ds(r0, c), :], sin_ref[pl.ds(r0, c), :])
        kr_ref[pl.ds(r0, c), :] = k2.astype(BF16)
        kv_ref[ncc + n] = chunk_kv(k2, v_l[0, pl.ds(r0, c), :])
        return carry
    lax.fori_loop(0, ncl, kv_body, 0, unroll=4)

    dec_f = cdec_ref[0, 0:1, :]
    dec_b = cdec_ref[0, 1:2, :]

    def scan_body(t, carry):
        sf, sb = carry
        sin_state_ref[t, 0:RET_DK, :] = sf.astype(BF16)
        sf = sf * dec_f + kv_ref[t, 0:RET_DK, :]
        tb = jnp.where(t < ncc, ncc - 1 - t, nc - 1 - (t - ncc))
        sin_state_ref[tb, RET_DK:2 * RET_DK, :] = sb.astype(BF16)
        sb = sb * dec_b + kv_ref[tb, RET_DK:2 * RET_DK, :]
        return sf, sb
    zero = jnp.zeros((RET_DK, RET_DV), F32)
    lax.fori_loop(0, nc, scan_body, (zero, zero))

    def chunk_out(q2, k2b, v, gate, state):
        qm = jnp.where(fwd_lanes, q2, 0.0).astype(BF16)
        scores = lax.dot_general(qm, k2b, (((1,), (1,)), ((), ())), preferred_element_type=F32) * dmask
        o = (jnp.dot(scores.astype(BF16), v, preferred_element_type=F32)
             + jnp.dot((q2 * tq).astype(BF16), state, preferred_element_type=F32))
        mu = jnp.mean(o, axis=-1, keepdims=True)
        var = jnp.mean(jnp.square(o - mu), axis=-1, keepdims=True)
        y = (o - mu) * lax.rsqrt(var + EPS) * gain_ref[0]
        gf = gate.astype(F32)
        return y * gf * jax.nn.sigmoid(gf)

    if ctx_out:
        for n in range(ncc):
            rows = slice(n * c, (n + 1) * c)
            yc_ref[0, rows, :] = chunk_out(qq_c[0, rows, :].astype(F32), kk_c[0, rows, :], v_c[0, rows, :],
                                           g_c[0, rows, :], sin_state_ref[n]).astype(yc_ref.dtype)
    else:
        yc_ref[...] = jnp.zeros_like(yc_ref)

    def out_body(n, carry):
        r0 = pl.multiple_of(n * c, c)
        q2 = _rope(qq_l[0, pl.ds(r0, c), :].astype(F32), cos_ref[pl.ds(r0, c), :], sin_ref[pl.ds(r0, c), :])
        yl_ref[0, pl.ds(r0, c), :] = chunk_out(q2, kr_ref[pl.ds(r0, c), :], v_l[0, pl.ds(r0, c), :],
                                               g_l[0, pl.ds(r0, c), :], sin_state_ref[ncc + n]).astype(yl_ref.dtype)
        return carry
    lax.fori_loop(0, ncl, out_body, 0, unroll=4)


def _ret_tables(decay_logit):
    c = RET_CHUNK
    lg = jax.nn.log_sigmoid(decay_logit.astype(F32))
    lf, lb = lg[0][:, None, None], lg[1][:, None, None]
    pos = jnp.arange(c, dtype=F32)
    diff = pos[:, None] - pos[None, :]
    dmask = (jnp.where(diff >= 0, jnp.exp(lf * jnp.maximum(diff, 0.0)), 0.0)
             + jnp.where(diff <= 0, jnp.exp(lb * jnp.maximum(-diff, 0.0)), 0.0)) * RET_DK ** -0.5
    col = lambda a, b_: jnp.concatenate([jnp.broadcast_to(a, a.shape[:2] + (RET_DK,)),
                                         jnp.broadcast_to(b_, b_.shape[:2] + (RET_DK,))], axis=-1)
    p = pos[None, :, None]
    tq = col(jnp.exp(lf * (p + 1.0)), jnp.exp(lb * (c - p)))
    tk = col(jnp.exp(lf * (c - 1.0 - p)), jnp.exp(lb * p)) * RET_DK ** -0.5
    cdec = jnp.zeros((lg.shape[1], 8, RET_DV), F32)
    cdec = cdec.at[:, 0, :].set(jnp.exp(lg[0] * c)[:, None]).at[:, 1, :].set(jnp.exp(lg[1] * c)[:, None])
    return dmask, tq, tk, cdec


def _retention_branch(zr_l, zr_c, decay_logit, gn_gain, cos2, sin2, ctx_out):
    b, l, _ = zr_l.shape
    lc = zr_c.shape[1]
    h = RET_HEADS
    nc = (l + lc) // RET_CHUNK
    dmask, tq, tk, cdec = _ret_tables(decay_logit)
    gain = gn_gain.astype(F32).reshape(h, 1, RET_DV)
    seq = lambda n, j: pl.BlockSpec((1, n, LANES), lambda i, hh: (i, 0, 4 * hh + j))
    head = lambda s: pl.BlockSpec((1,) + s, lambda i, hh: (hh, 0, 0))
    full = lambda s: pl.BlockSpec(s, lambda i, hh: (0, 0))
    yl, yc = pl.pallas_call(
        functools.partial(_ret_kernel, ctx_out=ctx_out),
        grid=(b, h),
        in_specs=[seq(l, 0), seq(l, 1), seq(l, 2), seq(l, 3), seq(lc, 0), seq(lc, 1), seq(lc, 2), seq(lc, 3),
                  full((l, LANES)), full((l, LANES)),
                  head((RET_CHUNK, RET_CHUNK)), head((RET_CHUNK, LANES)), head((RET_CHUNK, LANES)), head((8, RET_DV)),
                  head((1, RET_DV))],
        out_specs=[pl.BlockSpec((1, l, RET_DV), lambda i, hh: (i, 0, hh)),
                   pl.BlockSpec((1, lc, RET_DV), lambda i, hh: (i, 0, hh))],
        out_shape=[jax.ShapeDtypeStruct((b, l, h * RET_DV), BF16), jax.ShapeDtypeStruct((b, lc, h * RET_DV), BF16)],
        scratch_shapes=[pltpu.VMEM((l, LANES), BF16), pltpu.VMEM((nc, 2 * RET_DK, RET_DV), F32),
                        pltpu.VMEM((nc, 2 * RET_DK, RET_DV), BF16)],
        compiler_params=_cparams(2),
        name="retention",
    )(zr_l, zr_l, zr_l, zr_l, zr_c, zr_c, zr_c, zr_c, cos2, sin2, dmask, tq, tk, cdec, gain)
    return (yc if ctx_out else None), yl


GDN_SUPER = 128
GDN_UNITS = 4
GDN_HALO = 128


def _split_bf16(a):
    hi = a.astype(BF16)
    return hi, (a - hi.astype(F32)).astype(BF16)


def _mask_dot(mask_bf16, a):
    ah, al = _split_bf16(a)
    return jnp.dot(mask_bf16, ah, preferred_element_type=F32) + jnp.dot(mask_bf16, al, preferred_element_type=F32)


def _softplus(x):
    return jnp.maximum(x, 0.0) + jnp.log(1.0 + jnp.exp(-jnp.abs(x)))


def _gdn_kernel(nega_ref, dtb_ref, q_l, k_l, v_l, g_l, ab_l, q_c, k_c, v_c, g_c, ab_c,
                cwq_ref, cwk_ref, cwv_ref, pd_ref, pu_ref, gain_ref, yl_ref, yc_ref,
                kn_s, sin_s, qp_s, o0_s, cd_s, *, ctx_out):
    hd = pl.program_id(1)
    c = GDN_CHUNK
    sup = GDN_SUPER
    per = sup // c
    lc, l = q_c.shape[1], q_l.shape[1]
    ncc, ncl = lc // c, l // c
    nc = ncc + ncl

    ri = lax.broadcasted_iota(jnp.int32, (sup, sup), 0)
    ci = lax.broadcasted_iota(jnp.int32, (sup, sup), 1)
    same = (ri // c) == (ci // c)
    eye = (ri == ci).astype(F32)
    incl = (same & (ri >= ci), same & (ri <= ci))
    strict = (same & (ri > ci), same & (ri < ci))
    incl_b = tuple(m.astype(BF16) for m in incl)
    same_b = same.astype(BF16)
    lane = lax.broadcasted_iota(jnp.int32, (sup, LANES), 1)
    rowc = lax.broadcasted_iota(jnp.int32, (sup, LANES), 0)

    def conv_silu(z_ref, w_ref, r0, ls):
        z = z_ref[0, pl.ds(r0, sup), :]
        if isinstance(r0, int):
            zero = jnp.zeros((GDN_HALO, LANES), BF16)
            prev = z_ref[0, r0 - GDN_HALO:r0, :] if r0 > 0 else zero
            nxt = z_ref[0, r0 + sup:r0 + sup + GDN_HALO, :] if r0 + sup < ls else zero
        else:
            p0 = pl.multiple_of(jnp.maximum(r0 - GDN_HALO, 0), GDN_HALO)
            n0 = pl.multiple_of(jnp.minimum(r0 + sup, ls - GDN_HALO), GDN_HALO)
            prev = jnp.where(r0 > 0, z_ref[0, pl.ds(p0, GDN_HALO), :], jnp.zeros((), BF16))
            nxt = jnp.where(r0 + sup < ls, z_ref[0, pl.ds(n0, GDN_HALO), :], jnp.zeros((), BF16))
        win = jnp.concatenate([prev, z, nxt], axis=0)
        z_dn = jnp.dot(pd_ref[...], win, preferred_element_type=F32)
        z_up = jnp.dot(pu_ref[...], win, preferred_element_type=F32)
        y = z_dn * w_ref[0, 0:1, :] + z.astype(F32) * w_ref[0, 1:2, :] + z_up * w_ref[0, 2:3, :]
        return y * jax.nn.sigmoid(y)

    def prep(units):
        common = []
        for refs, r0, base, ls in units:
            q_ref, k_ref, v_ref, ab_ref = refs
            q = conv_silu(q_ref, cwq_ref, r0, ls)
            k = conv_silu(k_ref, cwk_ref, r0, ls)
            v = conv_silu(v_ref, cwv_ref, r0, ls)
            q = q * lax.rsqrt(jnp.sum(q * q, axis=-1, keepdims=True) + EPS) * GDN_DK ** -0.5
            k = k * lax.rsqrt(jnp.sum(k * k, axis=-1, keepdims=True) + EPS)
            kb16 = k.astype(BF16)
            qk = lax.dot_general(q.astype(BF16), kb16, (((1,), (1,)), ((), ())), preferred_element_type=F32)
            ab = ab_ref[0, pl.ds(r0, sup), :]
            common.append((q, k, v, kb16, qk, ab, base + r0))
        chains = []
        for q, k, v, kb16, qk, ab, row0 in common:
            for d in range(2):
                ia = d * GDN_HEADS + hd
                ib = 2 * GDN_HEADS + ia
                a_col = jnp.sum(jnp.where(lane == ia, ab, 0.0), axis=1, keepdims=True)
                b_col = jnp.sum(jnp.where(lane == ib, ab, 0.0), axis=1, keepdims=True)
                g_col = nega_ref[d, hd] * _softplus(a_col + dtb_ref[d, hd])
                beta = jax.nn.sigmoid(b_col)
                gcum = _mask_dot(incl_b[d], jnp.broadcast_to(g_col, (sup, LANES)))
                gtot = _mask_dot(same_b, jnp.broadcast_to(g_col, (sup, LANES)))
                gc = gcum[:, 0:1]
                decay = jnp.where(incl[d], jnp.exp(jnp.where(incl[d], gcum - gcum.T, 0.0)), 0.0)
                kbeta = k * beta
                kk = lax.dot_general(kbeta.astype(BF16), kb16, (((1,), (1,)), ((), ())),
                                     preferred_element_type=F32)
                a_mat = jnp.where(strict[d], kk * decay, 0.0)
                eg = jnp.exp(gc)
                rhs = jnp.concatenate([kbeta * eg, v * beta], axis=1).astype(BF16)
                aqk = jnp.where(incl[d], qk * decay, 0.0).astype(BF16)
                ke = (k * jnp.exp(gtot[:, 0:1] - gc)).astype(BF16)
                cds = [jnp.exp(gtot[j * c:j * c + 1, :]) for j in range(per)]
                chains.append(dict(d=d, row0=row0, a=a_mat, rhs=rhs, aqk=aqk, ke=ke, cds=cds, qeg=q * eg))
        for ch in chains:
            ch['inv'] = eye - ch['a']
            ch['pw'] = ch['a'].astype(BF16)
        for _ in range(5):
            for ch in chains:
                ch['pw'] = jnp.dot(ch['pw'], ch['pw'], preferred_element_type=F32).astype(BF16)
            for ch in chains:
                ch['inv'] = ch['inv'] + jnp.dot(ch['inv'].astype(BF16), ch['pw'], preferred_element_type=F32)
        for ch in chains:
            ch['wu'] = jnp.dot(ch['inv'].astype(BF16), ch['rhs'], preferred_element_type=F32).astype(BF16)
        for ch in chains:
            d, row0 = ch['d'], ch['row0']
            awu = jnp.dot(ch['aqk'], ch['wu'], preferred_element_type=F32)
            rows = pl.ds(row0, sup)
            qp_s[d, rows, :] = (ch['qeg'] - awu[:, :GDN_DK]).astype(BF16)
            o0_s[d, rows, :] = awu[:, GDN_DK:]
            for j in range(per):
                kej = jnp.where((rowc // c) == j, ch['ke'], jnp.zeros((), BF16))
                idx = row0 // c + j
                kn_s[d, idx] = lax.dot_general(kej, ch['wu'], (((0,), (0,)), ((), ())),
                                               preferred_element_type=F32).astype(BF16)
                cd_s[d, idx] = ch['cds'][j]

    prep([((q_c, k_c, v_c, ab_c), n * sup, 0, lc) for n in range(lc // sup)])

    def prep_body(n, carry):
        prep([((q_l, k_l, v_l, ab_l), pl.multiple_of((GDN_UNITS * n + j) * sup, sup), lc, l)
              for j in range(GDN_UNITS)])
        return carry
    lax.fori_loop(0, l // (GDN_UNITS * sup), prep_body, 0)

    def chunk_step(d, idx, s):
        sb = s.astype(BF16)
        sin_s[d, idx] = sb
        kn = kn_s[d, idx]
        return (s * cd_s[d, idx] - jnp.dot(kn[:, :GDN_DK], sb, preferred_element_type=F32)
                + kn[:, GDN_DK:].astype(F32))

    def scan_body(t, carry):
        sf, sb = carry
        tb = jnp.where(t < ncc, ncc - 1 - t, nc - 1 - (t - ncc))
        return chunk_step(0, t, sf), chunk_step(1, tb, sb)
    zero = jnp.zeros((GDN_DK, GDN_DV), F32)
    lax.fori_loop(0, nc, scan_body, (zero, zero))

    def finish(y_ref, gate_ref, base, n_rows):
        def body(n, carry):
            r0 = pl.multiple_of(n * c, c)
            rows = pl.ds(base + r0, c)
            idx = base // c + n
            o = (o0_s[0, rows, :] + o0_s[1, rows, :]
                 + jnp.dot(qp_s[0, rows, :], sin_s[0, idx], preferred_element_type=F32)
                 + jnp.dot(qp_s[1, rows, :], sin_s[1, idx], preferred_element_type=F32))
            y = o * lax.rsqrt(jnp.mean(o * o, axis=-1, keepdims=True) + EPS) * gain_ref[...]
            gf = gate_ref[0, pl.ds(r0, c), :].astype(F32)
            y_ref[0, pl.ds(r0, c), :] = (y * gf * jax.nn.sigmoid(gf)).astype(y_ref.dtype)
            return carry
        lax.fori_loop(0, n_rows // c, body, 0, unroll=4)

    finish(yl_ref, g_l, lc, l)
    if ctx_out:
        finish(yc_ref, g_c, 0, lc)
    else:
        yc_ref[...] = jnp.zeros_like(yc_ref)


def _gdn_branch(zg_l, zab_l, zg_c, zab_c, conv_w, a_log, dt_bias, norm_gain, ctx_out):
    b, l, _ = zg_l.shape
    lc = zg_c.shape[1]
    h = GDN_HEADS
    sup = GDN_SUPER
    ltot = l + lc
    nc = ltot // GDN_CHUNK
    neg_a = -jnp.exp(a_log.astype(F32))
    cw = conv_w.astype(F32).T.reshape(3 * h, LANES, SHORT_CONV).transpose(0, 2, 1)
    win = sup + 2 * GDN_HALO
    i = np.arange(sup)
    pd = np.zeros((sup, win), np.float32)
    pu = np.zeros((sup, win), np.float32)
    pd[i, GDN_HALO + i - 1] = 1.0
    pu[i, GDN_HALO + i + 1] = 1.0
    seq = lambda n, j: pl.BlockSpec((1, n, LANES), lambda bi, hh: (bi, 0, j * h + hh))
    abs_ = lambda n: pl.BlockSpec((1, n, LANES), lambda bi, hh: (bi, 0, 0))
    cws = lambda j: pl.BlockSpec((1, SHORT_CONV, LANES), lambda bi, hh: (j * h + hh, 0, 0))
    full = lambda s: pl.BlockSpec(s, lambda bi, hh: (0, 0))
    smem = pl.BlockSpec(memory_space=pltpu.SMEM)
    yl, yc = pl.pallas_call(
        functools.partial(_gdn_kernel, ctx_out=ctx_out),
        grid=(b, h),
        in_specs=[smem, smem, seq(l, 0), seq(l, 1), seq(l, 2), seq(l, 3), abs_(l),
                  seq(lc, 0), seq(lc, 1), seq(lc, 2), seq(lc, 3), abs_(lc),
                  cws(0), cws(1), cws(2), full((sup, win)), full((sup, win)), full((1, GDN_DV))],
        out_specs=[pl.BlockSpec((1, l, GDN_DV), lambda bi, hh: (bi, 0, hh)),
                   pl.BlockSpec((1, lc, GDN_DV), lambda bi, hh: (bi, 0, hh))],
        out_shape=[jax.ShapeDtypeStruct((b, l, h * GDN_DV), BF16), jax.ShapeDtypeStruct((b, lc, h * GDN_DV), BF16)],
        scratch_shapes=[pltpu.VMEM((2, nc, GDN_DK, GDN_DK + GDN_DV), BF16), pltpu.VMEM((2, nc, GDN_DK, GDN_DV), BF16),
                        pltpu.VMEM((2, ltot, GDN_DK), BF16), pltpu.VMEM((2, ltot, GDN_DV), F32),
                        pltpu.VMEM((2, nc, 1, GDN_DV), F32)],
        compiler_params=_cparams(2),
        name="gated_deltanet",
    )(neg_a, dt_bias.astype(F32), zg_l, zg_l, zg_l, zg_l, zab_l, zg_c, zg_c, zg_c, zg_c, zab_c,
      cw, cw, cw, jnp.asarray(pd, BF16), jnp.asarray(pu, BF16), norm_gain.astype(F32).reshape(1, GDN_DV))
    return (yc if ctx_out else None), yl


def kernel(x, c, ctx, c_ctx, w_mod, b_mod, norm1, norm2, w_in, ret_decay, ret_gn, win_qnorm, win_knorm, win_sink,
           na_qnorm, na_knorm, na_rpb, gdn_conv, gdn_a_log, gdn_dt_bias, gdn_norm, w_branch, w_merge, w_out,
           w_router, router_bias, w_e_gate, w_e_up, w_e_down):
    b, l, d = x.shape
    lc = ctx.shape[1]
    depth = w_mod.shape[0]
    cos2, sin2 = _rope_tables(l)

    n_rows = 16
    cc = jnp.zeros((n_rows, d), F32).at[:b].set(c).at[b].set(c_ctx)
    mod = _modulation(cc, w_mod, b_mod).reshape(depth, n_rows, 6, d)

    wr = jnp.zeros((d, LANES), F32).at[:, :N_EXPERTS].set(w_router)
    wr_hi = wr.astype(BF16)
    wr_lo = (wr - wr_hi.astype(F32)).astype(BF16)

    xl, xc = x, ctx
    for layer in range(depth):
        ctx_out = layer < depth - 1
        mod_l = mod[layer, :b]
        mod_c = mod[layer, b:b + 1]
        w_all = _pack_w_in(w_in[layer])
        hl, *zl_s = _inproj(xl, mod_l, False, norm1[layer], w_all, 256)
        hc, *zc_s = _inproj(xc, mod_c, True, norm1[layer], w_all, 256)
        zr_l, zw_l, zn_l, zg_l, zab_l = zl_s
        zr_c, zw_c, zn_c, zg_c, zab_c = zc_s
        ret_c, ret_l = _retention_branch(zr_l, zr_c, ret_decay[layer], ret_gn[layer], cos2, sin2, ctx_out)
        win_c, win_l = _window_branch(zw_l, zw_c, win_qnorm[layer], win_knorm[layer], win_sink[layer], cos2, sin2,
                                      ctx_out)
        na_c, na_l = _neighborhood_branch(zn_l, zn_c, na_qnorm[layer], na_knorm[layer], na_rpb[layer], ctx_out)
        gdn_c, gdn_l = _gdn_branch(zg_l, zab_l, zg_c, zab_c, gdn_conv[layer], gdn_a_log[layer], gdn_dt_bias[layer],
                                   gdn_norm[layer], ctx_out)
        wm = w_merge[layer].astype(BF16)
        wb = w_branch[layer].astype(BF16)
        wo = w_out[layer].astype(BF16)
        ys_l = [ret_l, win_l, na_l, gdn_l]
        xl, h2l, sc_l = _merge(xl, hl, ys_l, mod_l, False, norm2[layer], wm, wb, wo, wr_hi, wr_lo, 256)
        if ctx_out:
            ys_c = [ret_c, win_c, na_c, gdn_c]
            xc, h2c, sc_c = _merge(xc, hc, ys_c, mod_c, True, norm2[layer], wm, wb, wo, wr_hi, wr_lo, 256)
            tokens = jnp.concatenate([h2c.reshape(b * lc, d), h2l.reshape(b * l, d)], axis=0)
            scores = jnp.concatenate([sc_c.reshape(b * lc, LANES), sc_l.reshape(b * l, LANES)], axis=0)
            y2, wts = _moe(tokens, scores[:, :N_EXPERTS], router_bias, w_e_gate[layer], w_e_up[layer],
                           w_e_down[layer])
            xc = _combine(xc, y2, wts, mod_c, True, 0, 256)
            xl = _combine(xl, y2, wts, mod_l, False, b * lc, 256)
        else:
            y2, wts = _moe(h2l.reshape(b * l, d), sc_l.reshape(b * l, LANES)[:, :N_EXPERTS], router_bias,
                           w_e_gate[layer], w_e_up[layer], w_e_down[layer])
            xl = _combine(xl, y2, wts, mod_l, False, 0, 256)
    return xl
```

```python
import functools

import numpy as np
import jax
import jax.numpy as jnp
from jax import lax
from jax.experimental import pallas as pl
from jax.experimental.pallas import tpu as pltpu

F32 = jnp.float32
BF16 = jnp.bfloat16
EPS = 1e-6
NEG_INF = -1e30
D_MODEL = 1024
GRID_W = 64
HEAD_DIM = 64
ROPE_THETA = 10000.0
RET_HEADS, RET_DK, RET_DV, RET_CHUNK = 4, 64, 128, 128
WIN_HEADS, WIN_KV_HEADS, WINDOW, WIN_BLOCK = 8, 2, 128, 128
NA_HEADS, NA_KH, NA_KW, NA_QCOLS = 8, 8, 16, 16
NA_BAND = NA_QCOLS + NA_KW
GDN_HEADS, GDN_DK, GDN_DV, GDN_CHUNK, SHORT_CONV = 4, 128, 128, 64, 3
GDN_QKV = 2 * GDN_HEADS * GDN_DK + GDN_HEADS * GDN_DV
N_BRANCH, BRANCH_W = 4, 512
N_EXPERTS, N_GROUPS, TOP_K, D_EXPERT = 32, 8, 2, 512
EXPERTS_PER_GROUP = N_EXPERTS // N_GROUPS

LANES = 128
VMEM_LIMIT = 56 * 1024 * 1024
MOE_ROWS = 256
MOE_ROW_ALIGN = 8

W_RET = RET_HEADS * 4 * LANES
W_WIN = (WIN_HEADS + 2 * WIN_KV_HEADS) * HEAD_DIM
W_NA = 3 * NA_HEADS * HEAD_DIM
W_GDN = GDN_QKV + GDN_HEADS * GDN_DV
W_AB = LANES
SECTION_WIDTHS = (W_RET, W_WIN, W_NA, W_GDN, W_AB)
W_ALL = sum(SECTION_WIDTHS)


def _cparams(n_axes):
    return pltpu.CompilerParams(dimension_semantics=("arbitrary",) * n_axes, vmem_limit_bytes=VMEM_LIMIT)


def _mod_kernel(c_ref, w_ref, b_ref, o_ref):
    c = c_ref[...]
    a = (c * jax.nn.sigmoid(c)).astype(BF16)
    o_ref[0] = jnp.dot(a, w_ref[0].astype(BF16), preferred_element_type=F32) + b_ref[0]


def _modulation(cc, w_mod, b_mod):
    depth, d, n = w_mod.shape
    r = cc.shape[0]
    tn = 1536
    return pl.pallas_call(
        _mod_kernel,
        grid=(depth, n // tn),
        in_specs=[pl.BlockSpec((r, d), lambda l, j: (0, 0)),
                  pl.BlockSpec((1, d, tn), lambda l, j: (l, 0, j)),
                  pl.BlockSpec((1, 1, tn), lambda l, j: (l, 0, j))],
        out_specs=pl.BlockSpec((1, r, tn), lambda l, j: (l, 0, j)),
        out_shape=jax.ShapeDtypeStruct((depth, r, n), F32),
        compiler_params=_cparams(2),
        name="modulation",
    )(cc, w_mod, b_mod.reshape(depth, 1, n))


def _inproj_kernel(x_ref, mod_ref, gain_ref, w_ref, h_ref, *z_refs):
    x = x_ref[0]
    ms = jnp.mean(x * x, axis=-1, keepdims=True)
    shift = mod_ref[0, 0:1, :]
    scale = mod_ref[0, 1:2, :]
    h = x * lax.rsqrt(ms + EPS) * gain_ref[...] * (1.0 + scale) + shift
    hb = h.astype(BF16)
    h_ref[0] = hb
    off = 0
    for ref in z_refs:
        width = ref.shape[-1]
        for c0 in range(0, width, 512):
            c1 = min(c0 + 512, width)
            z = jnp.dot(hb, w_ref[:, off + c0:off + c1], preferred_element_type=F32)
            ref[0, :, c0:c1] = z.astype(ref.dtype)
        off += width


def _inproj(x, mod, mod_is_shared, gain, w_all, tm):
    b, l, d = x.shape
    mod_map = (lambda i, j: (0, 0, 0)) if mod_is_shared else (lambda i, j: (i, 0, 0))
    dtypes = (BF16, BF16, BF16, BF16, F32)
    out_shape = [jax.ShapeDtypeStruct((b, l, d), BF16)]
    out_specs = [pl.BlockSpec((1, tm, d), lambda i, j: (i, j, 0))]
    for w, dt in zip(SECTION_WIDTHS, dtypes):
        out_shape.append(jax.ShapeDtypeStruct((b, l, w), dt))
        out_specs.append(pl.BlockSpec((1, tm, w), lambda i, j: (i, j, 0)))
    return pl.pallas_call(
        _inproj_kernel,
        grid=(b, l // tm),
        in_specs=[pl.BlockSpec((1, tm, d), lambda i, j: (i, j, 0)),
                  pl.BlockSpec((1, 6, d), mod_map),
                  pl.BlockSpec((1, d), lambda i, j: (0, 0)),
                  pl.BlockSpec((d, W_ALL), lambda i, j: (0, 0))],
        out_specs=out_specs,
        out_shape=out_shape,
        compiler_params=_cparams(2),
        name="inproj",
    )(x, mod, gain.reshape(1, d), w_all)


def _pack_w_in(w_in):
    d = w_in.shape[0]
    hq, hv = RET_HEADS * RET_DK, RET_HEADS * RET_DV
    cols = []
    for h in range(RET_HEADS):
        q = w_in[:, h * RET_DK:(h + 1) * RET_DK]
        k = w_in[:, hq + h * RET_DK:hq + (h + 1) * RET_DK]
        cols += [q, q, k, k, w_in[:, 2 * hq + h * RET_DV:2 * hq + (h + 1) * RET_DV],
                 w_in[:, 2 * hq + hv + h * RET_DV:2 * hq + hv + (h + 1) * RET_DV]]
    rest = w_in[:, 2 * hq + 2 * hv:]
    pad = jnp.zeros((d, W_ALL - W_RET - rest.shape[1]), w_in.dtype)
    return jnp.concatenate(cols + [rest, pad], axis=1).astype(BF16)


def _merge_kernel(x_ref, h_ref, y0_ref, y1_ref, y2_ref, y3_ref, mod_ref, gain_ref, wm_ref, wb_ref, wo_ref,
                  wrh_ref, wrl_ref, xo_ref, h2_ref, sc_ref):
    d = x_ref.shape[-1]
    h = h_ref[0]
    acc = jnp.zeros(x_ref.shape[1:], F32)
    for i, y_ref in enumerate((y0_ref, y1_ref, y2_ref, y3_ref)):
        gate = jax.nn.sigmoid(jnp.dot(h, wm_ref[:, i * d:(i + 1) * d], preferred_element_type=F32))
        acc = acc + gate * jnp.dot(y_ref[0], wb_ref[i], preferred_element_type=F32)
    m = jnp.dot(acc.astype(BF16), wo_ref[...], preferred_element_type=F32)
    xn = x_ref[0] + mod_ref[0, 2:3, :] * m
    xo_ref[0] = xn
    ms = jnp.mean(xn * xn, axis=-1, keepdims=True)
    h2 = xn * lax.rsqrt(ms + EPS) * gain_ref[...] * (1.0 + mod_ref[0, 4:5, :]) + mod_ref[0, 3:4, :]
    hi = h2.astype(BF16)
    h2_ref[0] = h2
    lo = (h2 - hi.astype(F32)).astype(BF16)
    logits = (jnp.dot(hi, wrh_ref[...], preferred_element_type=F32)
              + jnp.dot(lo, wrh_ref[...], preferred_element_type=F32)
              + jnp.dot(hi, wrl_ref[...], preferred_element_type=F32))
    sc_ref[0] = jax.nn.sigmoid(logits)


def _merge(x, h, ys, mod, mod_is_shared, gain2, wm, wb, wo, wr_hi, wr_lo, tm):
    b, l, d = x.shape
    mod_map = (lambda i, j: (0, 0, 0)) if mod_is_shared else (lambda i, j: (i, 0, 0))
    tok = lambda w: pl.BlockSpec((1, tm, w), lambda i, j: (i, j, 0))
    full2 = lambda s: pl.BlockSpec(s, lambda i, j: (0, 0))
    return pl.pallas_call(
        _merge_kernel,
        grid=(b, l // tm),
        in_specs=[tok(d), tok(d)] + [tok(BRANCH_W)] * 4 + [
            pl.BlockSpec((1, 6, d), mod_map), full2((1, d)), full2(wm.shape),
            pl.BlockSpec(wb.shape, lambda i, j: (0, 0, 0)), full2(wo.shape), full2(wr_hi.shape), full2(wr_lo.shape)],
        out_specs=[tok(d), tok(d), tok(LANES)],
        out_shape=[jax.ShapeDtypeStruct((b, l, d), F32), jax.ShapeDtypeStruct((b, l, d), F32),
                   jax.ShapeDtypeStruct((b, l, LANES), F32)],
        compiler_params=_cparams(2),
        name="merge",
    )(x, h, *ys, mod, gain2.reshape(1, d), wm, wb, wo, wr_hi, wr_lo)


def _moe_kernel(be_ref, nv_ref, src_ref, dst_ref, h_hbm, wg_ref, wu_ref, wd_ref, y_hbm, xbuf, obuf, wg_s, wu_s, wd_s,
                gsem, ssem, *, n_real):
    i = pl.program_id(0)
    n = pl.num_programs(0)
    slot = i % 2
    nv = nv_ref[i]

    def start_gather(blk, dst_slot):
        def body(grp, carry):
            base = blk * MOE_ROWS + grp * MOE_ROW_ALIGN
            for j in range(MOE_ROW_ALIGN):
                pltpu.make_async_copy(h_hbm.at[pl.ds(src_ref[base + j], 1)], xbuf.at[dst_slot, grp, pl.ds(j, 1)],
                                      gsem.at[dst_slot]).start()
            return carry
        lax.fori_loop(0, nv_ref[blk] // MOE_ROW_ALIGN, body, 0)

    def wait_rows(buf, sem, cnt):
        grps = buf.at[pl.ds(0, cnt // MOE_ROW_ALIGN)]
        pltpu.make_async_copy(grps, grps, sem).wait()

    @pl.when(i == 0)
    def _():
        xbuf[...] = jnp.zeros_like(xbuf)
        fills = [pltpu.make_async_copy(xbuf.at[1, g], y_hbm.at[pl.ds(n_real + g * MOE_ROW_ALIGN, MOE_ROW_ALIGN)],
                                       ssem.at[1])
                 for g in range((y_hbm.shape[0] - n_real) // MOE_ROW_ALIGN)]
        for f in fills:
            f.start()
        for f in fills:
            f.wait()
        start_gather(0, 0)

    @pl.when(i + 1 < n)
    def _():
        start_gather(i + 1, 1 - slot)

    @pl.when((i >= 2) & (nv_ref[jnp.maximum(i - 2, 0)] > 0))
    def _():
        wait_rows(obuf.at[slot], ssem.at[slot], nv_ref[jnp.maximum(i - 2, 0)])

    @pl.when(nv > 0)
    def _():
        wait_rows(xbuf.at[slot], gsem.at[slot], nv)
        e = be_ref[i]

        @pl.when((i == 0) | (e != be_ref[jnp.maximum(i - 1, 0)]))
        def _():
            wg_s[...] = wg_ref[0].astype(BF16)
            wu_s[...] = wu_ref[0].astype(BF16)
            wd_s[...] = wd_ref[0].astype(BF16)

        d = xbuf.shape[-1]
        x = xbuf[slot].reshape(MOE_ROWS, d).astype(BF16)
        g = jnp.dot(x, wg_s[...], preferred_element_type=F32)
        u = jnp.dot(x, wu_s[...], preferred_element_type=F32)
        a = (g * jax.nn.sigmoid(g) * u).astype(BF16)
        obuf[slot] = jnp.dot(a, wd_s[...], preferred_element_type=F32).reshape(obuf.shape[1:])

        def body(grp, carry):
            base = i * MOE_ROWS + grp * MOE_ROW_ALIGN
            for j in range(MOE_ROW_ALIGN):
                pltpu.make_async_copy(obuf.at[slot, grp, pl.ds(j, 1)], y_hbm.at[pl.ds(dst_ref[base + j], 1)],
                                      ssem.at[slot]).start()
            return carry
        lax.fori_loop(0, nv // MOE_ROW_ALIGN, body, 0)

    @pl.when(i == n - 1)
    def _():
        @pl.when(nv > 0)
        def _():
            wait_rows(obuf.at[slot], ssem.at[slot], nv)

        @pl.when((n >= 2) & (nv_ref[jnp.maximum(i - 1, 0)] > 0))
        def _():
            wait_rows(obuf.at[1 - slot], ssem.at[1 - slot], nv_ref[jnp.maximum(i - 1, 0)])


def _moe_ffn(block_e, n_valid, src, dst, h2, w_gate, w_up, w_down):
    t, d = h2.shape
    grp_shape = (2, MOE_ROWS // MOE_ROW_ALIGN, MOE_ROW_ALIGN, d)
    n_blocks = block_e.shape[0]
    de = w_gate.shape[-1]
    wspec = lambda s: pl.BlockSpec((1,) + s, lambda i, be, nv, src, dst: (be[i], 0, 0))
    grid_spec = pltpu.PrefetchScalarGridSpec(
        num_scalar_prefetch=4,
        grid=(n_blocks,),
        in_specs=[pl.BlockSpec(memory_space=pl.ANY), wspec((d, de)), wspec((d, de)), wspec((de, d))],
        out_specs=pl.BlockSpec(memory_space=pl.ANY),
        scratch_shapes=[pltpu.VMEM(grp_shape, F32), pltpu.VMEM(grp_shape, F32),
                        pltpu.VMEM((d, de), BF16), pltpu.VMEM((d, de), BF16), pltpu.VMEM((de, d), BF16),
                        pltpu.SemaphoreType.DMA((2,)), pltpu.SemaphoreType.DMA((2,))],
    )
    return pl.pallas_call(
        functools.partial(_moe_kernel, n_real=TOP_K * t),
        grid_spec=grid_spec,
        out_shape=jax.ShapeDtypeStruct((TOP_K * t + MOE_ROW_ALIGN * N_EXPERTS, d), F32),
        compiler_params=pltpu.CompilerParams(dimension_semantics=("arbitrary",), vmem_limit_bytes=VMEM_LIMIT,
                                             has_side_effects=True),
        name="moe_ffn",
    )(block_e, n_valid, src, dst, h2, w_gate, w_up, w_down)


def _route(scores, router_bias):
    t = scores.shape[0]
    sel = (scores + router_bias.astype(F32)).reshape(t, N_GROUPS, EXPERTS_PER_GROUP)
    pairs = [sel[..., i] + sel[..., j] for i in range(EXPERTS_PER_GROUP) for j in range(i + 1, EXPERTS_PER_GROUP)]
    grp_score = functools.reduce(jnp.maximum, pairs)
    g_idx = jnp.argmax(grp_score, axis=-1).astype(jnp.int32)
    g_hot = (g_idx[:, None] == jnp.arange(N_GROUPS, dtype=jnp.int32)[None, :])[:, :, None]
    in_grp = jnp.sum(jnp.where(g_hot, sel, 0.0), axis=1)
    sc_grp = jnp.sum(jnp.where(g_hot, scores.reshape(t, N_GROUPS, EXPERTS_PER_GROUP), 0.0), axis=1)
    lane4 = jnp.arange(EXPERTS_PER_GROUP, dtype=jnp.int32)[None, :]
    i1 = jnp.argmax(in_grp, axis=-1).astype(jnp.int32)
    i2 = jnp.argmax(jnp.where(lane4 == i1[:, None], -jnp.inf, in_grp), axis=-1).astype(jnp.int32)
    e_idx = g_idx[:, None] * EXPERTS_PER_GROUP + jnp.stack([i1, i2], axis=-1)
    wts = jnp.stack([jnp.sum(jnp.where(lane4 == i1[:, None], sc_grp, 0.0), axis=-1),
                     jnp.sum(jnp.where(lane4 == i2[:, None], sc_grp, 0.0), axis=-1)], axis=-1)
    return e_idx, wts / jnp.sum(wts, axis=-1, keepdims=True)


def _moe(h2, scores, router_bias, w_gate, w_up, w_down):
    t, d = h2.shape
    e_idx, wts = _route(scores, router_bias)
    a = t * TOP_K
    flat_e = e_idx.T.reshape(-1)
    onehot = (flat_e[:, None] == jnp.arange(N_EXPERTS, dtype=jnp.int32)[None, :]).astype(jnp.int32)
    csum = jnp.cumsum(onehot, axis=0)
    counts = csum[-1]
    rank = jnp.sum(csum * onehot, axis=1) - 1
    padded = (counts + MOE_ROWS - 1) // MOE_ROWS * MOE_ROWS
    ends = jnp.cumsum(padded)
    pstarts = ends - padded
    dest = jnp.sum(pstarts[None, :] * onehot, axis=1) + rank
    n_blocks = -(-a // MOE_ROWS) + N_EXPERTS
    blk0 = jnp.arange(n_blocks, dtype=jnp.int32) * MOE_ROWS
    block_e = jnp.minimum(jnp.searchsorted(ends, blk0, side='right'), N_EXPERTS - 1).astype(jnp.int32)
    n_valid = jnp.clip((pstarts + counts)[block_e] - blk0, 0, MOE_ROWS)
    n_valid = ((n_valid + MOE_ROW_ALIGN - 1) // MOE_ROW_ALIGN * MOE_ROW_ALIGN).astype(jnp.int32)
    slot = jnp.arange(n_blocks * MOE_ROWS, dtype=jnp.int32)
    spare = a + MOE_ROW_ALIGN * jnp.repeat(block_e, MOE_ROWS) + slot % MOE_ROW_ALIGN
    dst = spare.at[dest].set(jnp.arange(a, dtype=jnp.int32))
    src = jnp.where(dst < a, dst % t, 0)
    return _moe_ffn(block_e, n_valid, src, dst, h2, w_gate, w_up, w_down), wts


def _combine_kernel(x_ref, y0_ref, y1_ref, w_ref, mod_ref, o_ref):
    w = w_ref[...]
    y = y0_ref[...] * w[:, 0:1] + y1_ref[...] * w[:, 1:2]
    o_ref[0] = x_ref[0] + mod_ref[0, 5:6, :] * y


def _combine(x, y, wts, mod, mod_is_shared, row_offset, tm):
    b, l, d = x.shape
    mod_map = (lambda i, j: (0, 0, 0)) if mod_is_shared else (lambda i, j: (i, 0, 0))
    off = row_offset // tm
    per = l // tm
    return pl.pallas_call(
        _combine_kernel,
        grid=(b, per),
        in_specs=[pl.BlockSpec((1, tm, d), lambda i, j: (i, j, 0)),
                  pl.BlockSpec((tm, d), lambda i, j: (off + i * per + j, 0)),
                  pl.BlockSpec((tm, d), lambda i, j: (wts.shape[0] // tm + off + i * per + j, 0)),
                  pl.BlockSpec((tm, TOP_K), lambda i, j: (off + i * per + j, 0)),
                  pl.BlockSpec((1, 6, d), mod_map)],
        out_specs=pl.BlockSpec((1, tm, d), lambda i, j: (i, j, 0)),
        out_shape=jax.ShapeDtypeStruct((b, l, d), F32),
        compiler_params=_cparams(2),
        name="moe_combine",
    )(x, y, y, wts, mod)


def _head_rms(x, gain, e_ref):
    ms = jnp.dot((x * x).astype(BF16), e_ref[...], preferred_element_type=F32) * (1.0 / HEAD_DIM)
    return x * lax.rsqrt(ms + EPS) * gain


def _rope(y, cos, sin):
    lane = lax.broadcasted_iota(jnp.int32, y.shape, 1)
    half = HEAD_DIM // 2
    rot = jnp.where((lane % HEAD_DIM) < half, pltpu.roll(y, LANES - half, 1), pltpu.roll(y, half, 1))
    return y * cos + rot * sin


def _attend(q, tiles, sink):
    scores = []
    for k, _, bias in tiles:
        s = lax.dot_general(q, k, (((1,), (1,)), ((), ())), preferred_element_type=F32)
        scores.append(s if bias is None else s + bias)
    if all(s.shape[1] % LANES == 0 for s in scores):
        scores = [jnp.concatenate(scores, axis=1)]
    m = sink
    for s in scores:
        mt = jnp.max(s, axis=-1, keepdims=True)
        m = mt if m is None else jnp.maximum(m, mt)
    den = jnp.exp(sink - m) if sink is not None else jnp.zeros_like(m)
    probs = []
    for s in scores:
        p = jnp.exp(s - m)
        den = den + jnp.sum(p, axis=-1, keepdims=True)
        probs.append(p.astype(BF16))
    if len(probs) != len(tiles):
        offs = np.cumsum([0] + [k.shape[0] for k, _, _ in tiles])
        probs = [probs[0][:, offs[i]:offs[i + 1]] for i in range(len(tiles))]
    o = None
    for p, (_, v, _) in zip(probs, tiles):
        pv = jnp.dot(p, v, preferred_element_type=F32)
        o = pv if o is None else o + pv
    return o * (1.0 / den)


def _lane_half(shape):
    return lax.broadcasted_iota(jnp.int32, shape, 1) // HEAD_DIM


def _win_group_queries(slabs, g, sink_ref):
    r = WIN_HEADS // WIN_KV_HEADS
    rows = slabs[0].shape[0]
    half = _lane_half(slabs[0].shape)
    parts, sinks = [], []
    for j in range(r):
        head = g * r + j
        slab = slabs[head // 2]
        if head % 2 != g:
            slab = pltpu.roll(slab, HEAD_DIM, 1)
        parts.append(jnp.where(half == g, slab, 0.0))
        sinks.append(jnp.full((rows, 1), sink_ref[head], F32))
    return jnp.concatenate(parts, axis=0).astype(BF16), jnp.concatenate(sinks, axis=0)


def _win_store(o_ref, o, g, rows):
    r = WIN_HEADS // WIN_KV_HEADS
    half = _lane_half((rows, LANES))
    for pair in range(r // 2):
        a = o[(2 * pair) * rows:(2 * pair + 1) * rows]
        b = o[(2 * pair + 1) * rows:(2 * pair + 2) * rows]
        if g == 0:
            b = pltpu.roll(b, HEAD_DIM, 1)
        else:
            a = pltpu.roll(a, HEAD_DIM, 1)
        s = (g * r + 2 * pair) // 2
        o_ref[0, :, s * LANES:(s + 1) * LANES] = jnp.where(half == 0, a, b).astype(o_ref.dtype)


def _win_kernel(sink_ref, q_ref, kv_ref, ckv_ref, cos_ref, sin_ref, qg_ref, kg_ref, e_ref, o_ref, ks_ref, kcs_ref):
    n = pl.program_id(1)
    nb = pl.num_programs(1)
    l = kv_ref.shape[1]
    blk = WIN_BLOCK
    prep_rows = 512

    @pl.when(n == 0)
    def _():
        def body(i, carry):
            r0 = pl.multiple_of(i * prep_rows, prep_rows)
            y = _head_rms(kv_ref[0, pl.ds(r0, prep_rows), 0:LANES].astype(F32), kg_ref[...], e_ref)
            ks_ref[pl.ds(r0, prep_rows), :] = _rope(y, cos_ref[pl.ds(r0, prep_rows), :],
                                                    sin_ref[pl.ds(r0, prep_rows), :]).astype(BF16)
            return carry
        lax.fori_loop(0, l // prep_rows, body, 0)
        kcs_ref[...] = _head_rms(ckv_ref[0, :, 0:LANES].astype(F32), kg_ref[...], e_ref).astype(BF16)

    r0 = pl.multiple_of(n * blk, blk)
    cos = cos_ref[pl.ds(r0, blk), :]
    sin = sin_ref[pl.ds(r0, blk), :]
    slabs = []
    for s in range(q_ref.shape[-1] // LANES):
        y = _head_rms(q_ref[0, :, s * LANES:(s + 1) * LANES].astype(F32), qg_ref[...], e_ref)
        slabs.append(_rope(y, cos, sin))

    r = WIN_HEADS // WIN_KV_HEADS
    qi = lax.broadcasted_iota(jnp.int32, (r * blk, blk), 0) % blk
    kj = lax.broadcasted_iota(jnp.int32, (r * blk, blk), 1)
    ok_prev = (kj >= qi) & (n > 0)
    ok_next = (kj <= qi) & (n < nb - 1)
    bias_prev = jnp.where(ok_prev, 0.0, NEG_INF)
    bias_next = jnp.where(ok_next, 0.0, NEG_INF)
    tiles = []
    for kb, bias in ((jnp.maximum(n - 1, 0), bias_prev), (n, None), (jnp.minimum(n + 1, nb - 1), bias_next)):
        k0 = pl.multiple_of(kb * blk, blk)
        tiles.append((ks_ref[pl.ds(k0, blk), :], kv_ref[0, pl.ds(k0, blk), LANES:2 * LANES], bias))
    tiles.append((kcs_ref[...], ckv_ref[0, :, LANES:2 * LANES], None))
    for g in range(WIN_KV_HEADS):
        q, sink = _win_group_queries(slabs, g, sink_ref)
        _win_store(o_ref, _attend(q, tiles, sink), g, blk)


def _win_ctx_kernel(sink_ref, q_ref, ckv_ref, qg_ref, kg_ref, e_ref, o_ref):
    kc = _head_rms(ckv_ref[0, :, 0:LANES].astype(F32), kg_ref[...], e_ref).astype(BF16)
    tiles = [(kc, ckv_ref[0, :, LANES:2 * LANES], None)]
    slabs = [_head_rms(q_ref[0, :, s * LANES:(s + 1) * LANES].astype(F32), qg_ref[...], e_ref)
             for s in range(q_ref.shape[-1] // LANES)]
    for g in range(WIN_KV_HEADS):
        q, sink = _win_group_queries(slabs, g, sink_ref)
        _win_store(o_ref, _attend(q, tiles, sink), g, q_ref.shape[1])


def _block_diag_ones(width):
    i = np.arange(width) // HEAD_DIM
    return jnp.asarray(i[:, None] == i[None, :], BF16)


def _window_branch(zw_l, zw_c, q_gain, k_gain, sink, cos2, sin2, ctx_out):
    b, l, _ = zw_l.shape
    lc = zw_c.shape[1]
    wq = WIN_HEADS * HEAD_DIM
    qg = (jnp.tile(q_gain.astype(F32), 2) * HEAD_DIM ** -0.5).reshape(1, LANES)
    kg = jnp.tile(k_gain.astype(F32), 2).reshape(1, LANES)
    e = _block_diag_ones(LANES)
    smem = pl.BlockSpec(memory_space=pltpu.SMEM)
    full = lambda s, nd: pl.BlockSpec(s, lambda *a: (0,) * nd)
    yl = pl.pallas_call(
        _win_kernel,
        grid=(b, l // WIN_BLOCK),
        in_specs=[smem,
                  pl.BlockSpec((1, WIN_BLOCK, wq), lambda i, n: (i, n, 0)),
                  pl.BlockSpec((1, l, 2 * LANES), lambda i, n: (i, 0, wq // (2 * LANES))),
                  pl.BlockSpec((1, lc, 2 * LANES), lambda i, n: (i, 0, wq // (2 * LANES))),
                  full((l, LANES), 2), full((l, LANES), 2), full((1, LANES), 2), full((1, LANES), 2),
                  full((LANES, LANES), 2)],
        out_specs=pl.BlockSpec((1, WIN_BLOCK, wq), lambda i, n: (i, n, 0)),
        out_shape=jax.ShapeDtypeStruct((b, l, wq), BF16),
        scratch_shapes=[pltpu.VMEM((l, LANES), BF16), pltpu.VMEM((lc, LANES), BF16)],
        compiler_params=_cparams(2),
        name="window_attn",
    )(sink.astype(F32), zw_l, zw_l, zw_c, cos2, sin2, qg, kg, e)
    yc = None
    if ctx_out:
        yc = pl.pallas_call(
            _win_ctx_kernel,
            grid=(b,),
            in_specs=[smem,
                      pl.BlockSpec((1, lc, wq), lambda i: (i, 0, 0)),
                      pl.BlockSpec((1, lc, 2 * LANES), lambda i: (i, 0, wq // (2 * LANES))),
                      full((1, LANES), 2), full((1, LANES), 2), full((LANES, LANES), 2)],
            out_specs=pl.BlockSpec((1, lc, wq), lambda i: (i, 0, 0)),
            out_shape=jax.ShapeDtypeStruct((b, lc, wq), BF16),
            compiler_params=_cparams(1),
            name="window_ctx_attn",
        )(sink.astype(F32), zw_c, zw_c, qg, kg, e)
    return yc, yl


NA_ROWS = 8
NA_KROWS = NA_ROWS + NA_KH - 1


def _na_key_base(rg, rows):
    return jnp.clip(rg * NA_ROWS - NA_KH // 2, 0, rows - NA_KROWS)


def _na_kernel(q_ref, k_ref, v_ref, ck_ref, cv_ref, bias_ref, qg_ref, kg_ref, e_ref, o_ref, ks_ref, kcs_ref):
    rg = pl.program_id(1)
    l = k_ref.shape[1]
    rows = l // GRID_W
    n_slab = q_ref.shape[-1] // LANES
    prep_rows = 512

    @pl.when(rg == 0)
    def _():
        def body(i, carry):
            r0 = pl.multiple_of(i * prep_rows, prep_rows)
            ks_ref[pl.ds(r0, prep_rows), :] = _head_rms(k_ref[0, pl.ds(r0, prep_rows), :].astype(F32), kg_ref[...],
                                                        e_ref).astype(BF16)
            return carry
        lax.fori_loop(0, l // prep_rows, body, 0)
        kcs_ref[...] = _head_rms(ck_ref[0].astype(F32), kg_ref[...], e_ref).astype(BF16)

    nk = NA_KROWS * GRID_W
    k0 = pl.multiple_of(_na_key_base(rg, rows) * GRID_W, GRID_W)
    qn = _head_rms(q_ref[0].astype(F32), qg_ref[...], e_ref)
    half = _lane_half((q_ref.shape[1], LANES))
    for s in range(n_slab):
        cols = slice(s * LANES, (s + 1) * LANES)
        slab = qn[:, cols]
        tiles_kv = (ks_ref[pl.ds(k0, nk), cols], v_ref[0, pl.ds(k0, nk), cols])
        ctx_kv = (kcs_ref[:, cols], cv_ref[0, :, cols])
        outs = []
        for hh in range(2):
            q = jnp.where(half == hh, slab, 0.0).astype(BF16)
            bias = bias_ref[0, 2 * s + hh].astype(F32)
            outs.append(_attend(q, [tiles_kv + (bias,), ctx_kv + (None,)], None))
        o_ref[0, :, cols] = jnp.where(half == 0, outs[0], outs[1]).astype(o_ref.dtype)


def _na_ctx_kernel(q_ref, ck_ref, cv_ref, qg_ref, kg_ref, e_ref, o_ref):
    kc = _head_rms(ck_ref[0].astype(F32), kg_ref[...], e_ref).astype(BF16)
    qn = _head_rms(q_ref[0].astype(F32), qg_ref[...], e_ref)
    half = _lane_half((q_ref.shape[1], LANES))
    for s in range(q_ref.shape[-1] // LANES):
        cols = slice(s * LANES, (s + 1) * LANES)
        outs = []
        for hh in range(2):
            q = jnp.where(half == hh, qn[:, cols], 0.0).astype(BF16)
            outs.append(_attend(q, [(kc[:, cols], cv_ref[0, :, cols], None)], None))
        o_ref[0, :, cols] = jnp.where(half == 0, outs[0], outs[1]).astype(o_ref.dtype)


def _na_bias_classes(rows):
    n_rg = rows // NA_ROWS
    return list(range(n_rg)) if n_rg <= 3 else [0, 1, n_rg - 1]


def _na_bias_table(rpb, rows):
    kh = NA_KH
    ro, rv = [], []
    for rg in _na_bias_classes(rows):
        kbase = int(np.clip(rg * NA_ROWS - kh // 2, 0, rows - NA_KROWS))
        r = rg * NA_ROWS + np.arange(NA_ROWS)
        rstart = np.clip(r - kh // 2, 0, rows - kh)
        kr = kbase + np.arange(NA_KROWS)
        rv.append((kr[None, :] >= rstart[:, None]) & (kr[None, :] < rstart[:, None] + kh))
        ro.append(np.clip(kr[None, :] - r[:, None] + kh - 1, 0, 2 * kh - 2))
    ro, rv = np.stack(ro), np.stack(rv)
    qc = np.arange(GRID_W)
    cstart = np.clip(qc - NA_KW // 2, 0, GRID_W - NA_KW)
    cv = (qc[None, :] >= cstart[:, None]) & (qc[None, :] < cstart[:, None] + NA_KW)
    co = np.clip(qc[None, :] - qc[:, None] + NA_KW - 1, 0, 2 * NA_KW - 2)
    n_cls = ro.shape[0]
    hp = lax.Precision.HIGHEST
    co_hot = jnp.asarray(co[None] == np.arange(2 * NA_KW - 1)[:, None, None], F32)
    ro_hot = jnp.asarray(ro[..., None] == np.arange(2 * kh - 1), F32)
    cols = jnp.einsum('hdc,cwx->hdwx', rpb.astype(F32), co_hot, precision=hp)
    tab = jnp.einsum('crkd,hdwx->chrwkx', ro_hot, cols, precision=hp)
    ok = rv[:, None, :, None, :, None] & cv[None, None, None, :, None, :]
    tab = jnp.where(ok, tab, NEG_INF)
    return tab.reshape(n_cls, rpb.shape[0], NA_ROWS * GRID_W, NA_KROWS * GRID_W).astype(BF16)


def _neighborhood_branch(zn_l, zn_c, q_gain, k_gain, rpb, ctx_out):
    b, l, _ = zn_l.shape
    lc = zn_c.shape[1]
    w = NA_HEADS * HEAD_DIM
    rows = l // GRID_W
    n_rg = rows // NA_ROWS
    n_slab = w // LANES
    qg = (jnp.tile(q_gain.astype(F32), 2 * n_slab) * HEAD_DIM ** -0.5).reshape(1, w)
    kg = jnp.tile(k_gain.astype(F32), 2 * n_slab).reshape(1, w)
    e = _block_diag_ones(w)
    bias = _na_bias_table(rpb, rows)
    if n_rg <= 3:
        cls_map = lambda i, r: (r, 0, 0, 0)
    else:
        cls_map = lambda i, r: ((r > 0).astype(jnp.int32) + (r == n_rg - 1).astype(jnp.int32), 0, 0, 0)
    tq = NA_ROWS * GRID_W
    full = lambda s, nd: pl.BlockSpec(s, lambda *a: (0,) * nd)
    yl = pl.pallas_call(
        _na_kernel,
        grid=(b, n_rg),
        in_specs=[pl.BlockSpec((1, tq, w), lambda i, r: (i, r, 0)),
                  pl.BlockSpec((1, l, w), lambda i, r: (i, 0, 1)),
                  pl.BlockSpec((1, l, w), lambda i, r: (i, 0, 2)),
                  pl.BlockSpec((1, lc, w), lambda i, r: (i, 0, 1)),
                  pl.BlockSpec((1, lc, w), lambda i, r: (i, 0, 2)),
                  pl.BlockSpec((1,) + bias.shape[1:], cls_map),
                  full((1, w), 2), full((1, w), 2), full((w, w), 2)],
        out_specs=pl.BlockSpec((1, tq, w), lambda i, r: (i, r, 0)),
        out_shape=jax.ShapeDtypeStruct((b, l, w), BF16),
        scratch_shapes=[pltpu.VMEM((l, w), BF16), pltpu.VMEM((lc, w), BF16)],
        compiler_params=_cparams(2),
        name="neighborhood_attn",
    )(zn_l, zn_l, zn_l, zn_c, zn_c, bias, qg, kg, e)
    yc = None
    if ctx_out:
        yc = pl.pallas_call(
            _na_ctx_kernel,
            grid=(b,),
            in_specs=[pl.BlockSpec((1, lc, w), lambda i: (i, 0, 0)),
                      pl.BlockSpec((1, lc, w), lambda i: (i, 0, 1)),
                      pl.BlockSpec((1, lc, w), lambda i: (i, 0, 2)),
                      full((1, w), 2), full((1, w), 2), full((w, w), 2)],
            out_specs=pl.BlockSpec((1, lc, w), lambda i: (i, 0, 0)),
            out_shape=jax.ShapeDtypeStruct((b, lc, w), BF16),
            compiler_params=_cparams(1),
            name="neighborhood_ctx_attn",
        )(zn_c, zn_c, zn_c, qg, kg, e)
    return yc, yl


def _rope_tables(n_tokens):
    t = np.arange(n_tokens)
    n_freq = HEAD_DIM // 4
    inv = ROPE_THETA ** (-jnp.arange(n_freq, dtype=F32) / n_freq)
    ang = jnp.concatenate([jnp.asarray(t // GRID_W, F32)[:, None] * inv, jnp.asarray(t % GRID_W, F32)[:, None] * inv],
                          axis=-1)
    cos, sin = jnp.cos(ang), jnp.sin(ang)
    return jnp.tile(jnp.concatenate([cos, cos], -1), (1, 2)), jnp.tile(jnp.concatenate([-sin, sin], -1), (1, 2))


def _ret_kernel(qq_l, kk_l, v_l, g_l, qq_c, kk_c, v_c, g_c, cos_ref, sin_ref, dmask_ref, tq_ref, tk_ref, cdec_ref,
                gain_ref, yl_ref, yc_ref, kr_ref, kv_ref, sin_state_ref, *, ctx_out):
    c = RET_CHUNK
    ncc = qq_c.shape[1] // c
    ncl = qq_l.shape[1] // c
    nc = ncc + ncl
    tk = tk_ref[0]
    tq = tq_ref[0]
    dmask = dmask_ref[0]
    fwd_lanes = lax.broadcasted_iota(jnp.int32, (c, LANES), 1) < RET_DK

    def chunk_kv(k2, v):
        kd = (k2 * tk).astype(BF16)
        return lax.dot_general(kd, v, (((0,), (0,)), ((), ())), preferred_element_type=F32)

    for n in range(ncc):
        kv_ref[n] = chunk_kv(kk_c[0, n * c:(n + 1) * c, :].astype(F32), v_c[0, n * c:(n + 1) * c, :])

    def kv_body(n, carry):
        r0 = pl.multiple_of(n * c, c)
        k2 = _rope(kk_l[0, pl.ds(r0, c), :].astype(F32), cos_ref[pl.ds(r0, c), :], sin_ref[pl.ds(r0, c), :])
        kr_ref[pl.ds(r0, c), :] = k2.astype(BF16)
        kv_ref[ncc + n] = chunk_kv(k2, v_l[0, pl.ds(r0, c), :])
        return carry
    lax.fori_loop(0, ncl, kv_body, 0, unroll=4)

    dec_f = cdec_ref[0, 0:1, :]
    dec_b = cdec_ref[0, 1:2, :]

    def scan_body(t, carry):
        sf, sb = carry
        sin_state_ref[t, 0:RET_DK, :] = sf.astype(BF16)
        sf = sf * dec_f + kv_ref[t, 0:RET_DK, :]
        tb = jnp.where(t < ncc, ncc - 1 - t, nc - 1 - (t - ncc))
        sin_state_ref[tb, RET_DK:2 * RET_DK, :] = sb.astype(BF16)
        sb = sb * dec_b + kv_ref[tb, RET_DK:2 * RET_DK, :]
        return sf, sb
    zero = jnp.zeros((RET_DK, RET_DV), F32)
    lax.fori_loop(0, nc, scan_body, (zero, zero))

    def chunk_out(q2, k2b, v, gate, state):
        qm = jnp.where(fwd_lanes, q2, 0.0).astype(BF16)
        scores = lax.dot_general(qm, k2b, (((1,), (1,)), ((), ())), preferred_element_type=F32) * dmask
        o = (jnp.dot(scores.astype(BF16), v, preferred_element_type=F32)
             + jnp.dot((q2 * tq).astype(BF16), state, preferred_element_type=F32))
        mu = jnp.mean(o, axis=-1, keepdims=True)
        var = jnp.mean(jnp.square(o - mu), axis=-1, keepdims=True)
        y = (o - mu) * lax.rsqrt(var + EPS) * gain_ref[0]
        gf = gate.astype(F32)
        return y * gf * jax.nn.sigmoid(gf)

    if ctx_out:
        for n in range(ncc):
            rows = slice(n * c, (n + 1) * c)
            yc_ref[0, rows, :] = chunk_out(qq_c[0, rows, :].astype(F32), kk_c[0, rows, :], v_c[0, rows, :],
                                           g_c[0, rows, :], sin_state_ref[n]).astype(yc_ref.dtype)
    else:
        yc_ref[...] = jnp.zeros_like(yc_ref)

    def out_body(n, carry):
        r0 = pl.multiple_of(n * c, c)
        q2 = _rope(qq_l[0, pl.ds(r0, c), :].astype(F32), cos_ref[pl.ds(r0, c), :], sin_ref[pl.ds(r0, c), :])
        yl_ref[0, pl.ds(r0, c), :] = chunk_out(q2, kr_ref[pl.ds(r0, c), :], v_l[0, pl.ds(r0, c), :],
                                               g_l[0, pl.ds(r0, c), :], sin_state_ref[ncc + n]).astype(yl_ref.dtype)
        return carry
    lax.fori_loop(0, ncl, out_body, 0, unroll=4)


def _ret_tables(decay_logit):
    c = RET_CHUNK
    lg = jax.nn.log_sigmoid(decay_logit.astype(F32))
    lf, lb = lg[0][:, None, None], lg[1][:, None, None]
    pos = jnp.arange(c, dtype=F32)
    diff = pos[:, None] - pos[None, :]
    dmask = (jnp.where(diff >= 0, jnp.exp(lf * jnp.maximum(diff, 0.0)), 0.0)
             + jnp.where(diff <= 0, jnp.exp(lb * jnp.maximum(-diff, 0.0)), 0.0)) * RET_DK ** -0.5
    col = lambda a, b_: jnp.concatenate([jnp.broadcast_to(a, a.shape[:2] + (RET_DK,)),
                                         jnp.broadcast_to(b_, b_.shape[:2] + (RET_DK,))], axis=-1)
    p = pos[None, :, None]
    tq = col(jnp.exp(lf * (p + 1.0)), jnp.exp(lb * (c - p)))
    tk = col(jnp.exp(lf * (c - 1.0 - p)), jnp.exp(lb * p)) * RET_DK ** -0.5
    cdec = jnp.zeros((lg.shape[1], 8, RET_DV), F32)
    cdec = cdec.at[:, 0, :].set(jnp.exp(lg[0] * c)[:, None]).at[:, 1, :].set(jnp.exp(lg[1] * c)[:, None])
    return dmask, tq, tk, cdec


def _retention_branch(zr_l, zr_c, decay_logit, gn_gain, cos2, sin2, ctx_out):
    b, l, _ = zr_l.shape
    lc = zr_c.shape[1]
    h = RET_HEADS
    nc = (l + lc) // RET_CHUNK
    dmask, tq, tk, cdec = _ret_tables(decay_logit)
    gain = gn_gain.astype(F32).reshape(h, 1, RET_DV)
    seq = lambda n, j: pl.BlockSpec((1, n, LANES), lambda i, hh: (i, 0, 4 * hh + j))
    head = lambda s: pl.BlockSpec((1,) + s, lambda i, hh: (hh, 0, 0))
    full = lambda s: pl.BlockSpec(s, lambda i, hh: (0, 0))
    yl, yc = pl.pallas_call(
        functools.partial(_ret_kernel, ctx_out=ctx_out),
        grid=(b, h),
        in_specs=[seq(l, 0), seq(l, 1), seq(l, 2), seq(l, 3), seq(lc, 0), seq(lc, 1), seq(lc, 2), seq(lc, 3),
                  full((l, LANES)), full((l, LANES)),
                  head((RET_CHUNK, RET_CHUNK)), head((RET_CHUNK, LANES)), head((RET_CHUNK, LANES)), head((8, RET_DV)),
                  head((1, RET_DV))],
        out_specs=[pl.BlockSpec((1, l, RET_DV), lambda i, hh: (i, 0, hh)),
                   pl.BlockSpec((1, lc, RET_DV), lambda i, hh: (i, 0, hh))],
        out_shape=[jax.ShapeDtypeStruct((b, l, h * RET_DV), BF16), jax.ShapeDtypeStruct((b, lc, h * RET_DV), BF16)],
        scratch_shapes=[pltpu.VMEM((l, LANES), BF16), pltpu.VMEM((nc, 2 * RET_DK, RET_DV), F32),
                        pltpu.VMEM((nc, 2 * RET_DK, RET_DV), BF16)],
        compiler_params=_cparams(2),
        name="retention",
    )(zr_l, zr_l, zr_l, zr_l, zr_c, zr_c, zr_c, zr_c, cos2, sin2, dmask, tq, tk, cdec, gain)
    return (yc if ctx_out else None), yl


GDN_SUPER = 128
GDN_UNITS = 4
GDN_HALO = 128


def _split_bf16(a):
    hi = a.astype(BF16)
    return hi, (a - hi.astype(F32)).astype(BF16)


def _mask_dot(mask_bf16, a):
    ah, al = _split_bf16(a)
    return jnp.dot(mask_bf16, ah, preferred_element_type=F32) + jnp.dot(mask_bf16, al, preferred_element_type=F32)


def _softplus(x):
    return jnp.maximum(x, 0.0) + jnp.log(1.0 + jnp.exp(-jnp.abs(x)))


def _gdn_kernel(nega_ref, dtb_ref, q_l, k_l, v_l, g_l, ab_l, q_c, k_c, v_c, g_c, ab_c,
                cwq_ref, cwk_ref, cwv_ref, pd_ref, pu_ref, gain_ref, yl_ref, yc_ref,
                kn_s, sin_s, qp_s, o0_s, cd_s, *, ctx_out):
    hd = pl.program_id(1)
    c = GDN_CHUNK
    sup = GDN_SUPER
    per = sup // c
    lc, l = q_c.shape[1], q_l.shape[1]
    ncc, ncl = lc // c, l // c
    nc = ncc + ncl

    ri = lax.broadcasted_iota(jnp.int32, (sup, sup), 0)
    ci = lax.broadcasted_iota(jnp.int32, (sup, sup), 1)
    same = (ri // c) == (ci // c)
    eye = (ri == ci).astype(F32)
    incl = (same & (ri >= ci), same & (ri <= ci))
    strict = (same & (ri > ci), same & (ri < ci))
    incl_b = tuple(m.astype(BF16) for m in incl)
    same_b = same.astype(BF16)
    lane = lax.broadcasted_iota(jnp.int32, (sup, LANES), 1)
    rowc = lax.broadcasted_iota(jnp.int32, (sup, LANES), 0)

    def conv_silu(z_ref, w_ref, r0, ls):
        z = z_ref[0, pl.ds(r0, sup), :]
        if isinstance(r0, int):
            zero = jnp.zeros((GDN_HALO, LANES), BF16)
            prev = z_ref[0, r0 - GDN_HALO:r0, :] if r0 > 0 else zero
            nxt = z_ref[0, r0 + sup:r0 + sup + GDN_HALO, :] if r0 + sup < ls else zero
        else:
            p0 = pl.multiple_of(jnp.maximum(r0 - GDN_HALO, 0), GDN_HALO)
            n0 = pl.multiple_of(jnp.minimum(r0 + sup, ls - GDN_HALO), GDN_HALO)
            prev = jnp.where(r0 > 0, z_ref[0, pl.ds(p0, GDN_HALO), :], jnp.zeros((), BF16))
            nxt = jnp.where(r0 + sup < ls, z_ref[0, pl.ds(n0, GDN_HALO), :], jnp.zeros((), BF16))
        win = jnp.concatenate([prev, z, nxt], axis=0)
        z_dn = jnp.dot(pd_ref[...], win, preferred_element_type=F32)
        z_up = jnp.dot(pu_ref[...], win, preferred_element_type=F32)
        y = z_dn * w_ref[0, 0:1, :] + z.astype(F32) * w_ref[0, 1:2, :] + z_up * w_ref[0, 2:3, :]
        return y * jax.nn.sigmoid(y)

    def prep(units):
        common = []
        for refs, r0, base, ls in units:
            q_ref, k_ref, v_ref, ab_ref = refs
            q = conv_silu(q_ref, cwq_ref, r0, ls)
            k = conv_silu(k_ref, cwk_ref, r0, ls)
            v = conv_silu(v_ref, cwv_ref, r0, ls)
            q = q * lax.rsqrt(jnp.sum(q * q, axis=-1, keepdims=True) + EPS) * GDN_DK ** -0.5
            k = k * lax.rsqrt(jnp.sum(k * k, axis=-1, keepdims=True) + EPS)
            kb16 = k.astype(BF16)
            qk = lax.dot_general(q.astype(BF16), kb16, (((1,), (1,)), ((), ())), preferred_element_type=F32)
            ab = ab_ref[0, pl.ds(r0, sup), :]
            common.append((q, k, v, kb16, qk, ab, base + r0))
        chains = []
        for q, k, v, kb16, qk, ab, row0 in common:
            for d in range(2):
                ia = d * GDN_HEADS + hd
                ib = 2 * GDN_HEADS + ia
                a_col = jnp.sum(jnp.where(lane == ia, ab, 0.0), axis=1, keepdims=True)
                b_col = jnp.sum(jnp.where(lane == ib, ab, 0.0), axis=1, keepdims=True)
                g_col = nega_ref[d, hd] * _softplus(a_col + dtb_ref[d, hd])
                beta = jax.nn.sigmoid(b_col)
                gcum = _mask_dot(incl_b[d], jnp.broadcast_to(g_col, (sup, LANES)))
                gtot = _mask_dot(same_b, jnp.broadcast_to(g_col, (sup, LANES)))
                gc = gcum[:, 0:1]
                decay = jnp.where(incl[d], jnp.exp(jnp.where(incl[d], gcum - gcum.T, 0.0)), 0.0)
                kbeta = k * beta
                kk = lax.dot_general(kbeta.astype(BF16), kb16, (((1,), (1,)), ((), ())),
                                     preferred_element_type=F32)
                a_mat = jnp.where(strict[d], kk * decay, 0.0)
                eg = jnp.exp(gc)
                rhs = jnp.concatenate([kbeta * eg, v * beta], axis=1).astype(BF16)
                aqk = jnp.where(incl[d], qk * decay, 0.0).astype(BF16)
                ke = (k * jnp.exp(gtot[:, 0:1] - gc)).astype(BF16)
                cds = [jnp.exp(gtot[j * c:j * c + 1, :]) for j in range(per)]
                chains.append(dict(d=d, row0=row0, a=a_mat, rhs=rhs, aqk=aqk, ke=ke, cds=cds, qeg=q * eg))
        for ch in chains:
            ch['inv'] = eye - ch['a']
            ch['pw'] = ch['a'].astype(BF16)
        for _ in range(5):
            for ch in chains:
                ch['pw'] = jnp.dot(ch['pw'], ch['pw'], preferred_element_type=F32).astype(BF16)
            for ch in chains:
                ch['inv'] = ch['inv'] + jnp.dot(ch['inv'].astype(BF16), ch['pw'], preferred_element_type=F32)
        for ch in chains:
            ch['wu'] = jnp.dot(ch['inv'].astype(BF16), ch['rhs'], preferred_element_type=F32).astype(BF16)
        for ch in chains:
            d, row0 = ch['d'], ch['row0']
            awu = jnp.dot(ch['aqk'], ch['wu'], preferred_element_type=F32)
            rows = pl.ds(row0, sup)
            qp_s[d, rows, :] = (ch['qeg'] - awu[:, :GDN_DK]).astype(BF16)
            o0_s[d, rows, :] = awu[:, GDN_DK:]
            for j in range(per):
                kej = jnp.where((rowc // c) == j, ch['ke'], jnp.zeros((), BF16))
                idx = row0 // c + j
                kn_s[d, idx] = lax.dot_general(kej, ch['wu'], (((0,), (0,)), ((), ())),
                                               preferred_element_type=F32).astype(BF16)
                cd_s[d, idx] = ch['cds'][j]

    prep([((q_c, k_c, v_c, ab_c), n * sup, 0, lc) for n in range(lc // sup)])

    def prep_body(n, carry):
        prep([((q_l, k_l, v_l, ab_l), pl.multiple_of((GDN_UNITS * n + j) * sup, sup), lc, l)
              for j in range(GDN_UNITS)])
        return carry
    lax.fori_loop(0, l // (GDN_UNITS * sup), prep_body, 0)

    def chunk_step(d, idx, s):
        sb = s.astype(BF16)
        sin_s[d, idx] = sb
        kn = kn_s[d, idx]
        return (s * cd_s[d, idx] - jnp.dot(kn[:, :GDN_DK], sb, preferred_element_type=F32)
                + kn[:, GDN_DK:].astype(F32))

    def scan_body(t, carry):
        sf, sb = carry
        tb = jnp.where(t < ncc, ncc - 1 - t, nc - 1 - (t - ncc))
        return chunk_step(0, t, sf), chunk_step(1, tb, sb)
    zero = jnp.zeros((GDN_DK, GDN_DV), F32)
    lax.fori_loop(0, nc, scan_body, (zero, zero))

    def finish(y_ref, gate_ref, base, n_rows):
        def body(n, carry):
            r0 = pl.multiple_of(n * c, c)
            rows = pl.ds(base + r0, c)
            idx = base // c + n
            o = (o0_s[0, rows, :] + o0_s[1, rows, :]
                 + jnp.dot(qp_s[0, rows, :], sin_s[0, idx], preferred_element_type=F32)
                 + jnp.dot(qp_s[1, rows, :], sin_s[1, idx], preferred_element_type=F32))
            y = o * lax.rsqrt(jnp.mean(o * o, axis=-1, keepdims=True) + EPS) * gain_ref[...]
            gf = gate_ref[0, pl.ds(r0, c), :].astype(F32)
            y_ref[0, pl.ds(r0, c), :] = (y * gf * jax.nn.sigmoid(gf)).astype(y_ref.dtype)
            return carry
        lax.fori_loop(0, n_rows // c, body, 0, unroll=4)

    finish(yl_ref, g_l, lc, l)
    if ctx_out:
        finish(yc_ref, g_c, 0, lc)
    else:
        yc_ref[...] = jnp.zeros_like(yc_ref)


def _gdn_branch(zg_l, zab_l, zg_c, zab_c, conv_w, a_log, dt_bias, norm_gain, ctx_out):
    b, l, _ = zg_l.shape
    lc = zg_c.shape[1]
    h = GDN_HEADS
    sup = GDN_SUPER
    ltot = l + lc
    nc = ltot // GDN_CHUNK
    neg_a = -jnp.exp(a_log.astype(F32))
    cw = conv_w.astype(F32).T.reshape(3 * h, LANES, SHORT_CONV).transpose(0, 2, 1)
    win = sup + 2 * GDN_HALO
    i = np.arange(sup)
    pd = np.zeros((sup, win), np.float32)
    pu = np.zeros((sup, win), np.float32)
    pd[i, GDN_HALO + i - 1] = 1.0
    pu[i, GDN_HALO + i + 1] = 1.0
    seq = lambda n, j: pl.BlockSpec((1, n, LANES), lambda bi, hh: (bi, 0, j * h + hh))
    abs_ = lambda n: pl.BlockSpec((1, n, LANES), lambda bi, hh: (bi, 0, 0))
    cws = lambda j: pl.BlockSpec((1, SHORT_CONV, LANES), lambda bi, hh: (j * h + hh, 0, 0))
    full = lambda s: pl.BlockSpec(s, lambda bi, hh: (0, 0))
    smem = pl.BlockSpec(memory_space=pltpu.SMEM)
    yl, yc = pl.pallas_call(
        functools.partial(_gdn_kernel, ctx_out=ctx_out),
        grid=(b, h),
        in_specs=[smem, smem, seq(l, 0), seq(l, 1), seq(l, 2), seq(l, 3), abs_(l),
                  seq(lc, 0), seq(lc, 1), seq(lc, 2), seq(lc, 3), abs_(lc),
                  cws(0), cws(1), cws(2), full((sup, win)), full((sup, win)), full((1, GDN_DV))],
        out_specs=[pl.BlockSpec((1, l, GDN_DV), lambda bi, hh: (bi, 0, hh)),
                   pl.BlockSpec((1, lc, GDN_DV), lambda bi, hh: (bi, 0, hh))],
        out_shape=[jax.ShapeDtypeStruct((b, l, h * GDN_DV), BF16), jax.ShapeDtypeStruct((b, lc, h * GDN_DV), BF16)],
        scratch_shapes=[pltpu.VMEM((2, nc, GDN_DK, GDN_DK + GDN_DV), BF16), pltpu.VMEM((2, nc, GDN_DK, GDN_DV), BF16),
                        pltpu.VMEM((2, ltot, GDN_DK), BF16), pltpu.VMEM((2, ltot, GDN_DV), F32),
                        pltpu.VMEM((2, nc, 1, GDN_DV), F32)],
        compiler_params=_cparams(2),
        name="gated_deltanet",
    )(neg_a, dt_bias.astype(F32), zg_l, zg_l, zg_l, zg_l, zab_l, zg_c, zg_c, zg_c, zg_c, zab_c,
      cw, cw, cw, jnp.asarray(pd, BF16), jnp.asarray(pu, BF16), norm_gain.astype(F32).reshape(1, GDN_DV))
    return (yc if ctx_out else None), yl


def kernel(x, c, ctx, c_ctx, w_mod, b_mod, norm1, norm2, w_in, ret_decay, ret_gn, win_qnorm, win_knorm, win_sink,
           na_qnorm, na_knorm, na_rpb, gdn_conv, gdn_a_log, gdn_dt_bias, gdn_norm, w_branch, w_merge, w_out,
           w_router, router_bias, w_e_gate, w_e_up, w_e_down):
    b, l, d = x.shape
    lc = ctx.shape[1]
    depth = w_mod.shape[0]
    cos2, sin2 = _rope_tables(l)

    n_rows = 16
    cc = jnp.zeros((n_rows, d), F32).at[:b].set(c).at[b].set(c_ctx)
    mod = _modulation(cc, w_mod, b_mod).reshape(depth, n_rows, 6, d)

    wr = jnp.zeros((d, LANES), F32).at[:, :N_EXPERTS].set(w_router)
    wr_hi = wr.astype(BF16)
    wr_lo = (wr - wr_hi.astype(F32)).astype(BF16)

    xl, xc = x, ctx
    for layer in range(depth):
        ctx_out = layer < depth - 1
        mod_l = mod[layer, :b]
        mod_c = mod[layer, b:b + 1]
        w_all = _pack_w_in(w_in[layer])
        hl, *zl_s = _inproj(xl, mod_l, False, norm1[layer], w_all, 256)
        hc, *zc_s = _inproj(xc, mod_c, True, norm1[layer], w_all, 256)
        zr_l, zw_l, zn_l, zg_l, zab_l = zl_s
        zr_c, zw_c, zn_c, zg_c, zab_c = zc_s
        ret_c, ret_l = _retention_branch(zr_l, zr_c, ret_decay[layer], ret_gn[layer], cos2, sin2, ctx_out)
        win_c, win_l = _window_branch(zw_l, zw_c, win_qnorm[layer], win_knorm[layer], win_sink[layer], cos2, sin2,
                                      ctx_out)
        na_c, na_l = _neighborhood_branch(zn_l, zn_c, na_qnorm[layer], na_knorm[layer], na_rpb[layer], ctx_out)
        gdn_c, gdn_l = _gdn_branch(zg_l, zab_l, zg_c, zab_c, gdn_conv[layer], gdn_a_log[layer], gdn_dt_bias[layer],
                                   gdn_norm[layer], ctx_out)
        wm = w_merge[layer].astype(BF16)
        wb = w_branch[layer].astype(BF16)
        wo = w_out[layer].astype(BF16)
        ys_l = [ret_l, win_l, na_l, gdn_l]
        xl, h2l, sc_l = _merge(xl, hl, ys_l, mod_l, False, norm2[layer], wm, wb, wo, wr_hi, wr_lo, 256)
        if ctx_out:
            ys_c = [ret_c, win_c, na_c, gdn_c]
            xc, h2c, sc_c = _merge(xc, hc, ys_c, mod_c, True, norm2[layer], wm, wb, wo, wr_hi, wr_lo, 256)
            tokens = jnp.concatenate([h2c.reshape(b * lc, d), h2l.reshape(b * l, d)], axis=0)
            scores = jnp.concatenate([sc_c.reshape(b * lc, LANES), sc_l.reshape(b * l, LANES)], axis=0)
            y, wts = _moe(tokens, scores[:, :N_EXPERTS], router_bias, w_e_gate[layer], w_e_up[layer], w_e_down[layer])
            xc = _combine(xc, y, wts, mod_c, True, 0, 256)
            xl = _combine(xl, y, wts, mod_l, False, b * lc, 256)
        else:
            y, wts = _moe(h2l.reshape(b * l, d), sc_l.reshape(b * l, LANES)[:, :N_EXPERTS], router_bias,
                          w_e_gate[layer], w_e_up[layer], w_e_down[layer])
            xl = _combine(xl, y, wts, mod_l, False, 0, 256)
    return xl
```

```python
import functools

import numpy as np
import jax
import jax.numpy as jnp
from jax import lax
from jax.experimental import pallas as pl
from jax.experimental.pallas import tpu as pltpu

F32 = jnp.float32
BF16 = jnp.bfloat16
EPS = 1e-6
NEG_INF = -1e30
D_MODEL = 1024
GRID_W = 64
HEAD_DIM = 64
ROPE_THETA = 10000.0
RET_HEADS, RET_DK, RET_DV, RET_CHUNK = 4, 64, 128, 128
WIN_HEADS, WIN_KV_HEADS, WINDOW, WIN_BLOCK = 8, 2, 128, 128
NA_HEADS, NA_KH, NA_KW, NA_QCOLS = 8, 8, 16, 16
NA_BAND = NA_QCOLS + NA_KW
GDN_HEADS, GDN_DK, GDN_DV, GDN_CHUNK, SHORT_CONV = 4, 128, 128, 64, 3
GDN_QKV = 2 * GDN_HEADS * GDN_DK + GDN_HEADS * GDN_DV
N_BRANCH, BRANCH_W = 4, 512
N_EXPERTS, N_GROUPS, TOP_K, D_EXPERT = 32, 8, 2, 512
EXPERTS_PER_GROUP = N_EXPERTS // N_GROUPS

LANES = 128
VMEM_LIMIT = 56 * 1024 * 1024
TOKEN_TILE = 512
MOE_ROWS = 256
MOE_ROW_ALIGN = 8

W_RET = RET_HEADS * 4 * LANES
W_WIN = (WIN_HEADS + 2 * WIN_KV_HEADS) * HEAD_DIM
W_NA = 3 * NA_HEADS * HEAD_DIM
W_GDN = GDN_QKV + GDN_HEADS * GDN_DV
W_AB = LANES
SECTION_WIDTHS = (W_RET, W_WIN, W_NA, W_GDN, W_AB)
W_ALL = sum(SECTION_WIDTHS)


def _cparams(n_axes):
    return pltpu.CompilerParams(dimension_semantics=("arbitrary",) * n_axes, vmem_limit_bytes=VMEM_LIMIT)


def _mod_kernel(c_ref, w_ref, b_ref, o_ref):
    c = c_ref[...]
    a = (c * jax.nn.sigmoid(c)).astype(BF16)
    o_ref[0] = jnp.dot(a, w_ref[0].astype(BF16), preferred_element_type=F32) + b_ref[0]


def _modulation(cc, w_mod, b_mod):
    depth, d, n = w_mod.shape
    r = cc.shape[0]
    tn = 1536
    return pl.pallas_call(
        _mod_kernel,
        grid=(depth, n // tn),
        in_specs=[pl.BlockSpec((r, d), lambda l, j: (0, 0)),
                  pl.BlockSpec((1, d, tn), lambda l, j: (l, 0, j)),
                  pl.BlockSpec((1, 1, tn), lambda l, j: (l, 0, j))],
        out_specs=pl.BlockSpec((1, r, tn), lambda l, j: (l, 0, j)),
        out_shape=jax.ShapeDtypeStruct((depth, r, n), F32),
        compiler_params=_cparams(2),
        name="modulation",
    )(cc, w_mod, b_mod.reshape(depth, 1, n))


def _inproj_kernel(x_ref, mod_ref, gain_ref, w_ref, h_ref, *z_refs):
    x = x_ref[0]
    ms = jnp.mean(x * x, axis=-1, keepdims=True)
    shift = mod_ref[0, 0:1, :]
    scale = mod_ref[0, 1:2, :]
    h = x * lax.rsqrt(ms + EPS) * gain_ref[...] * (1.0 + scale) + shift
    hb = h.astype(BF16)
    h_ref[0] = hb
    off = 0
    for ref in z_refs:
        width = ref.shape[-1]
        for c0 in range(0, width, 512):
            c1 = min(c0 + 512, width)
            z = jnp.dot(hb, w_ref[:, off + c0:off + c1], preferred_element_type=F32)
            ref[0, :, c0:c1] = z.astype(ref.dtype)
        off += width


def _inproj(x, mod, mod_is_shared, gain, w_all, tm):
    b, l, d = x.shape
    mod_map = (lambda i, j: (0, 0, 0)) if mod_is_shared else (lambda i, j: (i, 0, 0))
    dtypes = (BF16, BF16, BF16, BF16, F32)
    out_shape = [jax.ShapeDtypeStruct((b, l, d), BF16)]
    out_specs = [pl.BlockSpec((1, tm, d), lambda i, j: (i, j, 0))]
    for w, dt in zip(SECTION_WIDTHS, dtypes):
        out_shape.append(jax.ShapeDtypeStruct((b, l, w), dt))
        out_specs.append(pl.BlockSpec((1, tm, w), lambda i, j: (i, j, 0)))
    return pl.pallas_call(
        _inproj_kernel,
        grid=(b, l // tm),
        in_specs=[pl.BlockSpec((1, tm, d), lambda i, j: (i, j, 0)),
                  pl.BlockSpec((1, 6, d), mod_map),
                  pl.BlockSpec((1, d), lambda i, j: (0, 0)),
                  pl.BlockSpec((d, W_ALL), lambda i, j: (0, 0), pipeline_mode=pl.Buffered(1))],
        out_specs=out_specs,
        out_shape=out_shape,
        compiler_params=_cparams(2),
        name="inproj",
    )(x, mod, gain.reshape(1, d), w_all)


def _pack_w_in(w_in):
    d = w_in.shape[0]
    hq, hv = RET_HEADS * RET_DK, RET_HEADS * RET_DV
    cols = []
    for h in range(RET_HEADS):
        q = w_in[:, h * RET_DK:(h + 1) * RET_DK]
        k = w_in[:, hq + h * RET_DK:hq + (h + 1) * RET_DK]
        cols += [q, q, k, k, w_in[:, 2 * hq + h * RET_DV:2 * hq + (h + 1) * RET_DV],
                 w_in[:, 2 * hq + hv + h * RET_DV:2 * hq + hv + (h + 1) * RET_DV]]
    rest = w_in[:, 2 * hq + 2 * hv:]
    pad = jnp.zeros((d, W_ALL - W_RET - rest.shape[1]), w_in.dtype)
    return jnp.concatenate(cols + [rest, pad], axis=1).astype(BF16)


def _merge_kernel(x_ref, h_ref, y0_ref, y1_ref, y2_ref, y3_ref, mod_ref, gain_ref, wm_ref, wb_ref, wo_ref,
                  wrh_ref, wrl_ref, xo_ref, h2_ref, sc_ref):
    d = x_ref.shape[-1]
    h = h_ref[0]
    acc = jnp.zeros(x_ref.shape[1:], F32)
    for i, y_ref in enumerate((y0_ref, y1_ref, y2_ref, y3_ref)):
        gate = jax.nn.sigmoid(jnp.dot(h, wm_ref[:, i * d:(i + 1) * d], preferred_element_type=F32))
        acc = acc + gate * jnp.dot(y_ref[0], wb_ref[i], preferred_element_type=F32)
    m = jnp.dot(acc.astype(BF16), wo_ref[...], preferred_element_type=F32)
    xn = x_ref[0] + mod_ref[0, 2:3, :] * m
    xo_ref[0] = xn
    ms = jnp.mean(xn * xn, axis=-1, keepdims=True)
    h2 = xn * lax.rsqrt(ms + EPS) * gain_ref[...] * (1.0 + mod_ref[0, 4:5, :]) + mod_ref[0, 3:4, :]
    hi = h2.astype(BF16)
    h2_ref[0] = h2
    lo = (h2 - hi.astype(F32)).astype(BF16)
    logits = (jnp.dot(hi, wrh_ref[...], preferred_element_type=F32)
              + jnp.dot(lo, wrh_ref[...], preferred_element_type=F32)
              + jnp.dot(hi, wrl_ref[...], preferred_element_type=F32))
    sc_ref[0] = jax.nn.sigmoid(logits)


def _merge(x, h, ys, mod, mod_is_shared, gain2, wm, wb, wo, wr_hi, wr_lo, tm):
    b, l, d = x.shape
    mod_map = (lambda i, j: (0, 0, 0)) if mod_is_shared else (lambda i, j: (i, 0, 0))
    tok = lambda w: pl.BlockSpec((1, tm, w), lambda i, j: (i, j, 0))
    full2 = lambda s: pl.BlockSpec(s, lambda i, j: (0, 0), pipeline_mode=pl.Buffered(1))
    return pl.pallas_call(
        _merge_kernel,
        grid=(b, l // tm),
        in_specs=[tok(d), tok(d)] + [tok(BRANCH_W)] * 4 + [
            pl.BlockSpec((1, 6, d), mod_map), full2((1, d)), full2(wm.shape),
            pl.BlockSpec(wb.shape, lambda i, j: (0, 0, 0), pipeline_mode=pl.Buffered(1)), full2(wo.shape),
            full2(wr_hi.shape), full2(wr_lo.shape)],
        out_specs=[tok(d), tok(d), tok(LANES)],
        out_shape=[jax.ShapeDtypeStruct((b, l, d), F32), jax.ShapeDtypeStruct((b, l, d), F32),
                   jax.ShapeDtypeStruct((b, l, LANES), F32)],
        compiler_params=_cparams(2),
        name="merge",
    )(x, h, *ys, mod, gain2.reshape(1, d), wm, wb, wo, wr_hi, wr_lo)


def _moe_kernel(be_ref, nv_ref, src_ref, dst_ref, h_hbm, wg_ref, wu_ref, wd_ref, y_hbm, xbuf, obuf, wg_s, wu_s, wd_s,
                gsem, ssem, *, n_real):
    i = pl.program_id(0)
    n = pl.num_programs(0)
    slot = i % 2
    nv = nv_ref[i]

    def start_gather(blk, dst_slot):
        def body(grp, carry):
            base = blk * MOE_ROWS + grp * MOE_ROW_ALIGN
            for j in range(MOE_ROW_ALIGN):
                pltpu.make_async_copy(h_hbm.at[pl.ds(src_ref[base + j], 1)], xbuf.at[dst_slot, grp, pl.ds(j, 1)],
                                      gsem.at[dst_slot]).start()
            return carry
        lax.fori_loop(0, nv_ref[blk] // MOE_ROW_ALIGN, body, 0)

    def wait_rows(buf, sem, cnt):
        grps = buf.at[pl.ds(0, cnt // MOE_ROW_ALIGN)]
        pltpu.make_async_copy(grps, grps, sem).wait()

    @pl.when(i == 0)
    def _():
        xbuf[...] = jnp.zeros_like(xbuf)
        fills = [pltpu.make_async_copy(xbuf.at[1, g], y_hbm.at[pl.ds(n_real + g * MOE_ROW_ALIGN, MOE_ROW_ALIGN)],
                                       ssem.at[1])
                 for g in range((y_hbm.shape[0] - n_real) // MOE_ROW_ALIGN)]
        for f in fills:
            f.start()
        for f in fills:
            f.wait()
        start_gather(0, 0)

    @pl.when(i + 1 < n)
    def _():
        start_gather(i + 1, 1 - slot)

    @pl.when((i >= 2) & (nv_ref[jnp.maximum(i - 2, 0)] > 0))
    def _():
        wait_rows(obuf.at[slot], ssem.at[slot], nv_ref[jnp.maximum(i - 2, 0)])

    @pl.when(nv > 0)
    def _():
        wait_rows(xbuf.at[slot], gsem.at[slot], nv)
        e = be_ref[i]

        @pl.when((i == 0) | (e != be_ref[jnp.maximum(i - 1, 0)]))
        def _():
            wg_s[...] = wg_ref[0].astype(BF16)
            wu_s[...] = wu_ref[0].astype(BF16)
            wd_s[...] = wd_ref[0].astype(BF16)

        d = xbuf.shape[-1]
        x = xbuf[slot].reshape(MOE_ROWS, d).astype(BF16)
        g = jnp.dot(x, wg_s[...], preferred_element_type=F32)
        u = jnp.dot(x, wu_s[...], preferred_element_type=F32)
        a = (g * jax.nn.sigmoid(g) * u).astype(BF16)
        obuf[slot] = jnp.dot(a, wd_s[...], preferred_element_type=F32).reshape(obuf.shape[1:])

        def body(grp, carry):
            base = i * MOE_ROWS + grp * MOE_ROW_ALIGN
            for j in range(MOE_ROW_ALIGN):
                pltpu.make_async_copy(obuf.at[slot, grp, pl.ds(j, 1)], y_hbm.at[pl.ds(dst_ref[base + j], 1)],
                                      ssem.at[slot]).start()
            return carry
        lax.fori_loop(0, nv // MOE_ROW_ALIGN, body, 0)

    @pl.when(i == n - 1)
    def _():
        @pl.when(nv > 0)
        def _():
            wait_rows(obuf.at[slot], ssem.at[slot], nv)

        @pl.when((n >= 2) & (nv_ref[jnp.maximum(i - 1, 0)] > 0))
        def _():
            wait_rows(obuf.at[1 - slot], ssem.at[1 - slot], nv_ref[jnp.maximum(i - 1, 0)])


def _moe_ffn(block_e, n_valid, src, dst, h2, w_gate, w_up, w_down):
    t, d = h2.shape
    grp_shape = (2, MOE_ROWS // MOE_ROW_ALIGN, MOE_ROW_ALIGN, d)
    n_blocks = block_e.shape[0]
    de = w_gate.shape[-1]
    wspec = lambda s: pl.BlockSpec((1,) + s, lambda i, be, nv, src, dst: (be[i], 0, 0))
    grid_spec = pltpu.PrefetchScalarGridSpec(
        num_scalar_prefetch=4,
        grid=(n_blocks,),
        in_specs=[pl.BlockSpec(memory_space=pl.ANY), wspec((d, de)), wspec((d, de)), wspec((de, d))],
        out_specs=pl.BlockSpec(memory_space=pl.ANY),
        scratch_shapes=[pltpu.VMEM(grp_shape, F32), pltpu.VMEM(grp_shape, F32),
                        pltpu.VMEM((d, de), BF16), pltpu.VMEM((d, de), BF16), pltpu.VMEM((de, d), BF16),
                        pltpu.SemaphoreType.DMA((2,)), pltpu.SemaphoreType.DMA((2,))],
    )
    return pl.pallas_call(
        functools.partial(_moe_kernel, n_real=TOP_K * t),
        grid_spec=grid_spec,
        out_shape=jax.ShapeDtypeStruct((TOP_K * t + MOE_ROW_ALIGN * N_EXPERTS, d), F32),
        compiler_params=pltpu.CompilerParams(dimension_semantics=("arbitrary",), vmem_limit_bytes=VMEM_LIMIT,
                                             has_side_effects=True),
        name="moe_ffn",
    )(block_e, n_valid, src, dst, h2, w_gate, w_up, w_down)


def _route(scores, router_bias):
    t = scores.shape[0]
    sel = (scores + router_bias.astype(F32)).reshape(t, N_GROUPS, EXPERTS_PER_GROUP)
    pairs = [sel[..., i] + sel[..., j] for i in range(EXPERTS_PER_GROUP) for j in range(i + 1, EXPERTS_PER_GROUP)]
    grp_score = functools.reduce(jnp.maximum, pairs)
    g_idx = jnp.argmax(grp_score, axis=-1).astype(jnp.int32)
    g_hot = (g_idx[:, None] == jnp.arange(N_GROUPS, dtype=jnp.int32)[None, :])[:, :, None]
    in_grp = jnp.sum(jnp.where(g_hot, sel, 0.0), axis=1)
    sc_grp = jnp.sum(jnp.where(g_hot, scores.reshape(t, N_GROUPS, EXPERTS_PER_GROUP), 0.0), axis=1)
    lane4 = jnp.arange(EXPERTS_PER_GROUP, dtype=jnp.int32)[None, :]
    i1 = jnp.argmax(in_grp, axis=-1).astype(jnp.int32)
    i2 = jnp.argmax(jnp.where(lane4 == i1[:, None], -jnp.inf, in_grp), axis=-1).astype(jnp.int32)
    e_idx = g_idx[:, None] * EXPERTS_PER_GROUP + jnp.stack([i1, i2], axis=-1)
    wts = jnp.stack([jnp.sum(jnp.where(lane4 == i1[:, None], sc_grp, 0.0), axis=-1),
                     jnp.sum(jnp.where(lane4 == i2[:, None], sc_grp, 0.0), axis=-1)], axis=-1)
    return e_idx, wts / jnp.sum(wts, axis=-1, keepdims=True)


def _moe(h2, scores, router_bias, w_gate, w_up, w_down):
    t, d = h2.shape
    e_idx, wts = _route(scores, router_bias)
    a = t * TOP_K
    flat_e = e_idx.T.reshape(-1)
    onehot = (flat_e[:, None] == jnp.arange(N_EXPERTS, dtype=jnp.int32)[None, :]).astype(jnp.int32)
    rb = MOE_ROWS
    blocks = onehot.reshape(a // rb, rb, N_EXPERTS).astype(BF16)
    within = jnp.einsum('ij,bje->bie', jnp.tril(jnp.ones((rb, rb), BF16)), blocks, preferred_element_type=F32)
    before = jnp.cumsum(within[:, -1, :], axis=0) - within[:, -1, :]
    csum = (within + before[:, None, :]).astype(jnp.int32).reshape(a, N_EXPERTS)
    counts = csum[-1]
    rank = jnp.sum(csum * onehot, axis=1) - 1
    padded = (counts + MOE_ROWS - 1) // MOE_ROWS * MOE_ROWS
    ends = jnp.cumsum(padded)
    pstarts = ends - padded
    dest = jnp.sum(pstarts[None, :] * onehot, axis=1) + rank
    n_blocks = -(-a // MOE_ROWS) + N_EXPERTS
    blk0 = jnp.arange(n_blocks, dtype=jnp.int32) * MOE_ROWS
    block_e = jnp.minimum(jnp.searchsorted(ends, blk0, side='right'), N_EXPERTS - 1).astype(jnp.int32)
    n_valid = jnp.clip((pstarts + counts)[block_e] - blk0, 0, MOE_ROWS)
    n_valid = ((n_valid + MOE_ROW_ALIGN - 1) // MOE_ROW_ALIGN * MOE_ROW_ALIGN).astype(jnp.int32)
    slot = jnp.arange(n_blocks * MOE_ROWS, dtype=jnp.int32)
    spare = a + MOE_ROW_ALIGN * jnp.repeat(block_e, MOE_ROWS) + slot % MOE_ROW_ALIGN
    dst = spare.at[dest].set(jnp.arange(a, dtype=jnp.int32))
    src = jnp.where(dst < a, dst % t, 0)
    return _moe_ffn(block_e, n_valid, src, dst, h2, w_gate, w_up, w_down), wts


def _combine_kernel(x_ref, y0_ref, y1_ref, w_ref, mod_ref, o_ref):
    w = w_ref[...]
    y = y0_ref[...] * w[:, 0:1] + y1_ref[...] * w[:, 1:2]
    o_ref[0] = x_ref[0] + mod_ref[0, 5:6, :] * y


def _combine(x, y, wts, mod, mod_is_shared, row_offset, tm):
    b, l, d = x.shape
    mod_map = (lambda i, j: (0, 0, 0)) if mod_is_shared else (lambda i, j: (i, 0, 0))
    off = row_offset // tm
    per = l // tm
    return pl.pallas_call(
        _combine_kernel,
        grid=(b, per),
        in_specs=[pl.BlockSpec((1, tm, d), lambda i, j: (i, j, 0)),
                  pl.BlockSpec((tm, d), lambda i, j: (off + i * per + j, 0)),
                  pl.BlockSpec((tm, d), lambda i, j: (wts.shape[0] // tm + off + i * per + j, 0)),
                  pl.BlockSpec((tm, TOP_K), lambda i, j: (off + i * per + j, 0)),
                  pl.BlockSpec((1, 6, d), mod_map)],
        out_specs=pl.BlockSpec((1, tm, d), lambda i, j: (i, j, 0)),
        out_shape=jax.ShapeDtypeStruct((b, l, d), F32),
        compiler_params=_cparams(2),
        name="moe_combine",
    )(x, y, y, wts, mod)


def _head_rms(x, gain, e_ref):
    ms = jnp.dot((x * x).astype(BF16), e_ref[...], preferred_element_type=F32) * (1.0 / HEAD_DIM)
    return x * lax.rsqrt(ms + EPS) * gain


def _rope(y, cos, sin):
    lane = lax.broadcasted_iota(jnp.int32, y.shape, 1)
    half = HEAD_DIM // 2
    rot = jnp.where((lane % HEAD_DIM) < half, pltpu.roll(y, LANES - half, 1), pltpu.roll(y, half, 1))
    return y * cos + rot * sin


def _attend(problems):
    all_scores = []
    for q, tiles, _ in problems:
        scores = []
        for k, _, bias in tiles:
            s = lax.dot_general(q, k, (((1,), (1,)), ((), ())), preferred_element_type=F32)
            scores.append(s if bias is None else s + bias)
        if all(s.shape[1] % LANES == 0 for s in scores):
            scores = [jnp.concatenate(scores, axis=1)]
        all_scores.append(scores)
    maxes = []
    for scores, (_, _, sink) in zip(all_scores, problems):
        m = sink
        for s in scores:
            mt = jnp.max(s, axis=-1, keepdims=True)
            m = mt if m is None else jnp.maximum(m, mt)
        maxes.append(m)
    all_probs, dens = [], []
    for scores, m, (_, tiles, sink) in zip(all_scores, maxes, problems):
        den = jnp.exp(sink - m) if sink is not None else jnp.zeros_like(m)
        probs = []
        for s in scores:
            p = jnp.exp(s - m)
            den = den + jnp.sum(p, axis=-1, keepdims=True)
            probs.append(p.astype(BF16))
        if len(probs) != len(tiles):
            offs = np.cumsum([0] + [k.shape[0] for k, _, _ in tiles])
            probs = [probs[0][:, offs[i]:offs[i + 1]] for i in range(len(tiles))]
        all_probs.append(probs)
        dens.append(den)
    outs = []
    for probs, den, (_, tiles, _) in zip(all_probs, dens, problems):
        o = None
        for p, (_, v, _) in zip(probs, tiles):
            pv = jnp.dot(p, v, preferred_element_type=F32)
            o = pv if o is None else o + pv
        outs.append(o * (1.0 / den))
    return outs


def _lane_half(shape):
    return lax.broadcasted_iota(jnp.int32, shape, 1) // HEAD_DIM


def _win_group_queries(slabs, g, sink_ref):
    r = WIN_HEADS // WIN_KV_HEADS
    rows = slabs[0].shape[0]
    half = _lane_half(slabs[0].shape)
    parts, sinks = [], []
    for j in range(r):
        head = g * r + j
        slab = slabs[head // 2]
        if head % 2 != g:
            slab = pltpu.roll(slab, HEAD_DIM, 1)
        parts.append(jnp.where(half == g, slab, 0.0))
        sinks.append(jnp.full((rows, 1), sink_ref[head], F32))
    return jnp.concatenate(parts, axis=0).astype(BF16), jnp.concatenate(sinks, axis=0)


def _win_store(o_ref, o, g, rows):
    r = WIN_HEADS // WIN_KV_HEADS
    half = _lane_half((rows, LANES))
    for pair in range(r // 2):
        a = o[(2 * pair) * rows:(2 * pair + 1) * rows]
        b = o[(2 * pair + 1) * rows:(2 * pair + 2) * rows]
        if g == 0:
            b = pltpu.roll(b, HEAD_DIM, 1)
        else:
            a = pltpu.roll(a, HEAD_DIM, 1)
        s = (g * r + 2 * pair) // 2
        o_ref[0, :, s * LANES:(s + 1) * LANES] = jnp.where(half == 0, a, b).astype(o_ref.dtype)


def _win_kernel(sink_ref, q_ref, kv_ref, ckv_ref, cos_ref, sin_ref, qg_ref, kg_ref, e_ref, o_ref, ks_ref, kcs_ref):
    n = pl.program_id(1)
    nb = pl.num_programs(1)
    l = kv_ref.shape[1]
    blk = WIN_BLOCK
    prep_rows = 512

    @pl.when(n == 0)
    def _():
        def body(i, carry):
            r0 = pl.multiple_of(i * prep_rows, prep_rows)
            y = _head_rms(kv_ref[0, pl.ds(r0, prep_rows), 0:LANES].astype(F32), kg_ref[...], e_ref)
            ks_ref[pl.ds(r0, prep_rows), :] = _rope(y, cos_ref[pl.ds(r0, prep_rows), :],
                                                    sin_ref[pl.ds(r0, prep_rows), :]).astype(BF16)
            return carry
        lax.fori_loop(0, l // prep_rows, body, 0)
        kcs_ref[...] = _head_rms(ckv_ref[0, :, 0:LANES].astype(F32), kg_ref[...], e_ref).astype(BF16)

    r0 = pl.multiple_of(n * blk, blk)
    cos = cos_ref[pl.ds(r0, blk), :]
    sin = sin_ref[pl.ds(r0, blk), :]
    slabs = []
    for s in range(q_ref.shape[-1] // LANES):
        y = _head_rms(q_ref[0, :, s * LANES:(s + 1) * LANES].astype(F32), qg_ref[...], e_ref)
        slabs.append(_rope(y, cos, sin))

    r = WIN_HEADS // WIN_KV_HEADS
    qi = lax.broadcasted_iota(jnp.int32, (r * blk, blk), 0) % blk
    kj = lax.broadcasted_iota(jnp.int32, (r * blk, blk), 1)
    ok_prev = (kj >= qi) & (n > 0)
    ok_next = (kj <= qi) & (n < nb - 1)
    bias_prev = jnp.where(ok_prev, 0.0, NEG_INF)
    bias_next = jnp.where(ok_next, 0.0, NEG_INF)
    tiles = []
    for kb, bias in ((jnp.maximum(n - 1, 0), bias_prev), (n, None), (jnp.minimum(n + 1, nb - 1), bias_next)):
        k0 = pl.multiple_of(kb * blk, blk)
        tiles.append((ks_ref[pl.ds(k0, blk), :], kv_ref[0, pl.ds(k0, blk), LANES:2 * LANES], bias))
    tiles.append((kcs_ref[...], ckv_ref[0, :, LANES:2 * LANES], None))
    rows = 2 * blk
    tiles = [(k, v, None if bias is None else bias[:rows]) for k, v, bias in tiles]
    for g in range(WIN_KV_HEADS):
        q, sink = _win_group_queries(slabs, g, sink_ref)
        outs = [_attend([(q[h * rows:(h + 1) * rows], tiles, sink[h * rows:(h + 1) * rows])])[0]
                for h in range(q.shape[0] // rows)]
        _win_store(o_ref, jnp.concatenate(outs, axis=0), g, blk)


def _win_ctx_kernel(sink_ref, q_ref, ckv_ref, qg_ref, kg_ref, e_ref, o_ref):
    kc = _head_rms(ckv_ref[0, :, 0:LANES].astype(F32), kg_ref[...], e_ref).astype(BF16)
    tiles = [(kc, ckv_ref[0, :, LANES:2 * LANES], None)]
    slabs = [_head_rms(q_ref[0, :, s * LANES:(s + 1) * LANES].astype(F32), qg_ref[...], e_ref)
             for s in range(q_ref.shape[-1] // LANES)]
    groups = [_win_group_queries(slabs, g, sink_ref) for g in range(WIN_KV_HEADS)]
    outs = _attend([(q, tiles, sink) for q, sink in groups])
    for g in range(WIN_KV_HEADS):
        _win_store(o_ref, outs[g], g, q_ref.shape[1])


def _block_diag_ones(width):
    i = np.arange(width) // HEAD_DIM
    return jnp.asarray(i[:, None] == i[None, :], BF16)


def _window_branch(zw_l, zw_c, q_gain, k_gain, sink, cos2, sin2, ctx_out):
    b, l, _ = zw_l.shape
    lc = zw_c.shape[1]
    wq = WIN_HEADS * HEAD_DIM
    qg = (jnp.tile(q_gain.astype(F32), 2) * HEAD_DIM ** -0.5).reshape(1, LANES)
    kg = jnp.tile(k_gain.astype(F32), 2).reshape(1, LANES)
    e = _block_diag_ones(LANES)
    smem = pl.BlockSpec(memory_space=pltpu.SMEM)
    full = lambda s, nd: pl.BlockSpec(s, lambda *a: (0,) * nd)
    yl = pl.pallas_call(
        _win_kernel,
        grid=(b, l // WIN_BLOCK),
        in_specs=[smem,
                  pl.BlockSpec((1, WIN_BLOCK, wq), lambda i, n: (i, n, 0)),
                  pl.BlockSpec((1, l, 2 * LANES), lambda i, n: (i, 0, wq // (2 * LANES))),
                  pl.BlockSpec((1, lc, 2 * LANES), lambda i, n: (i, 0, wq // (2 * LANES))),
                  full((l, LANES), 2), full((l, LANES), 2), full((1, LANES), 2), full((1, LANES), 2),
                  full((LANES, LANES), 2)],
        out_specs=pl.BlockSpec((1, WIN_BLOCK, wq), lambda i, n: (i, n, 0)),
        out_shape=jax.ShapeDtypeStruct((b, l, wq), BF16),
        scratch_shapes=[pltpu.VMEM((l, LANES), BF16), pltpu.VMEM((lc, LANES), BF16)],
        compiler_params=_cparams(2),
        name="window_attn",
    )(sink.astype(F32), zw_l, zw_l, zw_c, cos2, sin2, qg, kg, e)
    yc = None
    if ctx_out:
        yc = pl.pallas_call(
            _win_ctx_kernel,
            grid=(b,),
            in_specs=[smem,
                      pl.BlockSpec((1, lc, wq), lambda i: (i, 0, 0)),
                      pl.BlockSpec((1, lc, 2 * LANES), lambda i: (i, 0, wq // (2 * LANES))),
                      full((1, LANES), 2), full((1, LANES), 2), full((LANES, LANES), 2)],
            out_specs=pl.BlockSpec((1, lc, wq), lambda i: (i, 0, 0)),
            out_shape=jax.ShapeDtypeStruct((b, lc, wq), BF16),
            compiler_params=_cparams(1),
            name="window_ctx_attn",
        )(sink.astype(F32), zw_c, zw_c, qg, kg, e)
    return yc, yl


NA_ROWS = 8
NA_KROWS = NA_ROWS + NA_KH - 1


def _na_key_base(rg, rows):
    return jnp.clip(rg * NA_ROWS - NA_KH // 2, 0, rows - NA_KROWS)


def _na_kernel(q_ref, k_ref, v_ref, ck_ref, cv_ref, bias_ref, qg_ref, kg_ref, e_ref, o_ref, ks_ref, kcs_ref):
    rg = pl.program_id(1)
    l = k_ref.shape[1]
    rows = l // GRID_W
    n_slab = q_ref.shape[-1] // LANES
    prep_rows = 512

    @pl.when(rg == 0)
    def _():
        def body(i, carry):
            r0 = pl.multiple_of(i * prep_rows, prep_rows)
            ks_ref[pl.ds(r0, prep_rows), :] = _head_rms(k_ref[0, pl.ds(r0, prep_rows), :].astype(F32), kg_ref[...],
                                                        e_ref).astype(BF16)
            return carry
        lax.fori_loop(0, l // prep_rows, body, 0)
        kcs_ref[...] = _head_rms(ck_ref[0].astype(F32), kg_ref[...], e_ref).astype(BF16)

    nk = NA_KROWS * GRID_W
    k0 = pl.multiple_of(_na_key_base(rg, rows) * GRID_W, GRID_W)
    qn = _head_rms(q_ref[0].astype(F32), qg_ref[...], e_ref)
    half = _lane_half((q_ref.shape[1], LANES))
    for s in range(n_slab):
        cols = slice(s * LANES, (s + 1) * LANES)
        slab = qn[:, cols]
        tiles_kv = (ks_ref[pl.ds(k0, nk), cols], v_ref[0, pl.ds(k0, nk), cols])
        ctx_kv = (kcs_ref[:, cols], cv_ref[0, :, cols])
        outs = _attend([(jnp.where(half == hh, slab, 0.0).astype(BF16),
                         [tiles_kv + (bias_ref[0, 2 * s + hh].astype(F32),), ctx_kv + (None,)], None)
                        for hh in range(2)])
        o_ref[0, :, cols] = jnp.where(half == 0, outs[0], outs[1]).astype(o_ref.dtype)


def _na_ctx_kernel(q_ref, ck_ref, cv_ref, qg_ref, kg_ref, e_ref, o_ref):
    kc = _head_rms(ck_ref[0].astype(F32), kg_ref[...], e_ref).astype(BF16)
    qn = _head_rms(q_ref[0].astype(F32), qg_ref[...], e_ref)
    half = _lane_half((q_ref.shape[1], LANES))
    for s in range(q_ref.shape[-1] // LANES):
        cols = slice(s * LANES, (s + 1) * LANES)
        outs = _attend([(jnp.where(half == hh, qn[:, cols], 0.0).astype(BF16),
                         [(kc[:, cols], cv_ref[0, :, cols], None)], None) for hh in range(2)])
        o_ref[0, :, cols] = jnp.where(half == 0, outs[0], outs[1]).astype(o_ref.dtype)


def _na_bias_classes(rows):
    n_rg = rows // NA_ROWS
    return list(range(n_rg)) if n_rg <= 3 else [0, 1, n_rg - 1]


def _na_bias_table(rpb, rows):
    kh = NA_KH
    ro, rv = [], []
    for rg in _na_bias_classes(rows):
        kbase = int(np.clip(rg * NA_ROWS - kh // 2, 0, rows - NA_KROWS))
        r = rg * NA_ROWS + np.arange(NA_ROWS)
        rstart = np.clip(r - kh // 2, 0, rows - kh)
        kr = kbase + np.arange(NA_KROWS)
        rv.append((kr[None, :] >= rstart[:, None]) & (kr[None, :] < rstart[:, None] + kh))
        ro.append(np.clip(kr[None, :] - r[:, None] + kh - 1, 0, 2 * kh - 2))
    ro, rv = np.stack(ro), np.stack(rv)
    qc = np.arange(GRID_W)
    cstart = np.clip(qc - NA_KW // 2, 0, GRID_W - NA_KW)
    cv = (qc[None, :] >= cstart[:, None]) & (qc[None, :] < cstart[:, None] + NA_KW)
    co = np.clip(qc[None, :] - qc[:, None] + NA_KW - 1, 0, 2 * NA_KW - 2)
    n_cls = ro.shape[0]
    hp = lax.Precision.HIGHEST
    co_hot = jnp.asarray(co[None] == np.arange(2 * NA_KW - 1)[:, None, None], F32)
    ro_hot = jnp.asarray(ro[..., None] == np.arange(2 * kh - 1), F32)
    cols = jnp.einsum('hdc,cwx->hdwx', rpb.astype(F32), co_hot, precision=hp)
    tab = jnp.einsum('crkd,hdwx->chrwkx', ro_hot, cols, precision=hp)
    ok = rv[:, None, :, None, :, None] & cv[None, None, None, :, None, :]
    tab = jnp.where(ok, tab, NEG_INF)
    return tab.reshape(n_cls, rpb.shape[0], NA_ROWS * GRID_W, NA_KROWS * GRID_W).astype(BF16)


def _neighborhood_branch(zn_l, zn_c, q_gain, k_gain, rpb, ctx_out):
    b, l, _ = zn_l.shape
    lc = zn_c.shape[1]
    w = NA_HEADS * HEAD_DIM
    rows = l // GRID_W
    n_rg = rows // NA_ROWS
    n_slab = w // LANES
    qg = (jnp.tile(q_gain.astype(F32), 2 * n_slab) * HEAD_DIM ** -0.5).reshape(1, w)
    kg = jnp.tile(k_gain.astype(F32), 2 * n_slab).reshape(1, w)
    e = _block_diag_ones(w)
    bias = _na_bias_table(rpb, rows)
    if n_rg <= 3:
        cls_map = lambda i, r: (r, 0, 0, 0)
    else:
        cls_map = lambda i, r: ((r > 0).astype(jnp.int32) + (r == n_rg - 1).astype(jnp.int32), 0, 0, 0)
    tq = NA_ROWS * GRID_W
    full = lambda s, nd: pl.BlockSpec(s, lambda *a: (0,) * nd)
    yl = pl.pallas_call(
        _na_kernel,
        grid=(b, n_rg),
        in_specs=[pl.BlockSpec((1, tq, w), lambda i, r: (i, r, 0)),
                  pl.BlockSpec((1, l, w), lambda i, r: (i, 0, 1)),
                  pl.BlockSpec((1, l, w), lambda i, r: (i, 0, 2)),
                  pl.BlockSpec((1, lc, w), lambda i, r: (i, 0, 1)),
                  pl.BlockSpec((1, lc, w), lambda i, r: (i, 0, 2)),
                  pl.BlockSpec((1,) + bias.shape[1:], cls_map),
                  full((1, w), 2), full((1, w), 2), full((w, w), 2)],
        out_specs=pl.BlockSpec((1, tq, w), lambda i, r: (i, r, 0)),
        out_shape=jax.ShapeDtypeStruct((b, l, w), BF16),
        scratch_shapes=[pltpu.VMEM((l, w), BF16), pltpu.VMEM((lc, w), BF16)],
        compiler_params=_cparams(2),
        name="neighborhood_attn",
    )(zn_l, zn_l, zn_l, zn_c, zn_c, bias, qg, kg, e)
    yc = None
    if ctx_out:
        yc = pl.pallas_call(
            _na_ctx_kernel,
            grid=(b,),
            in_specs=[pl.BlockSpec((1, lc, w), lambda i: (i, 0, 0)),
                      pl.BlockSpec((1, lc, w), lambda i: (i, 0, 1)),
                      pl.BlockSpec((1, lc, w), lambda i: (i, 0, 2)),
                      full((1, w), 2), full((1, w), 2), full((w, w), 2)],
            out_specs=pl.BlockSpec((1, lc, w), lambda i: (i, 0, 0)),
            out_shape=jax.ShapeDtypeStruct((b, lc, w), BF16),
            compiler_params=_cparams(1),
            name="neighborhood_ctx_attn",
        )(zn_c, zn_c, zn_c, qg, kg, e)
    return yc, yl


def _rope_tables(n_tokens):
    t = np.arange(n_tokens)
    n_freq = HEAD_DIM // 4
    inv = ROPE_THETA ** (-jnp.arange(n_freq, dtype=F32) / n_freq)
    ang = jnp.concatenate([jnp.asarray(t // GRID_W, F32)[:, None] * inv, jnp.asarray(t % GRID_W, F32)[:, None] * inv],
                          axis=-1)
    cos, sin = jnp.cos(ang), jnp.sin(ang)
    return jnp.tile(jnp.concatenate([cos, cos], -1), (1, 2)), jnp.tile(jnp.concatenate([-sin, sin], -1), (1, 2))


def _ret_kernel(qq_l, kk_l, v_l, g_l, qq_c, kk_c, v_c, g_c, cos_ref, sin_ref, dmask_ref, tq_ref, tk_ref, cdec_ref,
                gain_ref, yl_ref, yc_ref, kr_ref, kv_ref, sin_state_ref, *, ctx_out):
    c = RET_CHUNK
    ncc = qq_c.shape[1] // c
    ncl = qq_l.shape[1] // c
    nc = ncc + ncl
    tk = tk_ref[0]
    tq = tq_ref[0]
    dmask = dmask_ref[0]
    fwd_lanes = lax.broadcasted_iota(jnp.int32, (c, LANES), 1) < RET_DK

    def chunk_kv(k2, v):
        kd = (k2 * tk).astype(BF16)
        return lax.dot_general(kd, v, (((0,), (0,)), ((), ())), preferred_element_type=F32)

    for n in range(ncc):
        kv_ref[n] = chunk_kv(kk_c[0, n * c:(n + 1) * c, :].astype(F32), v_c[0, n * c:(n + 1) * c, :])

    def kv_body(n, carry):
        r0 = pl.multiple_of(n * c, c)
        k2 = _rope(kk_l[0, pl.ds(r0, c), :].astype(F32), cos_ref[pl.ds(r0, c), :], sin_ref[pl.ds(r0, c), :])
        kr_ref[pl.ds(r0, c), :] = k2.astype(BF16)
        kv_ref[ncc + n] = chunk_kv(k2, v_l[0, pl.ds(r0, c), :])
        return carry
    lax.fori_loop(0, ncl, kv_body, 0, unroll=4)

    dec_f = cdec_ref[0, 0:1, :]
    dec_b = cdec_ref[0, 1:2, :]

    def scan_body(t, carry):
        sf, sb = carry
        sin_state_ref[t, 0:RET_DK, :] = sf.astype(BF16)
        sf = sf * dec_f + kv_ref[t, 0:RET_DK, :]
        tb = jnp.where(t < ncc, ncc - 1 - t, nc - 1 - (t - ncc))
        sin_state_ref[tb, RET_DK:2 * RET_DK, :] = sb.astype(BF16)
        sb = sb * dec_b + kv_ref[tb, RET_DK:2 * RET_DK, :]
        return sf, sb
    zero = jnp.zeros((RET_DK, RET_DV), F32)
    lax.fori_loop(0, nc, scan_body, (zero, zero))

    def chunk_out(q2, k2b, v, gate, state):
        qm = jnp.where(fwd_lanes, q2, 0.0).astype(BF16)
        scores = lax.dot_general(qm, k2b, (((1,), (1,)), ((), ())), preferred_element_type=F32) * dmask
        o = (jnp.dot(scores.astype(BF16), v, preferred_element_type=F32)
             + jnp.dot((q2 * tq).astype(BF16), state, preferred_element_type=F32))
        mu = jnp.mean(o, axis=-1, keepdims=True)
        var = jnp.mean(jnp.square(o - mu), axis=-1, keepdims=True)
        y = (o - mu) * lax.rsqrt(var + EPS) * gain_ref[0]
        gf = gate.astype(F32)
        return y * gf * jax.nn.sigmoid(gf)

    if ctx_out:
        for n in range(ncc):
            rows = slice(n * c, (n + 1) * c)
            yc_ref[0, rows, :] = chunk_out(qq_c[0, rows, :].astype(F32), kk_c[0, rows, :], v_c[0, rows, :],
                                           g_c[0, rows, :], sin_state_ref[n]).astype(yc_ref.dtype)
    else:
        yc_ref[...] = jnp.zeros_like(yc_ref)

    def out_body(n, carry):
        r0 = pl.multiple_of(n * c, c)
        q2 = _rope(qq_l[0, pl.ds(r0, c), :].astype(F32), cos_ref[pl.ds(r0, c), :], sin_ref[pl.ds(r0, c), :])
        yl_ref[0, pl.ds(r0, c), :] = chunk_out(q2, kr_ref[pl.ds(r0, c), :], v_l[0, pl.ds(r0, c), :],
                                               g_l[0, pl.ds(r0, c), :], sin_state_ref[ncc + n]).astype(yl_ref.dtype)
        return carry
    lax.fori_loop(0, ncl, out_body, 0, unroll=4)


def _ret_tables(decay_logit):
    c = RET_CHUNK
    lg = jax.nn.log_sigmoid(decay_logit.astype(F32))
    lf, lb = lg[0][:, None, None], lg[1][:, None, None]
    pos = jnp.arange(c, dtype=F32)
    diff = pos[:, None] - pos[None, :]
    dmask = (jnp.where(diff >= 0, jnp.exp(lf * jnp.maximum(diff, 0.0)), 0.0)
             + jnp.where(diff <= 0, jnp.exp(lb * jnp.maximum(-diff, 0.0)), 0.0)) * RET_DK ** -0.5
    col = lambda a, b_: jnp.concatenate([jnp.broadcast_to(a, a.shape[:2] + (RET_DK,)),
                                         jnp.broadcast_to(b_, b_.shape[:2] + (RET_DK,))], axis=-1)
    p = pos[None, :, None]
    tq = col(jnp.exp(lf * (p + 1.0)), jnp.exp(lb * (c - p)))
    tk = col(jnp.exp(lf * (c - 1.0 - p)), jnp.exp(lb * p)) * RET_DK ** -0.5
    cdec = jnp.zeros((lg.shape[1], 8, RET_DV), F32)
    cdec = cdec.at[:, 0, :].set(jnp.exp(lg[0] * c)[:, None]).at[:, 1, :].set(jnp.exp(lg[1] * c)[:, None])
    return dmask, tq, tk, cdec


def _retention_branch(zr_l, zr_c, decay_logit, gn_gain, cos2, sin2, ctx_out):
    b, l, _ = zr_l.shape
    lc = zr_c.shape[1]
    h = RET_HEADS
    nc = (l + lc) // RET_CHUNK
    dmask, tq, tk, cdec = _ret_tables(decay_logit)
    gain = gn_gain.astype(F32).reshape(h, 1, RET_DV)
    seq = lambda n, j: pl.BlockSpec((1, n, LANES), lambda i, hh: (i, 0, 4 * hh + j))
    head = lambda s: pl.BlockSpec((1,) + s, lambda i, hh: (hh, 0, 0))
    full = lambda s: pl.BlockSpec(s, lambda i, hh: (0, 0))
    yl, yc = pl.pallas_call(
        functools.partial(_ret_kernel, ctx_out=ctx_out),
        grid=(b, h),
        in_specs=[seq(l, 0), seq(l, 1), seq(l, 2), seq(l, 3), seq(lc, 0), seq(lc, 1), seq(lc, 2), seq(lc, 3),
                  full((l, LANES)), full((l, LANES)),
                  head((RET_CHUNK, RET_CHUNK)), head((RET_CHUNK, LANES)), head((RET_CHUNK, LANES)), head((8, RET_DV)),
                  head((1, RET_DV))],
        out_specs=[pl.BlockSpec((1, l, RET_DV), lambda i, hh: (i, 0, hh)),
                   pl.BlockSpec((1, lc, RET_DV), lambda i, hh: (i, 0, hh))],
        out_shape=[jax.ShapeDtypeStruct((b, l, h * RET_DV), BF16), jax.ShapeDtypeStruct((b, lc, h * RET_DV), BF16)],
        scratch_shapes=[pltpu.VMEM((l, LANES), BF16), pltpu.VMEM((nc, 2 * RET_DK, RET_DV), F32),
                        pltpu.VMEM((nc, 2 * RET_DK, RET_DV), BF16)],
        compiler_params=_cparams(2),
        name="retention",
    )(zr_l, zr_l, zr_l, zr_l, zr_c, zr_c, zr_c, zr_c, cos2, sin2, dmask, tq, tk, cdec, gain)
    return (yc if ctx_out else None), yl


GDN_SUPER = 128
GDN_UNITS = 4
GDN_HALO = 128


def _split_bf16(a):
    hi = a.astype(BF16)
    return hi, (a - hi.astype(F32)).astype(BF16)


def _mask_dot(mask_bf16, a):
    ah, al = _split_bf16(a)
    return jnp.dot(mask_bf16, ah, preferred_element_type=F32) + jnp.dot(mask_bf16, al, preferred_element_type=F32)


def _softplus(x):
    return jnp.maximum(x, 0.0) + jnp.log(1.0 + jnp.exp(-jnp.abs(x)))


def _gdn_kernel(nega_ref, dtb_ref, q_l, k_l, v_l, g_l, ab_l, q_c, k_c, v_c, g_c, ab_c,
                cwq_ref, cwk_ref, cwv_ref, pd_ref, pu_ref, gain_ref, yl_ref, yc_ref,
                kn_s, sin_s, qp_s, o0_s, cd_s, *, ctx_out):
    hd = pl.program_id(1)
    c = GDN_CHUNK
    sup = GDN_SUPER
    per = sup // c
    lc, l = q_c.shape[1], q_l.shape[1]
    ncc, ncl = lc // c, l // c
    nc = ncc + ncl

    ri = lax.broadcasted_iota(jnp.int32, (sup, sup), 0)
    ci = lax.broadcasted_iota(jnp.int32, (sup, sup), 1)
    same = (ri // c) == (ci // c)
    eye = (ri == ci).astype(F32)
    incl = (same & (ri >= ci), same & (ri <= ci))
    strict = (same & (ri > ci), same & (ri < ci))
    incl_b = tuple(m.astype(BF16) for m in incl)
    same_b = same.astype(BF16)
    lane = lax.broadcasted_iota(jnp.int32, (sup, LANES), 1)
    rowc = lax.broadcasted_iota(jnp.int32, (sup, LANES), 0)

    def conv_silu(z_ref, w_ref, r0, ls):
        z = z_ref[0, pl.ds(r0, sup), :]
        if isinstance(r0, int):
            zero = jnp.zeros((GDN_HALO, LANES), BF16)
            prev = z_ref[0, r0 - GDN_HALO:r0, :] if r0 > 0 else zero
            nxt = z_ref[0, r0 + sup:r0 + sup + GDN_HALO, :] if r0 + sup < ls else zero
        else:
            p0 = pl.multiple_of(jnp.maximum(r0 - GDN_HALO, 0), GDN_HALO)
            n0 = pl.multiple_of(jnp.minimum(r0 + sup, ls - GDN_HALO), GDN_HALO)
            prev = jnp.where(r0 > 0, z_ref[0, pl.ds(p0, GDN_HALO), :], jnp.zeros((), BF16))
            nxt = jnp.where(r0 + sup < ls, z_ref[0, pl.ds(n0, GDN_HALO), :], jnp.zeros((), BF16))
        win = jnp.concatenate([prev, z, nxt], axis=0)
        z_dn = jnp.dot(pd_ref[...], win, preferred_element_type=F32)
        z_up = jnp.dot(pu_ref[...], win, preferred_element_type=F32)
        y = z_dn * w_ref[0, 0:1, :] + z.astype(F32) * w_ref[0, 1:2, :] + z_up * w_ref[0, 2:3, :]
        return y * jax.nn.sigmoid(y)

    def prep(units):
        common = []
        for refs, r0, base, ls in units:
            q_ref, k_ref, v_ref, ab_ref = refs
            q = conv_silu(q_ref, cwq_ref, r0, ls)
            k = conv_silu(k_ref, cwk_ref, r0, ls)
            v = conv_silu(v_ref, cwv_ref, r0, ls)
            q = q * lax.rsqrt(jnp.sum(q * q, axis=-1, keepdims=True) + EPS) * GDN_DK ** -0.5
            k = k * lax.rsqrt(jnp.sum(k * k, axis=-1, keepdims=True) + EPS)
            kb16 = k.astype(BF16)
            qk = lax.dot_general(q.astype(BF16), kb16, (((1,), (1,)), ((), ())), preferred_element_type=F32)
            ab = ab_ref[0, pl.ds(r0, sup), :]
            common.append((q, k, v, kb16, qk, ab, base + r0))
        chains = []
        for q, k, v, kb16, qk, ab, row0 in common:
            for d in range(2):
                ia = d * GDN_HEADS + hd
                ib = 2 * GDN_HEADS + ia
                a_col = jnp.sum(jnp.where(lane == ia, ab, 0.0), axis=1, keepdims=True)
                b_col = jnp.sum(jnp.where(lane == ib, ab, 0.0), axis=1, keepdims=True)
                g_col = nega_ref[d, hd] * _softplus(a_col + dtb_ref[d, hd])
                beta = jax.nn.sigmoid(b_col)
                gcum = _mask_dot(incl_b[d], jnp.broadcast_to(g_col, (sup, LANES)))
                gtot = _mask_dot(same_b, jnp.broadcast_to(g_col, (sup, LANES)))
                gc = gcum[:, 0:1]
                decay = jnp.where(incl[d], jnp.exp(jnp.where(incl[d], gcum - gcum.T, 0.0)), 0.0)
                kbeta = k * beta
                kk = lax.dot_general(kbeta.astype(BF16), kb16, (((1,), (1,)), ((), ())),
                                     preferred_element_type=F32)
                a_mat = jnp.where(strict[d], kk * decay, 0.0)
                eg = jnp.exp(gc)
                rhs = jnp.concatenate([kbeta * eg, v * beta], axis=1).astype(BF16)
                aqk = jnp.where(incl[d], qk * decay, 0.0).astype(BF16)
                ke = (k * jnp.exp(gtot[:, 0:1] - gc)).astype(BF16)
                cds = [jnp.exp(gtot[j * c:j * c + 1, :]) for j in range(per)]
                chains.append(dict(d=d, row0=row0, a=a_mat, rhs=rhs, aqk=aqk, ke=ke, cds=cds, qeg=q * eg))
        for ch in chains:
            ch['inv'] = eye - ch['a']
            ch['pw'] = ch['a'].astype(BF16)
        for _ in range(5):
            for ch in chains:
                ch['pw'] = jnp.dot(ch['pw'], ch['pw'], preferred_element_type=F32).astype(BF16)
            for ch in chains:
                ch['inv'] = ch['inv'] + jnp.dot(ch['inv'].astype(BF16), ch['pw'], preferred_element_type=F32)
        for ch in chains:
            ch['wu'] = jnp.dot(ch['inv'].astype(BF16), ch['rhs'], preferred_element_type=F32).astype(BF16)
        for ch in chains:
            d, row0 = ch['d'], ch['row0']
            awu = jnp.dot(ch['aqk'], ch['wu'], preferred_element_type=F32)
            rows = pl.ds(row0, sup)
            qp_s[d, rows, :] = (ch['qeg'] - awu[:, :GDN_DK]).astype(BF16)
            o0_s[d, rows, :] = awu[:, GDN_DK:]
            for j in range(per):
                kej = jnp.where((rowc // c) == j, ch['ke'], jnp.zeros((), BF16))
                idx = row0 // c + j
                kn_s[d, idx] = lax.dot_general(kej, ch['wu'], (((0,), (0,)), ((), ())),
                                               preferred_element_type=F32).astype(BF16)
                cd_s[d, idx] = ch['cds'][j]

    prep([((q_c, k_c, v_c, ab_c), n * sup, 0, lc) for n in range(lc // sup)])

    def prep_body(n, carry):
        prep([((q_l, k_l, v_l, ab_l), pl.multiple_of((GDN_UNITS * n + j) * sup, sup), lc, l)
              for j in range(GDN_UNITS)])
        return carry
    lax.fori_loop(0, l // (GDN_UNITS * sup), prep_body, 0)

    def chunk_step(d, idx, s):
        sb = s.astype(BF16)
        sin_s[d, idx] = sb
        kn = kn_s[d, idx]
        return (s * cd_s[d, idx] - jnp.dot(kn[:, :GDN_DK], sb, preferred_element_type=F32)
                + kn[:, GDN_DK:].astype(F32))

    def scan_body(t, carry):
        sf, sb = carry
        tb = jnp.where(t < ncc, ncc - 1 - t, nc - 1 - (t - ncc))
        return chunk_step(0, t, sf), chunk_step(1, tb, sb)
    zero = jnp.zeros((GDN_DK, GDN_DV), F32)
    lax.fori_loop(0, nc, scan_body, (zero, zero))

    def finish(y_ref, gate_ref, base, n_rows):
        def body(n, carry):
            r0 = pl.multiple_of(n * c, c)
            rows = pl.ds(base + r0, c)
            idx = base // c + n
            o = (o0_s[0, rows, :] + o0_s[1, rows, :]
                 + jnp.dot(qp_s[0, rows, :], sin_s[0, idx], preferred_element_type=F32)
                 + jnp.dot(qp_s[1, rows, :], sin_s[1, idx], preferred_element_type=F32))
            y = o * lax.rsqrt(jnp.mean(o * o, axis=-1, keepdims=True) + EPS) * gain_ref[...]
            gf = gate_ref[0, pl.ds(r0, c), :].astype(F32)
            y_ref[0, pl.ds(r0, c), :] = (y * gf * jax.nn.sigmoid(gf)).astype(y_ref.dtype)
            return carry
        lax.fori_loop(0, n_rows // c, body, 0, unroll=4)

    finish(yl_ref, g_l, lc, l)
    if ctx_out:
        finish(yc_ref, g_c, 0, lc)
    else:
        yc_ref[...] = jnp.zeros_like(yc_ref)


def _gdn_branch(zg_l, zab_l, zg_c, zab_c, conv_w, a_log, dt_bias, norm_gain, ctx_out):
    b, l, _ = zg_l.shape
    lc = zg_c.shape[1]
    h = GDN_HEADS
    sup = GDN_SUPER
    ltot = l + lc
    nc = ltot // GDN_CHUNK
    neg_a = -jnp.exp(a_log.astype(F32))
    cw = conv_w.astype(F32).T.reshape(3 * h, LANES, SHORT_CONV).transpose(0, 2, 1)
    win = sup + 2 * GDN_HALO
    i = np.arange(sup)
    pd = np.zeros((sup, win), np.float32)
    pu = np.zeros((sup, win), np.float32)
    pd[i, GDN_HALO + i - 1] = 1.0
    pu[i, GDN_HALO + i + 1] = 1.0
    seq = lambda n, j: pl.BlockSpec((1, n, LANES), lambda bi, hh: (bi, 0, j * h + hh))
    abs_ = lambda n: pl.BlockSpec((1, n, LANES), lambda bi, hh: (bi, 0, 0))
    cws = lambda j: pl.BlockSpec((1, SHORT_CONV, LANES), lambda bi, hh: (j * h + hh, 0, 0))
    full = lambda s: pl.BlockSpec(s, lambda bi, hh: (0, 0))
    smem = pl.BlockSpec(memory_space=pltpu.SMEM)
    yl, yc = pl.pallas_call(
        functools.partial(_gdn_kernel, ctx_out=ctx_out),
        grid=(b, h),
        in_specs=[smem, smem, seq(l, 0), seq(l, 1), seq(l, 2), seq(l, 3), abs_(l),
                  seq(lc, 0), seq(lc, 1), seq(lc, 2), seq(lc, 3), abs_(lc),
                  cws(0), cws(1), cws(2), full((sup, win)), full((sup, win)), full((1, GDN_DV))],
        out_specs=[pl.BlockSpec((1, l, GDN_DV), lambda bi, hh: (bi, 0, hh)),
                   pl.BlockSpec((1, lc, GDN_DV), lambda bi, hh: (bi, 0, hh))],
        out_shape=[jax.ShapeDtypeStruct((b, l, h * GDN_DV), BF16), jax.ShapeDtypeStruct((b, lc, h * GDN_DV), BF16)],
        scratch_shapes=[pltpu.VMEM((2, nc, GDN_DK, GDN_DK + GDN_DV), BF16), pltpu.VMEM((2, nc, GDN_DK, GDN_DV), BF16),
                        pltpu.VMEM((2, ltot, GDN_DK), BF16), pltpu.VMEM((2, ltot, GDN_DV), F32),
                        pltpu.VMEM((2, nc, 1, GDN_DV), F32)],
        compiler_params=_cparams(2),
        name="gated_deltanet",
    )(neg_a, dt_bias.astype(F32), zg_l, zg_l, zg_l, zg_l, zab_l, zg_c, zg_c, zg_c, zg_c, zab_c,
      cw, cw, cw, jnp.asarray(pd, BF16), jnp.asarray(pu, BF16), norm_gain.astype(F32).reshape(1, GDN_DV))
    return (yc if ctx_out else None), yl


def kernel(x, c, ctx, c_ctx, w_mod, b_mod, norm1, norm2, w_in, ret_decay, ret_gn, win_qnorm, win_knorm, win_sink,
           na_qnorm, na_knorm, na_rpb, gdn_conv, gdn_a_log, gdn_dt_bias, gdn_norm, w_branch, w_merge, w_out,
           w_router, router_bias, w_e_gate, w_e_up, w_e_down):
    b, l, d = x.shape
    lc = ctx.shape[1]
    depth = w_mod.shape[0]
    cos2, sin2 = _rope_tables(l)

    n_rows = 16
    cc = jnp.zeros((n_rows, d), F32).at[:b].set(c).at[b].set(c_ctx)
    mod = _modulation(cc, w_mod, b_mod).reshape(depth, n_rows, 6, d)

    wr = jnp.zeros((d, LANES), F32).at[:, :N_EXPERTS].set(w_router)
    wr_hi = wr.astype(BF16)
    wr_lo = (wr - wr_hi.astype(F32)).astype(BF16)

    xl, xc = x, ctx
    for layer in range(depth):
        ctx_out = layer < depth - 1
        mod_l = mod[layer, :b]
        mod_c = mod[layer, b:b + 1]
        w_all = _pack_w_in(w_in[layer])
        hl, *zl_s = _inproj(xl, mod_l, False, norm1[layer], w_all, min(l, TOKEN_TILE))
        hc, *zc_s = _inproj(xc, mod_c, True, norm1[layer], w_all, min(lc, TOKEN_TILE))
        zr_l, zw_l, zn_l, zg_l, zab_l = zl_s
        zr_c, zw_c, zn_c, zg_c, zab_c = zc_s
        ret_c, ret_l = _retention_branch(zr_l, zr_c, ret_decay[layer], ret_gn[layer], cos2, sin2, ctx_out)
        win_c, win_l = _window_branch(zw_l, zw_c, win_qnorm[layer], win_knorm[layer], win_sink[layer], cos2, sin2,
                                      ctx_out)
        na_c, na_l = _neighborhood_branch(zn_l, zn_c, na_qnorm[layer], na_knorm[layer], na_rpb[layer], ctx_out)
        gdn_c, gdn_l = _gdn_branch(zg_l, zab_l, zg_c, zab_c, gdn_conv[layer], gdn_a_log[layer], gdn_dt_bias[layer],
                                   gdn_norm[layer], ctx_out)
        wm = w_merge[layer].astype(BF16)
        wb = w_branch[layer].astype(BF16)
        wo = w_out[layer].astype(BF16)
        ys_l = [ret_l, win_l, na_l, gdn_l]
        xl, h2l, sc_l = _merge(xl, hl, ys_l, mod_l, False, norm2[layer], wm, wb, wo, wr_hi, wr_lo,
                               min(l, TOKEN_TILE))
        if ctx_out:
            ys_c = [ret_c, win_c, na_c, gdn_c]
            xc, h2c, sc_c = _merge(xc, hc, ys_c, mod_c, True, norm2[layer], wm, wb, wo, wr_hi, wr_lo,
                                   min(lc, TOKEN_TILE))
            tokens = jnp.concatenate([h2c.reshape(b * lc, d), h2l.reshape(b * l, d)], axis=0)
            scores = jnp.concatenate([sc_c.reshape(b * lc, LANES), sc_l.reshape(b * l, LANES)], axis=0)
            y, wts = _moe(tokens, scores[:, :N_EXPERTS], router_bias, w_e_gate[layer], w_e_up[layer], w_e_down[layer])
            xc = _combine(xc, y, wts, mod_c, True, 0, 256)
            xl = _combine(xl, y, wts, mod_l, False, b * lc, 256)
        else:
            y, wts = _moe(h2l.reshape(b * l, d), sc_l.reshape(b * l, LANES)[:, :N_EXPERTS], router_bias,
                          w_e_gate[layer], w_e_up[layer], w_e_down[layer])
            xl = _combine(xl, y, wts, mod_l, False, 0, 256)
    return xl
```

```python
import functools

import numpy as np
import jax
import jax.numpy as jnp
from jax import lax
from jax.experimental import pallas as pl
from jax.experimental.pallas import tpu as pltpu

F32 = jnp.float32
BF16 = jnp.bfloat16
EPS = 1e-6
NEG_INF = -1e30
D_MODEL = 1024
GRID_W = 64
HEAD_DIM = 64
ROPE_THETA = 10000.0
RET_HEADS, RET_DK, RET_DV, RET_CHUNK = 4, 64, 128, 128
WIN_HEADS, WIN_KV_HEADS, WINDOW, WIN_BLOCK = 8, 2, 128, 128
NA_HEADS, NA_KH, NA_KW, NA_QCOLS = 8, 8, 16, 16
NA_BAND = NA_QCOLS + NA_KW
GDN_HEADS, GDN_DK, GDN_DV, GDN_CHUNK, SHORT_CONV = 4, 128, 128, 64, 3
GDN_QKV = 2 * GDN_HEADS * GDN_DK + GDN_HEADS * GDN_DV
N_BRANCH, BRANCH_W = 4, 512
N_EXPERTS, N_GROUPS, TOP_K, D_EXPERT = 32, 8, 2, 512
EXPERTS_PER_GROUP = N_EXPERTS // N_GROUPS

LANES = 128
VMEM_LIMIT = 56 * 1024 * 1024
TOKEN_TILE = 512
MERGE_TILE = 256
MOE_ROWS = 256
MOE_ROW_ALIGN = 8

W_RET = RET_HEADS * 4 * LANES
W_WIN = (WIN_HEADS + 2 * WIN_KV_HEADS) * HEAD_DIM
W_NA = 3 * NA_HEADS * HEAD_DIM
W_GDN = GDN_QKV + GDN_HEADS * GDN_DV
W_AB = LANES
SECTION_WIDTHS = (W_RET, W_WIN, W_NA, W_GDN, W_AB)
W_ALL = sum(SECTION_WIDTHS)


def _cparams(n_axes):
    return pltpu.CompilerParams(dimension_semantics=("arbitrary",) * n_axes, vmem_limit_bytes=VMEM_LIMIT)


def _mod_kernel(c_ref, w_ref, b_ref, o_ref):
    c = c_ref[...]
    a = (c * jax.nn.sigmoid(c)).astype(BF16)
    o_ref[0] = jnp.dot(a, w_ref[0].astype(BF16), preferred_element_type=F32) + b_ref[0]


def _modulation(cc, w_mod, b_mod):
    depth, d, n = w_mod.shape
    r = cc.shape[0]
    tn = 1536
    return pl.pallas_call(
        _mod_kernel,
        grid=(depth, n // tn),
        in_specs=[pl.BlockSpec((r, d), lambda l, j: (0, 0)),
                  pl.BlockSpec((1, d, tn), lambda l, j: (l, 0, j)),
                  pl.BlockSpec((1, 1, tn), lambda l, j: (l, 0, j))],
        out_specs=pl.BlockSpec((1, r, tn), lambda l, j: (l, 0, j)),
        out_shape=jax.ShapeDtypeStruct((depth, r, n), F32),
        compiler_params=_cparams(2),
        name="modulation",
    )(cc, w_mod, b_mod.reshape(depth, 1, n))


def _inproj_kernel(x_ref, mod_ref, gain_ref, w_ref, h_ref, *z_refs):
    x = x_ref[0]
    ms = jnp.mean(x * x, axis=-1, keepdims=True)
    shift = mod_ref[0, 0:1, :]
    scale = mod_ref[0, 1:2, :]
    h = x * lax.rsqrt(ms + EPS) * gain_ref[...] * (1.0 + scale) + shift
    hb = h.astype(BF16)
    h_ref[0] = hb
    off = 0
    for ref in z_refs:
        width = ref.shape[-1]
        for c0 in range(0, width, 512):
            c1 = min(c0 + 512, width)
            z = jnp.dot(hb, w_ref[:, off + c0:off + c1], preferred_element_type=F32)
            ref[0, :, c0:c1] = z.astype(ref.dtype)
        off += width


def _inproj(x, mod, mod_is_shared, gain, w_all, tm):
    b, l, d = x.shape
    mod_map = (lambda i, j: (0, 0, 0)) if mod_is_shared else (lambda i, j: (i, 0, 0))
    dtypes = (BF16, BF16, BF16, BF16, F32)
    out_shape = [jax.ShapeDtypeStruct((b, l, d), BF16)]
    out_specs = [pl.BlockSpec((1, tm, d), lambda i, j: (i, j, 0))]
    for w, dt in zip(SECTION_WIDTHS, dtypes):
        out_shape.append(jax.ShapeDtypeStruct((b, l, w), dt))
        out_specs.append(pl.BlockSpec((1, tm, w), lambda i, j: (i, j, 0)))
    return pl.pallas_call(
        _inproj_kernel,
        grid=(b, l // tm),
        in_specs=[pl.BlockSpec((1, tm, d), lambda i, j: (i, j, 0)),
                  pl.BlockSpec((1, 6, d), mod_map),
                  pl.BlockSpec((1, d), lambda i, j: (0, 0)),
                  pl.BlockSpec((d, W_ALL), lambda i, j: (0, 0), pipeline_mode=pl.Buffered(1))],
        out_specs=out_specs,
        out_shape=out_shape,
        compiler_params=_cparams(2),
        name="inproj",
    )(x, mod, gain.reshape(1, d), w_all)


def _pack_w_in(w_in):
    d = w_in.shape[0]
    hq, hv = RET_HEADS * RET_DK, RET_HEADS * RET_DV
    cols = []
    for h in range(RET_HEADS):
        q = w_in[:, h * RET_DK:(h + 1) * RET_DK]
        k = w_in[:, hq + h * RET_DK:hq + (h + 1) * RET_DK]
        cols += [q, q, k, k, w_in[:, 2 * hq + h * RET_DV:2 * hq + (h + 1) * RET_DV],
                 w_in[:, 2 * hq + hv + h * RET_DV:2 * hq + hv + (h + 1) * RET_DV]]
    rest = w_in[:, 2 * hq + 2 * hv:]
    pad = jnp.zeros((d, W_ALL - W_RET - rest.shape[1]), w_in.dtype)
    return jnp.concatenate(cols + [rest, pad], axis=1).astype(BF16)


def _merge_kernel(x_ref, h_ref, y0_ref, y1_ref, y2_ref, y3_ref, mod_ref, gain_ref, wm_ref, wb_ref, wo_ref,
                  wrh_ref, wrl_ref, xo_ref, h2_ref, sc_ref):
    d = x_ref.shape[-1]
    h = h_ref[0]
    acc = jnp.zeros(x_ref.shape[1:], F32)
    for i, y_ref in enumerate((y0_ref, y1_ref, y2_ref, y3_ref)):
        gate = jax.nn.sigmoid(jnp.dot(h, wm_ref[:, i * d:(i + 1) * d], preferred_element_type=F32))
        acc = acc + gate * jnp.dot(y_ref[0], wb_ref[i], preferred_element_type=F32)
    m = jnp.dot(acc.astype(BF16), wo_ref[...], preferred_element_type=F32)
    xn = x_ref[0] + mod_ref[0, 2:3, :] * m
    xo_ref[0] = xn
    ms = jnp.mean(xn * xn, axis=-1, keepdims=True)
    h2 = xn * lax.rsqrt(ms + EPS) * gain_ref[...] * (1.0 + mod_ref[0, 4:5, :]) + mod_ref[0, 3:4, :]
    hi = h2.astype(BF16)
    h2_ref[0] = h2
    lo = (h2 - hi.astype(F32)).astype(BF16)
    logits = (jnp.dot(hi, wrh_ref[...], preferred_element_type=F32)
              + jnp.dot(lo, wrh_ref[...], preferred_element_type=F32)
              + jnp.dot(hi, wrl_ref[...], preferred_element_type=F32))
    sc_ref[0] = jax.nn.sigmoid(logits)


def _merge(x, h, ys, mod, mod_is_shared, gain2, wm, wb, wo, wr_hi, wr_lo, tm):
    b, l, d = x.shape
    mod_map = (lambda i, j: (0, 0, 0)) if mod_is_shared else (lambda i, j: (i, 0, 0))
    tok = lambda w: pl.BlockSpec((1, tm, w), lambda i, j: (i, j, 0))
    full2 = lambda s: pl.BlockSpec(s, lambda i, j: (0, 0), pipeline_mode=pl.Buffered(1))
    return pl.pallas_call(
        _merge_kernel,
        grid=(b, l // tm),
        in_specs=[tok(d), tok(d)] + [tok(BRANCH_W)] * 4 + [
            pl.BlockSpec((1, 6, d), mod_map), full2((1, d)), full2(wm.shape),
            pl.BlockSpec(wb.shape, lambda i, j: (0, 0, 0), pipeline_mode=pl.Buffered(1)), full2(wo.shape),
            full2(wr_hi.shape), full2(wr_lo.shape)],
        out_specs=[tok(d), tok(d), tok(LANES)],
        out_shape=[jax.ShapeDtypeStruct((b, l, d), F32), jax.ShapeDtypeStruct((b, l, d), F32),
                   jax.ShapeDtypeStruct((b, l, LANES), F32)],
        compiler_params=_cparams(2),
        name="merge",
    )(x, h, *ys, mod, gain2.reshape(1, d), wm, wb, wo, wr_hi, wr_lo)


def _moe_kernel(be_ref, nv_ref, src_ref, dst_ref, h_hbm, wg_ref, wu_ref, wd_ref, y_hbm, xbuf, obuf, wg_s, wu_s, wd_s,
                gsem, ssem, *, n_real):
    i = pl.program_id(0)
    n = pl.num_programs(0)
    slot = i % 2
    nv = nv_ref[i]

    def start_gather(blk, dst_slot):
        def body(grp, carry):
            base = blk * MOE_ROWS + grp * MOE_ROW_ALIGN
            for j in range(MOE_ROW_ALIGN):
                pltpu.make_async_copy(h_hbm.at[pl.ds(src_ref[base + j], 1)], xbuf.at[dst_slot, grp, pl.ds(j, 1)],
                                      gsem.at[dst_slot]).start()
            return carry
        lax.fori_loop(0, nv_ref[blk] // MOE_ROW_ALIGN, body, 0)

    def wait_rows(buf, sem, cnt):
        grps = buf.at[pl.ds(0, cnt // MOE_ROW_ALIGN)]
        pltpu.make_async_copy(grps, grps, sem).wait()

    @pl.when(i == 0)
    def _():
        xbuf[...] = jnp.zeros_like(xbuf)
        fills = [pltpu.make_async_copy(xbuf.at[1, g], y_hbm.at[pl.ds(n_real + g * MOE_ROW_ALIGN, MOE_ROW_ALIGN)],
                                       ssem.at[1])
                 for g in range((y_hbm.shape[0] - n_real) // MOE_ROW_ALIGN)]
        for f in fills:
            f.start()
        for f in fills:
            f.wait()
        start_gather(0, 0)

    @pl.when(i + 1 < n)
    def _():
        start_gather(i + 1, 1 - slot)

    @pl.when((i >= 2) & (nv_ref[jnp.maximum(i - 2, 0)] > 0))
    def _():
        wait_rows(obuf.at[slot], ssem.at[slot], nv_ref[jnp.maximum(i - 2, 0)])

    @pl.when(nv > 0)
    def _():
        wait_rows(xbuf.at[slot], gsem.at[slot], nv)
        e = be_ref[i]

        @pl.when((i == 0) | (e != be_ref[jnp.maximum(i - 1, 0)]))
        def _():
            wg_s[...] = wg_ref[0].astype(BF16)
            wu_s[...] = wu_ref[0].astype(BF16)
            wd_s[...] = wd_ref[0].astype(BF16)

        d = xbuf.shape[-1]
        x = xbuf[slot].reshape(MOE_ROWS, d).astype(BF16)
        g = jnp.dot(x, wg_s[...], preferred_element_type=F32)
        u = jnp.dot(x, wu_s[...], preferred_element_type=F32)
        a = (g * jax.nn.sigmoid(g) * u).astype(BF16)
        obuf[slot] = jnp.dot(a, wd_s[...], preferred_element_type=F32).reshape(obuf.shape[1:])

        def body(grp, carry):
            base = i * MOE_ROWS + grp * MOE_ROW_ALIGN
            for j in range(MOE_ROW_ALIGN):
                pltpu.make_async_copy(obuf.at[slot, grp, pl.ds(j, 1)], y_hbm.at[pl.ds(dst_ref[base + j], 1)],
                                      ssem.at[slot]).start()
            return carry
        lax.fori_loop(0, nv // MOE_ROW_ALIGN, body, 0)

    @pl.when(i == n - 1)
    def _():
        @pl.when(nv > 0)
        def _():
            wait_rows(obuf.at[slot], ssem.at[slot], nv)

        @pl.when((n >= 2) & (nv_ref[jnp.maximum(i - 1, 0)] > 0))
        def _():
            wait_rows(obuf.at[1 - slot], ssem.at[1 - slot], nv_ref[jnp.maximum(i - 1, 0)])


def _moe_ffn(block_e, n_valid, src, dst, h2, w_gate, w_up, w_down):
    t, d = h2.shape
    grp_shape = (2, MOE_ROWS // MOE_ROW_ALIGN, MOE_ROW_ALIGN, d)
    n_blocks = block_e.shape[0]
    de = w_gate.shape[-1]
    wspec = lambda s: pl.BlockSpec((1,) + s, lambda i, be, nv, src, dst: (be[i], 0, 0))
    grid_spec = pltpu.PrefetchScalarGridSpec(
        num_scalar_prefetch=4,
        grid=(n_blocks,),
        in_specs=[pl.BlockSpec(memory_space=pl.ANY), wspec((d, de)), wspec((d, de)), wspec((de, d))],
        out_specs=pl.BlockSpec(memory_space=pl.ANY),
        scratch_shapes=[pltpu.VMEM(grp_shape, F32), pltpu.VMEM(grp_shape, F32),
                        pltpu.VMEM((d, de), BF16), pltpu.VMEM((d, de), BF16), pltpu.VMEM((de, d), BF16),
                        pltpu.SemaphoreType.DMA((2,)), pltpu.SemaphoreType.DMA((2,))],
    )
    return pl.pallas_call(
        functools.partial(_moe_kernel, n_real=TOP_K * t),
        grid_spec=grid_spec,
        out_shape=jax.ShapeDtypeStruct((TOP_K * t + MOE_ROW_ALIGN * N_EXPERTS, d), F32),
        compiler_params=pltpu.CompilerParams(dimension_semantics=("arbitrary",), vmem_limit_bytes=VMEM_LIMIT,
                                             has_side_effects=True),
        name="moe_ffn",
    )(block_e, n_valid, src, dst, h2, w_gate, w_up, w_down)


def _route(scores, router_bias):
    t = scores.shape[0]
    sel = (scores + router_bias.astype(F32)).reshape(t, N_GROUPS, EXPERTS_PER_GROUP)
    pairs = [sel[..., i] + sel[..., j] for i in range(EXPERTS_PER_GROUP) for j in range(i + 1, EXPERTS_PER_GROUP)]
    grp_score = functools.reduce(jnp.maximum, pairs)
    g_idx = jnp.argmax(grp_score, axis=-1).astype(jnp.int32)
    g_hot = (g_idx[:, None] == jnp.arange(N_GROUPS, dtype=jnp.int32)[None, :])[:, :, None]
    in_grp = jnp.sum(jnp.where(g_hot, sel, 0.0), axis=1)
    sc_grp = jnp.sum(jnp.where(g_hot, scores.reshape(t, N_GROUPS, EXPERTS_PER_GROUP), 0.0), axis=1)
    lane4 = jnp.arange(EXPERTS_PER_GROUP, dtype=jnp.int32)[None, :]
    i1 = jnp.argmax(in_grp, axis=-1).astype(jnp.int32)
    i2 = jnp.argmax(jnp.where(lane4 == i1[:, None], -jnp.inf, in_grp), axis=-1).astype(jnp.int32)
    e_idx = g_idx[:, None] * EXPERTS_PER_GROUP + jnp.stack([i1, i2], axis=-1)
    wts = jnp.stack([jnp.sum(jnp.where(lane4 == i1[:, None], sc_grp, 0.0), axis=-1),
                     jnp.sum(jnp.where(lane4 == i2[:, None], sc_grp, 0.0), axis=-1)], axis=-1)
    return e_idx, wts / jnp.sum(wts, axis=-1, keepdims=True)


def _moe(h2, scores, router_bias, w_gate, w_up, w_down):
    t, d = h2.shape
    e_idx, wts = _route(scores, router_bias)
    a = t * TOP_K
    flat_e = e_idx.T.reshape(-1)
    onehot = (flat_e[:, None] == jnp.arange(N_EXPERTS, dtype=jnp.int32)[None, :]).astype(jnp.int32)
    rb = MOE_ROWS
    blocks = onehot.reshape(a // rb, rb, N_EXPERTS).astype(BF16)
    within = jnp.einsum('ij,bje->bie', jnp.tril(jnp.ones((rb, rb), BF16)), blocks, preferred_element_type=F32)
    before = jnp.cumsum(within[:, -1, :], axis=0) - within[:, -1, :]
    csum = (within + before[:, None, :]).astype(jnp.int32).reshape(a, N_EXPERTS)
    counts = csum[-1]
    rank = jnp.sum(csum * onehot, axis=1) - 1
    padded = (counts + MOE_ROWS - 1) // MOE_ROWS * MOE_ROWS
    ends = jnp.cumsum(padded)
    pstarts = ends - padded
    dest = jnp.sum(pstarts[None, :] * onehot, axis=1) + rank
    n_blocks = -(-a // MOE_ROWS) + N_EXPERTS
    blk0 = jnp.arange(n_blocks, dtype=jnp.int32) * MOE_ROWS
    block_e = jnp.minimum(jnp.sum((ends[None, :] <= blk0[:, None]).astype(jnp.int32), axis=1), N_EXPERTS - 1)
    n_valid = jnp.clip((pstarts + counts)[block_e] - blk0, 0, MOE_ROWS)
    n_valid = ((n_valid + MOE_ROW_ALIGN - 1) // MOE_ROW_ALIGN * MOE_ROW_ALIGN).astype(jnp.int32)
    slot = jnp.arange(n_blocks * MOE_ROWS, dtype=jnp.int32)
    spare = a + MOE_ROW_ALIGN * jnp.repeat(block_e, MOE_ROWS) + slot % MOE_ROW_ALIGN
    dst = spare.at[dest].set(jnp.arange(a, dtype=jnp.int32))
    src = jnp.where(dst < a, dst % t, 0)
    return _moe_ffn(block_e, n_valid, src, dst, h2, w_gate, w_up, w_down), wts


def _combine_kernel(x_ref, y0_ref, y1_ref, w_ref, mod_ref, o_ref):
    w = w_ref[...]
    y = y0_ref[...] * w[:, 0:1] + y1_ref[...] * w[:, 1:2]
    o_ref[0] = x_ref[0] + mod_ref[0, 5:6, :] * y


def _combine(x, y, wts, mod, mod_is_shared, row_offset, tm):
    b, l, d = x.shape
    mod_map = (lambda i, j: (0, 0, 0)) if mod_is_shared else (lambda i, j: (i, 0, 0))
    off = row_offset // tm
    per = l // tm
    return pl.pallas_call(
        _combine_kernel,
        grid=(b, per),
        in_specs=[pl.BlockSpec((1, tm, d), lambda i, j: (i, j, 0)),
                  pl.BlockSpec((tm, d), lambda i, j: (off + i * per + j, 0)),
                  pl.BlockSpec((tm, d), lambda i, j: (wts.shape[0] // tm + off + i * per + j, 0)),
                  pl.BlockSpec((tm, TOP_K), lambda i, j: (off + i * per + j, 0)),
                  pl.BlockSpec((1, 6, d), mod_map)],
        out_specs=pl.BlockSpec((1, tm, d), lambda i, j: (i, j, 0)),
        out_shape=jax.ShapeDtypeStruct((b, l, d), F32),
        compiler_params=_cparams(2),
        name="moe_combine",
    )(x, y, y, wts, mod)


def _head_rms(x, gain, e_ref):
    ms = jnp.dot((x * x).astype(BF16), e_ref[...], preferred_element_type=F32) * (1.0 / HEAD_DIM)
    return x * lax.rsqrt(ms + EPS) * gain


def _rope(y, cos, sin):
    lane = lax.broadcasted_iota(jnp.int32, y.shape, 1)
    half = HEAD_DIM // 2
    rot = jnp.where((lane % HEAD_DIM) < half, pltpu.roll(y, LANES - half, 1), pltpu.roll(y, half, 1))
    return y * cos + rot * sin


def _attend(problems):
    all_scores = []
    for q, tiles, _ in problems:
        scores = []
        for k, _, bias in tiles:
            s = lax.dot_general(q, k, (((1,), (1,)), ((), ())), preferred_element_type=F32)
            scores.append(s if bias is None else s + bias)
        if all(s.shape[1] % LANES == 0 for s in scores):
            scores = [jnp.concatenate(scores, axis=1)]
        all_scores.append(scores)
    maxes = []
    for scores, (_, _, sink) in zip(all_scores, problems):
        m = sink
        for s in scores:
            mt = jnp.max(s, axis=-1, keepdims=True)
            m = mt if m is None else jnp.maximum(m, mt)
        maxes.append(m)
    all_probs, dens = [], []
    for scores, m, (_, tiles, sink) in zip(all_scores, maxes, problems):
        den = jnp.exp(sink - m) if sink is not None else jnp.zeros_like(m)
        probs = []
        for s in scores:
            p = jnp.exp(s - m)
            den = den + jnp.sum(p, axis=-1, keepdims=True)
            probs.append(p.astype(BF16))
        if len(probs) != len(tiles):
            offs = np.cumsum([0] + [v.shape[0] for _, v, _ in tiles])
            probs = [probs[0][:, offs[i]:offs[i + 1]] for i in range(len(tiles))]
        all_probs.append(probs)
        dens.append(den)
    outs = []
    for probs, den, (_, tiles, _) in zip(all_probs, dens, problems):
        o = None
        for p, (_, v, _) in zip(probs, tiles):
            pv = jnp.dot(p, v, preferred_element_type=F32)
            o = pv if o is None else o + pv
        outs.append(o * (1.0 / den))
    return outs


def _lane_half(shape):
    return lax.broadcasted_iota(jnp.int32, shape, 1) // HEAD_DIM


def _win_group_queries(slabs, g, sink_ref):
    r = WIN_HEADS // WIN_KV_HEADS
    rows = slabs[0].shape[0]
    half = _lane_half(slabs[0].shape)
    parts, sinks = [], []
    for j in range(r):
        head = g * r + j
        slab = slabs[head // 2]
        if head % 2 != g:
            slab = pltpu.roll(slab, HEAD_DIM, 1)
        parts.append(jnp.where(half == g, slab, 0.0))
        sinks.append(jnp.full((rows, 1), sink_ref[head], F32))
    return jnp.concatenate(parts, axis=0).astype(BF16), jnp.concatenate(sinks, axis=0)


def _win_store(o_ref, o, g, rows):
    r = WIN_HEADS // WIN_KV_HEADS
    half = _lane_half((rows, LANES))
    for pair in range(r // 2):
        a = o[(2 * pair) * rows:(2 * pair + 1) * rows]
        b = o[(2 * pair + 1) * rows:(2 * pair + 2) * rows]
        if g == 0:
            b = pltpu.roll(b, HEAD_DIM, 1)
        else:
            a = pltpu.roll(a, HEAD_DIM, 1)
        s = (g * r + 2 * pair) // 2
        o_ref[0, :, s * LANES:(s + 1) * LANES] = jnp.where(half == 0, a, b).astype(o_ref.dtype)


def _win_kernel(sink_ref, q_ref, kv_ref, ckv_ref, cos_ref, sin_ref, qg_ref, kg_ref, e_ref, o_ref, ks_ref, kcs_ref):
    n = pl.program_id(1)
    nb = pl.num_programs(1)
    l = kv_ref.shape[1]
    blk = WIN_BLOCK
    prep_rows = 512

    @pl.when(n == 0)
    def _():
        def body(i, carry):
            r0 = pl.multiple_of(i * prep_rows, prep_rows)
            y = _head_rms(kv_ref[0, pl.ds(r0, prep_rows), 0:LANES].astype(F32), kg_ref[...], e_ref)
            ks_ref[pl.ds(r0, prep_rows), :] = _rope(y, cos_ref[pl.ds(r0, prep_rows), :],
                                                    sin_ref[pl.ds(r0, prep_rows), :]).astype(BF16)
            return carry
        lax.fori_loop(0, l // prep_rows, body, 0)
        kcs_ref[...] = _head_rms(ckv_ref[0, :, 0:LANES].astype(F32), kg_ref[...], e_ref).astype(BF16)

    r0 = pl.multiple_of(n * blk, blk)
    cos = cos_ref[pl.ds(r0, blk), :]
    sin = sin_ref[pl.ds(r0, blk), :]
    slabs = []
    for s in range(q_ref.shape[-1] // LANES):
        y = _head_rms(q_ref[0, :, s * LANES:(s + 1) * LANES].astype(F32), qg_ref[...], e_ref)
        slabs.append(_rope(y, cos, sin))

    r = WIN_HEADS // WIN_KV_HEADS
    qi = lax.broadcasted_iota(jnp.int32, (r * blk, blk), 0) % blk
    kj = lax.broadcasted_iota(jnp.int32, (r * blk, blk), 1)
    ok_prev = (kj >= qi) & (n > 0)
    ok_next = (kj <= qi) & (n < nb - 1)
    bias_prev = jnp.where(ok_prev, 0.0, NEG_INF)
    bias_next = jnp.where(ok_next, 0.0, NEG_INF)
    tiles = []
    for kb, bias in ((jnp.maximum(n - 1, 0), bias_prev), (n, None), (jnp.minimum(n + 1, nb - 1), bias_next)):
        k0 = pl.multiple_of(kb * blk, blk)
        tiles.append((ks_ref[pl.ds(k0, blk), :], kv_ref[0, pl.ds(k0, blk), LANES:2 * LANES], bias))
    tiles.append((kcs_ref[...], ckv_ref[0, :, LANES:2 * LANES], None))
    rows = 2 * blk
    tiles = [(k, v, None if bias is None else bias[:rows]) for k, v, bias in tiles]
    for g in range(WIN_KV_HEADS):
        q, sink = _win_group_queries(slabs, g, sink_ref)
        outs = [_attend([(q[h * rows:(h + 1) * rows], tiles, sink[h * rows:(h + 1) * rows])])[0]
                for h in range(q.shape[0] // rows)]
        _win_store(o_ref, jnp.concatenate(outs, axis=0), g, blk)


def _win_ctx_kernel(sink_ref, q_ref, ckv_ref, qg_ref, kg_ref, e_ref, o_ref):
    kc = _head_rms(ckv_ref[0, :, 0:LANES].astype(F32), kg_ref[...], e_ref).astype(BF16)
    tiles = [(kc, ckv_ref[0, :, LANES:2 * LANES], None)]
    slabs = [_head_rms(q_ref[0, :, s * LANES:(s + 1) * LANES].astype(F32), qg_ref[...], e_ref)
             for s in range(q_ref.shape[-1] // LANES)]
    groups = [_win_group_queries(slabs, g, sink_ref) for g in range(WIN_KV_HEADS)]
    outs = _attend([(q, tiles, sink) for q, sink in groups])
    for g in range(WIN_KV_HEADS):
        _win_store(o_ref, outs[g], g, q_ref.shape[1])


def _block_diag_ones(width):
    i = np.arange(width) // HEAD_DIM
    return jnp.asarray(i[:, None] == i[None, :], BF16)


def _window_branch(zw_l, zw_c, q_gain, k_gain, sink, cos2, sin2, ctx_out):
    b, l, _ = zw_l.shape
    lc = zw_c.shape[1]
    wq = WIN_HEADS * HEAD_DIM
    qg = (jnp.tile(q_gain.astype(F32), 2) * HEAD_DIM ** -0.5).reshape(1, LANES)
    kg = jnp.tile(k_gain.astype(F32), 2).reshape(1, LANES)
    e = _block_diag_ones(LANES)
    smem = pl.BlockSpec(memory_space=pltpu.SMEM)
    full = lambda s, nd: pl.BlockSpec(s, lambda *a: (0,) * nd)
    yl = pl.pallas_call(
        _win_kernel,
        grid=(b, l // WIN_BLOCK),
        in_specs=[smem,
                  pl.BlockSpec((1, WIN_BLOCK, wq), lambda i, n: (i, n, 0)),
                  pl.BlockSpec((1, l, 2 * LANES), lambda i, n: (i, 0, wq // (2 * LANES))),
                  pl.BlockSpec((1, lc, 2 * LANES), lambda i, n: (i, 0, wq // (2 * LANES))),
                  full((l, LANES), 2), full((l, LANES), 2), full((1, LANES), 2), full((1, LANES), 2),
                  full((LANES, LANES), 2)],
        out_specs=pl.BlockSpec((1, WIN_BLOCK, wq), lambda i, n: (i, n, 0)),
        out_shape=jax.ShapeDtypeStruct((b, l, wq), BF16),
        scratch_shapes=[pltpu.VMEM((l, LANES), BF16), pltpu.VMEM((lc, LANES), BF16)],
        compiler_params=_cparams(2),
        name="window_attn",
    )(sink.astype(F32), zw_l, zw_l, zw_c, cos2, sin2, qg, kg, e)
    yc = None
    if ctx_out:
        yc = pl.pallas_call(
            _win_ctx_kernel,
            grid=(b,),
            in_specs=[smem,
                      pl.BlockSpec((1, lc, wq), lambda i: (i, 0, 0)),
                      pl.BlockSpec((1, lc, 2 * LANES), lambda i: (i, 0, wq // (2 * LANES))),
                      full((1, LANES), 2), full((1, LANES), 2), full((LANES, LANES), 2)],
            out_specs=pl.BlockSpec((1, lc, wq), lambda i: (i, 0, 0)),
            out_shape=jax.ShapeDtypeStruct((b, lc, wq), BF16),
            compiler_params=_cparams(1),
            name="window_ctx_attn",
        )(sink.astype(F32), zw_c, zw_c, qg, kg, e)
    return yc, yl


NA_ROWS = 8
NA_KROWS = NA_ROWS + NA_KH - 1


def _na_key_base(rg, rows):
    return jnp.clip(rg * NA_ROWS - NA_KH // 2, 0, rows - NA_KROWS)


def _na_kernel(q_ref, k_ref, v_ref, ck_ref, cv_ref, bias_ref, qg_ref, kg_ref, e_ref, o_ref, ks_ref, kcs_ref):
    rg = pl.program_id(1)
    l = k_ref.shape[1]
    rows = l // GRID_W
    n_slab = q_ref.shape[-1] // LANES
    prep_rows = 512

    @pl.when(rg == 0)
    def _():
        def body(i, carry):
            r0 = pl.multiple_of(i * prep_rows, prep_rows)
            ks_ref[pl.ds(r0, prep_rows), :] = _head_rms(k_ref[0, pl.ds(r0, prep_rows), :].astype(F32), kg_ref[...],
                                                        e_ref).astype(BF16)
            return carry
        lax.fori_loop(0, l // prep_rows, body, 0)
        kcs_ref[...] = _head_rms(ck_ref[0].astype(F32), kg_ref[...], e_ref).astype(BF16)

    nk = NA_KROWS * GRID_W
    k0 = pl.multiple_of(_na_key_base(rg, rows) * GRID_W, GRID_W)
    qn = _head_rms(q_ref[0].astype(F32), qg_ref[...], e_ref)
    half = _lane_half((q_ref.shape[1], LANES))
    for s in range(n_slab):
        cols = slice(s * LANES, (s + 1) * LANES)
        slab = qn[:, cols]
        tiles_kv = (ks_ref[pl.ds(k0, nk), cols], v_ref[0, pl.ds(k0, nk), cols])
        ctx_kv = (kcs_ref[:, cols], cv_ref[0, :, cols])
        outs = [_attend([(jnp.where(half == hh, slab, 0.0).astype(BF16),
                          [tiles_kv + (bias_ref[0, 2 * s + hh].astype(F32),), ctx_kv + (None,)], None)])[0]
                for hh in range(2)]
        o_ref[0, :, cols] = jnp.where(half == 0, outs[0], outs[1]).astype(o_ref.dtype)


def _na_ctx_kernel(q_ref, ck_ref, cv_ref, qg_ref, kg_ref, e_ref, o_ref):
    kc = _head_rms(ck_ref[0].astype(F32), kg_ref[...], e_ref).astype(BF16)
    qn = _head_rms(q_ref[0].astype(F32), qg_ref[...], e_ref)
    half = _lane_half((q_ref.shape[1], LANES))
    for s in range(q_ref.shape[-1] // LANES):
        cols = slice(s * LANES, (s + 1) * LANES)
        outs = _attend([(jnp.where(half == hh, qn[:, cols], 0.0).astype(BF16),
                         [(kc[:, cols], cv_ref[0, :, cols], None)], None) for hh in range(2)])
        o_ref[0, :, cols] = jnp.where(half == 0, outs[0], outs[1]).astype(o_ref.dtype)


def _na_bias_classes(rows):
    n_rg = rows // NA_ROWS
    return list(range(n_rg)) if n_rg <= 3 else [0, 1, n_rg - 1]


def _na_bias_table(rpb, rows):
    kh = NA_KH
    ro, rv = [], []
    for rg in _na_bias_classes(rows):
        kbase = int(np.clip(rg * NA_ROWS - kh // 2, 0, rows - NA_KROWS))
        r = rg * NA_ROWS + np.arange(NA_ROWS)
        rstart = np.clip(r - kh // 2, 0, rows - kh)
        kr = kbase + np.arange(NA_KROWS)
        rv.append((kr[None, :] >= rstart[:, None]) & (kr[None, :] < rstart[:, None] + kh))
        ro.append(np.clip(kr[None, :] - r[:, None] + kh - 1, 0, 2 * kh - 2))
    ro, rv = np.stack(ro), np.stack(rv)
    qc = np.arange(GRID_W)
    cstart = np.clip(qc - NA_KW // 2, 0, GRID_W - NA_KW)
    cv = (qc[None, :] >= cstart[:, None]) & (qc[None, :] < cstart[:, None] + NA_KW)
    co = np.clip(qc[None, :] - qc[:, None] + NA_KW - 1, 0, 2 * NA_KW - 2)
    n_cls = ro.shape[0]
    hp = lax.Precision.HIGHEST
    co_hot = jnp.asarray(co[None] == np.arange(2 * NA_KW - 1)[:, None, None], F32)
    ro_hot = jnp.asarray(ro[..., None] == np.arange(2 * kh - 1), F32)
    cols = jnp.einsum('hdc,cwx->hdwx', rpb.astype(F32), co_hot, precision=hp)
    tab = jnp.einsum('crkd,hdwx->chrwkx', ro_hot, cols, precision=hp)
    ok = rv[:, None, :, None, :, None] & cv[None, None, None, :, None, :]
    tab = jnp.where(ok, tab, NEG_INF)
    return tab.reshape(n_cls, rpb.shape[0], NA_ROWS * GRID_W, NA_KROWS * GRID_W).astype(BF16)


def _neighborhood_branch(zn_l, zn_c, q_gain, k_gain, rpb, ctx_out):
    b, l, _ = zn_l.shape
    lc = zn_c.shape[1]
    w = NA_HEADS * HEAD_DIM
    rows = l // GRID_W
    n_rg = rows // NA_ROWS
    n_slab = w // LANES
    qg = (jnp.tile(q_gain.astype(F32), 2 * n_slab) * HEAD_DIM ** -0.5).reshape(1, w)
    kg = jnp.tile(k_gain.astype(F32), 2 * n_slab).reshape(1, w)
    e = _block_diag_ones(w)
    bias = _na_bias_table(rpb, rows)
    if n_rg <= 3:
        cls_map = lambda i, r: (r, 0, 0, 0)
    else:
        cls_map = lambda i, r: ((r > 0).astype(jnp.int32) + (r == n_rg - 1).astype(jnp.int32), 0, 0, 0)
    tq = NA_ROWS * GRID_W
    full = lambda s, nd: pl.BlockSpec(s, lambda *a: (0,) * nd)
    yl = pl.pallas_call(
        _na_kernel,
        grid=(b, n_rg),
        in_specs=[pl.BlockSpec((1, tq, w), lambda i, r: (i, r, 0)),
                  pl.BlockSpec((1, l, w), lambda i, r: (i, 0, 1)),
                  pl.BlockSpec((1, l, w), lambda i, r: (i, 0, 2)),
                  pl.BlockSpec((1, lc, w), lambda i, r: (i, 0, 1)),
                  pl.BlockSpec((1, lc, w), lambda i, r: (i, 0, 2)),
                  pl.BlockSpec((1,) + bias.shape[1:], cls_map),
                  full((1, w), 2), full((1, w), 2), full((w, w), 2)],
        out_specs=pl.BlockSpec((1, tq, w), lambda i, r: (i, r, 0)),
        out_shape=jax.ShapeDtypeStruct((b, l, w), BF16),
        scratch_shapes=[pltpu.VMEM((l, w), BF16), pltpu.VMEM((lc, w), BF16)],
        compiler_params=_cparams(2),
        name="neighborhood_attn",
    )(zn_l, zn_l, zn_l, zn_c, zn_c, bias, qg, kg, e)
    yc = None
    if ctx_out:
        yc = pl.pallas_call(
            _na_ctx_kernel,
            grid=(b,),
            in_specs=[pl.BlockSpec((1, lc, w), lambda i: (i, 0, 0)),
                      pl.BlockSpec((1, lc, w), lambda i: (i, 0, 1)),
                      pl.BlockSpec((1, lc, w), lambda i: (i, 0, 2)),
                      full((1, w), 2), full((1, w), 2), full((w, w), 2)],
            out_specs=pl.BlockSpec((1, lc, w), lambda i: (i, 0, 0)),
            out_shape=jax.ShapeDtypeStruct((b, lc, w), BF16),
            compiler_params=_cparams(1),
            name="neighborhood_ctx_attn",
        )(zn_c, zn_c, zn_c, qg, kg, e)
    return yc, yl


def _rope_tables(n_tokens):
    t = np.arange(n_tokens)
    n_freq = HEAD_DIM // 4
    inv = ROPE_THETA ** (-jnp.arange(n_freq, dtype=F32) / n_freq)
    ang = jnp.concatenate([jnp.asarray(t // GRID_W, F32)[:, None] * inv, jnp.asarray(t % GRID_W, F32)[:, None] * inv],
                          axis=-1)
    cos, sin = jnp.cos(ang), jnp.sin(ang)
    return jnp.tile(jnp.concatenate([cos, cos], -1), (1, 2)), jnp.tile(jnp.concatenate([-sin, sin], -1), (1, 2))


def _ret_kernel(qq_l, kk_l, v_l, g_l, qq_c, kk_c, v_c, g_c, cos_ref, sin_ref, dmask_ref, tq_ref, tk_ref, cdec_ref,
                gain_ref, yl_ref, yc_ref, kr_ref, kv_ref, sin_state_ref, *, ctx_out):
    c = RET_CHUNK
    ncc = qq_c.shape[1] // c
    ncl = qq_l.shape[1] // c
    nc = ncc + ncl
    tk = tk_ref[0]
    tq = tq_ref[0]
    dmask = dmask_ref[0]
    fwd_lanes = lax.broadcasted_iota(jnp.int32, (c, LANES), 1) < RET_DK

    def chunk_kv(k2, v):
        kd = (k2 * tk).astype(BF16)
        return lax.dot_general(kd, v, (((0,), (0,)), ((), ())), preferred_element_type=F32)

    for n in range(ncc):
        kv_ref[n] = chunk_kv(kk_c[0, n * c:(n + 1) * c, :].astype(F32), v_c[0, n * c:(n + 1) * c, :])

    def kv_body(n, carry):
        r0 = pl.multiple_of(n * c, c)
        k2 = _rope(kk_l[0, pl.ds(r0, c), :].astype(F32), cos_ref[pl.ds(r0, c), :], sin_ref[pl.ds(r0, c), :])
        kr_ref[pl.ds(r0, c), :] = k2.astype(BF16)
        kv_ref[ncc + n] = chunk_kv(k2, v_l[0, pl.ds(r0, c), :])
        return carry
    lax.fori_loop(0, ncl, kv_body, 0, unroll=4)

    dec_f = cdec_ref[0, 0:1, :]
    dec_b = cdec_ref[0, 1:2, :]

    def scan_body(t, carry):
        sf, sb = carry
        sin_state_ref[t, 0:RET_DK, :] = sf.astype(BF16)
        sf = sf * dec_f + kv_ref[t, 0:RET_DK, :]
        tb = jnp.where(t < ncc, ncc - 1 - t, nc - 1 - (t - ncc))
        sin_state_ref[tb, RET_DK:2 * RET_DK, :] = sb.astype(BF16)
        sb = sb * dec_b + kv_ref[tb, RET_DK:2 * RET_DK, :]
        return sf, sb
    zero = jnp.zeros((RET_DK, RET_DV), F32)
    lax.fori_loop(0, nc, scan_body, (zero, zero))

    def chunk_out(q2, k2b, v, gate, state):
        qm = jnp.where(fwd_lanes, q2, 0.0).astype(BF16)
        scores = lax.dot_general(qm, k2b, (((1,), (1,)), ((), ())), preferred_element_type=F32) * dmask
        o = (jnp.dot(scores.astype(BF16), v, preferred_element_type=F32)
             + jnp.dot((q2 * tq).astype(BF16), state, preferred_element_type=F32))
        mu = jnp.mean(o, axis=-1, keepdims=True)
        var = jnp.mean(jnp.square(o - mu), axis=-1, keepdims=True)
        y = (o - mu) * lax.rsqrt(var + EPS) * gain_ref[0]
        gf = gate.astype(F32)
        return y * gf * jax.nn.sigmoid(gf)

    if ctx_out:
        for n in range(ncc):
            rows = slice(n * c, (n + 1) * c)
            yc_ref[0, rows, :] = chunk_out(qq_c[0, rows, :].astype(F32), kk_c[0, rows, :], v_c[0, rows, :],
                                           g_c[0, rows, :], sin_state_ref[n]).astype(yc_ref.dtype)
    else:
        yc_ref[...] = jnp.zeros_like(yc_ref)

    def out_body(n, carry):
        r0 = pl.multiple_of(n * c, c)
        q2 = _rope(qq_l[0, pl.ds(r0, c), :].astype(F32), cos_ref[pl.ds(r0, c), :], sin_ref[pl.ds(r0, c), :])
        yl_ref[0, pl.ds(r0, c), :] = chunk_out(q2, kr_ref[pl.ds(r0, c), :], v_l[0, pl.ds(r0, c), :],
                                               g_l[0, pl.ds(r0, c), :], sin_state_ref[ncc + n]).astype(yl_ref.dtype)
        return carry
    lax.fori_loop(0, ncl, out_body, 0, unroll=4)


def _ret_tables(decay_logit):
    c = RET_CHUNK
    lg = jax.nn.log_sigmoid(decay_logit.astype(F32))
    lf, lb = lg[0][:, None, None], lg[1][:, None, None]
    pos = jnp.arange(c, dtype=F32)
    diff = pos[:, None] - pos[None, :]
    dmask = (jnp.where(diff >= 0, jnp.exp(lf * jnp.maximum(diff, 0.0)), 0.0)
             + jnp.where(diff <= 0, jnp.exp(lb * jnp.maximum(-diff, 0.0)), 0.0)) * RET_DK ** -0.5
    col = lambda a, b_: jnp.concatenate([jnp.broadcast_to(a, a.shape[:2] + (RET_DK,)),
                                         jnp.broadcast_to(b_, b_.shape[:2] + (RET_DK,))], axis=-1)
    p = pos[None, :, None]
    tq = col(jnp.exp(lf * (p + 1.0)), jnp.exp(lb * (c - p)))
    tk = col(jnp.exp(lf * (c - 1.0 - p)), jnp.exp(lb * p)) * RET_DK ** -0.5
    cdec = jnp.zeros((lg.shape[1], 8, RET_DV), F32)
    cdec = cdec.at[:, 0, :].set(jnp.exp(lg[0] * c)[:, None]).at[:, 1, :].set(jnp.exp(lg[1] * c)[:, None])
    return dmask, tq, tk, cdec


def _retention_branch(zr_l, zr_c, decay_logit, gn_gain, cos2, sin2, ctx_out):
    b, l, _ = zr_l.shape
    lc = zr_c.shape[1]
    h = RET_HEADS
    nc = (l + lc) // RET_CHUNK
    dmask, tq, tk, cdec = _ret_tables(decay_logit)
    gain = gn_gain.astype(F32).reshape(h, 1, RET_DV)
    seq = lambda n, j: pl.BlockSpec((1, n, LANES), lambda i, hh: (i, 0, 4 * hh + j))
    head = lambda s: pl.BlockSpec((1,) + s, lambda i, hh: (hh, 0, 0))
    full = lambda s: pl.BlockSpec(s, lambda i, hh: (0, 0))
    yl, yc = pl.pallas_call(
        functools.partial(_ret_kernel, ctx_out=ctx_out),
        grid=(b, h),
        in_specs=[seq(l, 0), seq(l, 1), seq(l, 2), seq(l, 3), seq(lc, 0), seq(lc, 1), seq(lc, 2), seq(lc, 3),
                  full((l, LANES)), full((l, LANES)),
                  head((RET_CHUNK, RET_CHUNK)), head((RET_CHUNK, LANES)), head((RET_CHUNK, LANES)), head((8, RET_DV)),
                  head((1, RET_DV))],
        out_specs=[pl.BlockSpec((1, l, RET_DV), lambda i, hh: (i, 0, hh)),
                   pl.BlockSpec((1, lc, RET_DV), lambda i, hh: (i, 0, hh))],
        out_shape=[jax.ShapeDtypeStruct((b, l, h * RET_DV), BF16), jax.ShapeDtypeStruct((b, lc, h * RET_DV), BF16)],
        scratch_shapes=[pltpu.VMEM((l, LANES), BF16), pltpu.VMEM((nc, 2 * RET_DK, RET_DV), F32),
                        pltpu.VMEM((nc, 2 * RET_DK, RET_DV), BF16)],
        compiler_params=_cparams(2),
        name="retention",
    )(zr_l, zr_l, zr_l, zr_l, zr_c, zr_c, zr_c, zr_c, cos2, sin2, dmask, tq, tk, cdec, gain)
    return (yc if ctx_out else None), yl


GDN_SUPER = 128
GDN_UNITS = 4
GDN_HALO = 128


def _split_bf16(a):
    hi = a.astype(BF16)
    return hi, (a - hi.astype(F32)).astype(BF16)


def _mask_dot(mask_bf16, a):
    ah, al = _split_bf16(a)
    return jnp.dot(mask_bf16, ah, preferred_element_type=F32) + jnp.dot(mask_bf16, al, preferred_element_type=F32)


def _softplus(x):
    return jnp.maximum(x, 0.0) + jnp.log(1.0 + jnp.exp(-jnp.abs(x)))


def _gdn_kernel(nega_ref, dtb_ref, q_l, k_l, v_l, g_l, ab_l, q_c, k_c, v_c, g_c, ab_c,
                cwq_ref, cwk_ref, cwv_ref, pd_ref, pu_ref, gain_ref, yl_ref, yc_ref,
                kn_s, sin_s, qp_s, o0_s, cd_s, *, ctx_out):
    hd = pl.program_id(1)
    c = GDN_CHUNK
    sup = GDN_SUPER
    per = sup // c
    lc, l = q_c.shape[1], q_l.shape[1]
    ncc, ncl = lc // c, l // c
    nc = ncc + ncl

    ri = lax.broadcasted_iota(jnp.int32, (sup, sup), 0)
    ci = lax.broadcasted_iota(jnp.int32, (sup, sup), 1)
    same = (ri // c) == (ci // c)
    eye = (ri == ci).astype(F32)
    incl = (same & (ri >= ci), same & (ri <= ci))
    strict = (same & (ri > ci), same & (ri < ci))
    incl_b = tuple(m.astype(BF16) for m in incl)
    same_b = same.astype(BF16)
    lane = lax.broadcasted_iota(jnp.int32, (sup, LANES), 1)
    rowc = lax.broadcasted_iota(jnp.int32, (sup, LANES), 0)

    def conv_silu(z_ref, w_ref, r0, ls):
        z = z_ref[0, pl.ds(r0, sup), :]
        if isinstance(r0, int):
            zero = jnp.zeros((GDN_HALO, LANES), BF16)
            prev = z_ref[0, r0 - GDN_HALO:r0, :] if r0 > 0 else zero
            nxt = z_ref[0, r0 + sup:r0 + sup + GDN_HALO, :] if r0 + sup < ls else zero
        else:
            p0 = pl.multiple_of(jnp.maximum(r0 - GDN_HALO, 0), GDN_HALO)
            n0 = pl.multiple_of(jnp.minimum(r0 + sup, ls - GDN_HALO), GDN_HALO)
            prev = jnp.where(r0 > 0, z_ref[0, pl.ds(p0, GDN_HALO), :], jnp.zeros((), BF16))
            nxt = jnp.where(r0 + sup < ls, z_ref[0, pl.ds(n0, GDN_HALO), :], jnp.zeros((), BF16))
        win = jnp.concatenate([prev, z, nxt], axis=0)
        z_dn = jnp.dot(pd_ref[...], win, preferred_element_type=F32)
        z_up = jnp.dot(pu_ref[...], win, preferred_element_type=F32)
        y = z_dn * w_ref[0, 0:1, :] + z.astype(F32) * w_ref[0, 1:2, :] + z_up * w_ref[0, 2:3, :]
        return y * jax.nn.sigmoid(y)

    def prep(units):
        common = []
        for refs, r0, base, ls in units:
            q_ref, k_ref, v_ref, ab_ref = refs
            q = conv_silu(q_ref, cwq_ref, r0, ls)
            k = conv_silu(k_ref, cwk_ref, r0, ls)
            v = conv_silu(v_ref, cwv_ref, r0, ls)
            q = q * lax.rsqrt(jnp.sum(q * q, axis=-1, keepdims=True) + EPS) * GDN_DK ** -0.5
            k = k * lax.rsqrt(jnp.sum(k * k, axis=-1, keepdims=True) + EPS)
            kb16 = k.astype(BF16)
            qk = lax.dot_general(q.astype(BF16), kb16, (((1,), (1,)), ((), ())), preferred_element_type=F32)
            ab = ab_ref[0, pl.ds(r0, sup), :]
            common.append((q, k, v, kb16, qk, ab, base + r0))
        chains = []
        for q, k, v, kb16, qk, ab, row0 in common:
            for d in range(2):
                ia = d * GDN_HEADS + hd
                ib = 2 * GDN_HEADS + ia
                a_col = jnp.sum(jnp.where(lane == ia, ab, 0.0), axis=1, keepdims=True)
                b_col = jnp.sum(jnp.where(lane == ib, ab, 0.0), axis=1, keepdims=True)
                g_col = nega_ref[d, hd] * _softplus(a_col + dtb_ref[d, hd])
                beta = jax.nn.sigmoid(b_col)
                gcum = _mask_dot(incl_b[d], jnp.broadcast_to(g_col, (sup, LANES)))
                gtot = _mask_dot(same_b, jnp.broadcast_to(g_col, (sup, LANES)))
                gc = gcum[:, 0:1]
                decay = jnp.where(incl[d], jnp.exp(jnp.where(incl[d], gcum - gcum.T, 0.0)), 0.0)
                kbeta = k * beta
                kk = lax.dot_general(kbeta.astype(BF16), kb16, (((1,), (1,)), ((), ())),
                                     preferred_element_type=F32)
                a_mat = jnp.where(strict[d], kk * decay, 0.0)
                eg = jnp.exp(gc)
                rhs = jnp.concatenate([kbeta * eg, v * beta], axis=1).astype(BF16)
                aqk = jnp.where(incl[d], qk * decay, 0.0).astype(BF16)
                ke = (k * jnp.exp(gtot[:, 0:1] - gc)).astype(BF16)
                cds = [jnp.exp(gtot[j * c:j * c + 1, :]) for j in range(per)]
                chains.append(dict(d=d, row0=row0, a=a_mat, rhs=rhs, aqk=aqk, ke=ke, cds=cds, qeg=q * eg))
        for ch in chains:
            ch['inv'] = eye - ch['a']
            ch['pw'] = ch['a'].astype(BF16)
        for _ in range(5):
            for ch in chains:
                ch['pw'] = jnp.dot(ch['pw'], ch['pw'], preferred_element_type=F32).astype(BF16)
            for ch in chains:
                ch['inv'] = ch['inv'] + jnp.dot(ch['inv'].astype(BF16), ch['pw'], preferred_element_type=F32)
        for ch in chains:
            ch['wu'] = jnp.dot(ch['inv'].astype(BF16), ch['rhs'], preferred_element_type=F32).astype(BF16)
        for ch in chains:
            d, row0 = ch['d'], ch['row0']
            awu = jnp.dot(ch['aqk'], ch['wu'], preferred_element_type=F32)
            rows = pl.ds(row0, sup)
            qp_s[d, rows, :] = (ch['qeg'] - awu[:, :GDN_DK]).astype(BF16)
            o0_s[d, rows, :] = awu[:, GDN_DK:]
            for j in range(per):
                kej = jnp.where((rowc // c) == j, ch['ke'], jnp.zeros((), BF16))
                idx = row0 // c + j
                kn_s[d, idx] = lax.dot_general(kej, ch['wu'], (((0,), (0,)), ((), ())),
                                               preferred_element_type=F32).astype(BF16)
                cd_s[d, idx] = ch['cds'][j]

    prep([((q_c, k_c, v_c, ab_c), n * sup, 0, lc) for n in range(lc // sup)])

    def prep_body(n, carry):
        prep([((q_l, k_l, v_l, ab_l), pl.multiple_of((GDN_UNITS * n + j) * sup, sup), lc, l)
              for j in range(GDN_UNITS)])
        return carry
    lax.fori_loop(0, l // (GDN_UNITS * sup), prep_body, 0)

    def chunk_step(d, idx, s):
        sb = s.astype(BF16)
        sin_s[d, idx] = sb
        kn = kn_s[d, idx]
        return (s * cd_s[d, idx] - jnp.dot(kn[:, :GDN_DK], sb, preferred_element_type=F32)
                + kn[:, GDN_DK:].astype(F32))

    def scan_body(t, carry):
        sf, sb = carry
        tb = jnp.where(t < ncc, ncc - 1 - t, nc - 1 - (t - ncc))
        return chunk_step(0, t, sf), chunk_step(1, tb, sb)
    zero = jnp.zeros((GDN_DK, GDN_DV), F32)
    lax.fori_loop(0, nc, scan_body, (zero, zero))

    def finish(y_ref, gate_ref, base, n_rows):
        def body(n, carry):
            r0 = pl.multiple_of(n * c, c)
            rows = pl.ds(base + r0, c)
            idx = base // c + n
            o = (o0_s[0, rows, :] + o0_s[1, rows, :]
                 + jnp.dot(qp_s[0, rows, :], sin_s[0, idx], preferred_element_type=F32)
                 + jnp.dot(qp_s[1, rows, :], sin_s[1, idx], preferred_element_type=F32))
            y = o * lax.rsqrt(jnp.mean(o * o, axis=-1, keepdims=True) + EPS) * gain_ref[...]
            gf = gate_ref[0, pl.ds(r0, c), :].astype(F32)
            y_ref[0, pl.ds(r0, c), :] = (y * gf * jax.nn.sigmoid(gf)).astype(y_ref.dtype)
            return carry
        lax.fori_loop(0, n_rows // c, body, 0, unroll=4)

    finish(yl_ref, g_l, lc, l)
    if ctx_out:
        finish(yc_ref, g_c, 0, lc)
    else:
        yc_ref[...] = jnp.zeros_like(yc_ref)


def _gdn_branch(zg_l, zab_l, zg_c, zab_c, conv_w, a_log, dt_bias, norm_gain, ctx_out):
    b, l, _ = zg_l.shape
    lc = zg_c.shape[1]
    h = GDN_HEADS
    sup = GDN_SUPER
    ltot = l + lc
    nc = ltot // GDN_CHUNK
    neg_a = -jnp.exp(a_log.astype(F32))
    cw = conv_w.astype(F32).T.reshape(3 * h, LANES, SHORT_CONV).transpose(0, 2, 1)
    win = sup + 2 * GDN_HALO
    i = np.arange(sup)
    pd = np.zeros((sup, win), np.float32)
    pu = np.zeros((sup, win), np.float32)
    pd[i, GDN_HALO + i - 1] = 1.0
    pu[i, GDN_HALO + i + 1] = 1.0
    seq = lambda n, j: pl.BlockSpec((1, n, LANES), lambda bi, hh: (bi, 0, j * h + hh))
    abs_ = lambda n: pl.BlockSpec((1, n, LANES), lambda bi, hh: (bi, 0, 0))
    cws = lambda j: pl.BlockSpec((1, SHORT_CONV, LANES), lambda bi, hh: (j * h + hh, 0, 0))
    full = lambda s: pl.BlockSpec(s, lambda bi, hh: (0, 0))
    smem = pl.BlockSpec(memory_space=pltpu.SMEM)
    yl, yc = pl.pallas_call(
        functools.partial(_gdn_kernel, ctx_out=ctx_out),
        grid=(b, h),
        in_specs=[smem, smem, seq(l, 0), seq(l, 1), seq(l, 2), seq(l, 3), abs_(l),
                  seq(lc, 0), seq(lc, 1), seq(lc, 2), seq(lc, 3), abs_(lc),
                  cws(0), cws(1), cws(2), full((sup, win)), full((sup, win)), full((1, GDN_DV))],
        out_specs=[pl.BlockSpec((1, l, GDN_DV), lambda bi, hh: (bi, 0, hh)),
                   pl.BlockSpec((1, lc, GDN_DV), lambda bi, hh: (bi, 0, hh))],
        out_shape=[jax.ShapeDtypeStruct((b, l, h * GDN_DV), BF16), jax.ShapeDtypeStruct((b, lc, h * GDN_DV), BF16)],
        scratch_shapes=[pltpu.VMEM((2, nc, GDN_DK, GDN_DK + GDN_DV), BF16), pltpu.VMEM((2, nc, GDN_DK, GDN_DV), BF16),
                        pltpu.VMEM((2, ltot, GDN_DK), BF16), pltpu.VMEM((2, ltot, GDN_DV), F32),
                        pltpu.VMEM((2, nc, 1, GDN_DV), F32)],
        compiler_params=_cparams(2),
        name="gated_deltanet",
    )(neg_a, dt_bias.astype(F32), zg_l, zg_l, zg_l, zg_l, zab_l, zg_c, zg_c, zg_c, zg_c, zab_c,
      cw, cw, cw, jnp.asarray(pd, BF16), jnp.asarray(pu, BF16), norm_gain.astype(F32).reshape(1, GDN_DV))
    return (yc if ctx_out else None), yl


def kernel(x, c, ctx, c_ctx, w_mod, b_mod, norm1, norm2, w_in, ret_decay, ret_gn, win_qnorm, win_knorm, win_sink,
           na_qnorm, na_knorm, na_rpb, gdn_conv, gdn_a_log, gdn_dt_bias, gdn_norm, w_branch, w_merge, w_out,
           w_router, router_bias, w_e_gate, w_e_up, w_e_down):
    b, l, d = x.shape
    lc = ctx.shape[1]
    depth = w_mod.shape[0]
    cos2, sin2 = _rope_tables(l)

    n_rows = 16
    cc = jnp.zeros((n_rows, d), F32).at[:b].set(c).at[b].set(c_ctx)
    mod = _modulation(cc, w_mod, b_mod).reshape(depth, n_rows, 6, d)

    wr = jnp.zeros((d, LANES), F32).at[:, :N_EXPERTS].set(w_router)
    wr_hi = wr.astype(BF16)
    wr_lo = (wr - wr_hi.astype(F32)).astype(BF16)

    xl, xc = x, ctx
    for layer in range(depth):
        ctx_out = layer < depth - 1
        mod_l = mod[layer, :b]
        mod_c = mod[layer, b:b + 1]
        w_all = _pack_w_in(w_in[layer])
        hl, *zl_s = _inproj(xl, mod_l, False, norm1[layer], w_all, min(l, TOKEN_TILE))
        hc, *zc_s = _inproj(xc, mod_c, True, norm1[layer], w_all, min(lc, TOKEN_TILE))
        zr_l, zw_l, zn_l, zg_l, zab_l = zl_s
        zr_c, zw_c, zn_c, zg_c, zab_c = zc_s
        ret_c, ret_l = _retention_branch(zr_l, zr_c, ret_decay[layer], ret_gn[layer], cos2, sin2, ctx_out)
        win_c, win_l = _window_branch(zw_l, zw_c, win_qnorm[layer], win_knorm[layer], win_sink[layer], cos2, sin2,
                                      ctx_out)
        na_c, na_l = _neighborhood_branch(zn_l, zn_c, na_qnorm[layer], na_knorm[layer], na_rpb[layer], ctx_out)
        gdn_c, gdn_l = _gdn_branch(zg_l, zab_l, zg_c, zab_c, gdn_conv[layer], gdn_a_log[layer], gdn_dt_bias[layer],
                                   gdn_norm[layer], ctx_out)
        wm = w_merge[layer].astype(BF16)
        wb = w_branch[layer].astype(BF16)
        wo = w_out[layer].astype(BF16)
        ys_l = [ret_l, win_l, na_l, gdn_l]
        xl, h2l, sc_l = _merge(xl, hl, ys_l, mod_l, False, norm2[layer], wm, wb, wo, wr_hi, wr_lo, MERGE_TILE)
        if ctx_out:
            ys_c = [ret_c, win_c, na_c, gdn_c]
            xc, h2c, sc_c = _merge(xc, hc, ys_c, mod_c, True, norm2[layer], wm, wb, wo, wr_hi, wr_lo, MERGE_TILE)
            tokens = jnp.concatenate([h2c.reshape(b * lc, d), h2l.reshape(b * l, d)], axis=0)
            scores = jnp.concatenate([sc_c.reshape(b * lc, LANES), sc_l.reshape(b * l, LANES)], axis=0)
            y, wts = _moe(tokens, scores[:, :N_EXPERTS], router_bias, w_e_gate[layer], w_e_up[layer], w_e_down[layer])
            xc = _combine(xc, y, wts, mod_c, True, 0, MERGE_TILE)
            xl = _combine(xl, y, wts, mod_l, False, b * lc, MERGE_TILE)
        else:
            y, wts = _moe(h2l.reshape(b * l, d), sc_l.reshape(b * l, LANES)[:, :N_EXPERTS], router_bias,
                          w_e_gate[layer], w_e_up[layer], w_e_down[layer])
            xl = _combine(xl, y, wts, mod_l, False, 0, MERGE_TILE)
    return xl
```

```python
import functools

import numpy as np
import jax
import jax.numpy as jnp
from jax import lax
from jax.experimental import pallas as pl
from jax.experimental.pallas import tpu as pltpu

F32 = jnp.float32
BF16 = jnp.bfloat16
EPS = 1e-6
NEG_INF = -1e30
D_MODEL = 1024
GRID_W = 64
HEAD_DIM = 64
ROPE_THETA = 10000.0
RET_HEADS, RET_DK, RET_DV, RET_CHUNK = 4, 64, 128, 128
WIN_HEADS, WIN_KV_HEADS, WINDOW, WIN_BLOCK = 8, 2, 128, 128
NA_HEADS, NA_KH, NA_KW, NA_QCOLS = 8, 8, 16, 16
NA_BAND = NA_QCOLS + NA_KW
GDN_HEADS, GDN_DK, GDN_DV, GDN_CHUNK, SHORT_CONV = 4, 128, 128, 64, 3
GDN_QKV = 2 * GDN_HEADS * GDN_DK + GDN_HEADS * GDN_DV
N_BRANCH, BRANCH_W = 4, 512
N_EXPERTS, N_GROUPS, TOP_K, D_EXPERT = 32, 8, 2, 512
EXPERTS_PER_GROUP = N_EXPERTS // N_GROUPS

LANES = 128
VMEM_LIMIT = 56 * 1024 * 1024
TOKEN_TILE = 512
MERGE_TILE = 256
MOE_ROWS = 256
MOE_ROW_ALIGN = 8

W_RET = RET_HEADS * 4 * LANES
W_WIN = (WIN_HEADS + 2 * WIN_KV_HEADS) * HEAD_DIM
W_NA = 3 * NA_HEADS * HEAD_DIM
W_GDN = GDN_QKV + GDN_HEADS * GDN_DV
W_AB = LANES
SECTION_WIDTHS = (W_RET, W_WIN, W_NA, W_GDN, W_AB)
W_ALL = sum(SECTION_WIDTHS)


def _cparams(n_axes):
    return pltpu.CompilerParams(dimension_semantics=("arbitrary",) * n_axes, vmem_limit_bytes=VMEM_LIMIT)


def _mod_kernel(c_ref, w_ref, b_ref, o_ref):
    c = c_ref[...]
    a = (c * jax.nn.sigmoid(c)).astype(BF16)
    o_ref[0] = jnp.dot(a, w_ref[0].astype(BF16), preferred_element_type=F32) + b_ref[0]


def _modulation(cc, w_mod, b_mod):
    depth, d, n = w_mod.shape
    r = cc.shape[0]
    tn = 1536
    return pl.pallas_call(
        _mod_kernel,
        grid=(depth, n // tn),
        in_specs=[pl.BlockSpec((r, d), lambda l, j: (0, 0)),
                  pl.BlockSpec((1, d, tn), lambda l, j: (l, 0, j)),
                  pl.BlockSpec((1, 1, tn), lambda l, j: (l, 0, j))],
        out_specs=pl.BlockSpec((1, r, tn), lambda l, j: (l, 0, j)),
        out_shape=jax.ShapeDtypeStruct((depth, r, n), F32),
        compiler_params=_cparams(2),
        name="modulation",
    )(cc, w_mod, b_mod.reshape(depth, 1, n))


def _inproj_kernel(x_ref, mod_ref, gain_ref, w_ref, h_ref, *z_refs):
    x = x_ref[0]
    ms = jnp.mean(x * x, axis=-1, keepdims=True)
    shift = mod_ref[0, 0:1, :]
    scale = mod_ref[0, 1:2, :]
    h = x * lax.rsqrt(ms + EPS) * gain_ref[...] * (1.0 + scale) + shift
    hb = h.astype(BF16)
    h_ref[0] = hb
    off = 0
    for ref in z_refs:
        width = ref.shape[-1]
        for c0 in range(0, width, 512):
            c1 = min(c0 + 512, width)
            z = jnp.dot(hb, w_ref[:, off + c0:off + c1], preferred_element_type=F32)
            ref[0, :, c0:c1] = z.astype(ref.dtype)
        off += width


def _inproj(x, mod, mod_is_shared, gain, w_all, tm):
    b, l, d = x.shape
    mod_map = (lambda i, j: (0, 0, 0)) if mod_is_shared else (lambda i, j: (i, 0, 0))
    dtypes = (BF16, BF16, BF16, BF16, F32)
    out_shape = [jax.ShapeDtypeStruct((b, l, d), BF16)]
    out_specs = [pl.BlockSpec((1, tm, d), lambda i, j: (i, j, 0))]
    for w, dt in zip(SECTION_WIDTHS, dtypes):
        out_shape.append(jax.ShapeDtypeStruct((b, l, w), dt))
        out_specs.append(pl.BlockSpec((1, tm, w), lambda i, j: (i, j, 0)))
    return pl.pallas_call(
        _inproj_kernel,
        grid=(b, l // tm),
        in_specs=[pl.BlockSpec((1, tm, d), lambda i, j: (i, j, 0)),
                  pl.BlockSpec((1, 6, d), mod_map),
                  pl.BlockSpec((1, d), lambda i, j: (0, 0)),
                  pl.BlockSpec((d, W_ALL), lambda i, j: (0, 0), pipeline_mode=pl.Buffered(1))],
        out_specs=out_specs,
        out_shape=out_shape,
        compiler_params=_cparams(2),
        name="inproj",
    )(x, mod, gain.reshape(1, d), w_all)


def _pack_w_in(w_in):
    d = w_in.shape[0]
    hq, hv = RET_HEADS * RET_DK, RET_HEADS * RET_DV
    cols = []
    for h in range(RET_HEADS):
        q = w_in[:, h * RET_DK:(h + 1) * RET_DK]
        k = w_in[:, hq + h * RET_DK:hq + (h + 1) * RET_DK]
        cols += [q, q, k, k, w_in[:, 2 * hq + h * RET_DV:2 * hq + (h + 1) * RET_DV],
                 w_in[:, 2 * hq + hv + h * RET_DV:2 * hq + hv + (h + 1) * RET_DV]]
    rest = w_in[:, 2 * hq + 2 * hv:]
    pad = jnp.zeros((d, W_ALL - W_RET - rest.shape[1]), w_in.dtype)
    return jnp.concatenate(cols + [rest, pad], axis=1).astype(BF16)


def _merge_kernel(x_ref, h_ref, y0_ref, y1_ref, y2_ref, y3_ref, mod_ref, gain_ref, wm_ref, wb_ref, wo_ref,
                  wrh_ref, wrl_ref, xo_ref, h2_ref, sc_ref):
    d = x_ref.shape[-1]
    h = h_ref[0]
    acc = jnp.zeros(x_ref.shape[1:], F32)
    for i, y_ref in enumerate((y0_ref, y1_ref, y2_ref, y3_ref)):
        gate = jax.nn.sigmoid(jnp.dot(h, wm_ref[:, i * d:(i + 1) * d], preferred_element_type=F32))
        acc = acc + gate * jnp.dot(y_ref[0], wb_ref[i], preferred_element_type=F32)
    m = jnp.dot(acc.astype(BF16), wo_ref[...], preferred_element_type=F32)
    xn = x_ref[0] + mod_ref[0, 2:3, :] * m
    xo_ref[0] = xn
    ms = jnp.mean(xn * xn, axis=-1, keepdims=True)
    h2 = xn * lax.rsqrt(ms + EPS) * gain_ref[...] * (1.0 + mod_ref[0, 4:5, :]) + mod_ref[0, 3:4, :]
    hi = h2.astype(BF16)
    h2_ref[0] = h2
    lo = (h2 - hi.astype(F32)).astype(BF16)
    logits = (jnp.dot(hi, wrh_ref[...], preferred_element_type=F32)
              + jnp.dot(lo, wrh_ref[...], preferred_element_type=F32)
              + jnp.dot(hi, wrl_ref[...], preferred_element_type=F32))
    sc_ref[0] = jax.nn.sigmoid(logits)


def _merge(x, h, ys, mod, mod_is_shared, gain2, wm, wb, wo, wr_hi, wr_lo, tm):
    b, l, d = x.shape
    mod_map = (lambda i, j: (0, 0, 0)) if mod_is_shared else (lambda i, j: (i, 0, 0))
    tok = lambda w: pl.BlockSpec((1, tm, w), lambda i, j: (i, j, 0))
    full2 = lambda s: pl.BlockSpec(s, lambda i, j: (0, 0), pipeline_mode=pl.Buffered(1))
    return pl.pallas_call(
        _merge_kernel,
        grid=(b, l // tm),
        in_specs=[tok(d), tok(d)] + [tok(BRANCH_W)] * 4 + [
            pl.BlockSpec((1, 6, d), mod_map), full2((1, d)), full2(wm.shape),
            pl.BlockSpec(wb.shape, lambda i, j: (0, 0, 0), pipeline_mode=pl.Buffered(1)), full2(wo.shape),
            full2(wr_hi.shape), full2(wr_lo.shape)],
        out_specs=[tok(d), tok(d), tok(LANES)],
        out_shape=[jax.ShapeDtypeStruct((b, l, d), F32), jax.ShapeDtypeStruct((b, l, d), F32),
                   jax.ShapeDtypeStruct((b, l, LANES), F32)],
        compiler_params=_cparams(2),
        name="merge",
    )(x, h, *ys, mod, gain2.reshape(1, d), wm, wb, wo, wr_hi, wr_lo)


def _moe_kernel(be_ref, nv_ref, src_ref, dst_ref, h_hbm, wg_ref, wu_ref, wd_ref, y_hbm, xbuf, obuf, wg_s, wu_s, wd_s,
                gsem, ssem, *, n_real):
    i = pl.program_id(0)
    n = pl.num_programs(0)
    slot = i % 2
    nv = nv_ref[i]

    def start_gather(blk, dst_slot):
        def body(grp, carry):
            base = blk * MOE_ROWS + grp * MOE_ROW_ALIGN
            for j in range(MOE_ROW_ALIGN):
                pltpu.make_async_copy(h_hbm.at[pl.ds(src_ref[base + j], 1)], xbuf.at[dst_slot, grp, pl.ds(j, 1)],
                                      gsem.at[dst_slot]).start()
            return carry
        lax.fori_loop(0, nv_ref[blk] // MOE_ROW_ALIGN, body, 0)

    def wait_rows(buf, sem, cnt):
        grps = buf.at[pl.ds(0, cnt // MOE_ROW_ALIGN)]
        pltpu.make_async_copy(grps, grps, sem).wait()

    @pl.when(i == 0)
    def _():
        xbuf[...] = jnp.zeros_like(xbuf)
        fills = [pltpu.make_async_copy(xbuf.at[1, g], y_hbm.at[pl.ds(n_real + g * MOE_ROW_ALIGN, MOE_ROW_ALIGN)],
                                       ssem.at[1])
                 for g in range((y_hbm.shape[0] - n_real) // MOE_ROW_ALIGN)]
        for f in fills:
            f.start()
        for f in fills:
            f.wait()
        start_gather(0, 0)

    @pl.when(i + 1 < n)
    def _():
        start_gather(i + 1, 1 - slot)

    @pl.when((i >= 2) & (nv_ref[jnp.maximum(i - 2, 0)] > 0))
    def _():
        wait_rows(obuf.at[slot], ssem.at[slot], nv_ref[jnp.maximum(i - 2, 0)])

    @pl.when(nv > 0)
    def _():
        wait_rows(xbuf.at[slot], gsem.at[slot], nv)
        e = be_ref[i]

        @pl.when((i == 0) | (e != be_ref[jnp.maximum(i - 1, 0)]))
        def _():
            wg_s[...] = wg_ref[0].astype(BF16)
            wu_s[...] = wu_ref[0].astype(BF16)
            wd_s[...] = wd_ref[0].astype(BF16)

        d = xbuf.shape[-1]
        x = xbuf[slot].reshape(MOE_ROWS, d).astype(BF16)
        g = jnp.dot(x, wg_s[...], preferred_element_type=F32)
        u = jnp.dot(x, wu_s[...], preferred_element_type=F32)
        a = (g * jax.nn.sigmoid(g) * u).astype(BF16)
        obuf[slot] = jnp.dot(a, wd_s[...], preferred_element_type=F32).reshape(obuf.shape[1:])

        def body(grp, carry):
            base = i * MOE_ROWS + grp * MOE_ROW_ALIGN
            for j in range(MOE_ROW_ALIGN):
                pltpu.make_async_copy(obuf.at[slot, grp, pl.ds(j, 1)], y_hbm.at[pl.ds(dst_ref[base + j], 1)],
                                      ssem.at[slot]).start()
            return carry
        lax.fori_loop(0, nv // MOE_ROW_ALIGN, body, 0)

    @pl.when(i == n - 1)
    def _():
        @pl.when(nv > 0)
        def _():
            wait_rows(obuf.at[slot], ssem.at[slot], nv)

        @pl.when((n >= 2) & (nv_ref[jnp.maximum(i - 1, 0)] > 0))
        def _():
            wait_rows(obuf.at[1 - slot], ssem.at[1 - slot], nv_ref[jnp.maximum(i - 1, 0)])


def _moe_ffn(block_e, n_valid, src, dst, h2, w_gate, w_up, w_down):
    t, d = h2.shape
    grp_shape = (2, MOE_ROWS // MOE_ROW_ALIGN, MOE_ROW_ALIGN, d)
    n_blocks = block_e.shape[0]
    de = w_gate.shape[-1]
    wspec = lambda s: pl.BlockSpec((1,) + s, lambda i, be, nv, src, dst: (be[i], 0, 0))
    grid_spec = pltpu.PrefetchScalarGridSpec(
        num_scalar_prefetch=4,
        grid=(n_blocks,),
        in_specs=[pl.BlockSpec(memory_space=pl.ANY), wspec((d, de)), wspec((d, de)), wspec((de, d))],
        out_specs=pl.BlockSpec(memory_space=pl.ANY),
        scratch_shapes=[pltpu.VMEM(grp_shape, F32), pltpu.VMEM(grp_shape, F32),
                        pltpu.VMEM((d, de), BF16), pltpu.VMEM((d, de), BF16), pltpu.VMEM((de, d), BF16),
                        pltpu.SemaphoreType.DMA((2,)), pltpu.SemaphoreType.DMA((2,))],
    )
    return pl.pallas_call(
        functools.partial(_moe_kernel, n_real=TOP_K * t),
        grid_spec=grid_spec,
        out_shape=jax.ShapeDtypeStruct((TOP_K * t + MOE_ROW_ALIGN * N_EXPERTS, d), F32),
        compiler_params=pltpu.CompilerParams(dimension_semantics=("arbitrary",), vmem_limit_bytes=VMEM_LIMIT,
                                             has_side_effects=True),
        name="moe_ffn",
    )(block_e, n_valid, src, dst, h2, w_gate, w_up, w_down)


def _route(scores, router_bias):
    t = scores.shape[0]
    sel = (scores + router_bias.astype(F32)).reshape(t, N_GROUPS, EXPERTS_PER_GROUP)
    pairs = [sel[..., i] + sel[..., j] for i in range(EXPERTS_PER_GROUP) for j in range(i + 1, EXPERTS_PER_GROUP)]
    grp_score = functools.reduce(jnp.maximum, pairs)
    g_idx = jnp.argmax(grp_score, axis=-1).astype(jnp.int32)
    g_hot = (g_idx[:, None] == jnp.arange(N_GROUPS, dtype=jnp.int32)[None, :])[:, :, None]
    in_grp = jnp.sum(jnp.where(g_hot, sel, 0.0), axis=1)
    sc_grp = jnp.sum(jnp.where(g_hot, scores.reshape(t, N_GROUPS, EXPERTS_PER_GROUP), 0.0), axis=1)
    lane4 = jnp.arange(EXPERTS_PER_GROUP, dtype=jnp.int32)[None, :]
    i1 = jnp.argmax(in_grp, axis=-1).astype(jnp.int32)
    i2 = jnp.argmax(jnp.where(lane4 == i1[:, None], -jnp.inf, in_grp), axis=-1).astype(jnp.int32)
    e_idx = g_idx[:, None] * EXPERTS_PER_GROUP + jnp.stack([i1, i2], axis=-1)
    wts = jnp.stack([jnp.sum(jnp.where(lane4 == i1[:, None], sc_grp, 0.0), axis=-1),
                     jnp.sum(jnp.where(lane4 == i2[:, None], sc_grp, 0.0), axis=-1)], axis=-1)
    return e_idx, wts / jnp.sum(wts, axis=-1, keepdims=True)


def _moe(h2, scores, router_bias, w_gate, w_up, w_down):
    t, d = h2.shape
    e_idx, wts = _route(scores, router_bias)
    a = t * TOP_K
    flat_e = e_idx.T.reshape(-1)
    onehot = (flat_e[:, None] == jnp.arange(N_EXPERTS, dtype=jnp.int32)[None, :]).astype(jnp.int32)
    rb = MOE_ROWS
    blocks = onehot.reshape(a // rb, rb, N_EXPERTS).astype(BF16)
    within = jnp.einsum('ij,bje->bie', jnp.tril(jnp.ones((rb, rb), BF16)), blocks, preferred_element_type=F32)
    before = jnp.cumsum(within[:, -1, :], axis=0) - within[:, -1, :]
    csum = (within + before[:, None, :]).astype(jnp.int32).reshape(a, N_EXPERTS)
    counts = csum[-1]
    rank = jnp.sum(csum * onehot, axis=1) - 1
    padded = (counts + MOE_ROWS - 1) // MOE_ROWS * MOE_ROWS
    ends = jnp.cumsum(padded)
    pstarts = ends - padded
    dest = jnp.sum(pstarts[None, :] * onehot, axis=1) + rank
    n_blocks = -(-a // MOE_ROWS) + N_EXPERTS
    blk0 = jnp.arange(n_blocks, dtype=jnp.int32) * MOE_ROWS
    block_e = jnp.minimum(jnp.sum((ends[None, :] <= blk0[:, None]).astype(jnp.int32), axis=1), N_EXPERTS - 1)
    n_valid = jnp.clip((pstarts + counts)[block_e] - blk0, 0, MOE_ROWS)
    n_valid = ((n_valid + MOE_ROW_ALIGN - 1) // MOE_ROW_ALIGN * MOE_ROW_ALIGN).astype(jnp.int32)
    slot = jnp.arange(n_blocks * MOE_ROWS, dtype=jnp.int32)
    spare = a + MOE_ROW_ALIGN * jnp.repeat(block_e, MOE_ROWS) + slot % MOE_ROW_ALIGN
    dst = spare.at[dest].set(jnp.arange(a, dtype=jnp.int32))
    src = jnp.where(dst < a, dst % t, 0)
    return _moe_ffn(block_e, n_valid, src, dst, h2, w_gate, w_up, w_down), wts


def _combine_kernel(x_ref, y0_ref, y1_ref, w_ref, mod_ref, o_ref):
    w = w_ref[...]
    y = y0_ref[...] * w[:, 0:1] + y1_ref[...] * w[:, 1:2]
    o_ref[0] = x_ref[0] + mod_ref[0, 5:6, :] * y


def _combine(x, y, wts, mod, mod_is_shared, row_offset, tm):
    b, l, d = x.shape
    mod_map = (lambda i, j: (0, 0, 0)) if mod_is_shared else (lambda i, j: (i, 0, 0))
    off = row_offset // tm
    per = l // tm
    return pl.pallas_call(
        _combine_kernel,
        grid=(b, per),
        in_specs=[pl.BlockSpec((1, tm, d), lambda i, j: (i, j, 0)),
                  pl.BlockSpec((tm, d), lambda i, j: (off + i * per + j, 0)),
                  pl.BlockSpec((tm, d), lambda i, j: (wts.shape[0] // tm + off + i * per + j, 0)),
                  pl.BlockSpec((tm, TOP_K), lambda i, j: (off + i * per + j, 0)),
                  pl.BlockSpec((1, 6, d), mod_map)],
        out_specs=pl.BlockSpec((1, tm, d), lambda i, j: (i, j, 0)),
        out_shape=jax.ShapeDtypeStruct((b, l, d), F32),
        compiler_params=_cparams(2),
        name="moe_combine",
    )(x, y, y, wts, mod)


def _head_rms(x, gain, e_ref):
    ms = jnp.dot((x * x).astype(BF16), e_ref[...], preferred_element_type=F32) * (1.0 / HEAD_DIM)
    return x * lax.rsqrt(ms + EPS) * gain


def _rope(y, cos, sin):
    lane = lax.broadcasted_iota(jnp.int32, y.shape, 1)
    half = HEAD_DIM // 2
    rot = jnp.where((lane % HEAD_DIM) < half, pltpu.roll(y, LANES - half, 1), pltpu.roll(y, half, 1))
    return y * cos + rot * sin


def _attend(problems):
    all_scores = []
    for q, tiles, _ in problems:
        scores = []
        for k, _, bias in tiles:
            s = lax.dot_general(q, k, (((1,), (1,)), ((), ())), preferred_element_type=F32)
            scores.append(s if bias is None else s + bias)
        if all(s.shape[1] % LANES == 0 for s in scores):
            scores = [jnp.concatenate(scores, axis=1)]
        all_scores.append(scores)
    maxes = []
    for scores, (_, _, sink) in zip(all_scores, problems):
        m = sink
        for s in scores:
            mt = jnp.max(s, axis=-1, keepdims=True)
            m = mt if m is None else jnp.maximum(m, mt)
        maxes.append(m)
    all_probs, dens = [], []
    for scores, m, (_, tiles, sink) in zip(all_scores, maxes, problems):
        den = jnp.exp(sink - m) if sink is not None else jnp.zeros_like(m)
        probs = []
        for s in scores:
            p = jnp.exp(s - m)
            den = den + jnp.sum(p, axis=-1, keepdims=True)
            probs.append(p.astype(BF16))
        if len(probs) != len(tiles):
            offs = np.cumsum([0] + [v.shape[0] for _, v, _ in tiles])
            probs = [probs[0][:, offs[i]:offs[i + 1]] for i in range(len(tiles))]
        all_probs.append(probs)
        dens.append(den)
    outs = []
    for probs, den, (_, tiles, _) in zip(all_probs, dens, problems):
        o = None
        for p, (_, v, _) in zip(probs, tiles):
            pv = jnp.dot(p, v, preferred_element_type=F32)
            o = pv if o is None else o + pv
        outs.append(o * (1.0 / den))
    return outs


def _lane_half(shape):
    return lax.broadcasted_iota(jnp.int32, shape, 1) // HEAD_DIM


def _win_group_queries(slabs, g, sink_ref):
    r = WIN_HEADS // WIN_KV_HEADS
    rows = slabs[0].shape[0]
    half = _lane_half(slabs[0].shape)
    parts, sinks = [], []
    for j in range(r):
        head = g * r + j
        slab = slabs[head // 2]
        if head % 2 != g:
            slab = pltpu.roll(slab, HEAD_DIM, 1)
        parts.append(jnp.where(half == g, slab, 0.0))
        sinks.append(jnp.full((rows, 1), sink_ref[head], F32))
    return jnp.concatenate(parts, axis=0).astype(BF16), jnp.concatenate(sinks, axis=0)


def _win_store(o_ref, o, g, rows):
    r = WIN_HEADS // WIN_KV_HEADS
    half = _lane_half((rows, LANES))
    for pair in range(r // 2):
        a = o[(2 * pair) * rows:(2 * pair + 1) * rows]
        b = o[(2 * pair + 1) * rows:(2 * pair + 2) * rows]
        if g == 0:
            b = pltpu.roll(b, HEAD_DIM, 1)
        else:
            a = pltpu.roll(a, HEAD_DIM, 1)
        s = (g * r + 2 * pair) // 2
        o_ref[0, :, s * LANES:(s + 1) * LANES] = jnp.where(half == 0, a, b).astype(o_ref.dtype)


def _win_kernel(sink_ref, q_ref, kv_ref, ckv_ref, cos_ref, sin_ref, qg_ref, kg_ref, e_ref, o_ref, ks_ref, kcs_ref):
    n = pl.program_id(1)
    nb = pl.num_programs(1)
    l = kv_ref.shape[1]
    blk = WIN_BLOCK
    prep_rows = 512

    @pl.when(n == 0)
    def _():
        def body(i, carry):
            r0 = pl.multiple_of(i * prep_rows, prep_rows)
            y = _head_rms(kv_ref[0, pl.ds(r0, prep_rows), 0:LANES].astype(F32), kg_ref[...], e_ref)
            ks_ref[pl.ds(r0, prep_rows), :] = _rope(y, cos_ref[pl.ds(r0, prep_rows), :],
                                                    sin_ref[pl.ds(r0, prep_rows), :]).astype(BF16)
            return carry
        lax.fori_loop(0, l // prep_rows, body, 0)
        kcs_ref[...] = _head_rms(ckv_ref[0, :, 0:LANES].astype(F32), kg_ref[...], e_ref).astype(BF16)

    r0 = pl.multiple_of(n * blk, blk)
    cos = cos_ref[pl.ds(r0, blk), :]
    sin = sin_ref[pl.ds(r0, blk), :]
    slabs = []
    for s in range(q_ref.shape[-1] // LANES):
        y = _head_rms(q_ref[0, :, s * LANES:(s + 1) * LANES].astype(F32), qg_ref[...], e_ref)
        slabs.append(_rope(y, cos, sin))

    r = WIN_HEADS // WIN_KV_HEADS
    qi = lax.broadcasted_iota(jnp.int32, (r * blk, blk), 0) % blk
    kj = lax.broadcasted_iota(jnp.int32, (r * blk, blk), 1)
    ok_prev = (kj >= qi) & (n > 0)
    ok_next = (kj <= qi) & (n < nb - 1)
    bias_prev = jnp.where(ok_prev, 0.0, NEG_INF)
    bias_next = jnp.where(ok_next, 0.0, NEG_INF)
    tiles = []
    for kb, bias in ((jnp.maximum(n - 1, 0), bias_prev), (n, None), (jnp.minimum(n + 1, nb - 1), bias_next)):
        k0 = pl.multiple_of(kb * blk, blk)
        tiles.append((ks_ref[pl.ds(k0, blk), :], kv_ref[0, pl.ds(k0, blk), LANES:2 * LANES], bias))
    tiles.append((kcs_ref[...], ckv_ref[0, :, LANES:2 * LANES], None))
    rows = 2 * blk
    tiles = [(k, v, None if bias is None else bias[:rows]) for k, v, bias in tiles]
    for g in range(WIN_KV_HEADS):
        q, sink = _win_group_queries(slabs, g, sink_ref)
        outs = [_attend([(q[h * rows:(h + 1) * rows], tiles, sink[h * rows:(h + 1) * rows])])[0]
                for h in range(q.shape[0] // rows)]
        _win_store(o_ref, jnp.concatenate(outs, axis=0), g, blk)


def _win_ctx_kernel(sink_ref, q_ref, ckv_ref, qg_ref, kg_ref, e_ref, o_ref):
    kc = _head_rms(ckv_ref[0, :, 0:LANES].astype(F32), kg_ref[...], e_ref).astype(BF16)
    tiles = [(kc, ckv_ref[0, :, LANES:2 * LANES], None)]
    slabs = [_head_rms(q_ref[0, :, s * LANES:(s + 1) * LANES].astype(F32), qg_ref[...], e_ref)
             for s in range(q_ref.shape[-1] // LANES)]
    groups = [_win_group_queries(slabs, g, sink_ref) for g in range(WIN_KV_HEADS)]
    outs = _attend([(q, tiles, sink) for q, sink in groups])
    for g in range(WIN_KV_HEADS):
        _win_store(o_ref, outs[g], g, q_ref.shape[1])


def _block_diag_ones(width):
    i = np.arange(width) // HEAD_DIM
    return jnp.asarray(i[:, None] == i[None, :], BF16)


def _window_branch(zw_l, zw_c, q_gain, k_gain, sink, cos2, sin2, ctx_out):
    b, l, _ = zw_l.shape
    lc = zw_c.shape[1]
    wq = WIN_HEADS * HEAD_DIM
    qg = (jnp.tile(q_gain.astype(F32), 2) * HEAD_DIM ** -0.5).reshape(1, LANES)
    kg = jnp.tile(k_gain.astype(F32), 2).reshape(1, LANES)
    e = _block_diag_ones(LANES)
    smem = pl.BlockSpec(memory_space=pltpu.SMEM)
    full = lambda s, nd: pl.BlockSpec(s, lambda *a: (0,) * nd)
    yl = pl.pallas_call(
        _win_kernel,
        grid=(b, l // WIN_BLOCK),
        in_specs=[smem,
                  pl.BlockSpec((1, WIN_BLOCK, wq), lambda i, n: (i, n, 0)),
                  pl.BlockSpec((1, l, 2 * LANES), lambda i, n: (i, 0, wq // (2 * LANES))),
                  pl.BlockSpec((1, lc, 2 * LANES), lambda i, n: (i, 0, wq // (2 * LANES))),
                  full((l, LANES), 2), full((l, LANES), 2), full((1, LANES), 2), full((1, LANES), 2),
                  full((LANES, LANES), 2)],
        out_specs=pl.BlockSpec((1, WIN_BLOCK, wq), lambda i, n: (i, n, 0)),
        out_shape=jax.ShapeDtypeStruct((b, l, wq), BF16),
        scratch_shapes=[pltpu.VMEM((l, LANES), BF16), pltpu.VMEM((lc, LANES), BF16)],
        compiler_params=_cparams(2),
        name="window_attn",
    )(sink.astype(F32), zw_l, zw_l, zw_c, cos2, sin2, qg, kg, e)
    yc = None
    if ctx_out:
        yc = pl.pallas_call(
            _win_ctx_kernel,
            grid=(b,),
            in_specs=[smem,
                      pl.BlockSpec((1, lc, wq), lambda i: (i, 0, 0)),
                      pl.BlockSpec((1, lc, 2 * LANES), lambda i: (i, 0, wq // (2 * LANES))),
                      full((1, LANES), 2), full((1, LANES), 2), full((LANES, LANES), 2)],
            out_specs=pl.BlockSpec((1, lc, wq), lambda i: (i, 0, 0)),
            out_shape=jax.ShapeDtypeStruct((b, lc, wq), BF16),
            compiler_params=_cparams(1),
            name="window_ctx_attn",
        )(sink.astype(F32), zw_c, zw_c, qg, kg, e)
    return yc, yl


NA_ROWS = 8
NA_KROWS = NA_ROWS + NA_KH - 1


def _na_key_base(rg, rows):
    return jnp.clip(rg * NA_ROWS - NA_KH // 2, 0, rows - NA_KROWS)


def _na_kernel(q_ref, k_ref, v_ref, ck_ref, cv_ref, bias_ref, qg_ref, kg_ref, e_ref, o_ref, ks_ref, kcs_ref):
    rg = pl.program_id(1)
    l = k_ref.shape[1]
    rows = l // GRID_W
    n_slab = q_ref.shape[-1] // LANES
    prep_rows = 512

    @pl.when(rg == 0)
    def _():
        def body(i, carry):
            r0 = pl.multiple_of(i * prep_rows, prep_rows)
            ks_ref[pl.ds(r0, prep_rows), :] = _head_rms(k_ref[0, pl.ds(r0, prep_rows), :].astype(F32), kg_ref[...],
                                                        e_ref).astype(BF16)
            return carry
        lax.fori_loop(0, l // prep_rows, body, 0)
        kcs_ref[...] = _head_rms(ck_ref[0].astype(F32), kg_ref[...], e_ref).astype(BF16)

    nk = NA_KROWS * GRID_W
    k0 = pl.multiple_of(_na_key_base(rg, rows) * GRID_W, GRID_W)
    qn = _head_rms(q_ref[0].astype(F32), qg_ref[...], e_ref)
    half = _lane_half((q_ref.shape[1], LANES))
    for s in range(n_slab):
        cols = slice(s * LANES, (s + 1) * LANES)
        slab = qn[:, cols]
        tiles_kv = (ks_ref[pl.ds(k0, nk), cols], v_ref[0, pl.ds(k0, nk), cols])
        ctx_kv = (kcs_ref[:, cols], cv_ref[0, :, cols])
        outs = [_attend([(jnp.where(half == hh, slab, 0.0).astype(BF16),
                          [tiles_kv + (bias_ref[0, 2 * s + hh].astype(F32),), ctx_kv + (None,)], None)])[0]
                for hh in range(2)]
        o_ref[0, :, cols] = jnp.where(half == 0, outs[0], outs[1]).astype(o_ref.dtype)


def _na_ctx_kernel(q_ref, ck_ref, cv_ref, qg_ref, kg_ref, e_ref, o_ref):
    kc = _head_rms(ck_ref[0].astype(F32), kg_ref[...], e_ref).astype(BF16)
    qn = _head_rms(q_ref[0].astype(F32), qg_ref[...], e_ref)
    half = _lane_half((q_ref.shape[1], LANES))
    for s in range(q_ref.shape[-1] // LANES):
        cols = slice(s * LANES, (s + 1) * LANES)
        outs = _attend([(jnp.where(half == hh, qn[:, cols], 0.0).astype(BF16),
                         [(kc[:, cols], cv_ref[0, :, cols], None)], None) for hh in range(2)])
        o_ref[0, :, cols] = jnp.where(half == 0, outs[0], outs[1]).astype(o_ref.dtype)


def _na_bias_classes(rows):
    n_rg = rows // NA_ROWS
    return list(range(n_rg)) if n_rg <= 3 else [0, 1, n_rg - 1]


def _na_bias_table(rpb, rows):
    kh = NA_KH
    ro, rv = [], []
    for rg in _na_bias_classes(rows):
        kbase = int(np.clip(rg * NA_ROWS - kh // 2, 0, rows - NA_KROWS))
        r = rg * NA_ROWS + np.arange(NA_ROWS)
        rstart = np.clip(r - kh // 2, 0, rows - kh)
        kr = kbase + np.arange(NA_KROWS)
        rv.append((kr[None, :] >= rstart[:, None]) & (kr[None, :] < rstart[:, None] + kh))
        ro.append(np.clip(kr[None, :] - r[:, None] + kh - 1, 0, 2 * kh - 2))
    ro, rv = np.stack(ro), np.stack(rv)
    qc = np.arange(GRID_W)
    cstart = np.clip(qc - NA_KW // 2, 0, GRID_W - NA_KW)
    cv = (qc[None, :] >= cstart[:, None]) & (qc[None, :] < cstart[:, None] + NA_KW)
    co = np.clip(qc[None, :] - qc[:, None] + NA_KW - 1, 0, 2 * NA_KW - 2)
    n_cls = ro.shape[0]
    hp = lax.Precision.HIGHEST
    co_hot = jnp.asarray(co[None] == np.arange(2 * NA_KW - 1)[:, None, None], F32)
    ro_hot = jnp.asarray(ro[..., None] == np.arange(2 * kh - 1), F32)
    cols = jnp.einsum('hdc,cwx->hdwx', rpb.astype(F32), co_hot, precision=hp)
    tab = jnp.einsum('crkd,hdwx->chrwkx', ro_hot, cols, precision=hp)
    ok = rv[:, None, :, None, :, None] & cv[None, None, None, :, None, :]
    tab = jnp.where(ok, tab, NEG_INF)
    return tab.reshape(n_cls, rpb.shape[0], NA_ROWS * GRID_W, NA_KROWS * GRID_W).astype(BF16)


def _neighborhood_branch(zn_l, zn_c, q_gain, k_gain, rpb, ctx_out):
    b, l, _ = zn_l.shape
    lc = zn_c.shape[1]
    w = NA_HEADS * HEAD_DIM
    rows = l // GRID_W
    n_rg = rows // NA_ROWS
    n_slab = w // LANES
    qg = (jnp.tile(q_gain.astype(F32), 2 * n_slab) * HEAD_DIM ** -0.5).reshape(1, w)
    kg = jnp.tile(k_gain.astype(F32), 2 * n_slab).reshape(1, w)
    e = _block_diag_ones(w)
    bias = _na_bias_table(rpb, rows)
    if n_rg <= 3:
        cls_map = lambda i, r: (r, 0, 0, 0)
    else:
        cls_map = lambda i, r: ((r > 0).astype(jnp.int32) + (r == n_rg - 1).astype(jnp.int32), 0, 0, 0)
    tq = NA_ROWS * GRID_W
    full = lambda s, nd: pl.BlockSpec(s, lambda *a: (0,) * nd)
    yl = pl.pallas_call(
        _na_kernel,
        grid=(b, n_rg),
        in_specs=[pl.BlockSpec((1, tq, w), lambda i, r: (i, r, 0)),
                  pl.BlockSpec((1, l, w), lambda i, r: (i, 0, 1)),
                  pl.BlockSpec((1, l, w), lambda i, r: (i, 0, 2)),
                  pl.BlockSpec((1, lc, w), lambda i, r: (i, 0, 1)),
                  pl.BlockSpec((1, lc, w), lambda i, r: (i, 0, 2)),
                  pl.BlockSpec((1,) + bias.shape[1:], cls_map),
                  full((1, w), 2), full((1, w), 2), full((w, w), 2)],
        out_specs=pl.BlockSpec((1, tq, w), lambda i, r: (i, r, 0)),
        out_shape=jax.ShapeDtypeStruct((b, l, w), BF16),
        scratch_shapes=[pltpu.VMEM((l, w), BF16), pltpu.VMEM((lc, w), BF16)],
        compiler_params=_cparams(2),
        name="neighborhood_attn",
    )(zn_l, zn_l, zn_l, zn_c, zn_c, bias, qg, kg, e)
    yc = None
    if ctx_out:
        yc = pl.pallas_call(
            _na_ctx_kernel,
            grid=(b,),
            in_specs=[pl.BlockSpec((1, lc, w), lambda i: (i, 0, 0)),
                      pl.BlockSpec((1, lc, w), lambda i: (i, 0, 1)),
                      pl.BlockSpec((1, lc, w), lambda i: (i, 0, 2)),
                      full((1, w), 2), full((1, w), 2), full((w, w), 2)],
            out_specs=pl.BlockSpec((1, lc, w), lambda i: (i, 0, 0)),
            out_shape=jax.ShapeDtypeStruct((b, lc, w), BF16),
            compiler_params=_cparams(1),
            name="neighborhood_ctx_attn",
        )(zn_c, zn_c, zn_c, qg, kg, e)
    return yc, yl


def _rope_tables(n_tokens):
    t = np.arange(n_tokens)
    n_freq = HEAD_DIM // 4
    inv = ROPE_THETA ** (-jnp.arange(n_freq, dtype=F32) / n_freq)
    ang = jnp.concatenate([jnp.asarray(t // GRID_W, F32)[:, None] * inv, jnp.asarray(t % GRID_W, F32)[:, None] * inv],
                          axis=-1)
    cos, sin = jnp.cos(ang), jnp.sin(ang)
    return jnp.tile(jnp.concatenate([cos, cos], -1), (1, 2)), jnp.tile(jnp.concatenate([-sin, sin], -1), (1, 2))


def _ret_kernel(qq_l, kk_l, v_l, g_l, qq_c, kk_c, v_c, g_c, cos_ref, sin_ref, dmask_ref, tq_ref, tk_ref, cdec_ref,
                gain_ref, yl_ref, yc_ref, kr_ref, kv_ref, sin_state_ref, *, ctx_out):
    c = RET_CHUNK
    ncc = qq_c.shape[1] // c
    ncl = qq_l.shape[1] // c
    nc = ncc + ncl
    tk = tk_ref[0]
    tq = tq_ref[0]
    dmask = dmask_ref[0]
    fwd_lanes = lax.broadcasted_iota(jnp.int32, (c, LANES), 1) < RET_DK

    def chunk_kv(k2, v):
        kd = (k2 * tk).astype(BF16)
        return lax.dot_general(kd, v, (((0,), (0,)), ((), ())), preferred_element_type=F32)

    for n in range(ncc):
        kv_ref[n] = chunk_kv(kk_c[0, n * c:(n + 1) * c, :].astype(F32), v_c[0, n * c:(n + 1) * c, :])

    def kv_body(n, carry):
        r0 = pl.multiple_of(n * c, c)
        k2 = _rope(kk_l[0, pl.ds(r0, c), :].astype(F32), cos_ref[pl.ds(r0, c), :], sin_ref[pl.ds(r0, c), :])
        kr_ref[pl.ds(r0, c), :] = k2.astype(BF16)
        kv_ref[ncc + n] = chunk_kv(k2, v_l[0, pl.ds(r0, c), :])
        return carry
    lax.fori_loop(0, ncl, kv_body, 0, unroll=4)

    dec_f = cdec_ref[0, 0:1, :]
    dec_b = cdec_ref[0, 1:2, :]

    def scan_body(t, carry):
        sf, sb = carry
        sin_state_ref[t, 0:RET_DK, :] = sf.astype(BF16)
        sf = sf * dec_f + kv_ref[t, 0:RET_DK, :]
        tb = jnp.where(t < ncc, ncc - 1 - t, nc - 1 - (t - ncc))
        sin_state_ref[tb, RET_DK:2 * RET_DK, :] = sb.astype(BF16)
        sb = sb * dec_b + kv_ref[tb, RET_DK:2 * RET_DK, :]
        return sf, sb
    zero = jnp.zeros((RET_DK, RET_DV), F32)
    lax.fori_loop(0, nc, scan_body, (zero, zero))

    def chunk_out(q2, k2b, v, gate, state):
        qm = jnp.where(fwd_lanes, q2, 0.0).astype(BF16)
        scores = lax.dot_general(qm, k2b, (((1,), (1,)), ((), ())), preferred_element_type=F32) * dmask
        o = (jnp.dot(scores.astype(BF16), v, preferred_element_type=F32)
             + jnp.dot((q2 * tq).astype(BF16), state, preferred_element_type=F32))
        mu = jnp.mean(o, axis=-1, keepdims=True)
        var = jnp.mean(jnp.square(o - mu), axis=-1, keepdims=True)
        y = (o - mu) * lax.rsqrt(var + EPS) * gain_ref[0]
        gf = gate.astype(F32)
        return y * gf * jax.nn.sigmoid(gf)

    if ctx_out:
        for n in range(ncc):
            rows = slice(n * c, (n + 1) * c)
            yc_ref[0, rows, :] = chunk_out(qq_c[0, rows, :].astype(F32), kk_c[0, rows, :], v_c[0, rows, :],
                                           g_c[0, rows, :], sin_state_ref[n]).astype(yc_ref.dtype)
    else:
        yc_ref[...] = jnp.zeros_like(yc_ref)

    def out_body(n, carry):
        r0 = pl.multiple_of(n * c, c)
        q2 = _rope(qq_l[0, pl.ds(r0, c), :].astype(F32), cos_ref[pl.ds(r0, c), :], sin_ref[pl.ds(r0, c), :])
        yl_ref[0, pl.ds(r0, c), :] = chunk_out(q2, kr_ref[pl.ds(r0, c), :], v_l[0, pl.ds(r0, c), :],
                                               g_l[0, pl.ds(r0, c), :], sin_state_ref[ncc + n]).astype(yl_ref.dtype)
        return carry
    lax.fori_loop(0, ncl, out_body, 0, unroll=4)


def _ret_tables(decay_logit):
    c = RET_CHUNK
    lg = jax.nn.log_sigmoid(decay_logit.astype(F32))
    lf, lb = lg[0][:, None, None], lg[1][:, None, None]
    pos = jnp.arange(c, dtype=F32)
    diff = pos[:, None] - pos[None, :]
    dmask = (jnp.where(diff >= 0, jnp.exp(lf * jnp.maximum(diff, 0.0)), 0.0)
             + jnp.where(diff <= 0, jnp.exp(lb * jnp.maximum(-diff, 0.0)), 0.0)) * RET_DK ** -0.5
    col = lambda a, b_: jnp.concatenate([jnp.broadcast_to(a, a.shape[:2] + (RET_DK,)),
                                         jnp.broadcast_to(b_, b_.shape[:2] + (RET_DK,))], axis=-1)
    p = pos[None, :, None]
    tq = col(jnp.exp(lf * (p + 1.0)), jnp.exp(lb * (c - p)))
    tk = col(jnp.exp(lf * (c - 1.0 - p)), jnp.exp(lb * p)) * RET_DK ** -0.5
    cdec = jnp.zeros((lg.shape[1], 8, RET_DV), F32)
    cdec = cdec.at[:, 0, :].set(jnp.exp(lg[0] * c)[:, None]).at[:, 1, :].set(jnp.exp(lg[1] * c)[:, None])
    return dmask, tq, tk, cdec


def _retention_branch(zr_l, zr_c, decay_logit, gn_gain, cos2, sin2, ctx_out):
    b, l, _ = zr_l.shape
    lc = zr_c.shape[1]
    h = RET_HEADS
    nc = (l + lc) // RET_CHUNK
    dmask, tq, tk, cdec = _ret_tables(decay_logit)
    gain = gn_gain.astype(F32).reshape(h, 1, RET_DV)
    seq = lambda n, j: pl.BlockSpec((1, n, LANES), lambda i, hh: (i, 0, 4 * hh + j))
    head = lambda s: pl.BlockSpec((1,) + s, lambda i, hh: (hh, 0, 0))
    full = lambda s: pl.BlockSpec(s, lambda i, hh: (0, 0))
    yl, yc = pl.pallas_call(
        functools.partial(_ret_kernel, ctx_out=ctx_out),
        grid=(b, h),
        in_specs=[seq(l, 0), seq(l, 1), seq(l, 2), seq(l, 3), seq(lc, 0), seq(lc, 1), seq(lc, 2), seq(lc, 3),
                  full((l, LANES)), full((l, LANES)),
                  head((RET_CHUNK, RET_CHUNK)), head((RET_CHUNK, LANES)), head((RET_CHUNK, LANES)), head((8, RET_DV)),
                  head((1, RET_DV))],
        out_specs=[pl.BlockSpec((1, l, RET_DV), lambda i, hh: (i, 0, hh)),
                   pl.BlockSpec((1, lc, RET_DV), lambda i, hh: (i, 0, hh))],
        out_shape=[jax.ShapeDtypeStruct((b, l, h * RET_DV), BF16), jax.ShapeDtypeStruct((b, lc, h * RET_DV), BF16)],
        scratch_shapes=[pltpu.VMEM((l, LANES), BF16), pltpu.VMEM((nc, 2 * RET_DK, RET_DV), F32),
                        pltpu.VMEM((nc, 2 * RET_DK, RET_DV), BF16)],
        compiler_params=_cparams(2),
        name="retention",
    )(zr_l, zr_l, zr_l, zr_l, zr_c, zr_c, zr_c, zr_c, cos2, sin2, dmask, tq, tk, cdec, gain)
    return (yc if ctx_out else None), yl


GDN_SUPER = 128
GDN_UNITS = 8
GDN_HALO = 128


def _mask_dot_col(mask_bf16, col):
    hi = col.astype(BF16).astype(F32)
    low_half = lax.broadcasted_iota(jnp.int32, (col.shape[0], LANES), 1) < LANES // 2
    r = jnp.dot(mask_bf16, jnp.where(low_half, hi, col - hi).astype(BF16), preferred_element_type=F32)
    return r + pltpu.roll(r, LANES // 2, 1)


def _softplus(x):
    return jnp.maximum(x, 0.0) + jnp.log(1.0 + jnp.exp(-jnp.abs(x)))


def _gdn_kernel(nega_ref, dtb_ref, q_l, k_l, v_l, g_l, ab_l, q_c, k_c, v_c, g_c, ab_c,
                cwq_ref, cwk_ref, cwv_ref, pd_ref, pu_ref, gain_ref, yl_ref, yc_ref,
                kn_s, sin_s, qp_s, o0_s, cd_s, *, ctx_out):
    hd = pl.program_id(1)
    c = GDN_CHUNK
    sup = GDN_SUPER
    per = sup // c
    lc, l = q_c.shape[1], q_l.shape[1]
    ncc, ncl = lc // c, l // c
    nc = ncc + ncl

    ri = lax.broadcasted_iota(jnp.int32, (sup, sup), 0)
    ci = lax.broadcasted_iota(jnp.int32, (sup, sup), 1)
    same = (ri // c) == (ci // c)
    eye = (ri == ci).astype(F32)
    incl = (same & (ri >= ci), same & (ri <= ci))
    strict = (same & (ri > ci), same & (ri < ci))
    incl_b = tuple(m.astype(BF16) for m in incl)
    same_b = same.astype(BF16)
    lane = lax.broadcasted_iota(jnp.int32, (sup, LANES), 1)
    rowc = lax.broadcasted_iota(jnp.int32, (sup, LANES), 0)

    def conv_silu(z_ref, w_ref, r0, ls):
        z = z_ref[0, pl.ds(r0, sup), :]
        if isinstance(r0, int):
            zero = jnp.zeros((GDN_HALO, LANES), BF16)
            prev = z_ref[0, r0 - GDN_HALO:r0, :] if r0 > 0 else zero
            nxt = z_ref[0, r0 + sup:r0 + sup + GDN_HALO, :] if r0 + sup < ls else zero
        else:
            p0 = pl.multiple_of(jnp.maximum(r0 - GDN_HALO, 0), GDN_HALO)
            n0 = pl.multiple_of(jnp.minimum(r0 + sup, ls - GDN_HALO), GDN_HALO)
            prev = jnp.where(r0 > 0, z_ref[0, pl.ds(p0, GDN_HALO), :], jnp.zeros((), BF16))
            nxt = jnp.where(r0 + sup < ls, z_ref[0, pl.ds(n0, GDN_HALO), :], jnp.zeros((), BF16))
        win = jnp.concatenate([prev, z, nxt], axis=0)
        z_dn = jnp.dot(pd_ref[...], win, preferred_element_type=F32)
        z_up = jnp.dot(pu_ref[...], win, preferred_element_type=F32)
        y = z_dn * w_ref[0, 0:1, :] + z.astype(F32) * w_ref[0, 1:2, :] + z_up * w_ref[0, 2:3, :]
        return y * jax.nn.sigmoid(y)

    def prep(units):
        common = []
        for refs, r0, base, ls in units:
            q_ref, k_ref, v_ref, ab_ref = refs
            q = conv_silu(q_ref, cwq_ref, r0, ls)
            k = conv_silu(k_ref, cwk_ref, r0, ls)
            v = conv_silu(v_ref, cwv_ref, r0, ls)
            q = q * lax.rsqrt(jnp.sum(q * q, axis=-1, keepdims=True) + EPS) * GDN_DK ** -0.5
            k = k * lax.rsqrt(jnp.sum(k * k, axis=-1, keepdims=True) + EPS)
            kb16 = k.astype(BF16)
            qk = lax.dot_general(q.astype(BF16), kb16, (((1,), (1,)), ((), ())), preferred_element_type=F32)
            ab = ab_ref[0, pl.ds(r0, sup), :]
            common.append((q, k, v, kb16, qk, ab, base + r0))
        chains = []
        for q, k, v, kb16, qk, ab, row0 in common:
            for d in range(2):
                ia = d * GDN_HEADS + hd
                ib = 2 * GDN_HEADS + ia
                a_col = jnp.sum(jnp.where(lane == ia, ab, 0.0), axis=1, keepdims=True)
                b_col = jnp.sum(jnp.where(lane == ib, ab, 0.0), axis=1, keepdims=True)
                g_col = nega_ref[d, hd] * _softplus(a_col + dtb_ref[d, hd])
                beta = jax.nn.sigmoid(b_col)
                gcum = _mask_dot_col(incl_b[d], g_col)
                gtot = _mask_dot_col(same_b, g_col)
                gc = gcum[:, 0:1]
                decay = jnp.where(incl[d], jnp.exp(jnp.where(incl[d], gcum - gcum.T, 0.0)), 0.0)
                kbeta = k * beta
                kk = lax.dot_general(kbeta.astype(BF16), kb16, (((1,), (1,)), ((), ())),
                                     preferred_element_type=F32)
                a_mat = jnp.where(strict[d], kk * decay, 0.0)
                eg = jnp.exp(gc)
                rhs = jnp.concatenate([kbeta * eg, v * beta], axis=1).astype(BF16)
                aqk = jnp.where(incl[d], qk * decay, 0.0).astype(BF16)
                ke = (k * jnp.exp(gtot[:, 0:1] - gc)).astype(BF16)
                cds = [jnp.exp(gtot[j * c:j * c + 1, :]) for j in range(per)]
                chains.append(dict(d=d, row0=row0, a=a_mat, rhs=rhs, aqk=aqk, ke=ke, cds=cds, qeg=q * eg))
        for ch in chains:
            ch['inv'] = eye - ch['a']
            ch['pw'] = ch['a'].astype(BF16)
        for _ in range(5):
            for ch in chains:
                ch['pw'] = jnp.dot(ch['pw'], ch['pw'], preferred_element_type=F32).astype(BF16)
            for ch in chains:
                ch['inv'] = ch['inv'] + jnp.dot(ch['inv'].astype(BF16), ch['pw'], preferred_element_type=F32)
        for ch in chains:
            ch['wu'] = jnp.dot(ch['inv'].astype(BF16), ch['rhs'], preferred_element_type=F32).astype(BF16)
        for ch in chains:
            d, row0 = ch['d'], ch['row0']
            awu = jnp.dot(ch['aqk'], ch['wu'], preferred_element_type=F32)
            rows = pl.ds(row0, sup)
            qp_s[d, rows, :] = (ch['qeg'] - awu[:, :GDN_DK]).astype(BF16)
            o0_s[d, rows, :] = awu[:, GDN_DK:]
            for j in range(per):
                kej = jnp.where((rowc // c) == j, ch['ke'], jnp.zeros((), BF16))
                idx = row0 // c + j
                kn_s[d, idx] = lax.dot_general(kej, ch['wu'], (((0,), (0,)), ((), ())),
                                               preferred_element_type=F32).astype(BF16)
                cd_s[d, idx] = ch['cds'][j]

    prep([((q_c, k_c, v_c, ab_c), n * sup, 0, lc) for n in range(lc // sup)])

    def prep_body(n, carry):
        prep([((q_l, k_l, v_l, ab_l), pl.multiple_of((GDN_UNITS * n + j) * sup, sup), lc, l)
              for j in range(GDN_UNITS)])
        return carry
    lax.fori_loop(0, l // (GDN_UNITS * sup), prep_body, 0)

    def chunk_step(d, idx, s):
        sb = s.astype(BF16)
        sin_s[d, idx] = sb
        kn = kn_s[d, idx]
        return (s * cd_s[d, idx] - jnp.dot(kn[:, :GDN_DK], sb, preferred_element_type=F32)
                + kn[:, GDN_DK:].astype(F32))

    def scan_body(t, carry):
        sf, sb = carry
        tb = jnp.where(t < ncc, ncc - 1 - t, nc - 1 - (t - ncc))
        return chunk_step(0, t, sf), chunk_step(1, tb, sb)
    zero = jnp.zeros((GDN_DK, GDN_DV), F32)
    lax.fori_loop(0, nc, scan_body, (zero, zero))

    def finish(y_ref, gate_ref, base, n_rows):
        def body(n, carry):
            r0 = pl.multiple_of(n * c, c)
            rows = pl.ds(base + r0, c)
            idx = base // c + n
            o = (o0_s[0, rows, :] + o0_s[1, rows, :]
                 + jnp.dot(qp_s[0, rows, :], sin_s[0, idx], preferred_element_type=F32)
                 + jnp.dot(qp_s[1, rows, :], sin_s[1, idx], preferred_element_type=F32))
            y = o * lax.rsqrt(jnp.mean(o * o, axis=-1, keepdims=True) + EPS) * gain_ref[...]
            gf = gate_ref[0, pl.ds(r0, c), :].astype(F32)
            y_ref[0, pl.ds(r0, c), :] = (y * gf * jax.nn.sigmoid(gf)).astype(y_ref.dtype)
            return carry
        lax.fori_loop(0, n_rows // c, body, 0, unroll=4)

    finish(yl_ref, g_l, lc, l)
    if ctx_out:
        finish(yc_ref, g_c, 0, lc)
    else:
        yc_ref[...] = jnp.zeros_like(yc_ref)


def _gdn_branch(zg_l, zab_l, zg_c, zab_c, conv_w, a_log, dt_bias, norm_gain, ctx_out):
    b, l, _ = zg_l.shape
    lc = zg_c.shape[1]
    h = GDN_HEADS
    sup = GDN_SUPER
    ltot = l + lc
    nc = ltot // GDN_CHUNK
    neg_a = -jnp.exp(a_log.astype(F32))
    cw = conv_w.astype(F32).T.reshape(3 * h, LANES, SHORT_CONV).transpose(0, 2, 1)
    win = sup + 2 * GDN_HALO
    i = np.arange(sup)
    pd = np.zeros((sup, win), np.float32)
    pu = np.zeros((sup, win), np.float32)
    pd[i, GDN_HALO + i - 1] = 1.0
    pu[i, GDN_HALO + i + 1] = 1.0
    seq = lambda n, j: pl.BlockSpec((1, n, LANES), lambda bi, hh: (bi, 0, j * h + hh))
    abs_ = lambda n: pl.BlockSpec((1, n, LANES), lambda bi, hh: (bi, 0, 0))
    cws = lambda j: pl.BlockSpec((1, SHORT_CONV, LANES), lambda bi, hh: (j * h + hh, 0, 0))
    full = lambda s: pl.BlockSpec(s, lambda bi, hh: (0, 0))
    smem = pl.BlockSpec(memory_space=pltpu.SMEM)
    yl, yc = pl.pallas_call(
        functools.partial(_gdn_kernel, ctx_out=ctx_out),
        grid=(b, h),
        in_specs=[smem, smem, seq(l, 0), seq(l, 1), seq(l, 2), seq(l, 3), abs_(l),
                  seq(lc, 0), seq(lc, 1), seq(lc, 2), seq(lc, 3), abs_(lc),
                  cws(0), cws(1), cws(2), full((sup, win)), full((sup, win)), full((1, GDN_DV))],
        out_specs=[pl.BlockSpec((1, l, GDN_DV), lambda bi, hh: (bi, 0, hh)),
                   pl.BlockSpec((1, lc, GDN_DV), lambda bi, hh: (bi, 0, hh))],
        out_shape=[jax.ShapeDtypeStruct((b, l, h * GDN_DV), BF16), jax.ShapeDtypeStruct((b, lc, h * GDN_DV), BF16)],
        scratch_shapes=[pltpu.VMEM((2, nc, GDN_DK, GDN_DK + GDN_DV), BF16), pltpu.VMEM((2, nc, GDN_DK, GDN_DV), BF16),
                        pltpu.VMEM((2, ltot, GDN_DK), BF16), pltpu.VMEM((2, ltot, GDN_DV), F32),
                        pltpu.VMEM((2, nc, 1, GDN_DV), F32)],
        compiler_params=_cparams(2),
        name="gated_deltanet",
    )(neg_a, dt_bias.astype(F32), zg_l, zg_l, zg_l, zg_l, zab_l, zg_c, zg_c, zg_c, zg_c, zab_c,
      cw, cw, cw, jnp.asarray(pd, BF16), jnp.asarray(pu, BF16), norm_gain.astype(F32).reshape(1, GDN_DV))
    return (yc if ctx_out else None), yl


def kernel(x, c, ctx, c_ctx, w_mod, b_mod, norm1, norm2, w_in, ret_decay, ret_gn, win_qnorm, win_knorm, win_sink,
           na_qnorm, na_knorm, na_rpb, gdn_conv, gdn_a_log, gdn_dt_bias, gdn_norm, w_branch, w_merge, w_out,
           w_router, router_bias, w_e_gate, w_e_up, w_e_down):
    b, l, d = x.shape
    lc = ctx.shape[1]
    depth = w_mod.shape[0]
    cos2, sin2 = _rope_tables(l)

    n_rows = 16
    cc = jnp.zeros((n_rows, d), F32).at[:b].set(c).at[b].set(c_ctx)
    mod = _modulation(cc, w_mod, b_mod).reshape(depth, n_rows, 6, d)

    wr = jnp.zeros((d, LANES), F32).at[:, :N_EXPERTS].set(w_router)
    wr_hi = wr.astype(BF16)
    wr_lo = (wr - wr_hi.astype(F32)).astype(BF16)

    xl, xc = x, ctx
    for layer in range(depth):
        ctx_out = layer < depth - 1
        mod_l = mod[layer, :b]
        mod_c = mod[layer, b:b + 1]
        w_all = _pack_w_in(w_in[layer])
        hl, *zl_s = _inproj(xl, mod_l, False, norm1[layer], w_all, min(l, TOKEN_TILE))
        hc, *zc_s = _inproj(xc, mod_c, True, norm1[layer], w_all, min(lc, TOKEN_TILE))
        zr_l, zw_l, zn_l, zg_l, zab_l = zl_s
        zr_c, zw_c, zn_c, zg_c, zab_c = zc_s
        ret_c, ret_l = _retention_branch(zr_l, zr_c, ret_decay[layer], ret_gn[layer], cos2, sin2, ctx_out)
        win_c, win_l = _window_branch(zw_l, zw_c, win_qnorm[layer], win_knorm[layer], win_sink[layer], cos2, sin2,
                                      ctx_out)
        na_c, na_l = _neighborhood_branch(zn_l, zn_c, na_qnorm[layer], na_knorm[layer], na_rpb[layer], ctx_out)
        gdn_c, gdn_l = _gdn_branch(zg_l, zab_l, zg_c, zab_c, gdn_conv[layer], gdn_a_log[layer], gdn_dt_bias[layer],
                                   gdn_norm[layer], ctx_out)
        wm = w_merge[layer].astype(BF16)
        wb = w_branch[layer].astype(BF16)
        wo = w_out[layer].astype(BF16)
        ys_l = [ret_l, win_l, na_l, gdn_l]
        xl, h2l, sc_l = _merge(xl, hl, ys_l, mod_l, False, norm2[layer], wm, wb, wo, wr_hi, wr_lo, MERGE_TILE)
        if ctx_out:
            ys_c = [ret_c, win_c, na_c, gdn_c]
            xc, h2c, sc_c = _merge(xc, hc, ys_c, mod_c, True, norm2[layer], wm, wb, wo, wr_hi, wr_lo, MERGE_TILE)
            tokens = jnp.concatenate([h2c.reshape(b * lc, d), h2l.reshape(b * l, d)], axis=0)
            scores = jnp.concatenate([sc_c.reshape(b * lc, LANES), sc_l.reshape(b * l, LANES)], axis=0)
            y, wts = _moe(tokens, scores[:, :N_EXPERTS], router_bias, w_e_gate[layer], w_e_up[layer], w_e_down[layer])
            xc = _combine(xc, y, wts, mod_c, True, 0, MERGE_TILE)
            xl = _combine(xl, y, wts, mod_l, False, b * lc, MERGE_TILE)
        else:
            y, wts = _moe(h2l.reshape(b * l, d), sc_l.reshape(b * l, LANES)[:, :N_EXPERTS], router_bias,
                          w_e_gate[layer], w_e_up[layer], w_e_down[layer])
            xl = _combine(xl, y, wts, mod_l, False, 0, MERGE_TILE)
    return xl
```

```python
import functools

import numpy as np
import jax
import jax.numpy as jnp
from jax import lax
from jax.experimental import pallas as pl
from jax.experimental.pallas import tpu as pltpu

F32 = jnp.float32
BF16 = jnp.bfloat16
EPS = 1e-6
NEG_INF = -1e30
D_MODEL = 1024
GRID_W = 64
HEAD_DIM = 64
ROPE_THETA = 10000.0
RET_HEADS, RET_DK, RET_DV, RET_CHUNK = 4, 64, 128, 128
WIN_HEADS, WIN_KV_HEADS, WINDOW, WIN_BLOCK = 8, 2, 128, 128
NA_HEADS, NA_KH, NA_KW, NA_QCOLS = 8, 8, 16, 16
NA_BAND = NA_QCOLS + NA_KW
GDN_HEADS, GDN_DK, GDN_DV, GDN_CHUNK, SHORT_CONV = 4, 128, 128, 64, 3
GDN_QKV = 2 * GDN_HEADS * GDN_DK + GDN_HEADS * GDN_DV
N_BRANCH, BRANCH_W = 4, 512
N_EXPERTS, N_GROUPS, TOP_K, D_EXPERT = 32, 8, 2, 512
EXPERTS_PER_GROUP = N_EXPERTS // N_GROUPS

LANES = 128
VMEM_LIMIT = 56 * 1024 * 1024
TOKEN_TILE = 512
MERGE_TILE = 256
MOE_ROWS = 256
MOE_ROW_ALIGN = 8

W_RET = RET_HEADS * 4 * LANES
W_WIN = (WIN_HEADS + 2 * WIN_KV_HEADS) * HEAD_DIM
W_NA = 3 * NA_HEADS * HEAD_DIM
W_GDN = GDN_QKV + GDN_HEADS * GDN_DV
W_AB = LANES
SECTION_WIDTHS = (W_RET, W_WIN, W_NA, W_GDN, W_AB)
W_ALL = sum(SECTION_WIDTHS)


def _cparams(n_axes):
    return pltpu.CompilerParams(dimension_semantics=("arbitrary",) * n_axes, vmem_limit_bytes=VMEM_LIMIT)


def _mod_kernel(c_ref, w_ref, b_ref, o_ref):
    c = c_ref[...]
    a = (c * jax.nn.sigmoid(c)).astype(BF16)
    o_ref[0] = jnp.dot(a, w_ref[0].astype(BF16), preferred_element_type=F32) + b_ref[0]


def _modulation(cc, w_mod, b_mod):
    depth, d, n = w_mod.shape
    r = cc.shape[0]
    tn = 1536
    return pl.pallas_call(
        _mod_kernel,
        grid=(depth, n // tn),
        in_specs=[pl.BlockSpec((r, d), lambda l, j: (0, 0)),
                  pl.BlockSpec((1, d, tn), lambda l, j: (l, 0, j)),
                  pl.BlockSpec((1, 1, tn), lambda l, j: (l, 0, j))],
        out_specs=pl.BlockSpec((1, r, tn), lambda l, j: (l, 0, j)),
        out_shape=jax.ShapeDtypeStruct((depth, r, n), F32),
        compiler_params=_cparams(2),
        name="modulation",
    )(cc, w_mod, b_mod.reshape(depth, 1, n))


def _inproj_kernel(x_ref, mod_ref, gain_ref, w_ref, h_ref, *z_refs):
    x = x_ref[0]
    ms = jnp.mean(x * x, axis=-1, keepdims=True)
    shift = mod_ref[0, 0:1, :]
    scale = mod_ref[0, 1:2, :]
    h = x * lax.rsqrt(ms + EPS) * gain_ref[...] * (1.0 + scale) + shift
    hb = h.astype(BF16)
    h_ref[0] = hb
    off = 0
    for ref in z_refs:
        width = ref.shape[-1]
        for c0 in range(0, width, 512):
            c1 = min(c0 + 512, width)
            z = jnp.dot(hb, w_ref[:, off + c0:off + c1], preferred_element_type=F32)
            ref[0, :, c0:c1] = z.astype(ref.dtype)
        off += width


def _inproj(x, mod, mod_is_shared, gain, w_all, tm):
    b, l, d = x.shape
    mod_map = (lambda i, j: (0, 0, 0)) if mod_is_shared else (lambda i, j: (i, 0, 0))
    dtypes = (BF16, BF16, BF16, BF16, F32)
    out_shape = [jax.ShapeDtypeStruct((b, l, d), BF16)]
    out_specs = [pl.BlockSpec((1, tm, d), lambda i, j: (i, j, 0))]
    for w, dt in zip(SECTION_WIDTHS, dtypes):
        out_shape.append(jax.ShapeDtypeStruct((b, l, w), dt))
        out_specs.append(pl.BlockSpec((1, tm, w), lambda i, j: (i, j, 0)))
    return pl.pallas_call(
        _inproj_kernel,
        grid=(b, l // tm),
        in_specs=[pl.BlockSpec((1, tm, d), lambda i, j: (i, j, 0)),
                  pl.BlockSpec((1, 6, d), mod_map),
                  pl.BlockSpec((1, d), lambda i, j: (0, 0)),
                  pl.BlockSpec((d, W_ALL), lambda i, j: (0, 0), pipeline_mode=pl.Buffered(1))],
        out_specs=out_specs,
        out_shape=out_shape,
        compiler_params=_cparams(2),
        name="inproj",
    )(x, mod, gain.reshape(1, d), w_all)


def _pack_w_in(w_in):
    d = w_in.shape[0]
    hq, hv = RET_HEADS * RET_DK, RET_HEADS * RET_DV
    cols = []
    for h in range(RET_HEADS):
        q = w_in[:, h * RET_DK:(h + 1) * RET_DK]
        k = w_in[:, hq + h * RET_DK:hq + (h + 1) * RET_DK]
        cols += [q, q, k, k, w_in[:, 2 * hq + h * RET_DV:2 * hq + (h + 1) * RET_DV],
                 w_in[:, 2 * hq + hv + h * RET_DV:2 * hq + hv + (h + 1) * RET_DV]]
    rest = w_in[:, 2 * hq + 2 * hv:]
    pad = jnp.zeros((d, W_ALL - W_RET - rest.shape[1]), w_in.dtype)
    return jnp.concatenate(cols + [rest, pad], axis=1).astype(BF16)


def _merge_kernel(x_ref, h_ref, y0_ref, y1_ref, y2_ref, y3_ref, mod_ref, gain_ref, wm_ref, wb_ref, wo_ref,
                  wrh_ref, wrl_ref, xo_ref, h2_ref, sc_ref):
    d = x_ref.shape[-1]
    h = h_ref[0]
    acc = jnp.zeros(x_ref.shape[1:], F32)
    for i, y_ref in enumerate((y0_ref, y1_ref, y2_ref, y3_ref)):
        gate = jax.nn.sigmoid(jnp.dot(h, wm_ref[:, i * d:(i + 1) * d], preferred_element_type=F32))
        acc = acc + gate * jnp.dot(y_ref[0], wb_ref[i], preferred_element_type=F32)
    m = jnp.dot(acc.astype(BF16), wo_ref[...], preferred_element_type=F32)
    xn = x_ref[0] + mod_ref[0, 2:3, :] * m
    xo_ref[0] = xn
    ms = jnp.mean(xn * xn, axis=-1, keepdims=True)
    h2 = xn * lax.rsqrt(ms + EPS) * gain_ref[...] * (1.0 + mod_ref[0, 4:5, :]) + mod_ref[0, 3:4, :]
    hi = h2.astype(BF16)
    h2_ref[0] = h2
    lo = (h2 - hi.astype(F32)).astype(BF16)
    logits = (jnp.dot(hi, wrh_ref[...], preferred_element_type=F32)
              + jnp.dot(lo, wrh_ref[...], preferred_element_type=F32)
              + jnp.dot(hi, wrl_ref[...], preferred_element_type=F32))
    sc_ref[0] = jax.nn.sigmoid(logits)


def _merge(x, h, ys, mod, mod_is_shared, gain2, wm, wb, wo, wr_hi, wr_lo, tm):
    b, l, d = x.shape
    mod_map = (lambda i, j: (0, 0, 0)) if mod_is_shared else (lambda i, j: (i, 0, 0))
    tok = lambda w: pl.BlockSpec((1, tm, w), lambda i, j: (i, j, 0))
    full2 = lambda s: pl.BlockSpec(s, lambda i, j: (0, 0), pipeline_mode=pl.Buffered(1))
    return pl.pallas_call(
        _merge_kernel,
        grid=(b, l // tm),
        in_specs=[tok(d), tok(d)] + [tok(BRANCH_W)] * 4 + [
            pl.BlockSpec((1, 6, d), mod_map), full2((1, d)), full2(wm.shape),
            pl.BlockSpec(wb.shape, lambda i, j: (0, 0, 0), pipeline_mode=pl.Buffered(1)), full2(wo.shape),
            full2(wr_hi.shape), full2(wr_lo.shape)],
        out_specs=[tok(d), tok(d), tok(LANES)],
        out_shape=[jax.ShapeDtypeStruct((b, l, d), F32), jax.ShapeDtypeStruct((b, l, d), F32),
                   jax.ShapeDtypeStruct((b, l, LANES), F32)],
        compiler_params=_cparams(2),
        name="merge",
    )(x, h, *ys, mod, gain2.reshape(1, d), wm, wb, wo, wr_hi, wr_lo)


def _moe_kernel(be_ref, nv_ref, src_ref, dst_ref, h_hbm, wg_ref, wu_ref, wd_ref, y_hbm, xbuf, obuf, wg_s, wu_s, wd_s,
                gsem, ssem, *, n_real):
    i = pl.program_id(0)
    n = pl.num_programs(0)
    slot = i % 2
    nv = nv_ref[i]

    def start_gather(blk, dst_slot):
        def body(grp, carry):
            base = blk * MOE_ROWS + grp * MOE_ROW_ALIGN
            for j in range(MOE_ROW_ALIGN):
                pltpu.make_async_copy(h_hbm.at[pl.ds(src_ref[base + j], 1)], xbuf.at[dst_slot, grp, pl.ds(j, 1)],
                                      gsem.at[dst_slot]).start()
            return carry
        lax.fori_loop(0, nv_ref[blk] // MOE_ROW_ALIGN, body, 0)

    def wait_rows(buf, sem, cnt):
        grps = buf.at[pl.ds(0, cnt // MOE_ROW_ALIGN)]
        pltpu.make_async_copy(grps, grps, sem).wait()

    @pl.when(i == 0)
    def _():
        xbuf[...] = jnp.zeros_like(xbuf)
        fills = [pltpu.make_async_copy(xbuf.at[1, g], y_hbm.at[pl.ds(n_real + g * MOE_ROW_ALIGN, MOE_ROW_ALIGN)],
                                       ssem.at[1])
                 for g in range((y_hbm.shape[0] - n_real) // MOE_ROW_ALIGN)]
        for f in fills:
            f.start()
        for f in fills:
            f.wait()
        start_gather(0, 0)

    @pl.when(i + 1 < n)
    def _():
        start_gather(i + 1, 1 - slot)

    @pl.when((i >= 2) & (nv_ref[jnp.maximum(i - 2, 0)] > 0))
    def _():
        wait_rows(obuf.at[slot], ssem.at[slot], nv_ref[jnp.maximum(i - 2, 0)])

    @pl.when(nv > 0)
    def _():
        wait_rows(xbuf.at[slot], gsem.at[slot], nv)
        e = be_ref[i]

        @pl.when((i == 0) | (e != be_ref[jnp.maximum(i - 1, 0)]))
        def _():
            wg_s[...] = wg_ref[0].astype(BF16)
            wu_s[...] = wu_ref[0].astype(BF16)
            wd_s[...] = wd_ref[0].astype(BF16)

        d = xbuf.shape[-1]
        x = xbuf[slot].reshape(MOE_ROWS, d).astype(BF16)
        g = jnp.dot(x, wg_s[...], preferred_element_type=F32)
        u = jnp.dot(x, wu_s[...], preferred_element_type=F32)
        a = (g * jax.nn.sigmoid(g) * u).astype(BF16)
        obuf[slot] = jnp.dot(a, wd_s[...], preferred_element_type=F32).reshape(obuf.shape[1:])

        def body(grp, carry):
            base = i * MOE_ROWS + grp * MOE_ROW_ALIGN
            for j in range(MOE_ROW_ALIGN):
                pltpu.make_async_copy(obuf.at[slot, grp, pl.ds(j, 1)], y_hbm.at[pl.ds(dst_ref[base + j], 1)],
                                      ssem.at[slot]).start()
            return carry
        lax.fori_loop(0, nv // MOE_ROW_ALIGN, body, 0)

    @pl.when(i == n - 1)
    def _():
        @pl.when(nv > 0)
        def _():
            wait_rows(obuf.at[slot], ssem.at[slot], nv)

        @pl.when((n >= 2) & (nv_ref[jnp.maximum(i - 1, 0)] > 0))
        def _():
            wait_rows(obuf.at[1 - slot], ssem.at[1 - slot], nv_ref[jnp.maximum(i - 1, 0)])


def _moe_ffn(block_e, n_valid, src, dst, h2, w_gate, w_up, w_down):
    t, d = h2.shape
    grp_shape = (2, MOE_ROWS // MOE_ROW_ALIGN, MOE_ROW_ALIGN, d)
    n_blocks = block_e.shape[0]
    de = w_gate.shape[-1]
    wspec = lambda s: pl.BlockSpec((1,) + s, lambda i, be, nv, src, dst: (be[i], 0, 0))
    grid_spec = pltpu.PrefetchScalarGridSpec(
        num_scalar_prefetch=4,
        grid=(n_blocks,),
        in_specs=[pl.BlockSpec(memory_space=pl.ANY), wspec((d, de)), wspec((d, de)), wspec((de, d))],
        out_specs=pl.BlockSpec(memory_space=pl.ANY),
        scratch_shapes=[pltpu.VMEM(grp_shape, F32), pltpu.VMEM(grp_shape, F32),
                        pltpu.VMEM((d, de), BF16), pltpu.VMEM((d, de), BF16), pltpu.VMEM((de, d), BF16),
                        pltpu.SemaphoreType.DMA((2,)), pltpu.SemaphoreType.DMA((2,))],
    )
    return pl.pallas_call(
        functools.partial(_moe_kernel, n_real=TOP_K * t),
        grid_spec=grid_spec,
        out_shape=jax.ShapeDtypeStruct((TOP_K * t + MOE_ROW_ALIGN * N_EXPERTS, d), F32),
        compiler_params=pltpu.CompilerParams(dimension_semantics=("arbitrary",), vmem_limit_bytes=VMEM_LIMIT,
                                             has_side_effects=True),
        name="moe_ffn",
    )(block_e, n_valid, src, dst, h2, w_gate, w_up, w_down)


def _route(scores, router_bias):
    t = scores.shape[0]
    sel = (scores + router_bias.astype(F32)).reshape(t, N_GROUPS, EXPERTS_PER_GROUP)
    pairs = [sel[..., i] + sel[..., j] for i in range(EXPERTS_PER_GROUP) for j in range(i + 1, EXPERTS_PER_GROUP)]
    grp_score = functools.reduce(jnp.maximum, pairs)
    g_idx = jnp.argmax(grp_score, axis=-1).astype(jnp.int32)
    g_hot = (g_idx[:, None] == jnp.arange(N_GROUPS, dtype=jnp.int32)[None, :])[:, :, None]
    in_grp = jnp.sum(jnp.where(g_hot, sel, 0.0), axis=1)
    sc_grp = jnp.sum(jnp.where(g_hot, scores.reshape(t, N_GROUPS, EXPERTS_PER_GROUP), 0.0), axis=1)
    lane4 = jnp.arange(EXPERTS_PER_GROUP, dtype=jnp.int32)[None, :]
    i1 = jnp.argmax(in_grp, axis=-1).astype(jnp.int32)
    i2 = jnp.argmax(jnp.where(lane4 == i1[:, None], -jnp.inf, in_grp), axis=-1).astype(jnp.int32)
    e_idx = g_idx[:, None] * EXPERTS_PER_GROUP + jnp.stack([i1, i2], axis=-1)
    wts = jnp.stack([jnp.sum(jnp.where(lane4 == i1[:, None], sc_grp, 0.0), axis=-1),
                     jnp.sum(jnp.where(lane4 == i2[:, None], sc_grp, 0.0), axis=-1)], axis=-1)
    return e_idx, wts / jnp.sum(wts, axis=-1, keepdims=True)


def _moe(h2, scores, router_bias, w_gate, w_up, w_down):
    t, d = h2.shape
    e_idx, wts = _route(scores, router_bias)
    a = t * TOP_K
    flat_e = e_idx.T.reshape(-1)
    onehot = (flat_e[:, None] == jnp.arange(N_EXPERTS, dtype=jnp.int32)[None, :]).astype(jnp.int32)
    rb = MOE_ROWS
    blocks = onehot.reshape(a // rb, rb, N_EXPERTS).astype(BF16)
    within = jnp.einsum('ij,bje->bie', jnp.tril(jnp.ones((rb, rb), BF16)), blocks, preferred_element_type=F32)
    before = jnp.cumsum(within[:, -1, :], axis=0) - within[:, -1, :]
    csum = (within + before[:, None, :]).astype(jnp.int32).reshape(a, N_EXPERTS)
    counts = csum[-1]
    rank = jnp.sum(csum * onehot, axis=1) - 1
    padded = (counts + MOE_ROWS - 1) // MOE_ROWS * MOE_ROWS
    ends = jnp.cumsum(padded)
    pstarts = ends - padded
    dest = jnp.sum(pstarts[None, :] * onehot, axis=1) + rank
    n_blocks = -(-a // MOE_ROWS) + N_EXPERTS
    blk0 = jnp.arange(n_blocks, dtype=jnp.int32) * MOE_ROWS
    block_e = jnp.minimum(jnp.sum((ends[None, :] <= blk0[:, None]).astype(jnp.int32), axis=1), N_EXPERTS - 1)
    n_valid = jnp.clip((pstarts + counts)[block_e] - blk0, 0, MOE_ROWS)
    n_valid = ((n_valid + MOE_ROW_ALIGN - 1) // MOE_ROW_ALIGN * MOE_ROW_ALIGN).astype(jnp.int32)
    slot = jnp.arange(n_blocks * MOE_ROWS, dtype=jnp.int32)
    spare = a + MOE_ROW_ALIGN * jnp.repeat(block_e, MOE_ROWS) + slot % MOE_ROW_ALIGN
    dst = spare.at[dest].set(jnp.arange(a, dtype=jnp.int32))
    src = jnp.where(dst < a, dst % t, 0)
    return _moe_ffn(block_e, n_valid, src, dst, h2, w_gate, w_up, w_down), wts


def _combine_kernel(x_ref, y0_ref, y1_ref, w_ref, mod_ref, o_ref):
    w = w_ref[...]
    y = y0_ref[...] * w[:, 0:1] + y1_ref[...] * w[:, 1:2]
    o_ref[0] = x_ref[0] + mod_ref[0, 5:6, :] * y


def _combine(x, y, wts, mod, mod_is_shared, row_offset, tm):
    b, l, d = x.shape
    mod_map = (lambda i, j: (0, 0, 0)) if mod_is_shared else (lambda i, j: (i, 0, 0))
    off = row_offset // tm
    per = l // tm
    return pl.pallas_call(
        _combine_kernel,
        grid=(b, per),
        in_specs=[pl.BlockSpec((1, tm, d), lambda i, j: (i, j, 0)),
                  pl.BlockSpec((tm, d), lambda i, j: (off + i * per + j, 0)),
                  pl.BlockSpec((tm, d), lambda i, j: (wts.shape[0] // tm + off + i * per + j, 0)),
                  pl.BlockSpec((tm, TOP_K), lambda i, j: (off + i * per + j, 0)),
                  pl.BlockSpec((1, 6, d), mod_map)],
        out_specs=pl.BlockSpec((1, tm, d), lambda i, j: (i, j, 0)),
        out_shape=jax.ShapeDtypeStruct((b, l, d), F32),
        compiler_params=_cparams(2),
        name="moe_combine",
    )(x, y, y, wts, mod)


def _head_rms(x, gain, e_ref):
    ms = jnp.dot((x * x).astype(BF16), e_ref[...], preferred_element_type=F32) * (1.0 / HEAD_DIM)
    return x * lax.rsqrt(ms + EPS) * gain


def _rope(y, cos, sin):
    lane = lax.broadcasted_iota(jnp.int32, y.shape, 1)
    half = HEAD_DIM // 2
    rot = jnp.where((lane % HEAD_DIM) < half, pltpu.roll(y, LANES - half, 1), pltpu.roll(y, half, 1))
    return y * cos + rot * sin


def _attend(problems):
    all_scores = []
    for q, tiles, _ in problems:
        scores = []
        for k, _, bias in tiles:
            s = lax.dot_general(q, k, (((1,), (1,)), ((), ())), preferred_element_type=F32)
            scores.append(s if bias is None else s + bias)
        if all(s.shape[1] % LANES == 0 for s in scores):
            scores = [jnp.concatenate(scores, axis=1)]
        all_scores.append(scores)
    maxes = []
    for scores, (_, _, sink) in zip(all_scores, problems):
        m = sink
        for s in scores:
            mt = jnp.max(s, axis=-1, keepdims=True)
            m = mt if m is None else jnp.maximum(m, mt)
        maxes.append(m)
    all_probs, dens = [], []
    for scores, m, (_, tiles, sink) in zip(all_scores, maxes, problems):
        den = jnp.exp(sink - m) if sink is not None else jnp.zeros_like(m)
        probs = []
        for s in scores:
            p = jnp.exp(s - m)
            den = den + jnp.sum(p, axis=-1, keepdims=True)
            probs.append(p.astype(BF16))
        if len(probs) != len(tiles):
            offs = np.cumsum([0] + [v.shape[0] for _, v, _ in tiles])
            probs = [probs[0][:, offs[i]:offs[i + 1]] for i in range(len(tiles))]
        all_probs.append(probs)
        dens.append(den)
    outs = []
    for probs, den, (_, tiles, _) in zip(all_probs, dens, problems):
        o = None
        for p, (_, v, _) in zip(probs, tiles):
            pv = jnp.dot(p, v, preferred_element_type=F32)
            o = pv if o is None else o + pv
        outs.append(o * (1.0 / den))
    return outs


def _lane_half(shape):
    return lax.broadcasted_iota(jnp.int32, shape, 1) // HEAD_DIM


def _win_group_queries(slabs, g, sink_ref):
    r = WIN_HEADS // WIN_KV_HEADS
    rows = slabs[0].shape[0]
    half = _lane_half(slabs[0].shape)
    parts, sinks = [], []
    for j in range(r):
        head = g * r + j
        slab = slabs[head // 2]
        if head % 2 != g:
            slab = pltpu.roll(slab, HEAD_DIM, 1)
        parts.append(jnp.where(half == g, slab, 0.0))
        sinks.append(jnp.full((rows, 1), sink_ref[head], F32))
    return jnp.concatenate(parts, axis=0).astype(BF16), jnp.concatenate(sinks, axis=0)


def _win_store(o_ref, o, g, rows):
    r = WIN_HEADS // WIN_KV_HEADS
    half = _lane_half((rows, LANES))
    for pair in range(r // 2):
        a = o[(2 * pair) * rows:(2 * pair + 1) * rows]
        b = o[(2 * pair + 1) * rows:(2 * pair + 2) * rows]
        if g == 0:
            b = pltpu.roll(b, HEAD_DIM, 1)
        else:
            a = pltpu.roll(a, HEAD_DIM, 1)
        s = (g * r + 2 * pair) // 2
        o_ref[0, :, s * LANES:(s + 1) * LANES] = jnp.where(half == 0, a, b).astype(o_ref.dtype)


def _win_kernel(sink_ref, q_ref, kv_ref, ckv_ref, cos_ref, sin_ref, qg_ref, kg_ref, e_ref, o_ref, ks_ref, kcs_ref):
    n = pl.program_id(1)
    nb = pl.num_programs(1)
    l = kv_ref.shape[1]
    blk = WIN_BLOCK
    prep_rows = 512

    @pl.when(n == 0)
    def _():
        def body(i, carry):
            r0 = pl.multiple_of(i * prep_rows, prep_rows)
            y = _head_rms(kv_ref[0, pl.ds(r0, prep_rows), 0:LANES].astype(F32), kg_ref[...], e_ref)
            ks_ref[pl.ds(r0, prep_rows), :] = _rope(y, cos_ref[pl.ds(r0, prep_rows), :],
                                                    sin_ref[pl.ds(r0, prep_rows), :]).astype(BF16)
            return carry
        lax.fori_loop(0, l // prep_rows, body, 0)
        kcs_ref[...] = _head_rms(ckv_ref[0, :, 0:LANES].astype(F32), kg_ref[...], e_ref).astype(BF16)

    r0 = pl.multiple_of(n * blk, blk)
    cos = cos_ref[pl.ds(r0, blk), :]
    sin = sin_ref[pl.ds(r0, blk), :]
    slabs = []
    for s in range(q_ref.shape[-1] // LANES):
        y = _head_rms(q_ref[0, :, s * LANES:(s + 1) * LANES].astype(F32), qg_ref[...], e_ref)
        slabs.append(_rope(y, cos, sin))

    r = WIN_HEADS // WIN_KV_HEADS
    qi = lax.broadcasted_iota(jnp.int32, (r * blk, blk), 0) % blk
    kj = lax.broadcasted_iota(jnp.int32, (r * blk, blk), 1)
    ok_prev = (kj >= qi) & (n > 0)
    ok_next = (kj <= qi) & (n < nb - 1)
    bias_prev = jnp.where(ok_prev, 0.0, NEG_INF)
    bias_next = jnp.where(ok_next, 0.0, NEG_INF)
    tiles = []
    for kb, bias in ((jnp.maximum(n - 1, 0), bias_prev), (n, None), (jnp.minimum(n + 1, nb - 1), bias_next)):
        k0 = pl.multiple_of(kb * blk, blk)
        tiles.append((ks_ref[pl.ds(k0, blk), :], kv_ref[0, pl.ds(k0, blk), LANES:2 * LANES], bias))
    tiles.append((kcs_ref[...], ckv_ref[0, :, LANES:2 * LANES], None))
    rows = 2 * blk
    tiles = [(k, v, None if bias is None else bias[:rows]) for k, v, bias in tiles]
    for g in range(WIN_KV_HEADS):
        q, sink = _win_group_queries(slabs, g, sink_ref)
        outs = [_attend([(q[h * rows:(h + 1) * rows], tiles, sink[h * rows:(h + 1) * rows])])[0]
                for h in range(q.shape[0] // rows)]
        _win_store(o_ref, jnp.concatenate(outs, axis=0), g, blk)


def _win_ctx_kernel(sink_ref, q_ref, ckv_ref, qg_ref, kg_ref, e_ref, o_ref):
    kc = _head_rms(ckv_ref[0, :, 0:LANES].astype(F32), kg_ref[...], e_ref).astype(BF16)
    tiles = [(kc, ckv_ref[0, :, LANES:2 * LANES], None)]
    slabs = [_head_rms(q_ref[0, :, s * LANES:(s + 1) * LANES].astype(F32), qg_ref[...], e_ref)
             for s in range(q_ref.shape[-1] // LANES)]
    groups = [_win_group_queries(slabs, g, sink_ref) for g in range(WIN_KV_HEADS)]
    outs = _attend([(q, tiles, sink) for q, sink in groups])
    for g in range(WIN_KV_HEADS):
        _win_store(o_ref, outs[g], g, q_ref.shape[1])


def _block_diag_ones(width):
    i = np.arange(width) // HEAD_DIM
    return jnp.asarray(i[:, None] == i[None, :], BF16)


def _window_branch(zw_l, zw_c, q_gain, k_gain, sink, cos2, sin2, ctx_out):
    b, l, _ = zw_l.shape
    lc = zw_c.shape[1]
    wq = WIN_HEADS * HEAD_DIM
    qg = (jnp.tile(q_gain.astype(F32), 2) * HEAD_DIM ** -0.5).reshape(1, LANES)
    kg = jnp.tile(k_gain.astype(F32), 2).reshape(1, LANES)
    e = _block_diag_ones(LANES)
    smem = pl.BlockSpec(memory_space=pltpu.SMEM)
    full = lambda s, nd: pl.BlockSpec(s, lambda *a: (0,) * nd)
    yl = pl.pallas_call(
        _win_kernel,
        grid=(b, l // WIN_BLOCK),
        in_specs=[smem,
                  pl.BlockSpec((1, WIN_BLOCK, wq), lambda i, n: (i, n, 0)),
                  pl.BlockSpec((1, l, 2 * LANES), lambda i, n: (i, 0, wq // (2 * LANES))),
                  pl.BlockSpec((1, lc, 2 * LANES), lambda i, n: (i, 0, wq // (2 * LANES))),
                  full((l, LANES), 2), full((l, LANES), 2), full((1, LANES), 2), full((1, LANES), 2),
                  full((LANES, LANES), 2)],
        out_specs=pl.BlockSpec((1, WIN_BLOCK, wq), lambda i, n: (i, n, 0)),
        out_shape=jax.ShapeDtypeStruct((b, l, wq), BF16),
        scratch_shapes=[pltpu.VMEM((l, LANES), BF16), pltpu.VMEM((lc, LANES), BF16)],
        compiler_params=_cparams(2),
        name="window_attn",
    )(sink.astype(F32), zw_l, zw_l, zw_c, cos2, sin2, qg, kg, e)
    yc = None
    if ctx_out:
        yc = pl.pallas_call(
            _win_ctx_kernel,
            grid=(b,),
            in_specs=[smem,
                      pl.BlockSpec((1, lc, wq), lambda i: (i, 0, 0)),
                      pl.BlockSpec((1, lc, 2 * LANES), lambda i: (i, 0, wq // (2 * LANES))),
                      full((1, LANES), 2), full((1, LANES), 2), full((LANES, LANES), 2)],
            out_specs=pl.BlockSpec((1, lc, wq), lambda i: (i, 0, 0)),
            out_shape=jax.ShapeDtypeStruct((b, lc, wq), BF16),
            compiler_params=_cparams(1),
            name="window_ctx_attn",
        )(sink.astype(F32), zw_c, zw_c, qg, kg, e)
    return yc, yl


NA_ROWS = 8
NA_KROWS = NA_ROWS + NA_KH - 1


def _na_key_base(rg, rows):
    return jnp.clip(rg * NA_ROWS - NA_KH // 2, 0, rows - NA_KROWS)


def _na_kernel(q_ref, k_ref, v_ref, ck_ref, cv_ref, bias_ref, qg_ref, kg_ref, e_ref, o_ref, ks_ref, kcs_ref):
    rg = pl.program_id(1)
    l = k_ref.shape[1]
    rows = l // GRID_W
    n_slab = q_ref.shape[-1] // LANES
    prep_rows = 512

    @pl.when(rg == 0)
    def _():
        def body(i, carry):
            r0 = pl.multiple_of(i * prep_rows, prep_rows)
            ks_ref[pl.ds(r0, prep_rows), :] = _head_rms(k_ref[0, pl.ds(r0, prep_rows), :].astype(F32), kg_ref[...],
                                                        e_ref).astype(BF16)
            return carry
        lax.fori_loop(0, l // prep_rows, body, 0)
        kcs_ref[...] = _head_rms(ck_ref[0].astype(F32), kg_ref[...], e_ref).astype(BF16)

    nk = NA_KROWS * GRID_W
    k0 = pl.multiple_of(_na_key_base(rg, rows) * GRID_W, GRID_W)
    qn = _head_rms(q_ref[0].astype(F32), qg_ref[...], e_ref)
    half = _lane_half((q_ref.shape[1], LANES))
    for s in range(n_slab):
        cols = slice(s * LANES, (s + 1) * LANES)
        slab = qn[:, cols]
        tiles_kv = (ks_ref[pl.ds(k0, nk), cols], v_ref[0, pl.ds(k0, nk), cols])
        ctx_kv = (kcs_ref[:, cols], cv_ref[0, :, cols])
        outs = [_attend([(jnp.where(half == hh, slab, 0.0).astype(BF16),
                          [tiles_kv + (bias_ref[0, 2 * s + hh].astype(F32),), ctx_kv + (None,)], None)])[0]
                for hh in range(2)]
        o_ref[0, :, cols] = jnp.where(half == 0, outs[0], outs[1]).astype(o_ref.dtype)


def _na_ctx_kernel(q_ref, ck_ref, cv_ref, qg_ref, kg_ref, e_ref, o_ref):
    kc = _head_rms(ck_ref[0].astype(F32), kg_ref[...], e_ref).astype(BF16)
    qn = _head_rms(q_ref[0].astype(F32), qg_ref[...], e_ref)
    half = _lane_half((q_ref.shape[1], LANES))
    for s in range(q_ref.shape[-1] // LANES):
        cols = slice(s * LANES, (s + 1) * LANES)
        outs = _attend([(jnp.where(half == hh, qn[:, cols], 0.0).astype(BF16),
                         [(kc[:, cols], cv_ref[0, :, cols], None)], None) for hh in range(2)])
        o_ref[0, :, cols] = jnp.where(half == 0, outs[0], outs[1]).astype(o_ref.dtype)


def _na_bias_classes(rows):
    n_rg = rows // NA_ROWS
    return list(range(n_rg)) if n_rg <= 3 else [0, 1, n_rg - 1]


def _na_bias_table(rpb, rows):
    kh = NA_KH
    ro, rv = [], []
    for rg in _na_bias_classes(rows):
        kbase = int(np.clip(rg * NA_ROWS - kh // 2, 0, rows - NA_KROWS))
        r = rg * NA_ROWS + np.arange(NA_ROWS)
        rstart = np.clip(r - kh // 2, 0, rows - kh)
        kr = kbase + np.arange(NA_KROWS)
        rv.append((kr[None, :] >= rstart[:, None]) & (kr[None, :] < rstart[:, None] + kh))
        ro.append(np.clip(kr[None, :] - r[:, None] + kh - 1, 0, 2 * kh - 2))
    ro, rv = np.stack(ro), np.stack(rv)
    qc = np.arange(GRID_W)
    cstart = np.clip(qc - NA_KW // 2, 0, GRID_W - NA_KW)
    cv = (qc[None, :] >= cstart[:, None]) & (qc[None, :] < cstart[:, None] + NA_KW)
    co = np.clip(qc[None, :] - qc[:, None] + NA_KW - 1, 0, 2 * NA_KW - 2)
    n_cls = ro.shape[0]
    hp = lax.Precision.HIGHEST
    co_hot = jnp.asarray(co[None] == np.arange(2 * NA_KW - 1)[:, None, None], F32)
    ro_hot = jnp.asarray(ro[..., None] == np.arange(2 * kh - 1), F32)
    cols = jnp.einsum('hdc,cwx->hdwx', rpb.astype(F32), co_hot, precision=hp)
    tab = jnp.einsum('crkd,hdwx->chrwkx', ro_hot, cols, precision=hp)
    ok = rv[:, None, :, None, :, None] & cv[None, None, None, :, None, :]
    tab = jnp.where(ok, tab, NEG_INF)
    return tab.reshape(n_cls, rpb.shape[0], NA_ROWS * GRID_W, NA_KROWS * GRID_W).astype(BF16)


def _neighborhood_branch(zn_l, zn_c, q_gain, k_gain, rpb, ctx_out):
    b, l, _ = zn_l.shape
    lc = zn_c.shape[1]
    w = NA_HEADS * HEAD_DIM
    rows = l // GRID_W
    n_rg = rows // NA_ROWS
    n_slab = w // LANES
    qg = (jnp.tile(q_gain.astype(F32), 2 * n_slab) * HEAD_DIM ** -0.5).reshape(1, w)
    kg = jnp.tile(k_gain.astype(F32), 2 * n_slab).reshape(1, w)
    e = _block_diag_ones(w)
    bias = _na_bias_table(rpb, rows)
    if n_rg <= 3:
        cls_map = lambda i, r: (r, 0, 0, 0)
    else:
        cls_map = lambda i, r: ((r > 0).astype(jnp.int32) + (r == n_rg - 1).astype(jnp.int32), 0, 0, 0)
    tq = NA_ROWS * GRID_W
    full = lambda s, nd: pl.BlockSpec(s, lambda *a: (0,) * nd)
    yl = pl.pallas_call(
        _na_kernel,
        grid=(b, n_rg),
        in_specs=[pl.BlockSpec((1, tq, w), lambda i, r: (i, r, 0)),
                  pl.BlockSpec((1, l, w), lambda i, r: (i, 0, 1)),
                  pl.BlockSpec((1, l, w), lambda i, r: (i, 0, 2)),
                  pl.BlockSpec((1, lc, w), lambda i, r: (i, 0, 1)),
                  pl.BlockSpec((1, lc, w), lambda i, r: (i, 0, 2)),
                  pl.BlockSpec((1,) + bias.shape[1:], cls_map),
                  full((1, w), 2), full((1, w), 2), full((w, w), 2)],
        out_specs=pl.BlockSpec((1, tq, w), lambda i, r: (i, r, 0)),
        out_shape=jax.ShapeDtypeStruct((b, l, w), BF16),
        scratch_shapes=[pltpu.VMEM((l, w), BF16), pltpu.VMEM((lc, w), BF16)],
        compiler_params=_cparams(2),
        name="neighborhood_attn",
    )(zn_l, zn_l, zn_l, zn_c, zn_c, bias, qg, kg, e)
    yc = None
    if ctx_out:
        yc = pl.pallas_call(
            _na_ctx_kernel,
            grid=(b,),
            in_specs=[pl.BlockSpec((1, lc, w), lambda i: (i, 0, 0)),
                      pl.BlockSpec((1, lc, w), lambda i: (i, 0, 1)),
                      pl.BlockSpec((1, lc, w), lambda i: (i, 0, 2)),
                      full((1, w), 2), full((1, w), 2), full((w, w), 2)],
            out_specs=pl.BlockSpec((1, lc, w), lambda i: (i, 0, 0)),
            out_shape=jax.ShapeDtypeStruct((b, lc, w), BF16),
            compiler_params=_cparams(1),
            name="neighborhood_ctx_attn",
        )(zn_c, zn_c, zn_c, qg, kg, e)
    return yc, yl


def _rope_tables(n_tokens):
    t = np.arange(n_tokens)
    n_freq = HEAD_DIM // 4
    inv = ROPE_THETA ** (-jnp.arange(n_freq, dtype=F32) / n_freq)
    ang = jnp.concatenate([jnp.asarray(t // GRID_W, F32)[:, None] * inv, jnp.asarray(t % GRID_W, F32)[:, None] * inv],
                          axis=-1)
    cos, sin = jnp.cos(ang), jnp.sin(ang)
    return jnp.tile(jnp.concatenate([cos, cos], -1), (1, 2)), jnp.tile(jnp.concatenate([-sin, sin], -1), (1, 2))


def _ret_kernel(qq_l, kk_l, v_l, g_l, qq_c, kk_c, v_c, g_c, cos_ref, sin_ref, dmask_ref, tq_ref, tk_ref, cdec_ref,
                gain_ref, yl_ref, yc_ref, kr_ref, kv_ref, sin_state_ref, *, ctx_out):
    c = RET_CHUNK
    ncc = qq_c.shape[1] // c
    ncl = qq_l.shape[1] // c
    nc = ncc + ncl
    tk = tk_ref[0]
    tq = tq_ref[0]
    dmask = dmask_ref[0]
    fwd_lanes = lax.broadcasted_iota(jnp.int32, (c, LANES), 1) < RET_DK

    def chunk_kv(k2, v):
        kd = (k2 * tk).astype(BF16)
        return lax.dot_general(kd, v, (((0,), (0,)), ((), ())), preferred_element_type=F32)

    for n in range(ncc):
        kv_ref[n] = chunk_kv(kk_c[0, n * c:(n + 1) * c, :].astype(F32), v_c[0, n * c:(n + 1) * c, :])

    def kv_body(n, carry):
        r0 = pl.multiple_of(n * c, c)
        k2 = _rope(kk_l[0, pl.ds(r0, c), :].astype(F32), cos_ref[pl.ds(r0, c), :], sin_ref[pl.ds(r0, c), :])
        kr_ref[pl.ds(r0, c), :] = k2.astype(BF16)
        kv_ref[ncc + n] = chunk_kv(k2, v_l[0, pl.ds(r0, c), :])
        return carry
    lax.fori_loop(0, ncl, kv_body, 0, unroll=4)

    dec_f = cdec_ref[0, 0:1, :]
    dec_b = cdec_ref[0, 1:2, :]

    def scan_body(t, carry):
        sf, sb = carry
        sin_state_ref[t, 0:RET_DK, :] = sf.astype(BF16)
        sf = sf * dec_f + kv_ref[t, 0:RET_DK, :]
        tb = jnp.where(t < ncc, ncc - 1 - t, nc - 1 - (t - ncc))
        sin_state_ref[tb, RET_DK:2 * RET_DK, :] = sb.astype(BF16)
        sb = sb * dec_b + kv_ref[tb, RET_DK:2 * RET_DK, :]
        return sf, sb
    zero = jnp.zeros((RET_DK, RET_DV), F32)
    lax.fori_loop(0, nc, scan_body, (zero, zero))

    def chunk_out(q2, k2b, v, gate, state):
        qm = jnp.where(fwd_lanes, q2, 0.0).astype(BF16)
        scores = lax.dot_general(qm, k2b, (((1,), (1,)), ((), ())), preferred_element_type=F32) * dmask
        o = (jnp.dot(scores.astype(BF16), v, preferred_element_type=F32)
             + jnp.dot((q2 * tq).astype(BF16), state, preferred_element_type=F32))
        mu = jnp.mean(o, axis=-1, keepdims=True)
        var = jnp.mean(jnp.square(o - mu), axis=-1, keepdims=True)
        y = (o - mu) * lax.rsqrt(var + EPS) * gain_ref[0]
        gf = gate.astype(F32)
        return y * gf * jax.nn.sigmoid(gf)

    if ctx_out:
        for n in range(ncc):
            rows = slice(n * c, (n + 1) * c)
            yc_ref[0, rows, :] = chunk_out(qq_c[0, rows, :].astype(F32), kk_c[0, rows, :], v_c[0, rows, :],
                                           g_c[0, rows, :], sin_state_ref[n]).astype(yc_ref.dtype)
    else:
        yc_ref[...] = jnp.zeros_like(yc_ref)

    def out_body(n, carry):
        r0 = pl.multiple_of(n * c, c)
        q2 = _rope(qq_l[0, pl.ds(r0, c), :].astype(F32), cos_ref[pl.ds(r0, c), :], sin_ref[pl.ds(r0, c), :])
        yl_ref[0, pl.ds(r0, c), :] = chunk_out(q2, kr_ref[pl.ds(r0, c), :], v_l[0, pl.ds(r0, c), :],
                                               g_l[0, pl.ds(r0, c), :], sin_state_ref[ncc + n]).astype(yl_ref.dtype)
        return carry
    lax.fori_loop(0, ncl, out_body, 0, unroll=4)


def _ret_tables(decay_logit):
    c = RET_CHUNK
    lg = jax.nn.log_sigmoid(decay_logit.astype(F32))
    lf, lb = lg[0][:, None, None], lg[1][:, None, None]
    pos = jnp.arange(c, dtype=F32)
    diff = pos[:, None] - pos[None, :]
    dmask = (jnp.where(diff >= 0, jnp.exp(lf * jnp.maximum(diff, 0.0)), 0.0)
             + jnp.where(diff <= 0, jnp.exp(lb * jnp.maximum(-diff, 0.0)), 0.0)) * RET_DK ** -0.5
    col = lambda a, b_: jnp.concatenate([jnp.broadcast_to(a, a.shape[:2] + (RET_DK,)),
                                         jnp.broadcast_to(b_, b_.shape[:2] + (RET_DK,))], axis=-1)
    p = pos[None, :, None]
    tq = col(jnp.exp(lf * (p + 1.0)), jnp.exp(lb * (c - p)))
    tk = col(jnp.exp(lf * (c - 1.0 - p)), jnp.exp(lb * p)) * RET_DK ** -0.5
    cdec = jnp.zeros((lg.shape[1], 8, RET_DV), F32)
    cdec = cdec.at[:, 0, :].set(jnp.exp(lg[0] * c)[:, None]).at[:, 1, :].set(jnp.exp(lg[1] * c)[:, None])
    return dmask, tq, tk, cdec


def _retention_branch(zr_l, zr_c, decay_logit, gn_gain, cos2, sin2, ctx_out):
    b, l, _ = zr_l.shape
    lc = zr_c.shape[1]
    h = RET_HEADS
    nc = (l + lc) // RET_CHUNK
    dmask, tq, tk, cdec = _ret_tables(decay_logit)
    gain = gn_gain.astype(F32).reshape(h, 1, RET_DV)
    seq = lambda n, j: pl.BlockSpec((1, n, LANES), lambda i, hh: (i, 0, 4 * hh + j))
    head = lambda s: pl.BlockSpec((1,) + s, lambda i, hh: (hh, 0, 0))
    full = lambda s: pl.BlockSpec(s, lambda i, hh: (0, 0))
    yl, yc = pl.pallas_call(
        functools.partial(_ret_kernel, ctx_out=ctx_out),
        grid=(b, h),
        in_specs=[seq(l, 0), seq(l, 1), seq(l, 2), seq(l, 3), seq(lc, 0), seq(lc, 1), seq(lc, 2), seq(lc, 3),
                  full((l, LANES)), full((l, LANES)),
                  head((RET_CHUNK, RET_CHUNK)), head((RET_CHUNK, LANES)), head((RET_CHUNK, LANES)), head((8, RET_DV)),
                  head((1, RET_DV))],
        out_specs=[pl.BlockSpec((1, l, RET_DV), lambda i, hh: (i, 0, hh)),
                   pl.BlockSpec((1, lc, RET_DV), lambda i, hh: (i, 0, hh))],
        out_shape=[jax.ShapeDtypeStruct((b, l, h * RET_DV), BF16), jax.ShapeDtypeStruct((b, lc, h * RET_DV), BF16)],
        scratch_shapes=[pltpu.VMEM((l, LANES), BF16), pltpu.VMEM((nc, 2 * RET_DK, RET_DV), F32),
                        pltpu.VMEM((nc, 2 * RET_DK, RET_DV), BF16)],
        compiler_params=_cparams(2),
        name="retention",
    )(zr_l, zr_l, zr_l, zr_l, zr_c, zr_c, zr_c, zr_c, cos2, sin2, dmask, tq, tk, cdec, gain)
    return (yc if ctx_out else None), yl


GDN_SUPER = 128
GDN_UNITS = 8
GDN_HALO = 128


def _mask_dot_col(mask_bf16, col):
    hi = col.astype(BF16).astype(F32)
    low_half = lax.broadcasted_iota(jnp.int32, (col.shape[0], LANES), 1) < LANES // 2
    r = jnp.dot(mask_bf16, jnp.where(low_half, hi, col - hi).astype(BF16), preferred_element_type=F32)
    return r + pltpu.roll(r, LANES // 2, 1)


def _softplus(x):
    return jnp.maximum(x, 0.0) + jnp.log(1.0 + jnp.exp(-jnp.abs(x)))


def _gdn_kernel(nega_ref, dtb_ref, q_l, k_l, v_l, g_l, ab_l, q_c, k_c, v_c, g_c, ab_c,
                cwq_ref, cwk_ref, cwv_ref, pd_ref, pu_ref, gain_ref, yl_ref, yc_ref,
                kn_s, sin_s, qp_s, o0_s, cd_s, *, ctx_out):
    hd = pl.program_id(1)
    c = GDN_CHUNK
    sup = GDN_SUPER
    per = sup // c
    lc, l = q_c.shape[1], q_l.shape[1]
    ncc, ncl = lc // c, l // c
    nc = ncc + ncl

    ri = lax.broadcasted_iota(jnp.int32, (sup, sup), 0)
    ci = lax.broadcasted_iota(jnp.int32, (sup, sup), 1)
    same = (ri // c) == (ci // c)
    eye = (ri == ci).astype(F32)
    incl = (same & (ri >= ci), same & (ri <= ci))
    strict = (same & (ri > ci), same & (ri < ci))
    incl_b = tuple(m.astype(BF16) for m in incl)
    same_b = same.astype(BF16)
    lane = lax.broadcasted_iota(jnp.int32, (sup, LANES), 1)
    rowc = lax.broadcasted_iota(jnp.int32, (sup, LANES), 0)

    def conv_silu(z_ref, w_ref, r0, ls):
        z = z_ref[0, pl.ds(r0, sup), :]
        if isinstance(r0, int):
            zero = jnp.zeros((GDN_HALO, LANES), BF16)
            prev = z_ref[0, r0 - GDN_HALO:r0, :] if r0 > 0 else zero
            nxt = z_ref[0, r0 + sup:r0 + sup + GDN_HALO, :] if r0 + sup < ls else zero
        else:
            p0 = pl.multiple_of(jnp.maximum(r0 - GDN_HALO, 0), GDN_HALO)
            n0 = pl.multiple_of(jnp.minimum(r0 + sup, ls - GDN_HALO), GDN_HALO)
            prev = jnp.where(r0 > 0, z_ref[0, pl.ds(p0, GDN_HALO), :], jnp.zeros((), BF16))
            nxt = jnp.where(r0 + sup < ls, z_ref[0, pl.ds(n0, GDN_HALO), :], jnp.zeros((), BF16))
        win = jnp.concatenate([prev, z, nxt], axis=0)
        z_dn = jnp.dot(pd_ref[...], win, preferred_element_type=F32)
        z_up = jnp.dot(pu_ref[...], win, preferred_element_type=F32)
        y = z_dn * w_ref[0, 0:1, :] + z.astype(F32) * w_ref[0, 1:2, :] + z_up * w_ref[0, 2:3, :]
        return y * jax.nn.sigmoid(y)

    def prep(units):
        common = []
        for refs, r0, base, ls in units:
            q_ref, k_ref, v_ref, ab_ref = refs
            q = conv_silu(q_ref, cwq_ref, r0, ls)
            k = conv_silu(k_ref, cwk_ref, r0, ls)
            v = conv_silu(v_ref, cwv_ref, r0, ls)
            q = q * lax.rsqrt(jnp.sum(q * q, axis=-1, keepdims=True) + EPS) * GDN_DK ** -0.5
            k = k * lax.rsqrt(jnp.sum(k * k, axis=-1, keepdims=True) + EPS)
            kb16 = k.astype(BF16)
            qk = lax.dot_general(q.astype(BF16), kb16, (((1,), (1,)), ((), ())), preferred_element_type=F32)
            ab = ab_ref[0, pl.ds(r0, sup), :]
            common.append((q, k, v, kb16, qk, ab, base + r0))
        chains = []
        for q, k, v, kb16, qk, ab, row0 in common:
            for d in range(2):
                ia = d * GDN_HEADS + hd
                ib = 2 * GDN_HEADS + ia
                a_col = jnp.sum(jnp.where(lane == ia, ab, 0.0), axis=1, keepdims=True)
                b_col = jnp.sum(jnp.where(lane == ib, ab, 0.0), axis=1, keepdims=True)
                g_col = nega_ref[d, hd] * _softplus(a_col + dtb_ref[d, hd])
                beta = jax.nn.sigmoid(b_col)
                gcum = _mask_dot_col(incl_b[d], g_col)
                gtot = _mask_dot_col(same_b, g_col)
                gc = gcum[:, 0:1]
                decay = jnp.where(incl[d], jnp.exp(jnp.where(incl[d], gcum - gcum.T, 0.0)), 0.0)
                kbeta = k * beta
                kk = lax.dot_general(kbeta.astype(BF16), kb16, (((1,), (1,)), ((), ())),
                                     preferred_element_type=F32)
                a_mat = jnp.where(strict[d], kk * decay, 0.0)
                eg = jnp.exp(gc)
                rhs = jnp.concatenate([kbeta * eg, v * beta], axis=1).astype(BF16)
                aqk = jnp.where(incl[d], qk * decay, 0.0).astype(BF16)
                ke = (k * jnp.exp(gtot[:, 0:1] - gc)).astype(BF16)
                cds = [jnp.exp(gtot[j * c:j * c + 1, :]) for j in range(per)]
                chains.append(dict(d=d, row0=row0, a=a_mat, rhs=rhs, aqk=aqk, ke=ke, cds=cds, qeg=q * eg))
        def coupling(d, s):
            return strict[d] & ((ri // (2 * s)) == (ci // (2 * s))) & ((ri // s) != (ci // s))
        for ch in chains:
            ch['inv'] = eye - jnp.where(coupling(ch['d'], 1), ch['a'], 0.0)
        s = 2
        while s < c:
            for ch in chains:
                a_s = jnp.where(coupling(ch['d'], s), ch['a'], 0.0).astype(BF16)
                ch['t'] = jnp.dot(a_s, ch['inv'].astype(BF16), preferred_element_type=F32).astype(BF16)
            for ch in chains:
                ch['inv'] = ch['inv'] - jnp.dot(ch['inv'].astype(BF16), ch['t'], preferred_element_type=F32)
            s *= 2
        for ch in chains:
            ch['wu'] = jnp.dot(ch['inv'].astype(BF16), ch['rhs'], preferred_element_type=F32).astype(BF16)
        for ch in chains:
            d, row0 = ch['d'], ch['row0']
            awu = jnp.dot(ch['aqk'], ch['wu'], preferred_element_type=F32)
            rows = pl.ds(row0, sup)
            qp_s[d, rows, :] = (ch['qeg'] - awu[:, :GDN_DK]).astype(BF16)
            o0_s[d, rows, :] = awu[:, GDN_DK:]
            for j in range(per):
                kej = jnp.where((rowc // c) == j, ch['ke'], jnp.zeros((), BF16))
                idx = row0 // c + j
                kn_s[d, idx] = lax.dot_general(kej, ch['wu'], (((0,), (0,)), ((), ())),
                                               preferred_element_type=F32).astype(BF16)
                cd_s[d, idx] = ch['cds'][j]

    prep([((q_c, k_c, v_c, ab_c), n * sup, 0, lc) for n in range(lc // sup)])

    def prep_body(n, carry):
        prep([((q_l, k_l, v_l, ab_l), pl.multiple_of((GDN_UNITS * n + j) * sup, sup), lc, l)
              for j in range(GDN_UNITS)])
        return carry
    lax.fori_loop(0, l // (GDN_UNITS * sup), prep_body, 0)

    def chunk_step(d, idx, s):
        sb = s.astype(BF16)
        sin_s[d, idx] = sb
        kn = kn_s[d, idx]
        return (s * cd_s[d, idx] - jnp.dot(kn[:, :GDN_DK], sb, preferred_element_type=F32)
                + kn[:, GDN_DK:].astype(F32))

    def scan_body(t, carry):
        sf, sb = carry
        tb = jnp.where(t < ncc, ncc - 1 - t, nc - 1 - (t - ncc))
        return chunk_step(0, t, sf), chunk_step(1, tb, sb)
    zero = jnp.zeros((GDN_DK, GDN_DV), F32)
    lax.fori_loop(0, nc, scan_body, (zero, zero))

    def finish(y_ref, gate_ref, base, n_rows):
        def body(n, carry):
            r0 = pl.multiple_of(n * c, c)
            rows = pl.ds(base + r0, c)
            idx = base // c + n
            o = (o0_s[0, rows, :] + o0_s[1, rows, :]
                 + jnp.dot(qp_s[0, rows, :], sin_s[0, idx], preferred_element_type=F32)
                 + jnp.dot(qp_s[1, rows, :], sin_s[1, idx], preferred_element_type=F32))
            y = o * lax.rsqrt(jnp.mean(o * o, axis=-1, keepdims=True) + EPS) * gain_ref[...]
            gf = gate_ref[0, pl.ds(r0, c), :].astype(F32)
            y_ref[0, pl.ds(r0, c), :] = (y * gf * jax.nn.sigmoid(gf)).astype(y_ref.dtype)
            return carry
        lax.fori_loop(0, n_rows // c, body, 0, unroll=4)

    finish(yl_ref, g_l, lc, l)
    if ctx_out:
        finish(yc_ref, g_c, 0, lc)
    else:
        yc_ref[...] = jnp.zeros_like(yc_ref)


def _gdn_branch(zg_l, zab_l, zg_c, zab_c, conv_w, a_log, dt_bias, norm_gain, ctx_out):
    b, l, _ = zg_l.shape
    lc = zg_c.shape[1]
    h = GDN_HEADS
    sup = GDN_SUPER
    ltot = l + lc
    assert l % (GDN_UNITS * sup) == 0 and lc % sup == 0, (l, lc)
    nc = ltot // GDN_CHUNK
    neg_a = -jnp.exp(a_log.astype(F32))
    cw = conv_w.astype(F32).T.reshape(3 * h, LANES, SHORT_CONV).transpose(0, 2, 1)
    win = sup + 2 * GDN_HALO
    i = np.arange(sup)
    pd = np.zeros((sup, win), np.float32)
    pu = np.zeros((sup, win), np.float32)
    pd[i, GDN_HALO + i - 1] = 1.0
    pu[i, GDN_HALO + i + 1] = 1.0
    seq = lambda n, j: pl.BlockSpec((1, n, LANES), lambda bi, hh: (bi, 0, j * h + hh))
    abs_ = lambda n: pl.BlockSpec((1, n, LANES), lambda bi, hh: (bi, 0, 0))
    cws = lambda j: pl.BlockSpec((1, SHORT_CONV, LANES), lambda bi, hh: (j * h + hh, 0, 0))
    full = lambda s: pl.BlockSpec(s, lambda bi, hh: (0, 0))
    smem = pl.BlockSpec(memory_space=pltpu.SMEM)
    yl, yc = pl.pallas_call(
        functools.partial(_gdn_kernel, ctx_out=ctx_out),
        grid=(b, h),
        in_specs=[smem, smem, seq(l, 0), seq(l, 1), seq(l, 2), seq(l, 3), abs_(l),
                  seq(lc, 0), seq(lc, 1), seq(lc, 2), seq(lc, 3), abs_(lc),
                  cws(0), cws(1), cws(2), full((sup, win)), full((sup, win)), full((1, GDN_DV))],
        out_specs=[pl.BlockSpec((1, l, GDN_DV), lambda bi, hh: (bi, 0, hh)),
                   pl.BlockSpec((1, lc, GDN_DV), lambda bi, hh: (bi, 0, hh))],
        out_shape=[jax.ShapeDtypeStruct((b, l, h * GDN_DV), BF16), jax.ShapeDtypeStruct((b, lc, h * GDN_DV), BF16)],
        scratch_shapes=[pltpu.VMEM((2, nc, GDN_DK, GDN_DK + GDN_DV), BF16), pltpu.VMEM((2, nc, GDN_DK, GDN_DV), BF16),
                        pltpu.VMEM((2, ltot, GDN_DK), BF16), pltpu.VMEM((2, ltot, GDN_DV), F32),
                        pltpu.VMEM((2, nc, 1, GDN_DV), F32)],
        compiler_params=_cparams(2),
        name="gated_deltanet",
    )(neg_a, dt_bias.astype(F32), zg_l, zg_l, zg_l, zg_l, zab_l, zg_c, zg_c, zg_c, zg_c, zab_c,
      cw, cw, cw, jnp.asarray(pd, BF16), jnp.asarray(pu, BF16), norm_gain.astype(F32).reshape(1, GDN_DV))
    return (yc if ctx_out else None), yl


def kernel(x, c, ctx, c_ctx, w_mod, b_mod, norm1, norm2, w_in, ret_decay, ret_gn, win_qnorm, win_knorm, win_sink,
           na_qnorm, na_knorm, na_rpb, gdn_conv, gdn_a_log, gdn_dt_bias, gdn_norm, w_branch, w_merge, w_out,
           w_router, router_bias, w_e_gate, w_e_up, w_e_down):
    b, l, d = x.shape
    lc = ctx.shape[1]
    depth = w_mod.shape[0]
    cos2, sin2 = _rope_tables(l)

    n_rows = 16
    cc = jnp.zeros((n_rows, d), F32).at[:b].set(c).at[b].set(c_ctx)
    mod = _modulation(cc, w_mod, b_mod).reshape(depth, n_rows, 6, d)

    wr = jnp.zeros((d, LANES), F32).at[:, :N_EXPERTS].set(w_router)
    wr_hi = wr.astype(BF16)
    wr_lo = (wr - wr_hi.astype(F32)).astype(BF16)

    xl, xc = x, ctx
    for layer in range(depth):
        ctx_out = layer < depth - 1
        mod_l = mod[layer, :b]
        mod_c = mod[layer, b:b + 1]
        w_all = _pack_w_in(w_in[layer])
        hl, *zl_s = _inproj(xl, mod_l, False, norm1[layer], w_all, min(l, TOKEN_TILE))
        hc, *zc_s = _inproj(xc, mod_c, True, norm1[layer], w_all, min(lc, TOKEN_TILE))
        zr_l, zw_l, zn_l, zg_l, zab_l = zl_s
        zr_c, zw_c, zn_c, zg_c, zab_c = zc_s
        ret_c, ret_l = _retention_branch(zr_l, zr_c, ret_decay[layer], ret_gn[layer], cos2, sin2, ctx_out)
        win_c, win_l = _window_branch(zw_l, zw_c, win_qnorm[layer], win_knorm[layer], win_sink[layer], cos2, sin2,
                                      ctx_out)
        na_c, na_l = _neighborhood_branch(zn_l, zn_c, na_qnorm[layer], na_knorm[layer], na_rpb[layer], ctx_out)
        gdn_c, gdn_l = _gdn_branch(zg_l, zab_l, zg_c, zab_c, gdn_conv[layer], gdn_a_log[layer], gdn_dt_bias[layer],
                                   gdn_norm[layer], ctx_out)
        wm = w_merge[layer].astype(BF16)
        wb = w_branch[layer].astype(BF16)
        wo = w_out[layer].astype(BF16)
        ys_l = [ret_l, win_l, na_l, gdn_l]
        xl, h2l, sc_l = _merge(xl, hl, ys_l, mod_l, False, norm2[layer], wm, wb, wo, wr_hi, wr_lo, MERGE_TILE)
        if ctx_out:
            ys_c = [ret_c, win_c, na_c, gdn_c]
            xc, h2c, sc_c = _merge(xc, hc, ys_c, mod_c, True, norm2[layer], wm, wb, wo, wr_hi, wr_lo, MERGE_TILE)
            tokens = jnp.concatenate([h2c.reshape(b * lc, d), h2l.reshape(b * l, d)], axis=0)
            scores = jnp.concatenate([sc_c.reshape(b * lc, LANES), sc_l.reshape(b * l, LANES)], axis=0)
            y, wts = _moe(tokens, scores[:, :N_EXPERTS], router_bias, w_e_gate[layer], w_e_up[layer], w_e_down[layer])
            xc = _combine(xc, y, wts, mod_c, True, 0, MERGE_TILE)
            xl = _combine(xl, y, wts, mod_l, False, b * lc, MERGE_TILE)
        else:
            y, wts = _moe(h2l.reshape(b * l, d), sc_l.reshape(b * l, LANES)[:, :N_EXPERTS], router_bias,
                          w_e_gate[layer], w_e_up[layer], w_e_down[layer])
            xl = _combine(xl, y, wts, mod_l, False, 0, MERGE_TILE)
    return xl
```

```python
import functools

import numpy as np
import jax
import jax.numpy as jnp
from jax import lax
from jax.experimental import pallas as pl
from jax.experimental.pallas import tpu as pltpu

F32 = jnp.float32
BF16 = jnp.bfloat16
EPS = 1e-6
NEG_INF = -1e30
D_MODEL = 1024
GRID_W = 64
HEAD_DIM = 64
ROPE_THETA = 10000.0
RET_HEADS, RET_DK, RET_DV, RET_CHUNK = 4, 64, 128, 128
WIN_HEADS, WIN_KV_HEADS, WINDOW, WIN_BLOCK = 8, 2, 128, 128
NA_HEADS, NA_KH, NA_KW, NA_QCOLS = 8, 8, 16, 16
NA_BAND = NA_QCOLS + NA_KW
GDN_HEADS, GDN_DK, GDN_DV, GDN_CHUNK, SHORT_CONV = 4, 128, 128, 64, 3
GDN_QKV = 2 * GDN_HEADS * GDN_DK + GDN_HEADS * GDN_DV
N_BRANCH, BRANCH_W = 4, 512
N_EXPERTS, N_GROUPS, TOP_K, D_EXPERT = 32, 8, 2, 512
EXPERTS_PER_GROUP = N_EXPERTS // N_GROUPS

LANES = 128
VMEM_LIMIT = 56 * 1024 * 1024
TOKEN_TILE = 512
MERGE_TILE = 256
MOE_ROWS = 256
MOE_ROW_ALIGN = 8

W_RET = RET_HEADS * 4 * LANES
W_WIN = (WIN_HEADS + 2 * WIN_KV_HEADS) * HEAD_DIM
W_NA = 3 * NA_HEADS * HEAD_DIM
W_GDN = GDN_QKV + GDN_HEADS * GDN_DV
W_AB = LANES
SECTION_WIDTHS = (W_RET, W_WIN, W_NA, W_GDN, W_AB)
W_ALL = sum(SECTION_WIDTHS)


def _cparams(n_axes):
    return pltpu.CompilerParams(dimension_semantics=("arbitrary",) * n_axes, vmem_limit_bytes=VMEM_LIMIT)


def _mod_kernel(c_ref, w_ref, b_ref, o_ref):
    c = c_ref[...]
    a = (c * jax.nn.sigmoid(c)).astype(BF16)
    o_ref[0] = jnp.dot(a, w_ref[0].astype(BF16), preferred_element_type=F32) + b_ref[0]


def _modulation(cc, w_mod, b_mod):
    depth, d, n = w_mod.shape
    r = cc.shape[0]
    tn = 1536
    return pl.pallas_call(
        _mod_kernel,
        grid=(depth, n // tn),
        in_specs=[pl.BlockSpec((r, d), lambda l, j: (0, 0)),
                  pl.BlockSpec((1, d, tn), lambda l, j: (l, 0, j)),
                  pl.BlockSpec((1, 1, tn), lambda l, j: (l, 0, j))],
        out_specs=pl.BlockSpec((1, r, tn), lambda l, j: (l, 0, j)),
        out_shape=jax.ShapeDtypeStruct((depth, r, n), F32),
        compiler_params=_cparams(2),
        name="modulation",
    )(cc, w_mod, b_mod.reshape(depth, 1, n))


def _inproj_kernel(x_ref, mod_ref, gain_ref, w_ref, h_ref, *z_refs):
    x = x_ref[0]
    ms = jnp.mean(x * x, axis=-1, keepdims=True)
    shift = mod_ref[0, 0:1, :]
    scale = mod_ref[0, 1:2, :]
    h = x * lax.rsqrt(ms + EPS) * gain_ref[...] * (1.0 + scale) + shift
    hb = h.astype(BF16)
    h_ref[0] = hb
    off = 0
    for ref in z_refs:
        width = ref.shape[-1]
        for c0 in range(0, width, 512):
            c1 = min(c0 + 512, width)
            z = jnp.dot(hb, w_ref[:, off + c0:off + c1], preferred_element_type=F32)
            ref[0, :, c0:c1] = z.astype(ref.dtype)
        off += width


def _inproj(x, mod, mod_is_shared, gain, w_all, tm):
    b, l, d = x.shape
    mod_map = (lambda i, j: (0, 0, 0)) if mod_is_shared else (lambda i, j: (i, 0, 0))
    dtypes = (BF16, BF16, BF16, BF16, F32)
    out_shape = [jax.ShapeDtypeStruct((b, l, d), BF16)]
    out_specs = [pl.BlockSpec((1, tm, d), lambda i, j: (i, j, 0))]
    for w, dt in zip(SECTION_WIDTHS, dtypes):
        out_shape.append(jax.ShapeDtypeStruct((b, l, w), dt))
        out_specs.append(pl.BlockSpec((1, tm, w), lambda i, j: (i, j, 0)))
    return pl.pallas_call(
        _inproj_kernel,
        grid=(b, l // tm),
        in_specs=[pl.BlockSpec((1, tm, d), lambda i, j: (i, j, 0)),
                  pl.BlockSpec((1, 6, d), mod_map),
                  pl.BlockSpec((1, d), lambda i, j: (0, 0)),
                  pl.BlockSpec((d, W_ALL), lambda i, j: (0, 0), pipeline_mode=pl.Buffered(1))],
        out_specs=out_specs,
        out_shape=out_shape,
        compiler_params=_cparams(2),
        name="inproj",
    )(x, mod, gain.reshape(1, d), w_all)


def _pack_w_in(w_in):
    d = w_in.shape[0]
    hq, hv = RET_HEADS * RET_DK, RET_HEADS * RET_DV
    cols = []
    for h in range(RET_HEADS):
        q = w_in[:, h * RET_DK:(h + 1) * RET_DK]
        k = w_in[:, hq + h * RET_DK:hq + (h + 1) * RET_DK]
        cols += [q, q, k, k, w_in[:, 2 * hq + h * RET_DV:2 * hq + (h + 1) * RET_DV],
                 w_in[:, 2 * hq + hv + h * RET_DV:2 * hq + hv + (h + 1) * RET_DV]]
    rest = w_in[:, 2 * hq + 2 * hv:]
    pad = jnp.zeros((d, W_ALL - W_RET - rest.shape[1]), w_in.dtype)
    return jnp.concatenate(cols + [rest, pad], axis=1).astype(BF16)


def _merge_kernel(x_ref, h_ref, y0_ref, y1_ref, y2_ref, y3_ref, mod_ref, gain_ref, wm_ref, wb_ref, wo_ref,
                  wrh_ref, wrl_ref, xo_ref, h2_ref, sc_ref):
    d = x_ref.shape[-1]
    h = h_ref[0]
    acc = jnp.zeros(x_ref.shape[1:], F32)
    for i, y_ref in enumerate((y0_ref, y1_ref, y2_ref, y3_ref)):
        gate = jax.nn.sigmoid(jnp.dot(h, wm_ref[:, i * d:(i + 1) * d], preferred_element_type=F32))
        acc = acc + gate * jnp.dot(y_ref[0], wb_ref[i], preferred_element_type=F32)
    m = jnp.dot(acc.astype(BF16), wo_ref[...], preferred_element_type=F32)
    xn = x_ref[0] + mod_ref[0, 2:3, :] * m
    xo_ref[0] = xn
    ms = jnp.mean(xn * xn, axis=-1, keepdims=True)
    h2 = xn * lax.rsqrt(ms + EPS) * gain_ref[...] * (1.0 + mod_ref[0, 4:5, :]) + mod_ref[0, 3:4, :]
    hi = h2.astype(BF16)
    h2_ref[0] = h2
    lo = (h2 - hi.astype(F32)).astype(BF16)
    logits = (jnp.dot(hi, wrh_ref[...], preferred_element_type=F32)
              + jnp.dot(lo, wrh_ref[...], preferred_element_type=F32)
              + jnp.dot(hi, wrl_ref[...], preferred_element_type=F32))
    sc_ref[0] = jax.nn.sigmoid(logits)


def _merge(x, h, ys, mod, mod_is_shared, gain2, wm, wb, wo, wr_hi, wr_lo, tm):
    b, l, d = x.shape
    mod_map = (lambda i, j: (0, 0, 0)) if mod_is_shared else (lambda i, j: (i, 0, 0))
    tok = lambda w: pl.BlockSpec((1, tm, w), lambda i, j: (i, j, 0))
    full2 = lambda s: pl.BlockSpec(s, lambda i, j: (0, 0), pipeline_mode=pl.Buffered(1))
    return pl.pallas_call(
        _merge_kernel,
        grid=(b, l // tm),
        in_specs=[tok(d), tok(d)] + [tok(BRANCH_W)] * 4 + [
            pl.BlockSpec((1, 6, d), mod_map), full2((1, d)), full2(wm.shape),
            pl.BlockSpec(wb.shape, lambda i, j: (0, 0, 0), pipeline_mode=pl.Buffered(1)), full2(wo.shape),
            full2(wr_hi.shape), full2(wr_lo.shape)],
        out_specs=[tok(d), tok(d), tok(LANES)],
        out_shape=[jax.ShapeDtypeStruct((b, l, d), F32), jax.ShapeDtypeStruct((b, l, d), F32),
                   jax.ShapeDtypeStruct((b, l, LANES), F32)],
        compiler_params=_cparams(2),
        name="merge",
    )(x, h, *ys, mod, gain2.reshape(1, d), wm, wb, wo, wr_hi, wr_lo)


def _moe_kernel(be_ref, nv_ref, src_ref, dst_ref, h_hbm, wg_ref, wu_ref, wd_ref, y_hbm, xbuf, obuf, wg_s, wu_s, wd_s,
                gsem, ssem, *, n_real):
    i = pl.program_id(0)
    n = pl.num_programs(0)
    slot = i % 2
    nv = nv_ref[i]

    def start_gather(blk, dst_slot):
        def body(grp, carry):
            base = blk * MOE_ROWS + grp * MOE_ROW_ALIGN
            for j in range(MOE_ROW_ALIGN):
                pltpu.make_async_copy(h_hbm.at[pl.ds(src_ref[base + j], 1)], xbuf.at[dst_slot, grp, pl.ds(j, 1)],
                                      gsem.at[dst_slot]).start()
            return carry
        lax.fori_loop(0, nv_ref[blk] // MOE_ROW_ALIGN, body, 0)

    def wait_rows(buf, sem, cnt):
        grps = buf.at[pl.ds(0, cnt // MOE_ROW_ALIGN)]
        pltpu.make_async_copy(grps, grps, sem).wait()

    @pl.when(i == 0)
    def _():
        xbuf[...] = jnp.zeros_like(xbuf)
        fills = [pltpu.make_async_copy(xbuf.at[1, g], y_hbm.at[pl.ds(n_real + g * MOE_ROW_ALIGN, MOE_ROW_ALIGN)],
                                       ssem.at[1])
                 for g in range((y_hbm.shape[0] - n_real) // MOE_ROW_ALIGN)]
        for f in fills:
            f.start()
        for f in fills:
            f.wait()
        start_gather(0, 0)

    @pl.when(i + 1 < n)
    def _():
        start_gather(i + 1, 1 - slot)

    @pl.when((i >= 2) & (nv_ref[jnp.maximum(i - 2, 0)] > 0))
    def _():
        wait_rows(obuf.at[slot], ssem.at[slot], nv_ref[jnp.maximum(i - 2, 0)])

    @pl.when(nv > 0)
    def _():
        wait_rows(xbuf.at[slot], gsem.at[slot], nv)
        e = be_ref[i]

        @pl.when((i == 0) | (e != be_ref[jnp.maximum(i - 1, 0)]))
        def _():
            wg_s[...] = wg_ref[0].astype(BF16)
            wu_s[...] = wu_ref[0].astype(BF16)
            wd_s[...] = wd_ref[0].astype(BF16)

        d = xbuf.shape[-1]
        x = xbuf[slot].reshape(MOE_ROWS, d).astype(BF16)
        g = jnp.dot(x, wg_s[...], preferred_element_type=F32)
        u = jnp.dot(x, wu_s[...], preferred_element_type=F32)
        a = (g * jax.nn.sigmoid(g) * u).astype(BF16)
        obuf[slot] = jnp.dot(a, wd_s[...], preferred_element_type=F32).reshape(obuf.shape[1:])

        def body(grp, carry):
            base = i * MOE_ROWS + grp * MOE_ROW_ALIGN
            for j in range(MOE_ROW_ALIGN):
                pltpu.make_async_copy(obuf.at[slot, grp, pl.ds(j, 1)], y_hbm.at[pl.ds(dst_ref[base + j], 1)],
                                      ssem.at[slot]).start()
            return carry
        lax.fori_loop(0, nv // MOE_ROW_ALIGN, body, 0)

    @pl.when(i == n - 1)
    def _():
        @pl.when(nv > 0)
        def _():
            wait_rows(obuf.at[slot], ssem.at[slot], nv)

        @pl.when((n >= 2) & (nv_ref[jnp.maximum(i - 1, 0)] > 0))
        def _():
            wait_rows(obuf.at[1 - slot], ssem.at[1 - slot], nv_ref[jnp.maximum(i - 1, 0)])


def _moe_ffn(block_e, n_valid, src, dst, h2, w_gate, w_up, w_down):
    t, d = h2.shape
    grp_shape = (2, MOE_ROWS // MOE_ROW_ALIGN, MOE_ROW_ALIGN, d)
    n_blocks = block_e.shape[0]
    de = w_gate.shape[-1]
    wspec = lambda s: pl.BlockSpec((1,) + s, lambda i, be, nv, src, dst: (be[i], 0, 0))
    grid_spec = pltpu.PrefetchScalarGridSpec(
        num_scalar_prefetch=4,
        grid=(n_blocks,),
        in_specs=[pl.BlockSpec(memory_space=pl.ANY), wspec((d, de)), wspec((d, de)), wspec((de, d))],
        out_specs=pl.BlockSpec(memory_space=pl.ANY),
        scratch_shapes=[pltpu.VMEM(grp_shape, F32), pltpu.VMEM(grp_shape, F32),
                        pltpu.VMEM((d, de), BF16), pltpu.VMEM((d, de), BF16), pltpu.VMEM((de, d), BF16),
                        pltpu.SemaphoreType.DMA((2,)), pltpu.SemaphoreType.DMA((2,))],
    )
    return pl.pallas_call(
        functools.partial(_moe_kernel, n_real=TOP_K * t),
        grid_spec=grid_spec,
        out_shape=jax.ShapeDtypeStruct((TOP_K * t + MOE_ROW_ALIGN * N_EXPERTS, d), F32),
        compiler_params=pltpu.CompilerParams(dimension_semantics=("arbitrary",), vmem_limit_bytes=VMEM_LIMIT,
                                             has_side_effects=True),
        name="moe_ffn",
    )(block_e, n_valid, src, dst, h2, w_gate, w_up, w_down)


def _route(scores, router_bias):
    t = scores.shape[0]
    sel = (scores + router_bias.astype(F32)).reshape(t, N_GROUPS, EXPERTS_PER_GROUP)
    pairs = [sel[..., i] + sel[..., j] for i in range(EXPERTS_PER_GROUP) for j in range(i + 1, EXPERTS_PER_GROUP)]
    grp_score = functools.reduce(jnp.maximum, pairs)
    g_idx = jnp.argmax(grp_score, axis=-1).astype(jnp.int32)
    g_hot = (g_idx[:, None] == jnp.arange(N_GROUPS, dtype=jnp.int32)[None, :])[:, :, None]
    in_grp = jnp.sum(jnp.where(g_hot, sel, 0.0), axis=1)
    sc_grp = jnp.sum(jnp.where(g_hot, scores.reshape(t, N_GROUPS, EXPERTS_PER_GROUP), 0.0), axis=1)
    lane4 = jnp.arange(EXPERTS_PER_GROUP, dtype=jnp.int32)[None, :]
    i1 = jnp.argmax(in_grp, axis=-1).astype(jnp.int32)
    i2 = jnp.argmax(jnp.where(lane4 == i1[:, None], -jnp.inf, in_grp), axis=-1).astype(jnp.int32)
    e_idx = g_idx[:, None] * EXPERTS_PER_GROUP + jnp.stack([i1, i2], axis=-1)
    wts = jnp.stack([jnp.sum(jnp.where(lane4 == i1[:, None], sc_grp, 0.0), axis=-1),
                     jnp.sum(jnp.where(lane4 == i2[:, None], sc_grp, 0.0), axis=-1)], axis=-1)
    return e_idx, wts / jnp.sum(wts, axis=-1, keepdims=True)


def _moe(h2, scores, router_bias, w_gate, w_up, w_down):
    t, d = h2.shape
    e_idx, wts = _route(scores, router_bias)
    a = t * TOP_K
    flat_e = e_idx.T.reshape(-1)
    onehot = (flat_e[:, None] == jnp.arange(N_EXPERTS, dtype=jnp.int32)[None, :]).astype(jnp.int32)
    rb = MOE_ROWS
    blocks = onehot.reshape(a // rb, rb, N_EXPERTS).astype(BF16)
    within = jnp.einsum('ij,bje->bie', jnp.tril(jnp.ones((rb, rb), BF16)), blocks, preferred_element_type=F32)
    before = jnp.cumsum(within[:, -1, :], axis=0) - within[:, -1, :]
    csum = (within + before[:, None, :]).astype(jnp.int32).reshape(a, N_EXPERTS)
    counts = csum[-1]
    rank = jnp.sum(csum * onehot, axis=1) - 1
    padded = (counts + MOE_ROWS - 1) // MOE_ROWS * MOE_ROWS
    ends = jnp.cumsum(padded)
    pstarts = ends - padded
    dest = jnp.sum(pstarts[None, :] * onehot, axis=1) + rank
    n_blocks = -(-a // MOE_ROWS) + N_EXPERTS
    blk0 = jnp.arange(n_blocks, dtype=jnp.int32) * MOE_ROWS
    block_e = jnp.minimum(jnp.sum((ends[None, :] <= blk0[:, None]).astype(jnp.int32), axis=1), N_EXPERTS - 1)
    n_valid = jnp.clip((pstarts + counts)[block_e] - blk0, 0, MOE_ROWS)
    n_valid = ((n_valid + MOE_ROW_ALIGN - 1) // MOE_ROW_ALIGN * MOE_ROW_ALIGN).astype(jnp.int32)
    slot = jnp.arange(n_blocks * MOE_ROWS, dtype=jnp.int32)
    spare = a + MOE_ROW_ALIGN * jnp.repeat(block_e, MOE_ROWS) + slot % MOE_ROW_ALIGN
    dst = spare.at[dest].set(jnp.arange(a, dtype=jnp.int32))
    src = jnp.where(dst < a, dst % t, 0)
    return _moe_ffn(block_e, n_valid, src, dst, h2, w_gate, w_up, w_down), wts


def _combine_kernel(x_ref, y0_ref, y1_ref, w_ref, mod_ref, o_ref):
    w = w_ref[...]
    y = y0_ref[...] * w[:, 0:1] + y1_ref[...] * w[:, 1:2]
    o_ref[0] = x_ref[0] + mod_ref[0, 5:6, :] * y


def _combine(x, y, wts, mod, mod_is_shared, row_offset, tm):
    b, l, d = x.shape
    mod_map = (lambda i, j: (0, 0, 0)) if mod_is_shared else (lambda i, j: (i, 0, 0))
    off = row_offset // tm
    per = l // tm
    return pl.pallas_call(
        _combine_kernel,
        grid=(b, per),
        in_specs=[pl.BlockSpec((1, tm, d), lambda i, j: (i, j, 0)),
                  pl.BlockSpec((tm, d), lambda i, j: (off + i * per + j, 0)),
                  pl.BlockSpec((tm, d), lambda i, j: (wts.shape[0] // tm + off + i * per + j, 0)),
                  pl.BlockSpec((tm, TOP_K), lambda i, j: (off + i * per + j, 0)),
                  pl.BlockSpec((1, 6, d), mod_map)],
        out_specs=pl.BlockSpec((1, tm, d), lambda i, j: (i, j, 0)),
        out_shape=jax.ShapeDtypeStruct((b, l, d), F32),
        compiler_params=_cparams(2),
        name="moe_combine",
    )(x, y, y, wts, mod)


def _head_rms(x, gain, e_ref):
    ms = jnp.dot((x * x).astype(BF16), e_ref[...], preferred_element_type=F32) * (1.0 / HEAD_DIM)
    return x * lax.rsqrt(ms + EPS) * gain


def _rope(y, cos, sin):
    lane = lax.broadcasted_iota(jnp.int32, y.shape, 1)
    half = HEAD_DIM // 2
    rot = jnp.where((lane % HEAD_DIM) < half, pltpu.roll(y, LANES - half, 1), pltpu.roll(y, half, 1))
    return y * cos + rot * sin


def _attend(problems):
    all_scores = []
    for q, tiles, _ in problems:
        scores = []
        for k, _, bias in tiles:
            s = lax.dot_general(q, k, (((1,), (1,)), ((), ())), preferred_element_type=F32)
            scores.append(s if bias is None else s + bias)
        if all(s.shape[1] % LANES == 0 for s in scores):
            scores = [jnp.concatenate(scores, axis=1)]
        all_scores.append(scores)
    maxes = []
    for scores, (_, _, sink) in zip(all_scores, problems):
        m = sink
        for s in scores:
            mt = jnp.max(s, axis=-1, keepdims=True)
            m = mt if m is None else jnp.maximum(m, mt)
        maxes.append(m)
    all_probs, dens = [], []
    for scores, m, (_, tiles, sink) in zip(all_scores, maxes, problems):
        den = jnp.exp(sink - m) if sink is not None else jnp.zeros_like(m)
        probs = []
        for s in scores:
            p = jnp.exp(s - m)
            den = den + jnp.sum(p, axis=-1, keepdims=True)
            probs.append(p.astype(BF16))
        if len(probs) != len(tiles):
            offs = np.cumsum([0] + [v.shape[0] for _, v, _ in tiles])
            probs = [probs[0][:, offs[i]:offs[i + 1]] for i in range(len(tiles))]
        all_probs.append(probs)
        dens.append(den)
    outs = []
    for probs, den, (_, tiles, _) in zip(all_probs, dens, problems):
        o = None
        for p, (_, v, _) in zip(probs, tiles):
            pv = jnp.dot(p, v, preferred_element_type=F32)
            o = pv if o is None else o + pv
        outs.append(o * (1.0 / den))
    return outs


def _lane_half(shape):
    return lax.broadcasted_iota(jnp.int32, shape, 1) // HEAD_DIM


def _win_group_queries(slabs, g, sink_ref):
    r = WIN_HEADS // WIN_KV_HEADS
    rows = slabs[0].shape[0]
    half = _lane_half(slabs[0].shape)
    parts, sinks = [], []
    for j in range(r):
        head = g * r + j
        slab = slabs[head // 2]
        if head % 2 != g:
            slab = pltpu.roll(slab, HEAD_DIM, 1)
        parts.append(jnp.where(half == g, slab, 0.0))
        sinks.append(jnp.full((rows, 1), sink_ref[head], F32))
    return jnp.concatenate(parts, axis=0).astype(BF16), jnp.concatenate(sinks, axis=0)


def _win_store(o_ref, o, g, rows):
    r = WIN_HEADS // WIN_KV_HEADS
    half = _lane_half((rows, LANES))
    for pair in range(r // 2):
        a = o[(2 * pair) * rows:(2 * pair + 1) * rows]
        b = o[(2 * pair + 1) * rows:(2 * pair + 2) * rows]
        if g == 0:
            b = pltpu.roll(b, HEAD_DIM, 1)
        else:
            a = pltpu.roll(a, HEAD_DIM, 1)
        s = (g * r + 2 * pair) // 2
        o_ref[0, :, s * LANES:(s + 1) * LANES] = jnp.where(half == 0, a, b).astype(o_ref.dtype)


def _win_kernel(sink_ref, q_ref, kv_ref, ckv_ref, cos_ref, sin_ref, qg_ref, kg_ref, e_ref, o_ref, ks_ref, kcs_ref):
    n = pl.program_id(1)
    nb = pl.num_programs(1)
    l = kv_ref.shape[1]
    blk = WIN_BLOCK
    prep_rows = 512

    @pl.when(n == 0)
    def _():
        def body(i, carry):
            r0 = pl.multiple_of(i * prep_rows, prep_rows)
            y = _head_rms(kv_ref[0, pl.ds(r0, prep_rows), 0:LANES].astype(F32), kg_ref[...], e_ref)
            ks_ref[pl.ds(r0, prep_rows), :] = _rope(y, cos_ref[pl.ds(r0, prep_rows), :],
                                                    sin_ref[pl.ds(r0, prep_rows), :]).astype(BF16)
            return carry
        lax.fori_loop(0, l // prep_rows, body, 0)
        kcs_ref[...] = _head_rms(ckv_ref[0, :, 0:LANES].astype(F32), kg_ref[...], e_ref).astype(BF16)

    r0 = pl.multiple_of(n * blk, blk)
    cos = cos_ref[pl.ds(r0, blk), :]
    sin = sin_ref[pl.ds(r0, blk), :]
    slabs = []
    for s in range(q_ref.shape[-1] // LANES):
        y = _head_rms(q_ref[0, :, s * LANES:(s + 1) * LANES].astype(F32), qg_ref[...], e_ref)
        slabs.append(_rope(y, cos, sin))

    r = WIN_HEADS // WIN_KV_HEADS
    qi = lax.broadcasted_iota(jnp.int32, (r * blk, blk), 0) % blk
    kj = lax.broadcasted_iota(jnp.int32, (r * blk, blk), 1)
    ok_prev = (kj >= qi) & (n > 0)
    ok_next = (kj <= qi) & (n < nb - 1)
    bias_prev = jnp.where(ok_prev, 0.0, NEG_INF)
    bias_next = jnp.where(ok_next, 0.0, NEG_INF)
    tiles = []
    for kb, bias in ((jnp.maximum(n - 1, 0), bias_prev), (n, None), (jnp.minimum(n + 1, nb - 1), bias_next)):
        k0 = pl.multiple_of(kb * blk, blk)
        tiles.append((ks_ref[pl.ds(k0, blk), :], kv_ref[0, pl.ds(k0, blk), LANES:2 * LANES], bias))
    tiles.append((kcs_ref[...], ckv_ref[0, :, LANES:2 * LANES], None))
    rows = 2 * blk
    tiles = [(k, v, None if bias is None else bias[:rows]) for k, v, bias in tiles]
    for g in range(WIN_KV_HEADS):
        q, sink = _win_group_queries(slabs, g, sink_ref)
        outs = [_attend([(q[h * rows:(h + 1) * rows], tiles, sink[h * rows:(h + 1) * rows])])[0]
                for h in range(q.shape[0] // rows)]
        _win_store(o_ref, jnp.concatenate(outs, axis=0), g, blk)


def _win_ctx_kernel(sink_ref, q_ref, ckv_ref, qg_ref, kg_ref, e_ref, o_ref):
    kc = _head_rms(ckv_ref[0, :, 0:LANES].astype(F32), kg_ref[...], e_ref).astype(BF16)
    tiles = [(kc, ckv_ref[0, :, LANES:2 * LANES], None)]
    slabs = [_head_rms(q_ref[0, :, s * LANES:(s + 1) * LANES].astype(F32), qg_ref[...], e_ref)
             for s in range(q_ref.shape[-1] // LANES)]
    groups = [_win_group_queries(slabs, g, sink_ref) for g in range(WIN_KV_HEADS)]
    outs = _attend([(q, tiles, sink) for q, sink in groups])
    for g in range(WIN_KV_HEADS):
        _win_store(o_ref, outs[g], g, q_ref.shape[1])


def _block_diag_ones(width):
    i = np.arange(width) // HEAD_DIM
    return jnp.asarray(i[:, None] == i[None, :], BF16)


def _window_branch(zw_l, zw_c, q_gain, k_gain, sink, cos2, sin2, ctx_out):
    b, l, _ = zw_l.shape
    lc = zw_c.shape[1]
    wq = WIN_HEADS * HEAD_DIM
    qg = (jnp.tile(q_gain.astype(F32), 2) * HEAD_DIM ** -0.5).reshape(1, LANES)
    kg = jnp.tile(k_gain.astype(F32), 2).reshape(1, LANES)
    e = _block_diag_ones(LANES)
    smem = pl.BlockSpec(memory_space=pltpu.SMEM)
    full = lambda s, nd: pl.BlockSpec(s, lambda *a: (0,) * nd)
    yl = pl.pallas_call(
        _win_kernel,
        grid=(b, l // WIN_BLOCK),
        in_specs=[smem,
                  pl.BlockSpec((1, WIN_BLOCK, wq), lambda i, n: (i, n, 0)),
                  pl.BlockSpec((1, l, 2 * LANES), lambda i, n: (i, 0, wq // (2 * LANES))),
                  pl.BlockSpec((1, lc, 2 * LANES), lambda i, n: (i, 0, wq // (2 * LANES))),
                  full((l, LANES), 2), full((l, LANES), 2), full((1, LANES), 2), full((1, LANES), 2),
                  full((LANES, LANES), 2)],
        out_specs=pl.BlockSpec((1, WIN_BLOCK, wq), lambda i, n: (i, n, 0)),
        out_shape=jax.ShapeDtypeStruct((b, l, wq), BF16),
        scratch_shapes=[pltpu.VMEM((l, LANES), BF16), pltpu.VMEM((lc, LANES), BF16)],
        compiler_params=_cparams(2),
        name="window_attn",
    )(sink.astype(F32), zw_l, zw_l, zw_c, cos2, sin2, qg, kg, e)
    yc = None
    if ctx_out:
        yc = pl.pallas_call(
            _win_ctx_kernel,
            grid=(b,),
            in_specs=[smem,
                      pl.BlockSpec((1, lc, wq), lambda i: (i, 0, 0)),
                      pl.BlockSpec((1, lc, 2 * LANES), lambda i: (i, 0, wq // (2 * LANES))),
                      full((1, LANES), 2), full((1, LANES), 2), full((LANES, LANES), 2)],
            out_specs=pl.BlockSpec((1, lc, wq), lambda i: (i, 0, 0)),
            out_shape=jax.ShapeDtypeStruct((b, lc, wq), BF16),
            compiler_params=_cparams(1),
            name="window_ctx_attn",
        )(sink.astype(F32), zw_c, zw_c, qg, kg, e)
    return yc, yl


NA_ROWS = 8
NA_KROWS = NA_ROWS + NA_KH - 1


def _na_key_base(rg, rows):
    return jnp.clip(rg * NA_ROWS - NA_KH // 2, 0, rows - NA_KROWS)


def _na_kernel(q_ref, k_ref, v_ref, ck_ref, cv_ref, bias_ref, qg_ref, kg_ref, e_ref, o_ref, ks_ref, kcs_ref):
    rg = pl.program_id(1)
    l = k_ref.shape[1]
    rows = l // GRID_W
    n_slab = q_ref.shape[-1] // LANES
    prep_rows = 512

    @pl.when(rg == 0)
    def _():
        def body(i, carry):
            r0 = pl.multiple_of(i * prep_rows, prep_rows)
            ks_ref[pl.ds(r0, prep_rows), :] = _head_rms(k_ref[0, pl.ds(r0, prep_rows), :].astype(F32), kg_ref[...],
                                                        e_ref).astype(BF16)
            return carry
        lax.fori_loop(0, l // prep_rows, body, 0)
        kcs_ref[...] = _head_rms(ck_ref[0].astype(F32), kg_ref[...], e_ref).astype(BF16)

    nk = NA_KROWS * GRID_W
    k0 = pl.multiple_of(_na_key_base(rg, rows) * GRID_W, GRID_W)
    qn = _head_rms(q_ref[0].astype(F32), qg_ref[...], e_ref)
    half = _lane_half((q_ref.shape[1], LANES))
    for s in range(n_slab):
        cols = slice(s * LANES, (s + 1) * LANES)
        slab = qn[:, cols]
        tiles_kv = (ks_ref[pl.ds(k0, nk), cols], v_ref[0, pl.ds(k0, nk), cols])
        ctx_kv = (kcs_ref[:, cols], cv_ref[0, :, cols])
        outs = [_attend([(jnp.where(half == hh, slab, 0.0).astype(BF16),
                          [tiles_kv + (bias_ref[0, 2 * s + hh].astype(F32),), ctx_kv + (None,)], None)])[0]
                for hh in range(2)]
        o_ref[0, :, cols] = jnp.where(half == 0, outs[0], outs[1]).astype(o_ref.dtype)


def _na_ctx_kernel(q_ref, ck_ref, cv_ref, qg_ref, kg_ref, e_ref, o_ref):
    kc = _head_rms(ck_ref[0].astype(F32), kg_ref[...], e_ref).astype(BF16)
    qn = _head_rms(q_ref[0].astype(F32), qg_ref[...], e_ref)
    half = _lane_half((q_ref.shape[1], LANES))
    for s in range(q_ref.shape[-1] // LANES):
        cols = slice(s * LANES, (s + 1) * LANES)
        outs = _attend([(jnp.where(half == hh, qn[:, cols], 0.0).astype(BF16),
                         [(kc[:, cols], cv_ref[0, :, cols], None)], None) for hh in range(2)])
        o_ref[0, :, cols] = jnp.where(half == 0, outs[0], outs[1]).astype(o_ref.dtype)


def _na_bias_classes(rows):
    n_rg = rows // NA_ROWS
    return list(range(n_rg)) if n_rg <= 3 else [0, 1, n_rg - 1]


def _na_bias_table(rpb, rows):
    kh = NA_KH
    ro, rv = [], []
    for rg in _na_bias_classes(rows):
        kbase = int(np.clip(rg * NA_ROWS - kh // 2, 0, rows - NA_KROWS))
        r = rg * NA_ROWS + np.arange(NA_ROWS)
        rstart = np.clip(r - kh // 2, 0, rows - kh)
        kr = kbase + np.arange(NA_KROWS)
        rv.append((kr[None, :] >= rstart[:, None]) & (kr[None, :] < rstart[:, None] + kh))
        ro.append(np.clip(kr[None, :] - r[:, None] + kh - 1, 0, 2 * kh - 2))
    ro, rv = np.stack(ro), np.stack(rv)
    qc = np.arange(GRID_W)
    cstart = np.clip(qc - NA_KW // 2, 0, GRID_W - NA_KW)
    cv = (qc[None, :] >= cstart[:, None]) & (qc[None, :] < cstart[:, None] + NA_KW)
    co = np.clip(qc[None, :] - qc[:, None] + NA_KW - 1, 0, 2 * NA_KW - 2)
    n_cls = ro.shape[0]
    hp = lax.Precision.HIGHEST
    co_hot = jnp.asarray(co[None] == np.arange(2 * NA_KW - 1)[:, None, None], F32)
    ro_hot = jnp.asarray(ro[..., None] == np.arange(2 * kh - 1), F32)
    cols = jnp.einsum('hdc,cwx->hdwx', rpb.astype(F32), co_hot, precision=hp)
    tab = jnp.einsum('crkd,hdwx->chrwkx', ro_hot, cols, precision=hp)
    ok = rv[:, None, :, None, :, None] & cv[None, None, None, :, None, :]
    tab = jnp.where(ok, tab, NEG_INF)
    return tab.reshape(n_cls, rpb.shape[0], NA_ROWS * GRID_W, NA_KROWS * GRID_W).astype(BF16)


def _neighborhood_branch(zn_l, zn_c, q_gain, k_gain, rpb, ctx_out):
    b, l, _ = zn_l.shape
    lc = zn_c.shape[1]
    w = NA_HEADS * HEAD_DIM
    rows = l // GRID_W
    n_rg = rows // NA_ROWS
    n_slab = w // LANES
    qg = (jnp.tile(q_gain.astype(F32), 2 * n_slab) * HEAD_DIM ** -0.5).reshape(1, w)
    kg = jnp.tile(k_gain.astype(F32), 2 * n_slab).reshape(1, w)
    e = _block_diag_ones(w)
    bias = _na_bias_table(rpb, rows)
    if n_rg <= 3:
        cls_map = lambda i, r: (r, 0, 0, 0)
    else:
        cls_map = lambda i, r: ((r > 0).astype(jnp.int32) + (r == n_rg - 1).astype(jnp.int32), 0, 0, 0)
    tq = NA_ROWS * GRID_W
    full = lambda s, nd: pl.BlockSpec(s, lambda *a: (0,) * nd)
    yl = pl.pallas_call(
        _na_kernel,
        grid=(b, n_rg),
        in_specs=[pl.BlockSpec((1, tq, w), lambda i, r: (i, r, 0)),
                  pl.BlockSpec((1, l, w), lambda i, r: (i, 0, 1)),
                  pl.BlockSpec((1, l, w), lambda i, r: (i, 0, 2)),
                  pl.BlockSpec((1, lc, w), lambda i, r: (i, 0, 1)),
                  pl.BlockSpec((1, lc, w), lambda i, r: (i, 0, 2)),
                  pl.BlockSpec((1,) + bias.shape[1:], cls_map),
                  full((1, w), 2), full((1, w), 2), full((w, w), 2)],
        out_specs=pl.BlockSpec((1, tq, w), lambda i, r: (i, r, 0)),
        out_shape=jax.ShapeDtypeStruct((b, l, w), BF16),
        scratch_shapes=[pltpu.VMEM((l, w), BF16), pltpu.VMEM((lc, w), BF16)],
        compiler_params=_cparams(2),
        name="neighborhood_attn",
    )(zn_l, zn_l, zn_l, zn_c, zn_c, bias, qg, kg, e)
    yc = None
    if ctx_out:
        yc = pl.pallas_call(
            _na_ctx_kernel,
            grid=(b,),
            in_specs=[pl.BlockSpec((1, lc, w), lambda i: (i, 0, 0)),
                      pl.BlockSpec((1, lc, w), lambda i: (i, 0, 1)),
                      pl.BlockSpec((1, lc, w), lambda i: (i, 0, 2)),
                      full((1, w), 2), full((1, w), 2), full((w, w), 2)],
            out_specs=pl.BlockSpec((1, lc, w), lambda i: (i, 0, 0)),
            out_shape=jax.ShapeDtypeStruct((b, lc, w), BF16),
            compiler_params=_cparams(1),
            name="neighborhood_ctx_attn",
        )(zn_c, zn_c, zn_c, qg, kg, e)
    return yc, yl


def _rope_tables(n_tokens):
    t = np.arange(n_tokens)
    n_freq = HEAD_DIM // 4
    inv = ROPE_THETA ** (-jnp.arange(n_freq, dtype=F32) / n_freq)
    ang = jnp.concatenate([jnp.asarray(t // GRID_W, F32)[:, None] * inv, jnp.asarray(t % GRID_W, F32)[:, None] * inv],
                          axis=-1)
    cos, sin = jnp.cos(ang), jnp.sin(ang)
    return jnp.tile(jnp.concatenate([cos, cos], -1), (1, 2)), jnp.tile(jnp.concatenate([-sin, sin], -1), (1, 2))


def _ret_kernel(qq_l, kk_l, v_l, g_l, qq_c, kk_c, v_c, g_c, cos_ref, sin_ref, dmask_ref, tq_ref, tk_ref, cdec_ref,
                gain_ref, yl_ref, yc_ref, kr_ref, kv_ref, sin_state_ref, *, ctx_out):
    c = RET_CHUNK
    ncc = qq_c.shape[1] // c
    ncl = qq_l.shape[1] // c
    nc = ncc + ncl
    tk = tk_ref[0]
    tq = tq_ref[0]
    dmask = dmask_ref[0]
    fwd_lanes = lax.broadcasted_iota(jnp.int32, (c, LANES), 1) < RET_DK

    def chunk_kv(k2, v):
        kd = (k2 * tk).astype(BF16)
        return lax.dot_general(kd, v, (((0,), (0,)), ((), ())), preferred_element_type=F32)

    for n in range(ncc):
        kv_ref[n] = chunk_kv(kk_c[0, n * c:(n + 1) * c, :].astype(F32), v_c[0, n * c:(n + 1) * c, :])

    def kv_body(n, carry):
        r0 = pl.multiple_of(n * c, c)
        k2 = _rope(kk_l[0, pl.ds(r0, c), :].astype(F32), cos_ref[pl.ds(r0, c), :], sin_ref[pl.ds(r0, c), :])
        kr_ref[pl.ds(r0, c), :] = k2.astype(BF16)
        kv_ref[ncc + n] = chunk_kv(k2, v_l[0, pl.ds(r0, c), :])
        return carry
    lax.fori_loop(0, ncl, kv_body, 0, unroll=4)

    dec_f = cdec_ref[0, 0:1, :]
    dec_b = cdec_ref[0, 1:2, :]

    def scan_body(t, carry):
        sf, sb = carry
        sin_state_ref[t, 0:RET_DK, :] = sf.astype(BF16)
        sf = sf * dec_f + kv_ref[t, 0:RET_DK, :]
        tb = jnp.where(t < ncc, ncc - 1 - t, nc - 1 - (t - ncc))
        sin_state_ref[tb, RET_DK:2 * RET_DK, :] = sb.astype(BF16)
        sb = sb * dec_b + kv_ref[tb, RET_DK:2 * RET_DK, :]
        return sf, sb
    zero = jnp.zeros((RET_DK, RET_DV), F32)
    lax.fori_loop(0, nc, scan_body, (zero, zero))

    def chunk_out(q2, k2b, v, gate, state):
        qm = jnp.where(fwd_lanes, q2, 0.0).astype(BF16)
        scores = lax.dot_general(qm, k2b, (((1,), (1,)), ((), ())), preferred_element_type=F32) * dmask
        o = (jnp.dot(scores.astype(BF16), v, preferred_element_type=F32)
             + jnp.dot((q2 * tq).astype(BF16), state, preferred_element_type=F32))
        mu = jnp.mean(o, axis=-1, keepdims=True)
        var = jnp.mean(jnp.square(o - mu), axis=-1, keepdims=True)
        y = (o - mu) * lax.rsqrt(var + EPS) * gain_ref[0]
        gf = gate.astype(F32)
        return y * gf * jax.nn.sigmoid(gf)

    if ctx_out:
        for n in range(ncc):
            rows = slice(n * c, (n + 1) * c)
            yc_ref[0, rows, :] = chunk_out(qq_c[0, rows, :].astype(F32), kk_c[0, rows, :], v_c[0, rows, :],
                                           g_c[0, rows, :], sin_state_ref[n]).astype(yc_ref.dtype)
    else:
        yc_ref[...] = jnp.zeros_like(yc_ref)

    def out_body(n, carry):
        r0 = pl.multiple_of(n * c, c)
        q2 = _rope(qq_l[0, pl.ds(r0, c), :].astype(F32), cos_ref[pl.ds(r0, c), :], sin_ref[pl.ds(r0, c), :])
        yl_ref[0, pl.ds(r0, c), :] = chunk_out(q2, kr_ref[pl.ds(r0, c), :], v_l[0, pl.ds(r0, c), :],
                                               g_l[0, pl.ds(r0, c), :], sin_state_ref[ncc + n]).astype(yl_ref.dtype)
        return carry
    lax.fori_loop(0, ncl, out_body, 0, unroll=4)


def _ret_tables(decay_logit):
    c = RET_CHUNK
    lg = jax.nn.log_sigmoid(decay_logit.astype(F32))
    lf, lb = lg[0][:, None, None], lg[1][:, None, None]
    pos = jnp.arange(c, dtype=F32)
    diff = pos[:, None] - pos[None, :]
    dmask = (jnp.where(diff >= 0, jnp.exp(lf * jnp.maximum(diff, 0.0)), 0.0)
             + jnp.where(diff <= 0, jnp.exp(lb * jnp.maximum(-diff, 0.0)), 0.0)) * RET_DK ** -0.5
    col = lambda a, b_: jnp.concatenate([jnp.broadcast_to(a, a.shape[:2] + (RET_DK,)),
                                         jnp.broadcast_to(b_, b_.shape[:2] + (RET_DK,))], axis=-1)
    p = pos[None, :, None]
    tq = col(jnp.exp(lf * (p + 1.0)), jnp.exp(lb * (c - p)))
    tk = col(jnp.exp(lf * (c - 1.0 - p)), jnp.exp(lb * p)) * RET_DK ** -0.5
    cdec = jnp.zeros((lg.shape[1], 8, RET_DV), F32)
    cdec = cdec.at[:, 0, :].set(jnp.exp(lg[0] * c)[:, None]).at[:, 1, :].set(jnp.exp(lg[1] * c)[:, None])
    return dmask, tq, tk, cdec


def _retention_branch(zr_l, zr_c, decay_logit, gn_gain, cos2, sin2, ctx_out):
    b, l, _ = zr_l.shape
    lc = zr_c.shape[1]
    h = RET_HEADS
    nc = (l + lc) // RET_CHUNK
    dmask, tq, tk, cdec = _ret_tables(decay_logit)
    gain = gn_gain.astype(F32).reshape(h, 1, RET_DV)
    seq = lambda n, j: pl.BlockSpec((1, n, LANES), lambda i, hh: (i, 0, 4 * hh + j))
    head = lambda s: pl.BlockSpec((1,) + s, lambda i, hh: (hh, 0, 0))
    full = lambda s: pl.BlockSpec(s, lambda i, hh: (0, 0))
    yl, yc = pl.pallas_call(
        functools.partial(_ret_kernel, ctx_out=ctx_out),
        grid=(b, h),
        in_specs=[seq(l, 0), seq(l, 1), seq(l, 2), seq(l, 3), seq(lc, 0), seq(lc, 1), seq(lc, 2), seq(lc, 3),
                  full((l, LANES)), full((l, LANES)),
                  head((RET_CHUNK, RET_CHUNK)), head((RET_CHUNK, LANES)), head((RET_CHUNK, LANES)), head((8, RET_DV)),
                  head((1, RET_DV))],
        out_specs=[pl.BlockSpec((1, l, RET_DV), lambda i, hh: (i, 0, hh)),
                   pl.BlockSpec((1, lc, RET_DV), lambda i, hh: (i, 0, hh))],
        out_shape=[jax.ShapeDtypeStruct((b, l, h * RET_DV), BF16), jax.ShapeDtypeStruct((b, lc, h * RET_DV), BF16)],
        scratch_shapes=[pltpu.VMEM((l, LANES), BF16), pltpu.VMEM((nc, 2 * RET_DK, RET_DV), F32),
                        pltpu.VMEM((nc, 2 * RET_DK, RET_DV), BF16)],
        compiler_params=_cparams(2),
        name="retention",
    )(zr_l, zr_l, zr_l, zr_l, zr_c, zr_c, zr_c, zr_c, cos2, sin2, dmask, tq, tk, cdec, gain)
    return (yc if ctx_out else None), yl


GDN_SUPER = 128
GDN_UNITS = 8
GDN_HALO = 128


def _mask_dot_col(mask_bf16, col):
    hi = col.astype(BF16).astype(F32)
    low_half = lax.broadcasted_iota(jnp.int32, (col.shape[0], LANES), 1) < LANES // 2
    r = jnp.dot(mask_bf16, jnp.where(low_half, hi, col - hi).astype(BF16), preferred_element_type=F32)
    return r + pltpu.roll(r, LANES // 2, 1)


def _softplus(x):
    return jnp.maximum(x, 0.0) + jnp.log(1.0 + jnp.exp(-jnp.abs(x)))


def _gdn_kernel(nega_ref, dtb_ref, q_l, k_l, v_l, g_l, ab_l, q_c, k_c, v_c, g_c, ab_c,
                cwq_ref, cwk_ref, cwv_ref, pd_ref, pu_ref, gain_ref, yl_ref, yc_ref,
                kn_s, sin_s, qp_s, o0_s, cd_s, *, ctx_out):
    hd = pl.program_id(1)
    c = GDN_CHUNK
    sup = GDN_SUPER
    per = sup // c
    lc, l = q_c.shape[1], q_l.shape[1]
    ncc, ncl = lc // c, l // c
    nc = ncc + ncl

    ri = lax.broadcasted_iota(jnp.int32, (sup, sup), 0)
    ci = lax.broadcasted_iota(jnp.int32, (sup, sup), 1)
    same = (ri // c) == (ci // c)
    eye = (ri == ci).astype(F32)
    incl = (same & (ri >= ci), same & (ri <= ci))
    strict = (same & (ri > ci), same & (ri < ci))
    incl_b = tuple(m.astype(BF16) for m in incl)
    same_b = same.astype(BF16)
    lane = lax.broadcasted_iota(jnp.int32, (sup, LANES), 1)
    rowc = lax.broadcasted_iota(jnp.int32, (sup, LANES), 0)

    def conv_silu(z_ref, w_ref, r0, ls):
        z = z_ref[0, pl.ds(r0, sup), :]
        if isinstance(r0, int):
            zero = jnp.zeros((GDN_HALO, LANES), BF16)
            prev = z_ref[0, r0 - GDN_HALO:r0, :] if r0 > 0 else zero
            nxt = z_ref[0, r0 + sup:r0 + sup + GDN_HALO, :] if r0 + sup < ls else zero
        else:
            p0 = pl.multiple_of(jnp.maximum(r0 - GDN_HALO, 0), GDN_HALO)
            n0 = pl.multiple_of(jnp.minimum(r0 + sup, ls - GDN_HALO), GDN_HALO)
            prev = jnp.where(r0 > 0, z_ref[0, pl.ds(p0, GDN_HALO), :], jnp.zeros((), BF16))
            nxt = jnp.where(r0 + sup < ls, z_ref[0, pl.ds(n0, GDN_HALO), :], jnp.zeros((), BF16))
        win = jnp.concatenate([prev, z, nxt], axis=0)
        z_dn = jnp.dot(pd_ref[...], win, preferred_element_type=F32)
        z_up = jnp.dot(pu_ref[...], win, preferred_element_type=F32)
        y = z_dn * w_ref[0, 0:1, :] + z.astype(F32) * w_ref[0, 1:2, :] + z_up * w_ref[0, 2:3, :]
        return y * jax.nn.sigmoid(y)

    def prep(units):
        common = []
        for refs, r0, base, ls in units:
            q_ref, k_ref, v_ref, ab_ref = refs
            q = conv_silu(q_ref, cwq_ref, r0, ls)
            k = conv_silu(k_ref, cwk_ref, r0, ls)
            v = conv_silu(v_ref, cwv_ref, r0, ls)
            q = q * lax.rsqrt(jnp.sum(q * q, axis=-1, keepdims=True) + EPS) * GDN_DK ** -0.5
            k = k * lax.rsqrt(jnp.sum(k * k, axis=-1, keepdims=True) + EPS)
            kb16 = k.astype(BF16)
            qk = lax.dot_general(q.astype(BF16), kb16, (((1,), (1,)), ((), ())), preferred_element_type=F32)
            ab = ab_ref[0, pl.ds(r0, sup), :]
            common.append((q, k, v, kb16, qk, ab, base + r0))
        chains = []
        for q, k, v, kb16, qk, ab, row0 in common:
            for d in range(2):
                ia = d * GDN_HEADS + hd
                ib = 2 * GDN_HEADS + ia
                a_col = jnp.sum(jnp.where(lane == ia, ab, 0.0), axis=1, keepdims=True)
                b_col = jnp.sum(jnp.where(lane == ib, ab, 0.0), axis=1, keepdims=True)
                g_col = nega_ref[d, hd] * _softplus(a_col + dtb_ref[d, hd])
                beta = jax.nn.sigmoid(b_col)
                gcum = _mask_dot_col(incl_b[d], g_col)
                gtot = _mask_dot_col(same_b, g_col)
                gc = gcum[:, 0:1]
                decay = jnp.where(incl[d], jnp.exp(jnp.where(incl[d], gcum - gcum.T, 0.0)), 0.0)
                kbeta = k * beta
                kk = lax.dot_general(kbeta.astype(BF16), kb16, (((1,), (1,)), ((), ())),
                                     preferred_element_type=F32)
                a_mat = jnp.where(strict[d], kk * decay, 0.0)
                eg = jnp.exp(gc)
                rhs = jnp.concatenate([kbeta * eg, v * beta], axis=1).astype(BF16)
                aqk = jnp.where(incl[d], qk * decay, 0.0).astype(BF16)
                ke = (k * jnp.exp(gtot[:, 0:1] - gc)).astype(BF16)
                cds = [jnp.exp(gtot[j * c:j * c + 1, :]) for j in range(per)]
                chains.append(dict(d=d, row0=row0, a=a_mat, rhs=rhs, aqk=aqk, ke=ke, cds=cds, qeg=q * eg))
        def coupling(d, s):
            return strict[d] & ((ri // (2 * s)) == (ci // (2 * s))) & ((ri // s) != (ci // s))
        for ch in chains:
            ch['inv'] = eye - jnp.where(coupling(ch['d'], 1), ch['a'], 0.0)
        s = 2
        while s < c:
            for ch in chains:
                a_s = jnp.where(coupling(ch['d'], s), ch['a'], 0.0).astype(BF16)
                ch['t'] = jnp.dot(a_s, ch['inv'].astype(BF16), preferred_element_type=F32).astype(BF16)
            for ch in chains:
                ch['inv'] = ch['inv'] - jnp.dot(ch['inv'].astype(BF16), ch['t'], preferred_element_type=F32)
            s *= 2
        for ch in chains:
            ch['wu'] = jnp.dot(ch['inv'].astype(BF16), ch['rhs'], preferred_element_type=F32).astype(BF16)
        for ch in chains:
            d, row0 = ch['d'], ch['row0']
            awu = jnp.dot(ch['aqk'], ch['wu'], preferred_element_type=F32)
            rows = pl.ds(row0, sup)
            qp_s[d, rows, :] = (ch['qeg'] - awu[:, :GDN_DK]).astype(BF16)
            o0_s[d, rows, :] = awu[:, GDN_DK:]
            for j in range(per):
                kej = jnp.where((rowc // c) == j, ch['ke'], jnp.zeros((), BF16))
                idx = row0 // c + j
                kn_s[d, idx] = lax.dot_general(kej, ch['wu'], (((0,), (0,)), ((), ())),
                                               preferred_element_type=F32).astype(BF16)
                cd_s[d, idx] = ch['cds'][j]

    prep([((q_c, k_c, v_c, ab_c), n * sup, 0, lc) for n in range(lc // sup)])

    def prep_body(n, carry):
        prep([((q_l, k_l, v_l, ab_l), pl.multiple_of((GDN_UNITS * n + j) * sup, sup), lc, l)
              for j in range(GDN_UNITS)])
        return carry
    lax.fori_loop(0, l // (GDN_UNITS * sup), prep_body, 0)

    def chunk_step(d, idx, s):
        sb = s.astype(BF16)
        sin_s[d, idx] = sb
        kn = kn_s[d, idx]
        return (s * cd_s[d, idx] - jnp.dot(kn[:, :GDN_DK], sb, preferred_element_type=F32)
                + kn[:, GDN_DK:].astype(F32))

    def scan_body(t, carry):
        sf, sb = carry
        tb = jnp.where(t < ncc, ncc - 1 - t, nc - 1 - (t - ncc))
        return chunk_step(0, t, sf), chunk_step(1, tb, sb)
    zero = jnp.zeros((GDN_DK, GDN_DV), F32)
    lax.fori_loop(0, nc, scan_body, (zero, zero), unroll=2)

    def finish(y_ref, gate_ref, base, n_rows):
        def body(n, carry):
            r0 = pl.multiple_of(n * c, c)
            rows = pl.ds(base + r0, c)
            idx = base // c + n
            o = (o0_s[0, rows, :] + o0_s[1, rows, :]
                 + jnp.dot(qp_s[0, rows, :], sin_s[0, idx], preferred_element_type=F32)
                 + jnp.dot(qp_s[1, rows, :], sin_s[1, idx], preferred_element_type=F32))
            y = o * lax.rsqrt(jnp.mean(o * o, axis=-1, keepdims=True) + EPS) * gain_ref[...]
            gf = gate_ref[0, pl.ds(r0, c), :].astype(F32)
            y_ref[0, pl.ds(r0, c), :] = (y * gf * jax.nn.sigmoid(gf)).astype(y_ref.dtype)
            return carry
        lax.fori_loop(0, n_rows // c, body, 0, unroll=8)

    finish(yl_ref, g_l, lc, l)
    if ctx_out:
        finish(yc_ref, g_c, 0, lc)
    else:
        yc_ref[...] = jnp.zeros_like(yc_ref)


def _gdn_branch(zg_l, zab_l, zg_c, zab_c, conv_w, a_log, dt_bias, norm_gain, ctx_out):
    b, l, _ = zg_l.shape
    lc = zg_c.shape[1]
    h = GDN_HEADS
    sup = GDN_SUPER
    ltot = l + lc
    assert l % (GDN_UNITS * sup) == 0 and lc % sup == 0, (l, lc)
    nc = ltot // GDN_CHUNK
    neg_a = -jnp.exp(a_log.astype(F32))
    cw = conv_w.astype(F32).T.reshape(3 * h, LANES, SHORT_CONV).transpose(0, 2, 1)
    win = sup + 2 * GDN_HALO
    i = np.arange(sup)
    pd = np.zeros((sup, win), np.float32)
    pu = np.zeros((sup, win), np.float32)
    pd[i, GDN_HALO + i - 1] = 1.0
    pu[i, GDN_HALO + i + 1] = 1.0
    seq = lambda n, j: pl.BlockSpec((1, n, LANES), lambda bi, hh: (bi, 0, j * h + hh))
    abs_ = lambda n: pl.BlockSpec((1, n, LANES), lambda bi, hh: (bi, 0, 0))
    cws = lambda j: pl.BlockSpec((1, SHORT_CONV, LANES), lambda bi, hh: (j * h + hh, 0, 0))
    full = lambda s: pl.BlockSpec(s, lambda bi, hh: (0, 0))
    smem = pl.BlockSpec(memory_space=pltpu.SMEM)
    yl, yc = pl.pallas_call(
        functools.partial(_gdn_kernel, ctx_out=ctx_out),
        grid=(b, h),
        in_specs=[smem, smem, seq(l, 0), seq(l, 1), seq(l, 2), seq(l, 3), abs_(l),
                  seq(lc, 0), seq(lc, 1), seq(lc, 2), seq(lc, 3), abs_(lc),
                  cws(0), cws(1), cws(2), full((sup, win)), full((sup, win)), full((1, GDN_DV))],
        out_specs=[pl.BlockSpec((1, l, GDN_DV), lambda bi, hh: (bi, 0, hh)),
                   pl.BlockSpec((1, lc, GDN_DV), lambda bi, hh: (bi, 0, hh))],
        out_shape=[jax.ShapeDtypeStruct((b, l, h * GDN_DV), BF16), jax.ShapeDtypeStruct((b, lc, h * GDN_DV), BF16)],
        scratch_shapes=[pltpu.VMEM((2, nc, GDN_DK, GDN_DK + GDN_DV), BF16), pltpu.VMEM((2, nc, GDN_DK, GDN_DV), BF16),
                        pltpu.VMEM((2, ltot, GDN_DK), BF16), pltpu.VMEM((2, ltot, GDN_DV), F32),
                        pltpu.VMEM((2, nc, 1, GDN_DV), F32)],
        compiler_params=_cparams(2),
        name="gated_deltanet",
    )(neg_a, dt_bias.astype(F32), zg_l, zg_l, zg_l, zg_l, zab_l, zg_c, zg_c, zg_c, zg_c, zab_c,
      cw, cw, cw, jnp.asarray(pd, BF16), jnp.asarray(pu, BF16), norm_gain.astype(F32).reshape(1, GDN_DV))
    return (yc if ctx_out else None), yl


def kernel(x, c, ctx, c_ctx, w_mod, b_mod, norm1, norm2, w_in, ret_decay, ret_gn, win_qnorm, win_knorm, win_sink,
           na_qnorm, na_knorm, na_rpb, gdn_conv, gdn_a_log, gdn_dt_bias, gdn_norm, w_branch, w_merge, w_out,
           w_router, router_bias, w_e_gate, w_e_up, w_e_down):
    b, l, d = x.shape
    lc = ctx.shape[1]
    depth = w_mod.shape[0]
    cos2, sin2 = _rope_tables(l)

    n_rows = 16
    cc = jnp.zeros((n_rows, d), F32).at[:b].set(c).at[b].set(c_ctx)
    mod = _modulation(cc, w_mod, b_mod).reshape(depth, n_rows, 6, d)

    wr = jnp.zeros((d, LANES), F32).at[:, :N_EXPERTS].set(w_router)
    wr_hi = wr.astype(BF16)
    wr_lo = (wr - wr_hi.astype(F32)).astype(BF16)

    xl, xc = x, ctx
    for layer in range(depth):
        ctx_out = layer < depth - 1
        mod_l = mod[layer, :b]
        mod_c = mod[layer, b:b + 1]
        w_all = _pack_w_in(w_in[layer])
        hl, *zl_s = _inproj(xl, mod_l, False, norm1[layer], w_all, min(l, TOKEN_TILE))
        hc, *zc_s = _inproj(xc, mod_c, True, norm1[layer], w_all, min(lc, TOKEN_TILE))
        zr_l, zw_l, zn_l, zg_l, zab_l = zl_s
        zr_c, zw_c, zn_c, zg_c, zab_c = zc_s
        ret_c, ret_l = _retention_branch(zr_l, zr_c, ret_decay[layer], ret_gn[layer], cos2, sin2, ctx_out)
        win_c, win_l = _window_branch(zw_l, zw_c, win_qnorm[layer], win_knorm[layer], win_sink[layer], cos2, sin2,
                                      ctx_out)
        na_c, na_l = _neighborhood_branch(zn_l, zn_c, na_qnorm[layer], na_knorm[layer], na_rpb[layer], ctx_out)
        gdn_c, gdn_l = _gdn_branch(zg_l, zab_l, zg_c, zab_c, gdn_conv[layer], gdn_a_log[layer], gdn_dt_bias[layer],
                                   gdn_norm[layer], ctx_out)
        wm = w_merge[layer].astype(BF16)
        wb = w_branch[layer].astype(BF16)
        wo = w_out[layer].astype(BF16)
        ys_l = [ret_l, win_l, na_l, gdn_l]
        xl, h2l, sc_l = _merge(xl, hl, ys_l, mod_l, False, norm2[layer], wm, wb, wo, wr_hi, wr_lo, MERGE_TILE)
        if ctx_out:
            ys_c = [ret_c, win_c, na_c, gdn_c]
            xc, h2c, sc_c = _merge(xc, hc, ys_c, mod_c, True, norm2[layer], wm, wb, wo, wr_hi, wr_lo, MERGE_TILE)
            tokens = jnp.concatenate([h2c.reshape(b * lc, d), h2l.reshape(b * l, d)], axis=0)
            scores = jnp.concatenate([sc_c.reshape(b * lc, LANES), sc_l.reshape(b * l, LANES)], axis=0)
            y, wts = _moe(tokens, scores[:, :N_EXPERTS], router_bias, w_e_gate[layer], w_e_up[layer], w_e_down[layer])
            xc = _combine(xc, y, wts, mod_c, True, 0, MERGE_TILE)
            xl = _combine(xl, y, wts, mod_l, False, b * lc, MERGE_TILE)
        else:
            y, wts = _moe(h2l.reshape(b * l, d), sc_l.reshape(b * l, LANES)[:, :N_EXPERTS], router_bias,
                          w_e_gate[layer], w_e_up[layer], w_e_down[layer])
            xl = _combine(xl, y, wts, mod_l, False, 0, MERGE_TILE)
    return xl
```
